```python
import math
import jax
import jax.numpy as jnp
from jax import lax
import numpy as np

D_MODEL = 2048
BATCH = 8
SEQ = 2048
DEPTH = 4

RMS_EPS = 1e-6
NEG_INF = -1e30
N_BRANCH = 3
SSM_WIDTH = D_MODEL // 4
SSM_GROUP = 16
SSM_GROUPS = SSM_WIDTH // SSM_GROUP
SSM_STATE = 64
DT_MIN = 1e-3
DT_MAX = 1e-1
HEAD_DIM = 64
DSWA_PATTERNS = ((128, 1), (512, 4), (2048, 16))
ATTN_WIDTH = D_MODEL // 4
HEADS_PER_PATTERN = ATTN_WIDTH // HEAD_DIM
N_ATTN_HEADS = HEADS_PER_PATTERN * len(DSWA_PATTERNS)
QKV_WIDTH = N_ATTN_HEADS * HEAD_DIM
CONV_WIDTH = D_MODEL // 4
CONV_K = 3
D_FF = 256 * ((8 * D_MODEL // 3 + 255) // 256)
FFN_CONV_K = 3
OFF_Q = SSM_WIDTH
OFF_K = OFF_Q + QKV_WIDTH
OFF_V = OFF_K + QKV_WIDTH
OFF_CONV = OFF_V + QKV_WIDTH
OFF_GATE = OFF_CONV + 3 * CONV_WIDTH
N_IN = OFF_GATE + N_BRANCH * D_MODEL

kernel_name = 'hybrid_ssm_dilated_attn_conv_trunk'


def alibi_slopes(n_heads):
    return np.array([2.0 ** (-8.0 * (h + 1) / n_heads) for h in range(n_heads)], dtype=np.float32)


def rms_norm(x, g):
    x32 = x.astype(jnp.float32)
    y = x32 * lax.rsqrt(jnp.mean(x32 * x32, axis=-1, keepdims=True) + RMS_EPS)
    return y.astype(x.dtype) * g


def causal_dwconv(z, w):
    k_width = w.shape[0]
    seq = z.shape[1]
    zp = jnp.pad(z, ((0, 0), (k_width - 1, 0), (0, 0)))
    return sum(w[k] * zp[:, k_width - 1 - k:k_width - 1 - k + seq] for k in range(k_width))


def s5_mixer(u, log_dt, a_re, a_im, b_re, b_im, c_re, c_im, d_skip, w_glu, b_glu):
    f32 = jnp.float32
    bsz, seq, _ = u.shape
    u32 = u.astype(f32)
    ug = u32.reshape(bsz, seq, SSM_GROUPS, SSM_GROUP)
    dt = jnp.exp(log_dt.astype(f32))[:, None]
    ar, ai = a_re.astype(f32), a_im.astype(f32)
    mag = jnp.exp(ar * dt)
    lr, li = mag * jnp.cos(ai * dt), mag * jnp.sin(ai * dt)
    den = ar * ar + ai * ai
    fr = ((lr - 1.0) * ar + li * ai) / den
    fi = (li * ar - (lr - 1.0) * ai) / den
    br, bi = b_re.astype(f32), b_im.astype(f32)
    bbr = fr[..., None] * br - fi[..., None] * bi
    bbi = fr[..., None] * bi + fi[..., None] * br
    xr = jnp.einsum('blgc,gnc->blgn', ug, bbr)
    xi = jnp.einsum('blgc,gnc->blgn', ug, bbi)
    lam_r = jnp.broadcast_to(lr, xr.shape)
    lam_i = jnp.broadcast_to(li, xr.shape)

    def combine(e1, e2):
        a1r, a1i, b1r, b1i = e1
        a2r, a2i, b2r, b2i = e2
        return (a2r * a1r - a2i * a1i,
                a2r * a1i + a2i * a1r,
                a2r * b1r - a2i * b1i + b2r,
                a2r * b1i + a2i * b1r + b2i)

    _, _, hr, hi = lax.associative_scan(combine, (lam_r, lam_i, xr, xi), axis=1)
    y = (jnp.einsum('blgn,gcn->blgc', hr, c_re.astype(f32))
         - jnp.einsum('blgn,gcn->blgc', hi, c_im.astype(f32)))
    y = y.reshape(bsz, seq, SSM_WIDTH) + d_skip.astype(f32) * u32
    g = jax.nn.gelu(y).astype(u.dtype)
    return g * jax.nn.sigmoid(g @ w_glu + b_glu)


def dilated_window_attention(q, k, v, slopes, window, dilation):
    f32 = jnp.float32
    bsz, seq, n_h, e = q.shape
    w_steps = window // dilation
    qb_len = w_steps
    unit = dilation * qb_len
    seq_p = -(-seq // unit) * unit
    pad = seq_p - seq
    m_len = seq_p // dilation
    nb = m_len // qb_len

    def to_blocks(t):
        t = jnp.pad(t, ((0, 0), (0, pad), (0, 0), (0, 0)))
        t = t.reshape(bsz, m_len, dilation, n_h, e).transpose(0, 2, 1, 3, 4)
        return t.reshape(bsz, dilation, nb, qb_len, n_h, e)

    def with_prev(t):
        prev = jnp.pad(t[:, :, :-1], ((0, 0), (0, 0), (1, 0), (0, 0), (0, 0), (0, 0)))
        return jnp.concatenate([prev, t], axis=3)

    qb = to_blocks(q)
    kk = with_prev(to_blocks(k))
    vv = with_prev(to_blocks(v))
    s = jnp.einsum('brnqhe,brnkhe->brnhqk', qb, kk).astype(f32) * (e ** -0.5)
    qi = jnp.arange(qb_len)[:, None]
    kj = jnp.arange(2 * qb_len)[None, :]
    dist = qb_len + qi - kj
    blk = jnp.arange(nb)[:, None, None]
    valid = (dist >= 0) & (dist <= w_steps) & ((blk > 0) | (kj >= qb_len))
    bias = -slopes[:, None, None] * (dist * dilation).astype(f32)
    s = jnp.where(valid[:, None], s + bias, NEG_INF)
    m = jnp.max(s, axis=-1, keepdims=True)
    p = jnp.exp(s - m)
    l = jnp.sum(p, axis=-1, keepdims=True)
    o = jnp.einsum('brnhqk,brnkhe->brnqhe', p / l, vv.astype(f32))
    lse = (m + jnp.log(l))[..., 0]
    o = o.reshape(bsz, dilation, m_len, n_h, e).transpose(0, 2, 1, 3, 4).reshape(bsz, seq_p, n_h, e)
    lse = lse.transpose(0, 1, 2, 4, 3).reshape(bsz, dilation, m_len, n_h)
    lse = lse.transpose(0, 2, 1, 3).reshape(bsz, seq_p, n_h)
    return o[:, :seq], lse[:, :seq]


def dilated_attention_mixer(q, k, v):
    bsz, seq = q.shape[0], q.shape[1]
    slopes = jnp.asarray(alibi_slopes(N_ATTN_HEADS))
    outs, lses = [], []
    for g, (window, dilation) in enumerate(DSWA_PATTERNS):
        hs = slice(g * HEADS_PER_PATTERN, (g + 1) * HEADS_PER_PATTERN)
        o, lse = dilated_window_attention(q[:, :, hs], k[:, :, hs], v[:, :, hs], slopes[hs], window, dilation)
        outs.append(o)
        lses.append(lse)
    alpha = jax.nn.softmax(jnp.stack(lses, axis=0), axis=0)
    o = jnp.sum(alpha[..., None] * jnp.stack(outs, axis=0), axis=0)
    return o.reshape(bsz, seq, ATTN_WIDTH).astype(q.dtype)


def hybrid_mixer(h, w_in, ssm_log_dt, ssm_a_re, ssm_a_im, ssm_b_re, ssm_b_im, ssm_c_re, ssm_c_im,
                 ssm_d, w_glu, b_glu, conv_mix_w, w_ssm_out, w_attn_out, w_conv_out, b_gate, w_o):
    bsz, seq, _ = h.shape
    proj = h @ w_in
    u = proj[..., :OFF_Q]
    q = proj[..., OFF_Q:OFF_K].reshape(bsz, seq, N_ATTN_HEADS, HEAD_DIM)
    k = proj[..., OFF_K:OFF_V].reshape(bsz, seq, N_ATTN_HEADS, HEAD_DIM)
    v = proj[..., OFF_V:OFF_CONV].reshape(bsz, seq, N_ATTN_HEADS, HEAD_DIM)
    conv_b, conv_c, conv_h = jnp.split(proj[..., OFF_CONV:OFF_GATE], 3, axis=-1)
    gates = jax.nn.sigmoid(proj[..., OFF_GATE:] + b_gate).reshape(bsz, seq, N_BRANCH, D_MODEL)
    y_ssm = s5_mixer(u, ssm_log_dt, ssm_a_re, ssm_a_im, ssm_b_re, ssm_b_im, ssm_c_re, ssm_c_im,
                     ssm_d, w_glu, b_glu) @ w_ssm_out
    y_attn = dilated_attention_mixer(q, k, v) @ w_attn_out
    y_conv = (conv_b * causal_dwconv(conv_c * conv_h, conv_mix_w)) @ w_conv_out
    merged = gates[:, :, 0] * y_ssm + gates[:, :, 1] * y_attn + gates[:, :, 2] * y_conv
    return merged @ w_o


def conv_ffn(h, w_up, ffn_conv_w, w_down):
    up = causal_dwconv(h @ w_up, ffn_conv_w)
    a, b = jnp.split(up, 2, axis=-1)
    return (jax.nn.silu(a) * b) @ w_down


def _fwd_setup_inputs(seed: int = 0) -> dict:
    key = jax.random.key(seed)
    ks = jax.random.split(key, 32)
    f32 = jnp.float32

    def nrm(k, shape, scale):
        return scale * jax.random.normal(k, shape, f32)

    nl = DEPTH
    n_idx = jnp.arange(SSM_STATE, dtype=f32)
    return {
        'x': nrm(ks[0], (BATCH, SEQ, D_MODEL), 1.0),
        'c': nrm(ks[1], (BATCH, D_MODEL), 1.0),
        'w_mod': nrm(ks[2], (nl, D_MODEL, 6 * D_MODEL), 0.5 * D_MODEL ** -0.5),
        'b_mod': nrm(ks[3], (nl, 6 * D_MODEL), 0.01),
        'g_pre_mix': 1.0 + nrm(ks[4], (nl, D_MODEL), 0.02),
        'g_post_mix': 1.0 + nrm(ks[5], (nl, D_MODEL), 0.02),
        'g_pre_ffn': 1.0 + nrm(ks[6], (nl, D_MODEL), 0.02),
        'g_post_ffn': 1.0 + nrm(ks[7], (nl, D_MODEL), 0.02),
        'w_in': nrm(ks[8], (nl, D_MODEL, N_IN), D_MODEL ** -0.5),
        'ssm_log_dt': jax.random.uniform(ks[9], (nl, SSM_GROUPS), f32, math.log(DT_MIN), math.log(DT_MAX)),
        'ssm_a_re': -0.5 + nrm(ks[10], (nl, SSM_GROUPS, SSM_STATE), 0.01),
        'ssm_a_im': jnp.pi * n_idx + nrm(ks[11], (nl, SSM_GROUPS, SSM_STATE), 0.01),
        'ssm_b_re': nrm(ks[12], (nl, SSM_GROUPS, SSM_STATE, SSM_GROUP), (2 * SSM_GROUP) ** -0.5),
        'ssm_b_im': nrm(ks[13], (nl, SSM_GROUPS, SSM_STATE, SSM_GROUP), (2 * SSM_GROUP) ** -0.5),
        'ssm_c_re': nrm(ks[14], (nl, SSM_GROUPS, SSM_GROUP, SSM_STATE), 0.5),
        'ssm_c_im': nrm(ks[15], (nl, SSM_GROUPS, SSM_GROUP, SSM_STATE), 0.5),
        'ssm_d': nrm(ks[16], (nl, SSM_WIDTH), 1.0),
        'w_glu': nrm(ks[17], (nl, SSM_WIDTH, SSM_WIDTH), SSM_WIDTH ** -0.5),
        'b_glu': nrm(ks[18], (nl, SSM_WIDTH), 0.01),
        'conv_mix_w': nrm(ks[19], (nl, CONV_K, CONV_WIDTH), CONV_K ** -0.5),
        'w_ssm_out': nrm(ks[20], (nl, SSM_WIDTH, D_MODEL), SSM_WIDTH ** -0.5),
        'w_attn_out': nrm(ks[21], (nl, ATTN_WIDTH, D_MODEL), ATTN_WIDTH ** -0.5),
        'w_conv_out': nrm(ks[22], (nl, CONV_WIDTH, D_MODEL), CONV_WIDTH ** -0.5),
        'b_gate': nrm(ks[23], (nl, N_BRANCH * D_MODEL), 0.01),
        'w_o': nrm(ks[24], (nl, D_MODEL, D_MODEL), D_MODEL ** -0.5),
        'w_up': nrm(ks[25], (nl, D_MODEL, 2 * D_FF), D_MODEL ** -0.5),
        'ffn_conv_w': nrm(ks[26], (nl, FFN_CONV_K, 2 * D_FF), FFN_CONV_K ** -0.5),
        'w_down': nrm(ks[27], (nl, D_FF, D_MODEL), D_FF ** -0.5),
    }


def _fwd_reference(x, c, w_mod, b_mod, g_pre_mix, g_post_mix, g_pre_ffn, g_post_ffn, w_in,
              ssm_log_dt, ssm_a_re, ssm_a_im, ssm_b_re, ssm_b_im, ssm_c_re, ssm_c_im, ssm_d,
              w_glu, b_glu, conv_mix_w, w_ssm_out, w_attn_out, w_conv_out, b_gate, w_o,
              w_up, ffn_conv_w, w_down):
    cond = jax.nn.silu(c)
    for l in range(DEPTH):
        mod = cond @ w_mod[l] + b_mod[l]
        sh1, sc1, gt1, sh2, sc2, gt2 = jnp.split(mod, 6, axis=-1)
        h = rms_norm(x, g_pre_mix[l]) * (1.0 + sc1[:, None]) + sh1[:, None]
        y = hybrid_mixer(h, w_in[l], ssm_log_dt[l], ssm_a_re[l], ssm_a_im[l], ssm_b_re[l], ssm_b_im[l],
                         ssm_c_re[l], ssm_c_im[l], ssm_d[l], w_glu[l], b_glu[l], conv_mix_w[l],
                         w_ssm_out[l], w_attn_out[l], w_conv_out[l], b_gate[l], w_o[l])
        x = x + gt1[:, None] * rms_norm(y, g_post_mix[l])
        h = rms_norm(x, g_pre_ffn[l]) * (1.0 + sc2[:, None]) + sh2[:, None]
        y = conv_ffn(h, w_up[l], ffn_conv_w[l], w_down[l])
        x = x + gt2[:, None] * rms_norm(y, g_post_ffn[l])
    return x


import jax as _jax
import jax.numpy as _jnp

TWIN_FORMAT = 'train_step'
FWD_PARAMS = ['x', 'c', 'w_mod', 'b_mod', 'g_pre_mix', 'g_post_mix', 'g_pre_ffn', 'g_post_ffn', 'w_in', 'ssm_log_dt', 'ssm_a_re', 'ssm_a_im', 'ssm_b_re', 'ssm_b_im', 'ssm_c_re', 'ssm_c_im', 'ssm_d', 'w_glu', 'b_glu', 'conv_mix_w', 'w_ssm_out', 'w_attn_out', 'w_conv_out', 'b_gate', 'w_o', 'w_up', 'ffn_conv_w', 'w_down']
TWIN_WEIGHTS = ['w_mod', 'b_mod', 'g_pre_mix', 'g_post_mix', 'g_pre_ffn', 'g_post_ffn', 'w_in', 'ssm_log_dt', 'ssm_a_re', 'ssm_a_im', 'ssm_b_re', 'ssm_b_im', 'ssm_c_re', 'ssm_c_im', 'ssm_d', 'w_glu', 'b_glu', 'conv_mix_w', 'w_ssm_out', 'w_attn_out', 'w_conv_out', 'b_gate', 'w_o', 'w_up', 'ffn_conv_w', 'w_down']
TWIN_DIFF_INPUT = 'x'
TWIN_INPUTS = ['x', 'c', 'w_mod', 'b_mod', 'g_pre_mix', 'g_post_mix', 'g_pre_ffn', 'g_post_ffn', 'w_in', 'ssm_log_dt', 'ssm_a_re', 'ssm_a_im', 'ssm_b_re', 'ssm_b_im', 'ssm_c_re', 'ssm_c_im', 'ssm_d', 'w_glu', 'b_glu', 'conv_mix_w', 'w_ssm_out', 'w_attn_out', 'w_conv_out', 'b_gate', 'w_o', 'w_up', 'ffn_conv_w', 'w_down', 'loss_target', 'm_w_mod', 'm_b_mod', 'm_g_pre_mix', 'm_g_post_mix', 'm_g_pre_ffn', 'm_g_post_ffn', 'm_w_in', 'm_ssm_log_dt', 'm_ssm_a_re', 'm_ssm_a_im', 'm_ssm_b_re', 'm_ssm_b_im', 'm_ssm_c_re', 'm_ssm_c_im', 'm_ssm_d', 'm_w_glu', 'm_b_glu', 'm_conv_mix_w', 'm_w_ssm_out', 'm_w_attn_out', 'm_w_conv_out', 'm_b_gate', 'm_w_o', 'm_w_up', 'm_ffn_conv_w', 'm_w_down', 'v_w_mod', 'v_b_mod', 'v_g_pre_mix', 'v_g_post_mix', 'v_g_pre_ffn', 'v_g_post_ffn', 'v_w_in', 'v_ssm_log_dt', 'v_ssm_a_re', 'v_ssm_a_im', 'v_ssm_b_re', 'v_ssm_b_im', 'v_ssm_c_re', 'v_ssm_c_im', 'v_ssm_d', 'v_w_glu', 'v_b_glu', 'v_conv_mix_w', 'v_w_ssm_out', 'v_w_attn_out', 'v_w_conv_out', 'v_b_gate', 'v_w_o', 'v_w_up', 'v_ffn_conv_w', 'v_w_down']
TWIN_OUTPUTS = ['loss', 'grad_x', 'grad_w_mod', 'grad_b_mod', 'grad_g_pre_mix', 'grad_g_post_mix', 'grad_g_pre_ffn', 'grad_g_post_ffn', 'grad_w_in', 'grad_ssm_log_dt', 'grad_ssm_a_re', 'grad_ssm_a_im', 'grad_ssm_b_re', 'grad_ssm_b_im', 'grad_ssm_c_re', 'grad_ssm_c_im', 'grad_ssm_d', 'grad_w_glu', 'grad_b_glu', 'grad_conv_mix_w', 'grad_w_ssm_out', 'grad_w_attn_out', 'grad_w_conv_out', 'grad_b_gate', 'grad_w_o', 'grad_w_up', 'grad_ffn_conv_w', 'grad_w_down', 'delta_w_mod', 'delta_b_mod', 'delta_g_pre_mix', 'delta_g_post_mix', 'delta_g_pre_ffn', 'delta_g_post_ffn', 'delta_w_in', 'delta_ssm_log_dt', 'delta_ssm_a_re', 'delta_ssm_a_im', 'delta_ssm_b_re', 'delta_ssm_b_im', 'delta_ssm_c_re', 'delta_ssm_c_im', 'delta_ssm_d', 'delta_w_glu', 'delta_b_glu', 'delta_conv_mix_w', 'delta_w_ssm_out', 'delta_w_attn_out', 'delta_w_conv_out', 'delta_b_gate', 'delta_w_o', 'delta_w_up', 'delta_ffn_conv_w', 'delta_w_down', 'new_m_w_mod', 'new_m_b_mod', 'new_m_g_pre_mix', 'new_m_g_post_mix', 'new_m_g_pre_ffn', 'new_m_g_post_ffn', 'new_m_w_in', 'new_m_ssm_log_dt', 'new_m_ssm_a_re', 'new_m_ssm_a_im', 'new_m_ssm_b_re', 'new_m_ssm_b_im', 'new_m_ssm_c_re', 'new_m_ssm_c_im', 'new_m_ssm_d', 'new_m_w_glu', 'new_m_b_glu', 'new_m_conv_mix_w', 'new_m_w_ssm_out', 'new_m_w_attn_out', 'new_m_w_conv_out', 'new_m_b_gate', 'new_m_w_o', 'new_m_w_up', 'new_m_ffn_conv_w', 'new_m_w_down', 'new_v_w_mod', 'new_v_b_mod', 'new_v_g_pre_mix', 'new_v_g_post_mix', 'new_v_g_pre_ffn', 'new_v_g_post_ffn', 'new_v_w_in', 'new_v_ssm_log_dt', 'new_v_ssm_a_re', 'new_v_ssm_a_im', 'new_v_ssm_b_re', 'new_v_ssm_b_im', 'new_v_ssm_c_re', 'new_v_ssm_c_im', 'new_v_ssm_d', 'new_v_w_glu', 'new_v_b_glu', 'new_v_conv_mix_w', 'new_v_w_ssm_out', 'new_v_w_attn_out', 'new_v_w_conv_out', 'new_v_b_gate', 'new_v_w_o', 'new_v_w_up', 'new_v_ffn_conv_w', 'new_v_w_down']
TWIN_LEAF_KINDS = {'loss': 'loss', 'grad_x': 'grad_x', 'grad_w_mod': 'grad_w', 'grad_b_mod': 'grad_w', 'grad_g_pre_mix': 'grad_w', 'grad_g_post_mix': 'grad_w', 'grad_g_pre_ffn': 'grad_w', 'grad_g_post_ffn': 'grad_w', 'grad_w_in': 'grad_w', 'grad_ssm_log_dt': 'grad_w', 'grad_ssm_a_re': 'grad_w', 'grad_ssm_a_im': 'grad_w', 'grad_ssm_b_re': 'grad_w', 'grad_ssm_b_im': 'grad_w', 'grad_ssm_c_re': 'grad_w', 'grad_ssm_c_im': 'grad_w', 'grad_ssm_d': 'grad_w', 'grad_w_glu': 'grad_w', 'grad_b_glu': 'grad_w', 'grad_conv_mix_w': 'grad_w', 'grad_w_ssm_out': 'grad_w', 'grad_w_attn_out': 'grad_w', 'grad_w_conv_out': 'grad_w', 'grad_b_gate': 'grad_w', 'grad_w_o': 'grad_w', 'grad_w_up': 'grad_w', 'grad_ffn_conv_w': 'grad_w', 'grad_w_down': 'grad_w', 'delta_w_mod': 'delta_w', 'delta_b_mod': 'delta_w', 'delta_g_pre_mix': 'delta_w', 'delta_g_post_mix': 'delta_w', 'delta_g_pre_ffn': 'delta_w', 'delta_g_post_ffn': 'delta_w', 'delta_w_in': 'delta_w', 'delta_ssm_log_dt': 'delta_w', 'delta_ssm_a_re': 'delta_w', 'delta_ssm_a_im': 'delta_w', 'delta_ssm_b_re': 'delta_w', 'delta_ssm_b_im': 'delta_w', 'delta_ssm_c_re': 'delta_w', 'delta_ssm_c_im': 'delta_w', 'delta_ssm_d': 'delta_w', 'delta_w_glu': 'delta_w', 'delta_b_glu': 'delta_w', 'delta_conv_mix_w': 'delta_w', 'delta_w_ssm_out': 'delta_w', 'delta_w_attn_out': 'delta_w', 'delta_w_conv_out': 'delta_w', 'delta_b_gate': 'delta_w', 'delta_w_o': 'delta_w', 'delta_w_up': 'delta_w', 'delta_ffn_conv_w': 'delta_w', 'delta_w_down': 'delta_w', 'new_m_w_mod': 'new_m', 'new_m_b_mod': 'new_m', 'new_m_g_pre_mix': 'new_m', 'new_m_g_post_mix': 'new_m', 'new_m_g_pre_ffn': 'new_m', 'new_m_g_post_ffn': 'new_m', 'new_m_w_in': 'new_m', 'new_m_ssm_log_dt': 'new_m', 'new_m_ssm_a_re': 'new_m', 'new_m_ssm_a_im': 'new_m', 'new_m_ssm_b_re': 'new_m', 'new_m_ssm_b_im': 'new_m', 'new_m_ssm_c_re': 'new_m', 'new_m_ssm_c_im': 'new_m', 'new_m_ssm_d': 'new_m', 'new_m_w_glu': 'new_m', 'new_m_b_glu': 'new_m', 'new_m_conv_mix_w': 'new_m', 'new_m_w_ssm_out': 'new_m', 'new_m_w_attn_out': 'new_m', 'new_m_w_conv_out': 'new_m', 'new_m_b_gate': 'new_m', 'new_m_w_o': 'new_m', 'new_m_w_up': 'new_m', 'new_m_ffn_conv_w': 'new_m', 'new_m_w_down': 'new_m', 'new_v_w_mod': 'new_v', 'new_v_b_mod': 'new_v', 'new_v_g_pre_mix': 'new_v', 'new_v_g_post_mix': 'new_v', 'new_v_g_pre_ffn': 'new_v', 'new_v_g_post_ffn': 'new_v', 'new_v_w_in': 'new_v', 'new_v_ssm_log_dt': 'new_v', 'new_v_ssm_a_re': 'new_v', 'new_v_ssm_a_im': 'new_v', 'new_v_ssm_b_re': 'new_v', 'new_v_ssm_b_im': 'new_v', 'new_v_ssm_c_re': 'new_v', 'new_v_ssm_c_im': 'new_v', 'new_v_ssm_d': 'new_v', 'new_v_w_glu': 'new_v', 'new_v_b_glu': 'new_v', 'new_v_conv_mix_w': 'new_v', 'new_v_w_ssm_out': 'new_v', 'new_v_w_attn_out': 'new_v', 'new_v_w_conv_out': 'new_v', 'new_v_b_gate': 'new_v', 'new_v_w_o': 'new_v', 'new_v_w_up': 'new_v', 'new_v_ffn_conv_w': 'new_v', 'new_v_w_down': 'new_v'}


def _forward(args):
    return _fwd_reference(*[args[k] for k in FWD_PARAMS])


def _output_shape():
    out = _jax.eval_shape(lambda: _forward(_fwd_setup_inputs(0)))
    return out.shape, out.dtype

N_MICROBATCH = 1
ADAM_LR = 0.001
ADAM_B1 = 0.9
ADAM_B2 = 0.999
ADAM_EPS = 1e-08
ADAM_WD = 0.01
ADAM_STEP = 10
PER_EXAMPLE_BATCH_AXIS = {'x': 0, 'c': 0, 'loss_target': 0}
SHARED_INPUTS = []
_WEIGHT_DTYPES = {'w_mod': _jnp.float32, 'b_mod': _jnp.float32, 'g_pre_mix': _jnp.float32, 'g_post_mix': _jnp.float32, 'g_pre_ffn': _jnp.float32, 'g_post_ffn': _jnp.float32, 'w_in': _jnp.float32, 'ssm_log_dt': _jnp.float32, 'ssm_a_re': _jnp.float32, 'ssm_a_im': _jnp.float32, 'ssm_b_re': _jnp.float32, 'ssm_b_im': _jnp.float32, 'ssm_c_re': _jnp.float32, 'ssm_c_im': _jnp.float32, 'ssm_d': _jnp.float32, 'w_glu': _jnp.float32, 'b_glu': _jnp.float32, 'conv_mix_w': _jnp.float32, 'w_ssm_out': _jnp.float32, 'w_attn_out': _jnp.float32, 'w_conv_out': _jnp.float32, 'b_gate': _jnp.float32, 'w_o': _jnp.float32, 'w_up': _jnp.float32, 'ffn_conv_w': _jnp.float32, 'w_down': _jnp.float32}
MOMENT_SCALE = {'w_mod': 4.099062e-01, 'b_mod': 7.701001e-01, 'g_pre_mix': 7.177277e-02, 'g_post_mix': 8.899588e-01, 'g_pre_ffn': 5.561354e-02, 'g_post_ffn': 8.896191e-01, 'w_in': 3.072885e-02, 'ssm_log_dt': 4.588915e+00, 'ssm_a_re': 2.089046e-02, 'ssm_a_im': 1.952303e-02, 'ssm_b_re': 9.899313e-03, 'ssm_b_im': 1.055826e-02, 'ssm_c_re': 3.578746e-03, 'ssm_c_im': 3.659986e-03, 'ssm_d': 5.440365e-02, 'w_glu': 1.330258e-02, 'b_glu': 2.683394e-02, 'conv_mix_w': 8.346887e-02, 'w_ssm_out': 2.707810e-02, 'w_attn_out': 2.106223e-02, 'w_conv_out': 4.128435e-02, 'b_gate': 1.140838e-02, 'w_o': 5.310188e-02, 'w_up': 2.623386e-02, 'ffn_conv_w': 2.687679e-02, 'w_down': 4.528861e-02}


def _to_microbatches(a, axis):
    t = _jnp.moveaxis(a, axis, 0)
    t = t.reshape((N_MICROBATCH, t.shape[0] // N_MICROBATCH) + t.shape[1:])
    return _jnp.moveaxis(t, 1, axis + 1)


def setup_inputs(seed: int = 0) -> dict:
    inp = _fwd_setup_inputs(seed)
    key = _jax.random.fold_in(_jax.random.key(seed), 7919)
    shape, _ = _output_shape()
    out = dict(inp)
    out["loss_target"] = _jax.random.normal(_jax.random.fold_in(key, 0), shape, _jnp.float32)
    for i, name in enumerate(TWIN_WEIGHTS):
        w = inp[name].astype(_jnp.float32)
        if MOMENT_SCALE is None:
            s = _jnp.sqrt(_jnp.mean(_jnp.square(w)) + 1e-30)
        else:
            s = MOMENT_SCALE[name]
        km, kv = _jax.random.split(_jax.random.fold_in(key, i + 1))
        out[name] = w
        out["m_" + name] = s * _jax.random.normal(km, w.shape, _jnp.float32)
        out["v_" + name] = (s * s) * _jax.random.uniform(kv, w.shape, _jnp.float32, 0.5, 1.5)
    if N_MICROBATCH > 1:
        for name, axis in PER_EXAMPLE_BATCH_AXIS.items():
            out[name] = _to_microbatches(out[name], axis)
    return {'x': out['x'], 'c': out['c'], 'w_mod': out['w_mod'], 'b_mod': out['b_mod'], 'g_pre_mix': out['g_pre_mix'], 'g_post_mix': out['g_post_mix'], 'g_pre_ffn': out['g_pre_ffn'], 'g_post_ffn': out['g_post_ffn'], 'w_in': out['w_in'], 'ssm_log_dt': out['ssm_log_dt'], 'ssm_a_re': out['ssm_a_re'], 'ssm_a_im': out['ssm_a_im'], 'ssm_b_re': out['ssm_b_re'], 'ssm_b_im': out['ssm_b_im'], 'ssm_c_re': out['ssm_c_re'], 'ssm_c_im': out['ssm_c_im'], 'ssm_d': out['ssm_d'], 'w_glu': out['w_glu'], 'b_glu': out['b_glu'], 'conv_mix_w': out['conv_mix_w'], 'w_ssm_out': out['w_ssm_out'], 'w_attn_out': out['w_attn_out'], 'w_conv_out': out['w_conv_out'], 'b_gate': out['b_gate'], 'w_o': out['w_o'], 'w_up': out['w_up'], 'ffn_conv_w': out['ffn_conv_w'], 'w_down': out['w_down'], 'loss_target': out['loss_target'], 'm_w_mod': out['m_w_mod'], 'm_b_mod': out['m_b_mod'], 'm_g_pre_mix': out['m_g_pre_mix'], 'm_g_post_mix': out['m_g_post_mix'], 'm_g_pre_ffn': out['m_g_pre_ffn'], 'm_g_post_ffn': out['m_g_post_ffn'], 'm_w_in': out['m_w_in'], 'm_ssm_log_dt': out['m_ssm_log_dt'], 'm_ssm_a_re': out['m_ssm_a_re'], 'm_ssm_a_im': out['m_ssm_a_im'], 'm_ssm_b_re': out['m_ssm_b_re'], 'm_ssm_b_im': out['m_ssm_b_im'], 'm_ssm_c_re': out['m_ssm_c_re'], 'm_ssm_c_im': out['m_ssm_c_im'], 'm_ssm_d': out['m_ssm_d'], 'm_w_glu': out['m_w_glu'], 'm_b_glu': out['m_b_glu'], 'm_conv_mix_w': out['m_conv_mix_w'], 'm_w_ssm_out': out['m_w_ssm_out'], 'm_w_attn_out': out['m_w_attn_out'], 'm_w_conv_out': out['m_w_conv_out'], 'm_b_gate': out['m_b_gate'], 'm_w_o': out['m_w_o'], 'm_w_up': out['m_w_up'], 'm_ffn_conv_w': out['m_ffn_conv_w'], 'm_w_down': out['m_w_down'], 'v_w_mod': out['v_w_mod'], 'v_b_mod': out['v_b_mod'], 'v_g_pre_mix': out['v_g_pre_mix'], 'v_g_post_mix': out['v_g_post_mix'], 'v_g_pre_ffn': out['v_g_pre_ffn'], 'v_g_post_ffn': out['v_g_post_ffn'], 'v_w_in': out['v_w_in'], 'v_ssm_log_dt': out['v_ssm_log_dt'], 'v_ssm_a_re': out['v_ssm_a_re'], 'v_ssm_a_im': out['v_ssm_a_im'], 'v_ssm_b_re': out['v_ssm_b_re'], 'v_ssm_b_im': out['v_ssm_b_im'], 'v_ssm_c_re': out['v_ssm_c_re'], 'v_ssm_c_im': out['v_ssm_c_im'], 'v_ssm_d': out['v_ssm_d'], 'v_w_glu': out['v_w_glu'], 'v_b_glu': out['v_b_glu'], 'v_conv_mix_w': out['v_conv_mix_w'], 'v_w_ssm_out': out['v_w_ssm_out'], 'v_w_attn_out': out['v_w_attn_out'], 'v_w_conv_out': out['v_w_conv_out'], 'v_b_gate': out['v_b_gate'], 'v_w_o': out['v_w_o'], 'v_w_up': out['v_w_up'], 'v_ffn_conv_w': out['v_ffn_conv_w'], 'v_w_down': out['v_w_down']}


def _loss(weights, diff, rest, loss_target):
    with _jax.named_scope("forward"):
        args = {**rest, TWIN_DIFF_INPUT: diff, **{k: w.astype(_WEIGHT_DTYPES[k]) for k, w in weights.items()}}
        y = _forward(args)
    with _jax.named_scope("loss_head"):
        err = _jnp.square(y.astype(_jnp.float32) - loss_target)
        return 0.5 * _jnp.sum(_jnp.mean(err, axis=-1)) if err.ndim else 0.5 * err


def _adamw(w, g, m, v):
    m = ADAM_B1 * m + (1.0 - ADAM_B1) * g
    v = ADAM_B2 * v + (1.0 - ADAM_B2) * _jnp.square(g)
    m_hat = m / (1.0 - ADAM_B1 ** ADAM_STEP)
    v_hat = v / (1.0 - ADAM_B2 ** ADAM_STEP)
    delta = -ADAM_LR * (m_hat / (_jnp.sqrt(v_hat) + ADAM_EPS) + ADAM_WD * w)
    return delta, m, v


def reference(x, c, w_mod, b_mod, g_pre_mix, g_post_mix, g_pre_ffn, g_post_ffn, w_in, ssm_log_dt, ssm_a_re, ssm_a_im, ssm_b_re, ssm_b_im, ssm_c_re, ssm_c_im, ssm_d, w_glu, b_glu, conv_mix_w, w_ssm_out, w_attn_out, w_conv_out, b_gate, w_o, w_up, ffn_conv_w, w_down, loss_target, m_w_mod, m_b_mod, m_g_pre_mix, m_g_post_mix, m_g_pre_ffn, m_g_post_ffn, m_w_in, m_ssm_log_dt, m_ssm_a_re, m_ssm_a_im, m_ssm_b_re, m_ssm_b_im, m_ssm_c_re, m_ssm_c_im, m_ssm_d, m_w_glu, m_b_glu, m_conv_mix_w, m_w_ssm_out, m_w_attn_out, m_w_conv_out, m_b_gate, m_w_o, m_w_up, m_ffn_conv_w, m_w_down, v_w_mod, v_b_mod, v_g_pre_mix, v_g_post_mix, v_g_pre_ffn, v_g_post_ffn, v_w_in, v_ssm_log_dt, v_ssm_a_re, v_ssm_a_im, v_ssm_b_re, v_ssm_b_im, v_ssm_c_re, v_ssm_c_im, v_ssm_d, v_w_glu, v_b_glu, v_conv_mix_w, v_w_ssm_out, v_w_attn_out, v_w_conv_out, v_b_gate, v_w_o, v_w_up, v_ffn_conv_w, v_w_down):
    given = dict(x=x, c=c, w_mod=w_mod, b_mod=b_mod, g_pre_mix=g_pre_mix, g_post_mix=g_post_mix, g_pre_ffn=g_pre_ffn, g_post_ffn=g_post_ffn, w_in=w_in, ssm_log_dt=ssm_log_dt, ssm_a_re=ssm_a_re, ssm_a_im=ssm_a_im, ssm_b_re=ssm_b_re, ssm_b_im=ssm_b_im, ssm_c_re=ssm_c_re, ssm_c_im=ssm_c_im, ssm_d=ssm_d, w_glu=w_glu, b_glu=b_glu, conv_mix_w=conv_mix_w, w_ssm_out=w_ssm_out, w_attn_out=w_attn_out, w_conv_out=w_conv_out, b_gate=b_gate, w_o=w_o, w_up=w_up, ffn_conv_w=ffn_conv_w, w_down=w_down, loss_target=loss_target, m_w_mod=m_w_mod, m_b_mod=m_b_mod, m_g_pre_mix=m_g_pre_mix, m_g_post_mix=m_g_post_mix, m_g_pre_ffn=m_g_pre_ffn, m_g_post_ffn=m_g_post_ffn, m_w_in=m_w_in, m_ssm_log_dt=m_ssm_log_dt, m_ssm_a_re=m_ssm_a_re, m_ssm_a_im=m_ssm_a_im, m_ssm_b_re=m_ssm_b_re, m_ssm_b_im=m_ssm_b_im, m_ssm_c_re=m_ssm_c_re, m_ssm_c_im=m_ssm_c_im, m_ssm_d=m_ssm_d, m_w_glu=m_w_glu, m_b_glu=m_b_glu, m_conv_mix_w=m_conv_mix_w, m_w_ssm_out=m_w_ssm_out, m_w_attn_out=m_w_attn_out, m_w_conv_out=m_w_conv_out, m_b_gate=m_b_gate, m_w_o=m_w_o, m_w_up=m_w_up, m_ffn_conv_w=m_ffn_conv_w, m_w_down=m_w_down, v_w_mod=v_w_mod, v_b_mod=v_b_mod, v_g_pre_mix=v_g_pre_mix, v_g_post_mix=v_g_post_mix, v_g_pre_ffn=v_g_pre_ffn, v_g_post_ffn=v_g_post_ffn, v_w_in=v_w_in, v_ssm_log_dt=v_ssm_log_dt, v_ssm_a_re=v_ssm_a_re, v_ssm_a_im=v_ssm_a_im, v_ssm_b_re=v_ssm_b_re, v_ssm_b_im=v_ssm_b_im, v_ssm_c_re=v_ssm_c_re, v_ssm_c_im=v_ssm_c_im, v_ssm_d=v_ssm_d, v_w_glu=v_w_glu, v_b_glu=v_b_glu, v_conv_mix_w=v_conv_mix_w, v_w_ssm_out=v_w_ssm_out, v_w_attn_out=v_w_attn_out, v_w_conv_out=v_w_conv_out, v_b_gate=v_b_gate, v_w_o=v_w_o, v_w_up=v_w_up, v_ffn_conv_w=v_ffn_conv_w, v_w_down=v_w_down)
    weights = {n: given[n] for n in TWIN_WEIGHTS}
    shared = {n: given[n] for n in SHARED_INPUTS}
    per_example = {n: given[n] for n in ['x', 'c']}
    grad_fn = _jax.value_and_grad(_loss, argnums=(0, 1))

    def one_microbatch(ex, loss_target):
        ex = dict(ex)
        diff = ex.pop(TWIN_DIFF_INPUT)
        return grad_fn(weights, diff, {**shared, **ex}, loss_target)

    if N_MICROBATCH == 1:
        loss, (grad_w, grad_x) = one_microbatch(per_example, given["loss_target"])
    else:
        def body(carry, xs):
            loss_sum, grad_sum = carry
            l_k, (gw_k, gx_k) = one_microbatch(xs[0], xs[1])
            with _jax.named_scope("update"):
                return (loss_sum + l_k, _jax.tree.map(_jnp.add, grad_sum, gw_k)), gx_k

        init = (_jnp.zeros((), _jnp.float32), _jax.tree.map(_jnp.zeros_like, weights))
        (loss, grad_w), grad_x = _jax.lax.scan(body, init, (per_example, given["loss_target"]))
    with _jax.named_scope("update"):
        delta_w, new_m, new_v = {}, {}, {}
        for n in TWIN_WEIGHTS:
            delta_w[n], new_m[n], new_v[n] = _adamw(weights[n], grad_w[n], given["m_" + n], given["v_" + n])
    return (loss, grad_x, *[grad_w[n] for n in TWIN_WEIGHTS], *[delta_w[n] for n in TWIN_WEIGHTS],
            *[new_m[n] for n in TWIN_WEIGHTS], *[new_v[n] for n in TWIN_WEIGHTS])
```

```python
import functools
import math

import numpy as np
import jax
import jax.numpy as jnp
from jax import lax
from jax.experimental import pallas as pl
from jax.experimental.pallas import tpu as pltpu

f32 = jnp.float32
bf16 = jnp.bfloat16

N_DEV = 8
V7X_VMEM_LIMIT_BYTES = 56 * 1024 * 1024
LANES = 128
SUBLANES = 8

RMS_EPS = 1e-6
NEG_INF = -1e30
HEAD_DIM = 64
SSM_GROUP = 16
SSM_STATE = 64
DSWA_PATTERNS = ((128, 1), (512, 4), (2048, 16))
QBLK = 128

ADAM_LR = 0.001
ADAM_B1 = 0.9
ADAM_B2 = 0.999
ADAM_EPS = 1e-08
ADAM_WD = 0.01
ADAM_STEP = 10

MESH = pl.DeviceIdType.MESH
ANY = pl.BlockSpec(memory_space=pl.ANY)


def _tile(n, pref, mult=LANES):
    t = min(pref, n) // mult * mult
    while t >= mult:
        if n % t == 0:
            return t
        t -= mult
    return n


def _params(*sem):
    return pltpu.CompilerParams(dimension_semantics=sem, vmem_limit_bytes=V7X_VMEM_LIMIT_BYTES)


def _mm(a, b, out_dtype, name):
    m, k = a.shape
    k2, n = b.shape
    assert k == k2 and a.dtype == bf16 and b.dtype == bf16, (a.shape, b.shape, a.dtype, b.dtype)
    tm, tn, tk = _tile(m, 1024, SUBLANES), _tile(n, 512), _tile(k, 2048)
    nk = k // tk

    def body(a_ref, b_ref, o_ref, acc_ref):
        kk = pl.program_id(2)

        @pl.when(kk == 0)
        def _():
            acc_ref[...] = jnp.zeros_like(acc_ref)

        acc_ref[...] += jnp.dot(a_ref[...], b_ref[...], preferred_element_type=f32)

        @pl.when(kk == nk - 1)
        def _():
            o_ref[...] = acc_ref[...].astype(o_ref.dtype)

    return pl.pallas_call(
        body, grid=(m // tm, n // tn, nk),
        in_specs=[pl.BlockSpec((tm, tk), lambda i, j, kk: (i, kk)), pl.BlockSpec((tk, tn), lambda i, j, kk: (kk, j))],
        out_specs=pl.BlockSpec((tm, tn), lambda i, j, kk: (i, j)),
        out_shape=jax.ShapeDtypeStruct((m, n), out_dtype),
        scratch_shapes=[pltpu.VMEM((tm, tn), f32)],
        compiler_params=_params("parallel", "parallel", "arbitrary"), name=name)(a, b)


def _rowwise(fn, rows, vecs, out_dtypes, tm, name):
    n_rows = rows[0].shape[0]
    nr, nv = len(rows), len(vecs)
    outs = jax.eval_shape(fn, *[jax.ShapeDtypeStruct((tm, r.shape[1]), f32) for r in rows],
                          *[jax.ShapeDtypeStruct(v.shape, f32) for v in vecs])

    def body(*refs):
        vals = [r[...].astype(f32) for r in refs[:nr]] + [v[...] for v in refs[nr:nr + nv]]
        for o_ref, val in zip(refs[nr + nv:], fn(*vals)):
            o_ref[...] = val.astype(o_ref.dtype)

    res = pl.pallas_call(
        body, grid=(n_rows // tm,),
        in_specs=[pl.BlockSpec((tm, r.shape[1]), lambda i: (i, 0)) for r in rows]
        + [pl.BlockSpec(v.shape, lambda i: (0, 0)) for v in vecs],
        out_specs=[pl.BlockSpec((tm, o.shape[1]), lambda i: (i, 0)) for o in outs],
        out_shape=[jax.ShapeDtypeStruct((n_rows, o.shape[1]), dt) for o, dt in zip(outs, out_dtypes)],
        compiler_params=_params("parallel"), name=name)(*rows, *vecs)
    return res


def _rowwise_bwd(fn, rows, vecs, cots, row_dtypes, need_vecs, tm, name, add_to=None):
    n_rows = rows[0].shape[0]
    nr, nv = len(rows), len(vecs)
    add_to = add_to or {}
    flat_cots = [c for group in cots for c in group]
    add_keys = sorted(add_to)
    row_out = [i for i, dt in enumerate(row_dtypes) if dt is not None]
    n_in = nr + nv + len(flat_cots) + len(add_keys)

    def body(*refs):
        vals = [r[...].astype(f32) for r in refs[:nr]] + [v[...] for v in refs[nr:nr + nv]]
        pos = nr + nv
        cvals = []
        for group in cots:
            acc = refs[pos][...].astype(f32)
            for extra in range(1, len(group)):
                acc = acc + refs[pos + extra][...].astype(f32)
            pos += len(group)
            cvals.append(acc)
        adds = {key: refs[pos + j] for j, key in enumerate(add_keys)}
        _, vjp = jax.vjp(fn, *vals)
        grads = vjp(tuple(cvals))
        out_refs = refs[n_in:]
        for j, i in enumerate(row_out):
            g = grads[i]
            if i in adds:
                g = g + adds[i][...].astype(f32)
            out_refs[j][...] = g.astype(out_refs[j].dtype)
        if need_vecs:
            step = pl.program_id(0)
            for j in range(nv):
                acc_ref = out_refs[len(row_out) + j]

                @pl.when(step == 0)
                def _(acc_ref=acc_ref):
                    acc_ref[...] = jnp.zeros_like(acc_ref)

                acc_ref[...] += grads[nr + j]

    row_spec = lambda a: pl.BlockSpec((tm, a.shape[1]), lambda i: (i, 0))
    in_specs = ([row_spec(r) for r in rows] + [pl.BlockSpec(v.shape, lambda i: (0, 0)) for v in vecs]
                + [row_spec(c) for c in flat_cots] + [row_spec(add_to[key]) for key in add_keys])
    out_specs = [row_spec(rows[i]) for i in row_out]
    out_shape = [jax.ShapeDtypeStruct(rows[i].shape, row_dtypes[i]) for i in row_out]
    if need_vecs:
        out_specs += [pl.BlockSpec(v.shape, lambda i: (0, 0)) for v in vecs]
        out_shape += [jax.ShapeDtypeStruct(v.shape, f32) for v in vecs]
    return pl.pallas_call(
        body, grid=(n_rows // tm,), in_specs=in_specs, out_specs=out_specs, out_shape=out_shape,
        compiler_params=_params("arbitrary"), name=name)(*rows, *vecs, *flat_cots, *[add_to[key] for key in add_keys])


def _f_prenorm(x, g, sc, sh):
    y = x * lax.rsqrt(jnp.mean(x * x, axis=-1, keepdims=True) + RMS_EPS)
    return (y * g * (1.0 + sc) + sh,)


def _f_postres(x, y, g, gt):
    n = y * lax.rsqrt(jnp.mean(y * y, axis=-1, keepdims=True) + RMS_EPS)
    return (x + gt * (n * g),)


def _f_gelu(yc, u, d):
    return (jax.nn.gelu(yc + d * u),)


def _f_glu(g, z, b):
    return (g * jax.nn.sigmoid(z + b),)


def _f_merge(p0, p1, p2, ys, ya, yc, b0, b1, b2):
    return (jax.nn.sigmoid(p0 + b0) * ys + jax.nn.sigmoid(p1 + b1) * ya + jax.nn.sigmoid(p2 + b2) * yc,)


def _f_amerge(o0, o1, o2, l0, l1, l2):
    m = jnp.maximum(jnp.maximum(l0, l1), l2)
    e0, e1, e2 = jnp.exp(l0 - m), jnp.exp(l1 - m), jnp.exp(l2 - m)
    return ((e0 * o0 + e1 * o1 + e2 * o2) / (e0 + e1 + e2),)


def _f_add(a, b):
    return (a + b,)


def _shift_down(z, k, row):
    return jnp.where(row >= k, pltpu.roll(z, k, 0), 0.0)


def _shift_up(z, k, row):
    n = z.shape[0]
    return jnp.where(row < n - k, pltpu.roll(z, n - k, 0), 0.0)


def _conv3(z, w, row):
    return w[0:1, :] * z + w[1:2, :] * _shift_down(z, 1, row) + w[2:3, :] * _shift_down(z, 2, row)


def _conv3_bwd(z, w, dy, row):
    dz = w[0:1, :] * dy + w[1:2, :] * _shift_up(dy, 1, row) + w[2:3, :] * _shift_up(dy, 2, row)
    dw = [jnp.sum(dy * z, axis=0, keepdims=True),
          jnp.sum(dy * _shift_down(z, 1, row), axis=0, keepdims=True),
          jnp.sum(dy * _shift_down(z, 2, row), axis=0, keepdims=True)]
    return dz, dw


def _convgate_fwd(proj, w, off, width, name):
    n_rows = proj.shape[0]
    tc = _tile(width, 256)
    nb = width // tc
    base = off // tc

    def body(b_ref, c_ref, h_ref, w_ref, o_ref):
        row = lax.broadcasted_iota(jnp.int32, (n_rows, tc), 0)
        o_ref[...] = (b_ref[...] * _conv3(c_ref[...] * h_ref[...], w_ref[...], row)).astype(o_ref.dtype)

    col = lambda s: pl.BlockSpec((n_rows, tc), lambda j: (0, base + s * nb + j))
    return pl.pallas_call(
        body, grid=(nb,), in_specs=[col(0), col(1), col(2), pl.BlockSpec((3, tc), lambda j: (0, j))],
        out_specs=pl.BlockSpec((n_rows, tc), lambda j: (0, j)),
        out_shape=jax.ShapeDtypeStruct((n_rows, width), bf16),
        compiler_params=_params("parallel"), name=name)(proj, proj, proj, w)


def _convgate_bwd(proj, w, dcm, off, width, name):
    n_rows = proj.shape[0]
    tc = _tile(width, 128)
    nb = width // tc
    base = off // tc

    def body(b_ref, c_ref, h_ref, w_ref, d_ref, db_ref, dc_ref, dh_ref, dw_ref):
        row = lax.broadcasted_iota(jnp.int32, (n_rows, tc), 0)
        cb, cc, ch, wv, d = b_ref[...], c_ref[...], h_ref[...], w_ref[...], d_ref[...].astype(f32)
        z = cc * ch
        db_ref[...] = (d * _conv3(z, wv, row)).astype(db_ref.dtype)
        dz, dw = _conv3_bwd(z, wv, d * cb, row)
        dc_ref[...] = (dz * ch).astype(dc_ref.dtype)
        dh_ref[...] = (dz * cc).astype(dh_ref.dtype)
        for k in range(3):
            dw_ref[k:k + 1, :] = dw[k]

    col = lambda s: pl.BlockSpec((n_rows, tc), lambda j: (0, base + s * nb + j))
    own = pl.BlockSpec((n_rows, tc), lambda j: (0, j))
    wspec = pl.BlockSpec((3, tc), lambda j: (0, j))
    sec = jax.ShapeDtypeStruct((n_rows, width), bf16)
    return pl.pallas_call(
        body, grid=(nb,), in_specs=[col(0), col(1), col(2), wspec, own], out_specs=[own, own, own, wspec],
        out_shape=[sec, sec, sec, jax.ShapeDtypeStruct((3, width), f32)],
        compiler_params=_params("parallel"), name=name)(proj, proj, proj, w, dcm)


def _ffnconv_fwd(up, w, name):
    n_rows, two_f = up.shape
    half = two_f // 2
    tc = _tile(half, 256)
    nb = half // tc

    def body(a_ref, b_ref, wa_ref, wb_ref, o_ref):
        row = lax.broadcasted_iota(jnp.int32, (n_rows, tc), 0)
        ua = _conv3(a_ref[...], wa_ref[...], row)
        ub = _conv3(b_ref[...], wb_ref[...], row)
        o_ref[...] = (jax.nn.silu(ua) * ub).astype(o_ref.dtype)

    return pl.pallas_call(
        body, grid=(nb,),
        in_specs=[pl.BlockSpec((n_rows, tc), lambda j: (0, j)), pl.BlockSpec((n_rows, tc), lambda j: (0, nb + j)),
                  pl.BlockSpec((3, tc), lambda j: (0, j)), pl.BlockSpec((3, tc), lambda j: (0, nb + j))],
        out_specs=pl.BlockSpec((n_rows, tc), lambda j: (0, j)),
        out_shape=jax.ShapeDtypeStruct((n_rows, half), bf16),
        compiler_params=_params("parallel"), name=name)(up, up, w, w)


def _ffnconv_bwd(up, w, dact, name):
    n_rows, two_f = up.shape
    half = two_f // 2
    tc = _tile(half, 128)
    nb = half // tc

    def body(a_ref, b_ref, wa_ref, wb_ref, d_ref, da_ref, db_ref, dwa_ref, dwb_ref):
        row = lax.broadcasted_iota(jnp.int32, (n_rows, tc), 0)
        a, b, wa, wb, d = a_ref[...], b_ref[...], wa_ref[...], wb_ref[...], d_ref[...].astype(f32)
        ua = _conv3(a, wa, row)
        ub = _conv3(b, wb, row)
        sg = jax.nn.sigmoid(ua)
        d_ua = d * ub * (sg * (1.0 + ua * (1.0 - sg)))
        d_ub = d * (ua * sg)
        da, dwa = _conv3_bwd(a, wa, d_ua, row)
        db, dwb = _conv3_bwd(b, wb, d_ub, row)
        da_ref[...] = da.astype(da_ref.dtype)
        db_ref[...] = db.astype(db_ref.dtype)
        for k in range(3):
            dwa_ref[k:k + 1, :] = dwa[k]
            dwb_ref[k:k + 1, :] = dwb[k]

    lo = pl.BlockSpec((n_rows, tc), lambda j: (0, j))
    hi = pl.BlockSpec((n_rows, tc), lambda j: (0, nb + j))
    wlo = pl.BlockSpec((3, tc), lambda j: (0, j))
    whi = pl.BlockSpec((3, tc), lambda j: (0, nb + j))
    sec = jax.ShapeDtypeStruct((n_rows, half), bf16)
    wsec = jax.ShapeDtypeStruct((3, half), f32)
    da, db, dwa, dwb = pl.pallas_call(
        body, grid=(nb,), in_specs=[lo, hi, wlo, whi, lo], out_specs=[lo, lo, wlo, wlo],
        out_shape=[sec, sec, wsec, wsec], compiler_params=_params("parallel"), name=name)(up, up, w, w, dact)
    return da, db, jnp.concatenate([dwa, dwb], axis=1)


SCAN_TC = 256


def _cmul(ar, ai, br, bi):
    return ar * br - ai * bi, ar * bi + ai * br


def _scan(x, lam, adjoint, name):
    n_rows, two_s = x.shape
    tc = min(SCAN_TC, two_s // 2)
    n_tiles = two_s // (2 * tc)
    n_chunks = n_rows // SUBLANES

    def body(x_ref, lam_ref, hf_ref, hb_ref):
        lr = lam_ref[:, 0:tc]
        li = lam_ref[:, tc:2 * tc]
        if adjoint:
            li = -li
        row = lax.broadcasted_iota(jnp.int32, (SUBLANES, tc), 0)
        powers = [(lr, li)]
        for _ in range(SUBLANES - 1):
            powers.append(_cmul(powers[-1][0], powers[-1][1], lr, li))
        pr = jnp.zeros((SUBLANES, tc), f32)
        pi = jnp.zeros((SUBLANES, tc), f32)
        for t in range(SUBLANES):
            e = (SUBLANES - 1 - t) if adjoint else t
            pr = jnp.where(row == t, powers[e][0], pr)
            pi = jnp.where(row == t, powers[e][1], pi)
        last = 0 if adjoint else SUBLANES - 1

        def chunk(i, carry):
            cr, ci = carry
            c = (n_chunks - 1 - i) if adjoint else i
            rows = pl.ds(pl.multiple_of(c * SUBLANES, SUBLANES), SUBLANES)
            xr = x_ref[rows, 0:tc]
            xi = x_ref[rows, tc:2 * tc]
            for k in (1, 2, 4):
                ar, ai = powers[k - 1]
                if adjoint:
                    sr = jnp.where(row < SUBLANES - k, pltpu.roll(xr, SUBLANES - k, 0), 0.0)
                    si = jnp.where(row < SUBLANES - k, pltpu.roll(xi, SUBLANES - k, 0), 0.0)
                else:
                    sr = jnp.where(row >= k, pltpu.roll(xr, k, 0), 0.0)
                    si = jnp.where(row >= k, pltpu.roll(xi, k, 0), 0.0)
                xr, xi = xr + ar * sr - ai * si, xi + ar * si + ai * sr
            hr = xr + pr * cr - pi * ci
            hi = xi + pr * ci + pi * cr
            hf_ref[rows, 0:tc] = hr
            hf_ref[rows, tc:2 * tc] = hi
            hb_ref[rows, 0:tc] = hr.astype(bf16)
            hb_ref[rows, tc:2 * tc] = hi.astype(bf16)
            return (jnp.sum(jnp.where(row == last, hr, 0.0), axis=0, keepdims=True),
                    jnp.sum(jnp.where(row == last, hi, 0.0), axis=0, keepdims=True))

        zero = jnp.zeros((1, tc), f32)
        lax.fori_loop(0, n_chunks, chunk, (zero, zero))

    blk = pl.BlockSpec((n_rows, 2 * tc), lambda j: (0, j))
    return pl.pallas_call(
        body, grid=(n_tiles,), in_specs=[blk, pl.BlockSpec((1, 2 * tc), lambda j: (0, j))], out_specs=[blk, blk],
        out_shape=[jax.ShapeDtypeStruct(x.shape, f32), jax.ShapeDtypeStruct(x.shape, bf16)],
        compiler_params=_params("parallel"), name=name)(x, lam)


def _lam_grad(g, h_prev, name):
    n_rows, two_s = g.shape
    tc = min(SCAN_TC, two_s // 2)

    def body(g_ref, h_ref, o_ref):
        gr, gi = g_ref[:, 0:tc], g_ref[:, tc:2 * tc]
        hr, hi = h_ref[:, 0:tc], h_ref[:, tc:2 * tc]
        o_ref[:, 0:tc] = jnp.sum(gr * hr + gi * hi, axis=0, keepdims=True)
        o_ref[:, tc:2 * tc] = jnp.sum(gi * hr - gr * hi, axis=0, keepdims=True)

    blk = pl.BlockSpec((n_rows, 2 * tc), lambda j: (0, j))
    return pl.pallas_call(
        body, grid=(two_s // (2 * tc),), in_specs=[blk, blk], out_specs=pl.BlockSpec((1, 2 * tc), lambda j: (0, j)),
        out_shape=jax.ShapeDtypeStruct((1, two_s), f32), compiler_params=_params("parallel"), name=name)(g, h_prev)


def _interleave(re, im, tc):
    lead = re.shape[:-1]
    s = re.shape[-1]
    return jnp.concatenate([re.reshape(*lead, s // tc, 1, tc), im.reshape(*lead, s // tc, 1, tc)], axis=-2).reshape(*lead, 2 * s)


def _deinterleave(z, tc):
    lead = z.shape[:-1]
    s = z.shape[-1] // 2
    z = z.reshape(*lead, s // tc, 2, tc)
    return z[..., 0, :].reshape(*lead, s), z[..., 1, :].reshape(*lead, s)


def _f_disc(log_dt, ar, ai, br, bi):
    dt = jnp.exp(log_dt)
    mag = jnp.exp(ar * dt)
    lr, li = mag * jnp.cos(ai * dt), mag * jnp.sin(ai * dt)
    den = ar * ar + ai * ai
    fr = ((lr - 1.0) * ar + li * ai) / den
    fi = (li * ar - (lr - 1.0) * ai) / den
    return lr, li, fr[None] * br - fi[None] * bi, fr[None] * bi + fi[None] * br


def _disc_fwd(log_dt, ar, ai, br, bi, name):
    def body(*refs):
        for o_ref, val in zip(refs[5:], _f_disc(*[r[...] for r in refs[:5]])):
            o_ref[...] = val

    sd = jax.ShapeDtypeStruct
    return pl.pallas_call(body, out_shape=[sd(ar.shape, f32), sd(ar.shape, f32), sd(br.shape, f32), sd(br.shape, f32)],
                          name=name)(log_dt, ar, ai, br, bi)


def _disc_bwd(log_dt, ar, ai, br, bi, cots, name):
    def body(*refs):
        _, vjp = jax.vjp(_f_disc, *[r[...] for r in refs[:5]])
        for o_ref, val in zip(refs[9:], vjp(tuple(r[...] for r in refs[5:9]))):
            o_ref[...] = val

    sd = jax.ShapeDtypeStruct
    return pl.pallas_call(
        body, out_shape=[sd(log_dt.shape, f32), sd(ar.shape, f32), sd(ar.shape, f32), sd(br.shape, f32), sd(br.shape, f32)],
        name=name)(log_dt, ar, ai, br, bi, *cots)


def _alibi_bias(n_heads_per_pattern):
    n_heads = len(DSWA_PATTERNS) * n_heads_per_pattern
    slopes = np.array([2.0 ** (-8.0 * (h + 1) / n_heads) for h in range(n_heads)], dtype=np.float32)
    qi = np.arange(QBLK)[:, None]
    kj = np.arange(QBLK)[None, :]
    cur, prev = [], []
    for h in range(n_heads):
        dil = DSWA_PATTERNS[h // n_heads_per_pattern][1]
        d_cur = qi - kj
        d_prev = QBLK + qi - kj
        cur.append(np.where(d_cur >= 0, -slopes[h] * (d_cur * dil).astype(np.float32), NEG_INF))
        prev.append(np.where(d_prev <= QBLK, -slopes[h] * (d_prev * dil).astype(np.float32), NEG_INF))
    return np.stack(cur).astype(np.float32), np.stack(prev).astype(np.float32)


def _blocks_per_residue(head, n_heads_per_pattern, n_blocks):
    pattern = head // n_heads_per_pattern
    out = n_blocks // DSWA_PATTERNS[-1][1]
    for p in range(len(DSWA_PATTERNS) - 2, -1, -1):
        out = jnp.where(pattern == p, n_blocks // DSWA_PATTERNS[p][1], out)
    return out


def _nt(a, b):
    return lax.dot_general(a, b, (((1,), (1,)), ((), ())), preferred_element_type=f32)


def _tn(a, b):
    return lax.dot_general(a, b, (((0,), (0,)), ((), ())), preferred_element_type=f32)


def _attn_fwd(q, k, v, hpp, name):
    n_heads, n_blocks = q.shape[0], q.shape[1]
    bias_cur, bias_prev = _alibi_bias(hpp)
    scale = HEAD_DIM ** -0.5

    def body(q_ref, kc_ref, kp_ref, vc_ref, vp_ref, bc_ref, bp_ref, o_ref, lse_ref):
        head, blk = pl.program_id(0), pl.program_id(1)
        has_prev = (blk % _blocks_per_residue(head, hpp, n_blocks)) > 0
        qv = q_ref[...]
        s_c = _nt(qv, kc_ref[...]) * scale + bc_ref[...]
        s_p = jnp.where(has_prev, _nt(qv, kp_ref[...]) * scale + bp_ref[...], NEG_INF)
        m = jnp.maximum(jnp.max(s_c, axis=-1, keepdims=True), jnp.max(s_p, axis=-1, keepdims=True))
        p_c = jnp.exp(s_c - m)
        p_p = jnp.exp(s_p - m)
        l = jnp.sum(p_c, axis=-1, keepdims=True) + jnp.sum(p_p, axis=-1, keepdims=True)
        acc = (jnp.dot(p_c.astype(bf16), vc_ref[...], preferred_element_type=f32)
               + jnp.dot(p_p.astype(bf16), vp_ref[...], preferred_element_type=f32))
        o_ref[...] = acc / l
        lse_ref[...] = m + jnp.log(l)

    cur = pl.BlockSpec((None, None, QBLK, HEAD_DIM), lambda a, b: (a, b, 0, 0))
    prev = pl.BlockSpec((None, None, QBLK, HEAD_DIM), lambda a, b: (a, jnp.maximum(b - 1, 0), 0, 0))
    bias = pl.BlockSpec((None, QBLK, QBLK), lambda a, b: (a, 0, 0))
    return pl.pallas_call(
        body, grid=(n_heads, n_blocks), in_specs=[cur, cur, prev, cur, prev, bias, bias],
        out_specs=[cur, pl.BlockSpec((None, None, QBLK, 1), lambda a, b: (a, b, 0, 0))],
        out_shape=[jax.ShapeDtypeStruct(q.shape, f32), jax.ShapeDtypeStruct(q.shape[:3] + (1,), f32)],
        compiler_params=_params("parallel", "parallel"), name=name)(q, k, k, v, v, jnp.asarray(bias_cur), jnp.asarray(bias_prev))


def _attn_bwd(q, k, v, o, lse, do, dlse_x, hpp, name):
    n_heads, n_blocks = q.shape[0], q.shape[1]
    bias_cur, bias_prev = _alibi_bias(hpp)
    scale = HEAD_DIM ** -0.5

    def body(q_ref, qn_ref, kc_ref, kp_ref, vc_ref, vp_ref, o_ref, on_ref, l_ref, ln_ref, do_ref, don_ref, dl_ref, dln_ref,
             bc_ref, bp_ref, dq_ref, dk_ref, dv_ref):
        head, blk = pl.program_id(0), pl.program_id(1)
        per = _blocks_per_residue(head, hpp, n_blocks)
        has_prev = (blk % per) > 0
        has_next = ((blk + 1) % per) > 0
        qv, kc, vc = q_ref[...], kc_ref[...], vc_ref[...]
        dov = do_ref[...]
        do_b = dov.astype(bf16)
        corr = jnp.sum(dl_ref[...], axis=-1, keepdims=True) - jnp.sum(dov * o_ref[...], axis=-1, keepdims=True)
        p_cc = jnp.exp(_nt(qv, kc) * scale + bc_ref[...] - l_ref[...])
        ds_cc = p_cc * (_nt(do_b, vc) + corr)
        p_cp = jnp.where(has_prev, jnp.exp(_nt(qv, kp_ref[...]) * scale + bp_ref[...] - l_ref[...]), 0.0)
        ds_cp = p_cp * (_nt(do_b, vp_ref[...]) + corr)
        dq = (jnp.dot(ds_cc.astype(bf16), kc, preferred_element_type=f32)
              + jnp.dot(ds_cp.astype(bf16), kp_ref[...], preferred_element_type=f32))
        dq_ref[...] = (dq * scale).astype(dq_ref.dtype)
        qn = qn_ref[...]
        don = don_ref[...]
        don_b = don.astype(bf16)
        corr_n = jnp.sum(dln_ref[...], axis=-1, keepdims=True) - jnp.sum(don * on_ref[...], axis=-1, keepdims=True)
        p_nc = jnp.where(has_next, jnp.exp(_nt(qn, kc) * scale + bp_ref[...] - ln_ref[...]), 0.0)
        ds_nc = p_nc * (_nt(don_b, vc) + corr_n)
        dk = _tn(ds_cc.astype(bf16), qv) + _tn(ds_nc.astype(bf16), qn)
        dk_ref[...] = (dk * scale).astype(dk_ref.dtype)
        dv = _tn(p_cc.astype(bf16), do_b) + _tn(p_nc.astype(bf16), don_b)
        dv_ref[...] = dv.astype(dv_ref.dtype)

    def spec(shift, last):
        def index(a, b):
            return (a, jnp.clip(b + shift, 0, n_blocks - 1), 0, 0)
        return pl.BlockSpec((None, None, QBLK, last), index)

    cur, prev, nxt = spec(0, HEAD_DIM), spec(-1, HEAD_DIM), spec(1, HEAD_DIM)
    cur1, nxt1 = spec(0, 1), spec(1, 1)
    bias = pl.BlockSpec((None, QBLK, QBLK), lambda a, b: (a, 0, 0))
    out = jax.ShapeDtypeStruct(q.shape, bf16)
    return pl.pallas_call(
        body, grid=(n_heads, n_blocks),
        in_specs=[cur, nxt, cur, prev, cur, prev, cur, nxt, cur1, nxt1, cur, nxt, cur, nxt, bias, bias],
        out_specs=[cur, cur, cur], out_shape=[out, out, out],
        compiler_params=_params("parallel", "parallel"), name=name)(
            q, q, k, k, v, v, o, o, lse, lse, do, do, dlse_x, dlse_x, jnp.asarray(bias_cur), jnp.asarray(bias_prev))


def _to_blocks(t, hpp):
    n_rows = t.shape[0]
    t = t.reshape(n_rows, len(DSWA_PATTERNS), hpp, HEAD_DIM)
    out = []
    for p, (_, dil) in enumerate(DSWA_PATTERNS):
        tp = t[:, p].reshape(n_rows // dil, dil, hpp, HEAD_DIM).transpose(2, 1, 0, 3)
        out.append(tp.reshape(hpp, n_rows // QBLK, QBLK, HEAD_DIM))
    return jnp.concatenate(out, axis=0)


def _from_blocks(t, hpp):
    n_blocks, last = t.shape[1], t.shape[3]
    n_rows = n_blocks * QBLK
    out = []
    for p, (_, dil) in enumerate(DSWA_PATTERNS):
        tp = t[p * hpp:(p + 1) * hpp].reshape(hpp, dil, n_rows // dil, last).transpose(2, 1, 0, 3)
        out.append(tp.reshape(n_rows, hpp * last))
    return out


def _loss_head(y, target, name):
    n_rows, d = y.shape
    tm = _tile(n_rows, 256, SUBLANES)

    def body(y_ref, t_ref, dy_ref, loss_ref):
        diff = y_ref[...] - t_ref[...]
        dy_ref[...] = diff * (1.0 / d)

        @pl.when(pl.program_id(0) == 0)
        def _():
            loss_ref[...] = jnp.zeros_like(loss_ref)

        loss_ref[...] += jnp.sum(jnp.sum(diff * diff, axis=1, keepdims=True), axis=0, keepdims=True) * (0.5 / d)

    row = pl.BlockSpec((tm, d), lambda i: (i, 0))
    return pl.pallas_call(
        body, grid=(n_rows // tm,), in_specs=[row, row], out_specs=[row, pl.BlockSpec((1, 1), lambda i: (0, 0))],
        out_shape=[jax.ShapeDtypeStruct(y.shape, f32), jax.ShapeDtypeStruct((1, 1), f32)],
        compiler_params=_params("arbitrary"), name=name)(y, target)


def _as2d(a):
    if a.ndim >= 2 and a.shape[-1] >= LANES:
        return a.reshape(-1, a.shape[-1])
    return a.reshape(-1, LANES) if a.size % LANES == 0 else a.reshape(1, -1)


ELEMENTWISE_BLOCK_BYTES = 1 << 20


def _row_tile(n_rows, n_cols, n_slabs=1):
    return _tile(n_rows, max(SUBLANES, ELEMENTWISE_BLOCK_BYTES // (4 * n_cols * n_slabs)), SUBLANES)


def _adamw(w, g, m, v, name):
    shape = w.shape
    w2, g2, m2, v2 = _as2d(w), _as2d(g), _as2d(m), _as2d(v)
    n_rows, n_cols = w2.shape
    tm = _row_tile(n_rows, n_cols)

    def body(w_ref, g_ref, m_ref, v_ref, d_ref, mo_ref, vo_ref):
        gv = g_ref[...]
        mn = ADAM_B1 * m_ref[...] + (1.0 - ADAM_B1) * gv
        vn = ADAM_B2 * v_ref[...] + (1.0 - ADAM_B2) * jnp.square(gv)
        m_hat = mn / (1.0 - ADAM_B1 ** ADAM_STEP)
        v_hat = vn / (1.0 - ADAM_B2 ** ADAM_STEP)
        d_ref[...] = -ADAM_LR * (m_hat / (jnp.sqrt(v_hat) + ADAM_EPS) + ADAM_WD * w_ref[...])
        mo_ref[...] = mn
        vo_ref[...] = vn

    row = pl.BlockSpec((tm, n_cols), lambda i: (i, 0))
    out = jax.ShapeDtypeStruct(w2.shape, f32)
    d, mn, vn = pl.pallas_call(body, grid=(n_rows // tm,), in_specs=[row] * 4, out_specs=[row] * 3, out_shape=[out] * 3,
                               compiler_params=_params("parallel"), name=name)(w2, g2, m2, v2)
    return d.reshape(shape), mn.reshape(shape), vn.reshape(shape)


def _sum_slabs(x, name):
    n = x.shape[0]
    shape = x.shape[1:]
    x3 = x.reshape(n, -1, shape[-1])
    n_rows, n_cols = x3.shape[1:]
    tm = _row_tile(n_rows, n_cols, n)

    def body(x_ref, o_ref):
        acc = x_ref[0].astype(f32)
        for s in range(1, n):
            acc = acc + x_ref[s].astype(f32)
        o_ref[...] = acc

    out = pl.pallas_call(
        body, grid=(n_rows // tm,), in_specs=[pl.BlockSpec((n, tm, n_cols), lambda i: (0, i, 0))],
        out_specs=pl.BlockSpec((tm, n_cols), lambda i: (i, 0)), out_shape=jax.ShapeDtypeStruct((n_rows, n_cols), f32),
        compiler_params=_params("parallel"), name=name)(x3)
    return out.reshape(shape)


def _mod_fwd(c_all, w_mod, b_cols, name):
    depth, d, cols = w_mod.shape
    tn = _tile(cols, 512)

    def body(c_ref, w_ref, b_ref, o_ref):
        cond = jax.nn.silu(c_ref[...])
        o_ref[...] = jnp.dot(cond, w_ref[...], preferred_element_type=f32, precision=lax.Precision.HIGHEST) + b_ref[...]

    return pl.pallas_call(
        body, grid=(depth, cols // tn),
        in_specs=[pl.BlockSpec((N_DEV, d), lambda l, j: (0, 0)), pl.BlockSpec((None, d, tn), lambda l, j: (l, 0, j)),
                  pl.BlockSpec((None, 1, tn), lambda l, j: (l, 0, j))],
        out_specs=pl.BlockSpec((None, N_DEV, tn), lambda l, j: (l, 0, j)),
        out_shape=jax.ShapeDtypeStruct((depth, N_DEV, cols), f32),
        compiler_params=_params("parallel", "parallel"), name=name)(c_all, w_mod, b_cols)


def _mod_wgrad(c_all_t, dmod_cols, name):
    d = c_all_t.shape[0]
    depth, _, cols = dmod_cols.shape
    tm = _tile(d, 256, SUBLANES)

    def body(c_ref, g_ref, o_ref):
        cond = jax.nn.silu(c_ref[...])
        o_ref[...] = jnp.dot(cond, g_ref[...], preferred_element_type=f32, precision=lax.Precision.HIGHEST)

    return pl.pallas_call(
        body, grid=(depth, d // tm),
        in_specs=[pl.BlockSpec((tm, N_DEV), lambda l, i: (i, 0)), pl.BlockSpec((None, N_DEV, cols), lambda l, i: (l, 0, 0))],
        out_specs=pl.BlockSpec((None, tm, cols), lambda l, i: (l, i, 0)),
        out_shape=jax.ShapeDtypeStruct((depth, d, cols), f32),
        compiler_params=_params("parallel", "parallel"), name=name)(c_all_t, dmod_cols)


def _position():
    return lax.axis_index("x"), lax.axis_index("y"), lax.axis_index("c")


def _linear(p):
    return 4 * p[0] + 2 * p[1] + p[2]


def _all_gather(xs, name):
    n = len(xs)

    def body(*refs):
        x_refs, o_refs = refs[:n], refs[n:2 * n]
        send_sems, recv_sems, local_sems = refs[2 * n:]
        x, y, c = _position()
        me, sibling = (x, y, c), (x, y, 1 - c)
        chips = [(1 - x, y), (x, 1 - y), (1 - x, 1 - y)]

        def copy(a, k, block, to, src=None):
            slab = o_refs[a].at[_linear(block)]
            return pltpu.make_async_remote_copy(
                src_ref=slab if src is None else src, dst_ref=slab, send_sem=send_sems.at[a, k], recv_sem=recv_sems.at[a, k],
                device_id=to, device_id_type=MESH)

        mine = [pltpu.make_async_copy(x_refs[a], o_refs[a].at[_linear(me)], local_sems.at[a]) for a in range(n)]
        first = []
        for a in range(n):
            mine[a].start()
            first.append(copy(a, 0, me, sibling, src=x_refs[a]))
            first += [copy(a, 1 + j, me, (*chip, c), src=x_refs[a]) for j, chip in enumerate(chips)]
        for cp in first:
            cp.start()
        passed = []
        for j, chip in enumerate(chips):
            for a in range(n):
                copy(a, 1 + j, (*chip, c), me).wait_recv()
                cp = copy(a, 4 + j, (*chip, c), sibling)
                cp.start()
                passed.append(cp)
        for a in range(n):
            copy(a, 0, sibling, me).wait_recv()
            for j, chip in enumerate(chips):
                copy(a, 4 + j, (*chip, 1 - c), me).wait_recv()
        for cp in first + passed:
            cp.wait_send()
        for cp in mine:
            cp.wait()

    return pl.pallas_call(
        body, in_specs=[ANY] * n, out_specs=[ANY] * n,
        out_shape=[jax.ShapeDtypeStruct((N_DEV,) + x.shape, x.dtype) for x in xs],
        scratch_shapes=[pltpu.SemaphoreType.DMA((n, 7)), pltpu.SemaphoreType.DMA((n, 7)), pltpu.SemaphoreType.DMA((n,))],
        name=name)(*xs)


def _exchange(xs, name):
    n = len(xs)

    def body(*refs):
        x_refs, o_refs = refs[:n], refs[n:2 * n]
        send_sems, recv_sems, local_sems = refs[2 * n:]
        x, y, c = _position()
        me = (x, y, c)
        peers = []
        for mask in range(1, N_DEV):
            peers.append((1 - x if mask & 4 else x, 1 - y if mask & 2 else y, 1 - c if mask & 1 else c))
        mine = [pltpu.make_async_copy(x_refs[a].at[_linear(me)], o_refs[a].at[_linear(me)], local_sems.at[a]) for a in range(n)]
        sends = []
        for a in range(n):
            mine[a].start()
            for k, peer in enumerate(peers):
                sends.append(pltpu.make_async_remote_copy(
                    src_ref=x_refs[a].at[_linear(peer)], dst_ref=o_refs[a].at[_linear(me)],
                    send_sem=send_sems.at[a, k], recv_sem=recv_sems.at[a, k], device_id=peer, device_id_type=MESH))
        for cp in sends:
            cp.start()
        for a in range(n):
            for k, peer in enumerate(peers):
                pltpu.make_async_remote_copy(
                    src_ref=x_refs[a].at[_linear(peer)], dst_ref=o_refs[a].at[_linear(peer)],
                    send_sem=send_sems.at[a, k], recv_sem=recv_sems.at[a, k], device_id=peer, device_id_type=MESH).wait_recv()
        for cp in sends:
            cp.wait_send()
        for cp in mine:
            cp.wait()

    return pl.pallas_call(
        body, in_specs=[ANY] * n, out_specs=[ANY] * n,
        out_shape=[jax.ShapeDtypeStruct(x.shape, x.dtype) for x in xs],
        scratch_shapes=[pltpu.SemaphoreType.DMA((n, 7)), pltpu.SemaphoreType.DMA((n, 7)), pltpu.SemaphoreType.DMA((n,))],
        name=name)(*xs)


def _reduce_scatter(xs, name):
    got = _exchange(xs, name)
    return [_sum_slabs(g, f"{name}_sum{i}") for i, g in enumerate(got)]


def _block_diag_in(bb, n_groups):
    eye = jnp.eye(n_groups, dtype=bb.dtype)
    m = eye[:, None, :, None] * bb.transpose(1, 0, 2)[:, :, None, :]
    return m.reshape(n_groups * SSM_GROUP, n_groups * SSM_STATE)


def _block_diag_in_t(m, n_groups):
    m = m.reshape(n_groups, SSM_GROUP, n_groups, SSM_STATE)
    return jnp.stack([m[g, :, g, :] for g in range(n_groups)], axis=1)


def _block_diag_out(cc, n_groups):
    eye = jnp.eye(n_groups, dtype=cc.dtype)
    m = eye[:, None, :, None] * cc.transpose(0, 2, 1)[:, :, None, :]
    return m.reshape(n_groups * SSM_STATE, n_groups * SSM_GROUP)


def _block_diag_out_t(m, n_groups):
    m = m.reshape(n_groups, SSM_STATE, n_groups, SSM_GROUP)
    return jnp.stack([m[g, :, g, :].T for g in range(n_groups)], axis=0)


def _rows_interleave(re, im, tc):
    return _interleave(re.T, im.T, tc).T


def _rows_deinterleave(z, tc):
    re, im = _deinterleave(z.T, tc)
    return re.T, im.T


def _ssm_setup(p, tag):
    n_groups = p["ssm_a_re"].shape[0]
    s = n_groups * SSM_STATE
    tc = min(SCAN_TC, s)
    log_dt = p["ssm_log_dt"].reshape(n_groups, 1)
    br_t = p["ssm_b_re"].transpose(2, 0, 1)
    bi_t = p["ssm_b_im"].transpose(2, 0, 1)
    disc_in = (log_dt, p["ssm_a_re"], p["ssm_a_im"], br_t, bi_t)
    lr, li, bbr, bbi = _disc_fwd(*disc_in, name=f"ssm_disc{tag}")
    lam = _interleave(lr.reshape(1, s), li.reshape(1, s), tc)
    bb = _interleave(_block_diag_in(bbr, n_groups), _block_diag_in(bbi, n_groups), tc).astype(bf16)
    cc = _rows_interleave(_block_diag_out(p["ssm_c_re"], n_groups), -_block_diag_out(p["ssm_c_im"], n_groups), tc).astype(bf16)
    return dict(disc_in=disc_in, lam=lam, bb=bb, bb_t=bb.T, cc=cc, cc_t=cc.T, tc=tc, n_groups=n_groups)


def _layer_fwd(x, mod, w, p, tag):
    n_rows, d = x.shape
    w4 = d // 4
    hpp = w4 // HEAD_DIM
    off_q, off_k, off_v, off_conv, off_gate = w4, 4 * w4, 7 * w4, 10 * w4, 13 * w4
    sh1, sc1, gt1, sh2, sc2, gt2 = mod
    vec = lambda a: a.reshape(1, -1)
    s = dict(x=x)
    (s["h"],) = _rowwise(_f_prenorm, [x], [vec(p["g_pre_mix"]), sc1, sh1], [bf16], 256, f"prenorm_mix{tag}")
    proj = s["proj"] = _mm(s["h"], w["w_in"], f32, f"in_proj{tag}")
    ssm = s["ssm"] = _ssm_setup(p, tag)
    s["u32"] = proj[:, :off_q]
    s["u"] = s["u32"].astype(bf16)
    xin = _mm(s["u"], ssm["bb"], f32, f"ssm_in{tag}")
    s["hf"], s["hb"] = _scan(xin, ssm["lam"], False, f"ssm_scan{tag}")
    s["yc"] = _mm(s["hb"], ssm["cc"], f32, f"ssm_out{tag}")
    (s["g"],) = _rowwise(_f_gelu, [s["yc"], s["u32"]], [vec(p["ssm_d"])], [bf16], 256, f"ssm_gelu{tag}")
    s["z"] = _mm(s["g"], w["w_glu"], f32, f"glu_proj{tag}")
    (s["s_ssm"],) = _rowwise(_f_glu, [s["g"], s["z"]], [vec(p["b_glu"])], [bf16], 256, f"glu{tag}")
    s["y_ssm"] = _mm(s["s_ssm"], w["w_ssm_out"], f32, f"ssm_proj{tag}")
    s["qb"] = _to_blocks(proj[:, off_q:off_k].astype(bf16), hpp)
    s["kb"] = _to_blocks(proj[:, off_k:off_v].astype(bf16), hpp)
    s["vb"] = _to_blocks(proj[:, off_v:off_conv].astype(bf16), hpp)
    s["ob"], s["lseb"] = _attn_fwd(s["qb"], s["kb"], s["vb"], hpp, f"attn{tag}")
    s["o_tok"] = _from_blocks(s["ob"], hpp)
    s["lse_x"] = [jnp.repeat(l, HEAD_DIM, axis=1) for l in _from_blocks(s["lseb"], hpp)]
    (s["attn_o"],) = _rowwise(_f_amerge, s["o_tok"] + s["lse_x"], [], [bf16], 256, f"attn_merge{tag}")
    s["y_attn"] = _mm(s["attn_o"], w["w_attn_out"], f32, f"attn_proj{tag}")
    s["cm"] = _convgate_fwd(proj, p["conv_mix_w"], off_conv, w4, f"convgate{tag}")
    s["y_conv"] = _mm(s["cm"], w["w_conv_out"], f32, f"conv_proj{tag}")
    s["pg"] = [proj[:, off_gate + i * d:off_gate + (i + 1) * d] for i in range(3)]
    s["bg"] = [p["b_gate"][i * d:(i + 1) * d].reshape(1, d) for i in range(3)]
    (s["merged"],) = _rowwise(_f_merge, s["pg"] + [s["y_ssm"], s["y_attn"], s["y_conv"]], s["bg"], [bf16], 128, f"merge{tag}")
    s["y"] = _mm(s["merged"], w["w_o"], f32, f"o_proj{tag}")
    (s["x1"],) = _rowwise(_f_postres, [x, s["y"]], [vec(p["g_post_mix"]), gt1], [f32], 256, f"postres_mix{tag}")
    (s["h2"],) = _rowwise(_f_prenorm, [s["x1"]], [vec(p["g_pre_ffn"]), sc2, sh2], [bf16], 256, f"prenorm_ffn{tag}")
    s["up"] = _mm(s["h2"], w["w_up"], f32, f"up_proj{tag}")
    s["act"] = _ffnconv_fwd(s["up"], p["ffn_conv_w"], f"ffnconv{tag}")
    s["y2"] = _mm(s["act"], w["w_down"], f32, f"down_proj{tag}")
    (x2,) = _rowwise(_f_postres, [s["x1"], s["y2"]], [vec(p["g_post_ffn"]), gt2], [f32], 256, f"postres_ffn{tag}")
    return x2, s


def _layer_bwd(dx2, s, mod, w, wt, p, tag):
    x = s["x"]
    n_rows, d = x.shape
    w4 = d // 4
    hpp = w4 // HEAD_DIM
    off_conv = 10 * w4
    sh1, sc1, gt1, sh2, sc2, gt2 = mod
    vec = lambda a: a.reshape(1, -1)
    gw, gp = {}, {}
    dy2, d_gpf, d_gt2 = _rowwise_bwd(_f_postres, [s["x1"], s["y2"]], [vec(p["g_post_ffn"]), gt2], [[dx2]], [None, bf16], True,
                                     256, f"postres_ffn_bwd{tag}")
    gp["g_post_ffn"] = d_gpf
    dact = _mm(dy2, wt["w_down"], bf16, f"down_dx{tag}")
    gw["w_down"] = _mm(s["act"].T, dy2, bf16, f"down_dw{tag}")
    da, db, gp["ffn_conv_w"] = _ffnconv_bwd(s["up"], p["ffn_conv_w"], dact, f"ffnconv_bwd{tag}")
    dup = jnp.concatenate([da, db], axis=1)
    dh2 = _mm(dup, wt["w_up"], f32, f"up_dx{tag}")
    gw["w_up"] = _mm(s["h2"].T, dup, bf16, f"up_dw{tag}")
    dx1, d_g, d_sc2, d_sh2 = _rowwise_bwd(_f_prenorm, [s["x1"]], [vec(p["g_pre_ffn"]), sc2, sh2], [[dh2]], [f32], True,
                                          256, f"prenorm_ffn_bwd{tag}", add_to={0: dx2})
    gp["g_pre_ffn"] = d_g
    dy, d_gpm, d_gt1 = _rowwise_bwd(_f_postres, [x, s["y"]], [vec(p["g_post_mix"]), gt1], [[dx1]], [None, bf16], True,
                                    256, f"postres_mix_bwd{tag}")
    gp["g_post_mix"] = d_gpm
    dmerged = _mm(dy, wt["w_o"], f32, f"o_dx{tag}")
    gw["w_o"] = _mm(s["merged"].T, dy, bf16, f"o_dw{tag}")
    res = _rowwise_bwd(_f_merge, s["pg"] + [s["y_ssm"], s["y_attn"], s["y_conv"]], s["bg"], [[dmerged]], [bf16] * 6, True,
                       128, f"merge_bwd{tag}")
    dpg, (dys, dya, dyv), dbg = list(res[0:3]), res[3:6], res[6:9]
    gp["b_gate"] = jnp.concatenate(dbg, axis=1)
    ds_ssm = _mm(dys, wt["w_ssm_out"], f32, f"ssm_proj_dx{tag}")
    gw["w_ssm_out"] = _mm(s["s_ssm"].T, dys, bf16, f"ssm_proj_dw{tag}")
    dattn_o = _mm(dya, wt["w_attn_out"], f32, f"attn_proj_dx{tag}")
    gw["w_attn_out"] = _mm(s["attn_o"].T, dya, bf16, f"attn_proj_dw{tag}")
    dcm = _mm(dyv, wt["w_conv_out"], f32, f"conv_proj_dx{tag}")
    gw["w_conv_out"] = _mm(s["cm"].T, dyv, bf16, f"conv_proj_dw{tag}")
    ssm = s["ssm"]
    tc, n_groups = ssm["tc"], ssm["n_groups"]
    dg_a, dz, gp["b_glu"] = _rowwise_bwd(_f_glu, [s["g"], s["z"]], [vec(p["b_glu"])], [[ds_ssm]], [f32, bf16], True,
                                         256, f"glu_bwd{tag}")
    dg_b = _mm(dz, wt["w_glu"], f32, f"glu_proj_dx{tag}")
    gw["w_glu"] = _mm(s["g"].T, dz, bf16, f"glu_proj_dw{tag}")
    dyc, du_skip, gp["ssm_d"] = _rowwise_bwd(_f_gelu, [s["yc"], s["u32"]], [vec(p["ssm_d"])], [[dg_a, dg_b]], [bf16, f32], True,
                                             256, f"ssm_gelu_bwd{tag}")
    dh_state = _mm(dyc, ssm["cc_t"], f32, f"ssm_out_dx{tag}")
    g_cc = _mm(s["hb"].T, dyc, f32, f"ssm_out_dw{tag}")
    gf, gb = _scan(dh_state, ssm["lam"], True, f"ssm_scan_bwd{tag}")
    h_prev = jnp.concatenate([jnp.zeros((1, s["hf"].shape[1]), f32), s["hf"][:-1]], axis=0)
    d_lam = _lam_grad(gf, h_prev, f"ssm_lam_grad{tag}")
    du_x = _mm(gb, ssm["bb_t"], f32, f"ssm_in_dx{tag}")
    g_bb = _mm(s["u"].T, gb, f32, f"ssm_in_dw{tag}")
    (du,) = _rowwise(_f_add, [du_skip, du_x], [], [bf16], 256, f"ssm_du{tag}")
    d_lr, d_li = _deinterleave(d_lam, tc)
    g_bbr, g_bbi = _deinterleave(g_bb, tc)
    cots = (d_lr.reshape(n_groups, SSM_STATE), d_li.reshape(n_groups, SSM_STATE),
            _block_diag_in_t(g_bbr, n_groups), _block_diag_in_t(g_bbi, n_groups))
    d_ldt, d_ar, d_ai, d_br, d_bi = _disc_bwd(*ssm["disc_in"], cots, f"ssm_disc_bwd{tag}")
    gp["ssm_log_dt"], gp["ssm_a_re"], gp["ssm_a_im"] = d_ldt.reshape(-1), d_ar, d_ai
    gp["ssm_b_re"], gp["ssm_b_im"] = d_br.transpose(1, 2, 0), d_bi.transpose(1, 2, 0)
    g_ccr, g_cci = _rows_deinterleave(g_cc, tc)
    gp["ssm_c_re"], gp["ssm_c_im"] = _block_diag_out_t(g_ccr, n_groups), -_block_diag_out_t(g_cci, n_groups)
    res = _rowwise_bwd(_f_amerge, s["o_tok"] + s["lse_x"], [], [[dattn_o]], [f32] * 6, False, 256, f"attn_merge_bwd{tag}")
    do_b = _to_blocks(jnp.concatenate(res[0:3], axis=1), hpp)
    dl_b = _to_blocks(jnp.concatenate(res[3:6], axis=1), hpp)
    dqb, dkb, dvb = _attn_bwd(s["qb"], s["kb"], s["vb"], s["ob"], s["lseb"], do_b, dl_b, hpp, f"attn_bwd{tag}")
    dq, dk, dv = (jnp.concatenate(_from_blocks(t, hpp), axis=1) for t in (dqb, dkb, dvb))
    dcb, dcc, dch, gp["conv_mix_w"] = _convgate_bwd(s["proj"], p["conv_mix_w"], dcm, off_conv, w4, f"convgate_bwd{tag}")
    dproj = jnp.concatenate([du, dq, dk, dv, dcb, dcc, dch] + dpg, axis=1)
    dh = _mm(dproj, wt["w_in"], f32, f"in_dx{tag}")
    gw["w_in"] = _mm(s["h"].T, dproj, bf16, f"in_dw{tag}")
    dx0, d_g, d_sc1, d_sh1 = _rowwise_bwd(_f_prenorm, [x], [vec(p["g_pre_mix"]), sc1, sh1], [[dh]], [f32], True,
                                          256, f"prenorm_mix_bwd{tag}", add_to={0: dx1})
    gp["g_pre_mix"] = d_g
    return dx0, [d_sh1, d_sc1, d_gt1, d_sh2, d_sc2, d_gt2], gw, gp


COL_SHARDED = ("w_in", "w_up", "w_ssm_out", "w_attn_out", "w_conv_out")
ROW_SHARDED = ("w_down", "w_o", "w_glu")
SMALL_SHARDED = ("conv_mix_w", "ffn_conv_w")
REPLICATED = ("b_mod", "g_pre_mix", "g_post_mix", "g_pre_ffn", "g_post_ffn", "ssm_log_dt", "ssm_a_re", "ssm_a_im", "ssm_b_re",
              "ssm_b_im", "ssm_c_re", "ssm_c_im", "ssm_d", "b_glu", "b_gate")
WEIGHTS = ('w_mod', 'b_mod', 'g_pre_mix', 'g_post_mix', 'g_pre_ffn', 'g_post_ffn', 'w_in', 'ssm_log_dt', 'ssm_a_re', 'ssm_a_im',
           'ssm_b_re', 'ssm_b_im', 'ssm_c_re', 'ssm_c_im', 'ssm_d', 'w_glu', 'b_glu', 'conv_mix_w', 'w_ssm_out', 'w_attn_out',
           'w_conv_out', 'b_gate', 'w_o', 'w_up', 'ffn_conv_w', 'w_down')


def _full_cols(g):
    return g.transpose(1, 0, 2).reshape(g.shape[1], -1)


def _col_slabs(m):
    return m.reshape(m.shape[0], N_DEV, -1).transpose(1, 0, 2)


def _step(x, c, loss_target, wts, m_in, v_in):
    depth = wts["w_in"].shape[0]
    n_rows, d = x.shape[1], x.shape[2]
    me = _linear(_position())
    x2d, target2d = x[0], loss_target[0]

    (c_all,) = _all_gather([c], "gather_c")
    c_all = c_all.reshape(N_DEV, d)
    mod_cols = wts["w_mod"].shape[2]
    b_cols = lax.dynamic_slice_in_dim(wts["b_mod"], me * mod_cols, mod_cols, axis=1).reshape(depth, 1, mod_cols)
    mod_mine = _mod_fwd(c_all, wts["w_mod"], b_cols, "mod_fwd")
    (mod_all,) = _all_gather([mod_mine], "gather_mod")
    mod_me = lax.dynamic_index_in_dim(mod_all, me, axis=2, keepdims=False).transpose(1, 0, 2).reshape(depth, 6, 1, d)

    mats, mats_t = [], []
    for l in range(depth):
        names = COL_SHARDED + ROW_SHARDED
        got = _all_gather([wts[k][l].astype(bf16) for k in names], f"gather_w{l}")
        full = {}
        for k, g in zip(names, got):
            full[k] = _full_cols(g) if k in COL_SHARDED else g.reshape(-1, g.shape[2])
        mats.append(full)
        mats_t.append({k: v.T for k, v in full.items()})
    (conv_mix_all, ffn_conv_all) = _all_gather([wts["conv_mix_w"], wts["ffn_conv_w"]], "gather_conv_w")
    small = []
    for l in range(depth):
        p = {k: wts[k][l] for k in REPLICATED if k != "b_mod"}
        p["conv_mix_w"] = _full_cols(conv_mix_all[:, l])
        p["ffn_conv_w"] = _full_cols(ffn_conv_all[:, l])
        small.append(p)

    saved = []
    h = x2d
    for l in range(depth):
        mod = [mod_me[l, i] for i in range(6)]
        h, s = _layer_fwd(h, mod, mats[l], small[l], f"_l{l}")
        saved.append(s)
    dy, loss_part = _loss_head(h, target2d, "loss_head")
    loss = lax.psum(loss_part[0, 0], ("x", "y", "c"))

    grads = {k: [None] * depth for k in WEIGHTS}
    dmod_rows = [None] * depth
    small_parts = [None] * depth
    for l in reversed(range(depth)):
        mod = [mod_me[l, i] for i in range(6)]
        dy, dmod, gw, gp = _layer_bwd(dy, saved[l], mod, mats[l], mats_t[l], small[l], f"_l{l}")
        dmod_rows[l] = jnp.concatenate(dmod, axis=1)
        slabs = [(_col_slabs(gw[k]) if k in COL_SHARDED else gw[k].reshape(N_DEV, -1, gw[k].shape[1])) for k in COL_SHARDED + ROW_SHARDED]
        slabs += [_col_slabs(gp[k]) for k in SMALL_SHARDED]
        summed = _reduce_scatter(slabs, f"scatter_g{l}")
        for k, g in zip(COL_SHARDED + ROW_SHARDED + SMALL_SHARDED, summed):
            grads[k][l] = g
        small_parts[l] = gp
    grad_x = dy.reshape(x.shape)

    dmod_mine = jnp.concatenate(dmod_rows, axis=0)
    rep_names = [k for k in REPLICATED if k != "b_mod"]
    pieces = [dmod_mine.reshape(-1)] + [jnp.stack([small_parts[l][k].reshape(-1) for l in range(depth)]).reshape(-1) for k in rep_names]
    sizes = [int(pc.shape[0]) for pc in pieces]
    total = sum(sizes)
    padded = -(-total // (SUBLANES * LANES)) * (SUBLANES * LANES)
    pack = jnp.concatenate(pieces + [jnp.zeros((padded - total,), f32)]).reshape(-1, LANES)
    (pack_all,) = _all_gather([pack], "gather_small_grads")
    pack_sum = _sum_slabs(pack_all, "sum_small_grads").reshape(-1)
    offs = np.cumsum([0] + sizes)
    grads["b_mod"] = pack_sum[offs[0]:offs[1]].reshape(wts["b_mod"].shape)
    for i, k in enumerate(rep_names):
        grads[k] = pack_sum[offs[i + 1]:offs[i + 2]].reshape(wts[k].shape)
    dmod_all = pack_all.reshape(N_DEV, -1)[:, :sizes[0]].reshape(N_DEV, depth, 6 * d)
    dmod_cols = lax.dynamic_slice_in_dim(dmod_all, me * mod_cols, mod_cols, axis=2).transpose(1, 0, 2)
    grads["w_mod"] = _mod_wgrad(c_all.T, dmod_cols, "mod_wgrad")
    for k in COL_SHARDED + ROW_SHARDED + SMALL_SHARDED:
        grads[k] = jnp.stack(grads[k])

    delta, new_m, new_v = {}, {}, {}
    for k in WEIGHTS:
        delta[k], new_m[k], new_v[k] = _adamw(wts[k], grads[k], m_in[k], v_in[k], f"adamw_{k}")
    return (loss, grad_x, *[grads[k] for k in WEIGHTS], *[delta[k] for k in WEIGHTS], *[new_m[k] for k in WEIGHTS],
            *[new_v[k] for k in WEIGHTS])


def kernel(x, c, w_mod, b_mod, g_pre_mix, g_post_mix, g_pre_ffn, g_post_ffn, w_in, ssm_log_dt, ssm_a_re, ssm_a_im, ssm_b_re, ssm_b_im, ssm_c_re, ssm_c_im, ssm_d, w_glu, b_glu, conv_mix_w, w_ssm_out, w_attn_out, w_conv_out, b_gate, w_o, w_up, ffn_conv_w, w_down, loss_target, m_w_mod, m_b_mod, m_g_pre_mix, m_g_post_mix, m_g_pre_ffn, m_g_post_ffn, m_w_in, m_ssm_log_dt, m_ssm_a_re, m_ssm_a_im, m_ssm_b_re, m_ssm_b_im, m_ssm_c_re, m_ssm_c_im, m_ssm_d, m_w_glu, m_b_glu, m_conv_mix_w, m_w_ssm_out, m_w_attn_out, m_w_conv_out, m_b_gate, m_w_o, m_w_up, m_ffn_conv_w, m_w_down, v_w_mod, v_b_mod, v_g_pre_mix, v_g_post_mix, v_g_pre_ffn, v_g_post_ffn, v_w_in, v_ssm_log_dt, v_ssm_a_re, v_ssm_a_im, v_ssm_b_re, v_ssm_b_im, v_ssm_c_re, v_ssm_c_im, v_ssm_d, v_w_glu, v_b_glu, v_conv_mix_w, v_w_ssm_out, v_w_attn_out, v_w_conv_out, v_b_gate, v_w_o, v_w_up, v_ffn_conv_w, v_w_down):
    wts = dict(w_mod=w_mod, b_mod=b_mod, g_pre_mix=g_pre_mix, g_post_mix=g_post_mix, g_pre_ffn=g_pre_ffn, g_post_ffn=g_post_ffn, w_in=w_in, ssm_log_dt=ssm_log_dt, ssm_a_re=ssm_a_re, ssm_a_im=ssm_a_im, ssm_b_re=ssm_b_re, ssm_b_im=ssm_b_im, ssm_c_re=ssm_c_re, ssm_c_im=ssm_c_im, ssm_d=ssm_d, w_glu=w_glu, b_glu=b_glu, conv_mix_w=conv_mix_w, w_ssm_out=w_ssm_out, w_attn_out=w_attn_out, w_conv_out=w_conv_out, b_gate=b_gate, w_o=w_o, w_up=w_up, ffn_conv_w=ffn_conv_w, w_down=w_down)
    m_in = dict(w_mod=m_w_mod, b_mod=m_b_mod, g_pre_mix=m_g_pre_mix, g_post_mix=m_g_post_mix, g_pre_ffn=m_g_pre_ffn, g_post_ffn=m_g_post_ffn, w_in=m_w_in, ssm_log_dt=m_ssm_log_dt, ssm_a_re=m_ssm_a_re, ssm_a_im=m_ssm_a_im, ssm_b_re=m_ssm_b_re, ssm_b_im=m_ssm_b_im, ssm_c_re=m_ssm_c_re, ssm_c_im=m_ssm_c_im, ssm_d=m_ssm_d, w_glu=m_w_glu, b_glu=m_b_glu, conv_mix_w=m_conv_mix_w, w_ssm_out=m_w_ssm_out, w_attn_out=m_w_attn_out, w_conv_out=m_w_conv_out, b_gate=m_b_gate, w_o=m_w_o, w_up=m_w_up, ffn_conv_w=m_ffn_conv_w, w_down=m_w_down)
    v_in = dict(w_mod=v_w_mod, b_mod=v_b_mod, g_pre_mix=v_g_pre_mix, g_post_mix=v_g_post_mix, g_pre_ffn=v_g_pre_ffn, g_post_ffn=v_g_post_ffn, w_in=v_w_in, ssm_log_dt=v_ssm_log_dt, ssm_a_re=v_ssm_a_re, ssm_a_im=v_ssm_a_im, ssm_b_re=v_ssm_b_re, ssm_b_im=v_ssm_b_im, ssm_c_re=v_ssm_c_re, ssm_c_im=v_ssm_c_im, ssm_d=v_ssm_d, w_glu=v_w_glu, b_glu=v_b_glu, conv_mix_w=v_conv_mix_w, w_ssm_out=v_w_ssm_out, w_attn_out=v_w_attn_out, w_conv_out=v_w_conv_out, b_gate=v_b_gate, w_o=v_w_o, w_up=v_w_up, ffn_conv_w=v_ffn_conv_w, w_down=v_w_down)
    return _step(x, c, loss_target, wts, m_in, v_in)
```

```python
import functools
import math

import numpy as np
import jax
import jax.numpy as jnp
from jax import lax
from jax.experimental import pallas as pl
from jax.experimental.pallas import tpu as pltpu

f32 = jnp.float32
bf16 = jnp.bfloat16

N_DEV = 8
N_CHIPS = 4
V7X_VMEM_LIMIT_BYTES = 56 * 1024 * 1024
LANES = 128
SUBLANES = 8

RMS_EPS = 1e-6
NEG_INF = -1e30
HEAD_DIM = 64
SSM_GROUP = 16
SSM_STATE = 64
DSWA_PATTERNS = ((128, 1), (512, 4), (2048, 16))
QBLK = 128

ADAM_LR = 0.001
ADAM_B1 = 0.9
ADAM_B2 = 0.999
ADAM_EPS = 1e-08
ADAM_WD = 0.01
ADAM_STEP = 10

MESH = pl.DeviceIdType.MESH
ANY = pl.BlockSpec(memory_space=pl.ANY)


def _tile(n, pref, mult=LANES):
    t = min(pref, n) // mult * mult
    while t >= mult:
        if n % t == 0:
            return t
        t -= mult
    return n


def _params(*sem):
    return pltpu.CompilerParams(dimension_semantics=sem, vmem_limit_bytes=V7X_VMEM_LIMIT_BYTES)


def _mm(a, b, out_dtype, name, rhs_t=False):
    m, k = a.shape
    n, k2 = b.shape if rhs_t else b.shape[::-1]
    assert k == k2 and a.dtype == bf16 and b.dtype == bf16, (a.shape, b.shape, a.dtype, b.dtype)
    tm, tn, tk = _tile(m, 1024, SUBLANES), _tile(n, 512), _tile(k, 2048)
    nk = k // tk
    contract = (((1,), (1 if rhs_t else 0,)), ((), ()))

    def body(a_ref, b_ref, o_ref, acc_ref):
        kk = pl.program_id(2)

        @pl.when(kk == 0)
        def _():
            acc_ref[...] = jnp.zeros_like(acc_ref)

        acc_ref[...] += lax.dot_general(a_ref[...], b_ref[...], contract, preferred_element_type=f32)

        @pl.when(kk == nk - 1)
        def _():
            o_ref[...] = acc_ref[...].astype(o_ref.dtype)

    b_spec = pl.BlockSpec((tn, tk), lambda i, j, kk: (j, kk)) if rhs_t else pl.BlockSpec((tk, tn), lambda i, j, kk: (kk, j))
    return pl.pallas_call(
        body, grid=(m // tm, n // tn, nk),
        in_specs=[pl.BlockSpec((tm, tk), lambda i, j, kk: (i, kk)), b_spec],
        out_specs=pl.BlockSpec((tm, tn), lambda i, j, kk: (i, j)),
        out_shape=jax.ShapeDtypeStruct((m, n), out_dtype),
        scratch_shapes=[pltpu.VMEM((tm, tn), f32)],
        compiler_params=_params("parallel", "parallel", "arbitrary"), name=name)(a, b)


def _rowwise(fn, rows, vecs, out_dtypes, tm, name):
    n_rows = rows[0].shape[0]
    nr, nv = len(rows), len(vecs)
    outs = jax.eval_shape(fn, *[jax.ShapeDtypeStruct((tm, r.shape[1]), f32) for r in rows],
                          *[jax.ShapeDtypeStruct(v.shape, f32) for v in vecs])

    def body(*refs):
        vals = [r[...].astype(f32) for r in refs[:nr]] + [v[...] for v in refs[nr:nr + nv]]
        for o_ref, val in zip(refs[nr + nv:], fn(*vals)):
            o_ref[...] = val.astype(o_ref.dtype)

    res = pl.pallas_call(
        body, grid=(n_rows // tm,),
        in_specs=[pl.BlockSpec((tm, r.shape[1]), lambda i: (i, 0)) for r in rows]
        + [pl.BlockSpec(v.shape, lambda i: (0, 0)) for v in vecs],
        out_specs=[pl.BlockSpec((tm, o.shape[1]), lambda i: (i, 0)) for o in outs],
        out_shape=[jax.ShapeDtypeStruct((n_rows, o.shape[1]), dt) for o, dt in zip(outs, out_dtypes)],
        compiler_params=_params("parallel"), name=name)(*rows, *vecs)
    return res


def _rowwise_bwd(fn, rows, vecs, cots, row_dtypes, need_vecs, tm, name, add_to=None):
    n_rows = rows[0].shape[0]
    nr, nv = len(rows), len(vecs)
    add_to = add_to or {}
    flat_cots = [c for group in cots for c in group]
    add_keys = sorted(add_to)
    row_out = [i for i, dt in enumerate(row_dtypes) if dt is not None]
    n_in = nr + nv + len(flat_cots) + len(add_keys)

    def body(*refs):
        vals = [r[...].astype(f32) for r in refs[:nr]] + [v[...] for v in refs[nr:nr + nv]]
        pos = nr + nv
        cvals = []
        for group in cots:
            acc = refs[pos][...].astype(f32)
            for extra in range(1, len(group)):
                acc = acc + refs[pos + extra][...].astype(f32)
            pos += len(group)
            cvals.append(acc)
        adds = {key: refs[pos + j] for j, key in enumerate(add_keys)}
        _, vjp = jax.vjp(fn, *vals)
        grads = vjp(tuple(cvals))
        out_refs = refs[n_in:]
        for j, i in enumerate(row_out):
            g = grads[i]
            if i in adds:
                g = g + adds[i][...].astype(f32)
            out_refs[j][...] = g.astype(out_refs[j].dtype)
        if need_vecs:
            step = pl.program_id(0)
            for j in range(nv):
                acc_ref = out_refs[len(row_out) + j]

                @pl.when(step == 0)
                def _(acc_ref=acc_ref):
                    acc_ref[...] = jnp.zeros_like(acc_ref)

                acc_ref[...] += grads[nr + j]

    row_spec = lambda a: pl.BlockSpec((tm, a.shape[1]), lambda i: (i, 0))
    in_specs = ([row_spec(r) for r in rows] + [pl.BlockSpec(v.shape, lambda i: (0, 0)) for v in vecs]
                + [row_spec(c) for c in flat_cots] + [row_spec(add_to[key]) for key in add_keys])
    out_specs = [row_spec(rows[i]) for i in row_out]
    out_shape = [jax.ShapeDtypeStruct(rows[i].shape, row_dtypes[i]) for i in row_out]
    if need_vecs:
        out_specs += [pl.BlockSpec(v.shape, lambda i: (0, 0)) for v in vecs]
        out_shape += [jax.ShapeDtypeStruct(v.shape, f32) for v in vecs]
    return pl.pallas_call(
        body, grid=(n_rows // tm,), in_specs=in_specs, out_specs=out_specs, out_shape=out_shape,
        compiler_params=_params("arbitrary"), name=name)(*rows, *vecs, *flat_cots, *[add_to[key] for key in add_keys])


def _f_prenorm(x, g, sc, sh):
    y = x * lax.rsqrt(jnp.mean(x * x, axis=-1, keepdims=True) + RMS_EPS)
    return (y * g * (1.0 + sc) + sh,)


def _f_postres(x, y, g, gt):
    n = y * lax.rsqrt(jnp.mean(y * y, axis=-1, keepdims=True) + RMS_EPS)
    return (x + gt * (n * g),)


def _f_gelu(yc, u, d):
    return (jax.nn.gelu(yc + d * u),)


def _f_glu(g, z, b):
    return (g * jax.nn.sigmoid(z + b),)


def _f_merge(p0, p1, p2, ys, ya, yc, b0, b1, b2):
    return (jax.nn.sigmoid(p0 + b0) * ys + jax.nn.sigmoid(p1 + b1) * ya + jax.nn.sigmoid(p2 + b2) * yc,)


def _f_amerge(o0, o1, o2, l0, l1, l2):
    m = jnp.maximum(jnp.maximum(l0, l1), l2)
    e0, e1, e2 = jnp.exp(l0 - m), jnp.exp(l1 - m), jnp.exp(l2 - m)
    return ((e0 * o0 + e1 * o1 + e2 * o2) / (e0 + e1 + e2),)


def _f_add(a, b):
    return (a + b,)


def _shift_down(z, k, row):
    return jnp.where(row >= k, pltpu.roll(z, k, 0), 0.0)


def _shift_up(z, k, row):
    n = z.shape[0]
    return jnp.where(row < n - k, pltpu.roll(z, n - k, 0), 0.0)


def _conv3(z, w, row):
    return w[0:1, :] * z + w[1:2, :] * _shift_down(z, 1, row) + w[2:3, :] * _shift_down(z, 2, row)


def _conv3_bwd(z, w, dy, row):
    dz = w[0:1, :] * dy + w[1:2, :] * _shift_up(dy, 1, row) + w[2:3, :] * _shift_up(dy, 2, row)
    dw = [jnp.sum(dy * z, axis=0, keepdims=True),
          jnp.sum(dy * _shift_down(z, 1, row), axis=0, keepdims=True),
          jnp.sum(dy * _shift_down(z, 2, row), axis=0, keepdims=True)]
    return dz, dw


def _convgate_fwd(proj, w, off, width, name):
    n_rows = proj.shape[0]
    tc = _tile(width, 256)
    nb = width // tc
    base = off // tc

    def body(b_ref, c_ref, h_ref, w_ref, o_ref):
        row = lax.broadcasted_iota(jnp.int32, (n_rows, tc), 0)
        o_ref[...] = (b_ref[...] * _conv3(c_ref[...] * h_ref[...], w_ref[...], row)).astype(o_ref.dtype)

    col = lambda s: pl.BlockSpec((n_rows, tc), lambda j: (0, base + s * nb + j))
    return pl.pallas_call(
        body, grid=(nb,), in_specs=[col(0), col(1), col(2), pl.BlockSpec((3, tc), lambda j: (0, j))],
        out_specs=pl.BlockSpec((n_rows, tc), lambda j: (0, j)),
        out_shape=jax.ShapeDtypeStruct((n_rows, width), bf16),
        compiler_params=_params("parallel"), name=name)(proj, proj, proj, w)


def _convgate_bwd(proj, w, dcm, off, width, name):
    n_rows = proj.shape[0]
    tc = _tile(width, 128)
    nb = width // tc
    base = off // tc

    def body(b_ref, c_ref, h_ref, w_ref, d_ref, db_ref, dc_ref, dh_ref, dw_ref):
        row = lax.broadcasted_iota(jnp.int32, (n_rows, tc), 0)
        cb, cc, ch, wv, d = b_ref[...], c_ref[...], h_ref[...], w_ref[...], d_ref[...].astype(f32)
        z = cc * ch
        db_ref[...] = (d * _conv3(z, wv, row)).astype(db_ref.dtype)
        dz, dw = _conv3_bwd(z, wv, d * cb, row)
        dc_ref[...] = (dz * ch).astype(dc_ref.dtype)
        dh_ref[...] = (dz * cc).astype(dh_ref.dtype)
        for k in range(3):
            dw_ref[k:k + 1, :] = dw[k]

    col = lambda s: pl.BlockSpec((n_rows, tc), lambda j: (0, base + s * nb + j))
    own = pl.BlockSpec((n_rows, tc), lambda j: (0, j))
    wspec = pl.BlockSpec((3, tc), lambda j: (0, j))
    sec = jax.ShapeDtypeStruct((n_rows, width), bf16)
    return pl.pallas_call(
        body, grid=(nb,), in_specs=[col(0), col(1), col(2), wspec, own], out_specs=[own, own, own, wspec],
        out_shape=[sec, sec, sec, jax.ShapeDtypeStruct((3, width), f32)],
        compiler_params=_params("parallel"), name=name)(proj, proj, proj, w, dcm)


def _ffnconv_fwd(up, w, name):
    n_rows, two_f = up.shape
    half = two_f // 2
    tc = _tile(half, 256)
    nb = half // tc

    def body(a_ref, b_ref, wa_ref, wb_ref, o_ref):
        row = lax.broadcasted_iota(jnp.int32, (n_rows, tc), 0)
        ua = _conv3(a_ref[...], wa_ref[...], row)
        ub = _conv3(b_ref[...], wb_ref[...], row)
        o_ref[...] = (jax.nn.silu(ua) * ub).astype(o_ref.dtype)

    return pl.pallas_call(
        body, grid=(nb,),
        in_specs=[pl.BlockSpec((n_rows, tc), lambda j: (0, j)), pl.BlockSpec((n_rows, tc), lambda j: (0, nb + j)),
                  pl.BlockSpec((3, tc), lambda j: (0, j)), pl.BlockSpec((3, tc), lambda j: (0, nb + j))],
        out_specs=pl.BlockSpec((n_rows, tc), lambda j: (0, j)),
        out_shape=jax.ShapeDtypeStruct((n_rows, half), bf16),
        compiler_params=_params("parallel"), name=name)(up, up, w, w)


def _ffnconv_bwd(up, w, dact, name):
    n_rows, two_f = up.shape
    half = two_f // 2
    tc = _tile(half, 128)
    nb = half // tc

    def body(a_ref, b_ref, wa_ref, wb_ref, d_ref, da_ref, db_ref, dwa_ref, dwb_ref):
        row = lax.broadcasted_iota(jnp.int32, (n_rows, tc), 0)
        a, b, wa, wb, d = a_ref[...], b_ref[...], wa_ref[...], wb_ref[...], d_ref[...].astype(f32)
        ua = _conv3(a, wa, row)
        ub = _conv3(b, wb, row)
        sg = jax.nn.sigmoid(ua)
        d_ua = d * ub * (sg * (1.0 + ua * (1.0 - sg)))
        d_ub = d * (ua * sg)
        da, dwa = _conv3_bwd(a, wa, d_ua, row)
        db, dwb = _conv3_bwd(b, wb, d_ub, row)
        da_ref[...] = da.astype(da_ref.dtype)
        db_ref[...] = db.astype(db_ref.dtype)
        for k in range(3):
            dwa_ref[k:k + 1, :] = dwa[k]
            dwb_ref[k:k + 1, :] = dwb[k]

    lo = pl.BlockSpec((n_rows, tc), lambda j: (0, j))
    hi = pl.BlockSpec((n_rows, tc), lambda j: (0, nb + j))
    wlo = pl.BlockSpec((3, tc), lambda j: (0, j))
    whi = pl.BlockSpec((3, tc), lambda j: (0, nb + j))
    sec = jax.ShapeDtypeStruct((n_rows, half), bf16)
    wsec = jax.ShapeDtypeStruct((3, half), f32)
    da, db, dwa, dwb = pl.pallas_call(
        body, grid=(nb,), in_specs=[lo, hi, wlo, whi, lo], out_specs=[lo, lo, wlo, wlo],
        out_shape=[sec, sec, wsec, wsec], compiler_params=_params("parallel"), name=name)(up, up, w, w, dact)
    return da, db, jnp.concatenate([dwa, dwb], axis=1)


SCAN_TC = 256


def _cmul(ar, ai, br, bi):
    return ar * br - ai * bi, ar * bi + ai * br


def _scan(x, lam, adjoint, name):
    n_rows, two_s = x.shape
    tc = min(SCAN_TC, two_s // 2)
    n_tiles = two_s // (2 * tc)
    n_chunks = n_rows // SUBLANES

    def body(x_ref, lam_ref, hf_ref, hb_ref):
        lr = lam_ref[:, 0:tc]
        li = lam_ref[:, tc:2 * tc]
        if adjoint:
            li = -li
        row = lax.broadcasted_iota(jnp.int32, (SUBLANES, tc), 0)
        powers = [(lr, li)]
        for _ in range(SUBLANES - 1):
            powers.append(_cmul(powers[-1][0], powers[-1][1], lr, li))
        pr = jnp.zeros((SUBLANES, tc), f32)
        pi = jnp.zeros((SUBLANES, tc), f32)
        for t in range(SUBLANES):
            e = (SUBLANES - 1 - t) if adjoint else t
            pr = jnp.where(row == t, powers[e][0], pr)
            pi = jnp.where(row == t, powers[e][1], pi)
        last = 0 if adjoint else SUBLANES - 1

        def chunk(i, carry):
            cr, ci = carry
            c = (n_chunks - 1 - i) if adjoint else i
            rows = pl.ds(pl.multiple_of(c * SUBLANES, SUBLANES), SUBLANES)
            xr = x_ref[rows, 0:tc]
            xi = x_ref[rows, tc:2 * tc]
            for k in (1, 2, 4):
                ar, ai = powers[k - 1]
                if adjoint:
                    sr = jnp.where(row < SUBLANES - k, pltpu.roll(xr, SUBLANES - k, 0), 0.0)
                    si = jnp.where(row < SUBLANES - k, pltpu.roll(xi, SUBLANES - k, 0), 0.0)
                else:
                    sr = jnp.where(row >= k, pltpu.roll(xr, k, 0), 0.0)
                    si = jnp.where(row >= k, pltpu.roll(xi, k, 0), 0.0)
                xr, xi = xr + ar * sr - ai * si, xi + ar * si + ai * sr
            hr = xr + pr * cr - pi * ci
            hi = xi + pr * ci + pi * cr
            hf_ref[rows, 0:tc] = hr
            hf_ref[rows, tc:2 * tc] = hi
            hb_ref[rows, 0:tc] = hr.astype(bf16)
            hb_ref[rows, tc:2 * tc] = hi.astype(bf16)
            return (jnp.sum(jnp.where(row == last, hr, 0.0), axis=0, keepdims=True),
                    jnp.sum(jnp.where(row == last, hi, 0.0), axis=0, keepdims=True))

        zero = jnp.zeros((1, tc), f32)
        lax.fori_loop(0, n_chunks, chunk, (zero, zero))

    blk = pl.BlockSpec((n_rows, 2 * tc), lambda j: (0, j))
    return pl.pallas_call(
        body, grid=(n_tiles,), in_specs=[blk, pl.BlockSpec((1, 2 * tc), lambda j: (0, j))], out_specs=[blk, blk],
        out_shape=[jax.ShapeDtypeStruct(x.shape, f32), jax.ShapeDtypeStruct(x.shape, bf16)],
        compiler_params=_params("parallel"), name=name)(x, lam)


def _lam_grad(g, h_prev, name):
    n_rows, two_s = g.shape
    tc = min(SCAN_TC, two_s // 2)

    def body(g_ref, h_ref, o_ref):
        gr, gi = g_ref[:, 0:tc], g_ref[:, tc:2 * tc]
        hr, hi = h_ref[:, 0:tc], h_ref[:, tc:2 * tc]
        o_ref[:, 0:tc] = jnp.sum(gr * hr + gi * hi, axis=0, keepdims=True)
        o_ref[:, tc:2 * tc] = jnp.sum(gi * hr - gr * hi, axis=0, keepdims=True)

    blk = pl.BlockSpec((n_rows, 2 * tc), lambda j: (0, j))
    return pl.pallas_call(
        body, grid=(two_s // (2 * tc),), in_specs=[blk, blk], out_specs=pl.BlockSpec((1, 2 * tc), lambda j: (0, j)),
        out_shape=jax.ShapeDtypeStruct((1, two_s), f32), compiler_params=_params("parallel"), name=name)(g, h_prev)


def _interleave(re, im, tc):
    lead = re.shape[:-1]
    s = re.shape[-1]
    return jnp.concatenate([re.reshape(*lead, s // tc, 1, tc), im.reshape(*lead, s // tc, 1, tc)], axis=-2).reshape(*lead, 2 * s)


def _deinterleave(z, tc):
    lead = z.shape[:-1]
    s = z.shape[-1] // 2
    z = z.reshape(*lead, s // tc, 2, tc)
    return z[..., 0, :].reshape(*lead, s), z[..., 1, :].reshape(*lead, s)


def _f_disc(log_dt, ar, ai, br, bi):
    dt = jnp.exp(log_dt)
    mag = jnp.exp(ar * dt)
    lr, li = mag * jnp.cos(ai * dt), mag * jnp.sin(ai * dt)
    den = ar * ar + ai * ai
    fr = ((lr - 1.0) * ar + li * ai) / den
    fi = (li * ar - (lr - 1.0) * ai) / den
    return lr, li, fr[None] * br - fi[None] * bi, fr[None] * bi + fi[None] * br


def _disc_fwd(log_dt, ar, ai, br, bi, name):
    def body(*refs):
        for o_ref, val in zip(refs[5:], _f_disc(*[r[...] for r in refs[:5]])):
            o_ref[...] = val

    sd = jax.ShapeDtypeStruct
    return pl.pallas_call(body, out_shape=[sd(ar.shape, f32), sd(ar.shape, f32), sd(br.shape, f32), sd(br.shape, f32)],
                          name=name)(log_dt, ar, ai, br, bi)


def _disc_bwd(log_dt, ar, ai, br, bi, cots, name):
    def body(*refs):
        _, vjp = jax.vjp(_f_disc, *[r[...] for r in refs[:5]])
        for o_ref, val in zip(refs[9:], vjp(tuple(r[...] for r in refs[5:9]))):
            o_ref[...] = val

    sd = jax.ShapeDtypeStruct
    return pl.pallas_call(
        body, out_shape=[sd(log_dt.shape, f32), sd(ar.shape, f32), sd(ar.shape, f32), sd(br.shape, f32), sd(br.shape, f32)],
        name=name)(log_dt, ar, ai, br, bi, *cots)


def _alibi_bias(n_heads_per_pattern):
    n_heads = len(DSWA_PATTERNS) * n_heads_per_pattern
    slopes = np.array([2.0 ** (-8.0 * (h + 1) / n_heads) for h in range(n_heads)], dtype=np.float32)
    qi = np.arange(QBLK)[:, None]
    kj = np.arange(QBLK)[None, :]
    cur, prev = [], []
    for h in range(n_heads):
        dil = DSWA_PATTERNS[h // n_heads_per_pattern][1]
        d_cur = qi - kj
        d_prev = QBLK + qi - kj
        cur.append(np.where(d_cur >= 0, -slopes[h] * (d_cur * dil).astype(np.float32), NEG_INF))
        prev.append(np.where(d_prev <= QBLK, -slopes[h] * (d_prev * dil).astype(np.float32), NEG_INF))
    return np.stack(cur).astype(np.float32), np.stack(prev).astype(np.float32)


def _blocks_per_residue(head, n_heads_per_pattern, n_blocks):
    pattern = head // n_heads_per_pattern
    out = n_blocks // DSWA_PATTERNS[-1][1]
    for p in range(len(DSWA_PATTERNS) - 2, -1, -1):
        out = jnp.where(pattern == p, n_blocks // DSWA_PATTERNS[p][1], out)
    return out


def _qk(a, b):
    return jnp.einsum('bqe,bke->bqk', a, b, preferred_element_type=f32)


def _pv(p, v):
    return jnp.einsum('bqk,bke->bqe', p.astype(bf16), v, preferred_element_type=f32)


def _ptq(p, a):
    return jnp.einsum('bqk,bqe->bke', p.astype(bf16), a, preferred_element_type=f32)


def _prev_block(t):
    return jnp.concatenate([jnp.zeros((1,) + t.shape[1:], t.dtype), t[:-1]], axis=0)


def _next_block(t):
    return jnp.concatenate([t[1:], jnp.zeros((1,) + t.shape[1:], t.dtype)], axis=0)


def _has_prev(head, hpp, n_blocks):
    blk = lax.broadcasted_iota(jnp.int32, (n_blocks, 1, 1), 0)
    return (blk & (_blocks_per_residue(head, hpp, n_blocks) - 1)) > 0


def _attn_fwd(q, k, v, hpp, name):
    n_heads, n_blocks = q.shape[0], q.shape[1]
    bias_cur, bias_prev = _alibi_bias(hpp)
    scale = HEAD_DIM ** -0.5

    def body(q_ref, k_ref, v_ref, bc_ref, bp_ref, o_ref, lse_ref):
        has_prev = _has_prev(pl.program_id(0), hpp, n_blocks)
        qv, kv, vv = q_ref[...], k_ref[...], v_ref[...]
        s_c = _qk(qv, kv) * scale + bc_ref[...][None]
        s_p = jnp.where(has_prev, _qk(qv, _prev_block(kv)) * scale + bp_ref[...][None], NEG_INF)
        m = jnp.maximum(jnp.max(s_c, axis=-1, keepdims=True), jnp.max(s_p, axis=-1, keepdims=True))
        p_c = jnp.exp(s_c - m)
        p_p = jnp.exp(s_p - m)
        l = jnp.sum(p_c, axis=-1, keepdims=True) + jnp.sum(p_p, axis=-1, keepdims=True)
        o_ref[...] = (_pv(p_c, vv) + _pv(p_p, _prev_block(vv))) / l
        lse_ref[...] = m + jnp.log(l)

    head = pl.BlockSpec((None, n_blocks, QBLK, HEAD_DIM), lambda a: (a, 0, 0, 0))
    bias = pl.BlockSpec((None, QBLK, QBLK), lambda a: (a, 0, 0))
    return pl.pallas_call(
        body, grid=(n_heads,), in_specs=[head, head, head, bias, bias],
        out_specs=[head, pl.BlockSpec((None, n_blocks, QBLK, 1), lambda a: (a, 0, 0, 0))],
        out_shape=[jax.ShapeDtypeStruct(q.shape, f32), jax.ShapeDtypeStruct(q.shape[:3] + (1,), f32)],
        compiler_params=_params("parallel"), name=name)(q, k, v, jnp.asarray(bias_cur), jnp.asarray(bias_prev))


def _attn_bwd(q, k, v, o, lse, do, dlse_x, hpp, name):
    n_heads, n_blocks = q.shape[0], q.shape[1]
    bias_cur, bias_prev = _alibi_bias(hpp)
    scale = HEAD_DIM ** -0.5

    def body(q_ref, k_ref, v_ref, o_ref, l_ref, do_ref, dl_ref, bc_ref, bp_ref, dq_ref, dk_ref, dv_ref):
        has_prev = _has_prev(pl.program_id(0), hpp, n_blocks)
        qv, kv, vv = q_ref[...], k_ref[...], v_ref[...]
        kp, vp = _prev_block(kv), _prev_block(vv)
        dov = do_ref[...]
        do_b = dov.astype(bf16)
        lse = l_ref[...]
        corr = jnp.sum(dl_ref[...], axis=-1, keepdims=True) - jnp.sum(dov * o_ref[...], axis=-1, keepdims=True)
        p_c = jnp.exp(_qk(qv, kv) * scale + bc_ref[...][None] - lse)
        ds_c = p_c * (_qk(do_b, vv) + corr)
        p_p = jnp.where(has_prev, jnp.exp(_qk(qv, kp) * scale + bp_ref[...][None] - lse), 0.0)
        ds_p = p_p * (_qk(do_b, vp) + corr)
        dq_ref[...] = ((_pv(ds_c, kv) + _pv(ds_p, kp)) * scale).astype(dq_ref.dtype)
        dk_ref[...] = ((_ptq(ds_c, qv) + _next_block(_ptq(ds_p, qv))) * scale).astype(dk_ref.dtype)
        dv_ref[...] = (_ptq(p_c, do_b) + _next_block(_ptq(p_p, do_b))).astype(dv_ref.dtype)

    head = pl.BlockSpec((None, n_blocks, QBLK, HEAD_DIM), lambda a: (a, 0, 0, 0))
    head1 = pl.BlockSpec((None, n_blocks, QBLK, 1), lambda a: (a, 0, 0, 0))
    bias = pl.BlockSpec((None, QBLK, QBLK), lambda a: (a, 0, 0))
    out = jax.ShapeDtypeStruct(q.shape, bf16)
    return pl.pallas_call(
        body, grid=(n_heads,), in_specs=[head, head, head, head, head1, head, head, bias, bias],
        out_specs=[head, head, head], out_shape=[out, out, out],
        compiler_params=_params("parallel"), name=name)(q, k, v, o, lse, do, dlse_x, jnp.asarray(bias_cur), jnp.asarray(bias_prev))


def _to_blocks(t, hpp):
    n_rows = t.shape[0]
    t = t.reshape(n_rows, len(DSWA_PATTERNS), hpp, HEAD_DIM)
    out = []
    for p, (_, dil) in enumerate(DSWA_PATTERNS):
        tp = t[:, p].reshape(n_rows // dil, dil, hpp, HEAD_DIM).transpose(2, 1, 0, 3)
        out.append(tp.reshape(hpp, n_rows // QBLK, QBLK, HEAD_DIM))
    return jnp.concatenate(out, axis=0)


def _from_blocks(t, hpp):
    n_blocks, last = t.shape[1], t.shape[3]
    n_rows = n_blocks * QBLK
    out = []
    for p, (_, dil) in enumerate(DSWA_PATTERNS):
        tp = t[p * hpp:(p + 1) * hpp].reshape(hpp, dil, n_rows // dil, last).transpose(2, 1, 0, 3)
        out.append(tp.reshape(n_rows, hpp * last))
    return out


def _loss_head(y, target, name):
    n_rows, d = y.shape
    tm = _tile(n_rows, 256, SUBLANES)

    def body(y_ref, t_ref, dy_ref, loss_ref):
        diff = y_ref[...] - t_ref[...]
        dy_ref[...] = diff * (1.0 / d)

        @pl.when(pl.program_id(0) == 0)
        def _():
            loss_ref[...] = jnp.zeros_like(loss_ref)

        loss_ref[...] += jnp.sum(jnp.sum(diff * diff, axis=1, keepdims=True), axis=0, keepdims=True) * (0.5 / d)

    row = pl.BlockSpec((tm, d), lambda i: (i, 0))
    return pl.pallas_call(
        body, grid=(n_rows // tm,), in_specs=[row, row], out_specs=[row, pl.BlockSpec((1, 1), lambda i: (0, 0))],
        out_shape=[jax.ShapeDtypeStruct(y.shape, f32), jax.ShapeDtypeStruct((1, 1), f32)],
        compiler_params=_params("arbitrary"), name=name)(y, target)


def _as2d(a):
    if a.ndim >= 2 and a.shape[-1] >= LANES:
        return a.reshape(-1, a.shape[-1])
    return a.reshape(-1, LANES) if a.size % LANES == 0 else a.reshape(1, -1)


ELEMENTWISE_BLOCK_BYTES = 2 << 20


def _row_tile(n_rows, n_cols):
    return _tile(n_rows, max(SUBLANES, ELEMENTWISE_BLOCK_BYTES // (4 * n_cols)), SUBLANES)


def _adamw(w, g, m, v, name):
    shape = w.shape
    w2, g2, m2, v2 = _as2d(w), _as2d(g), _as2d(m), _as2d(v)
    n_rows, n_cols = w2.shape
    tm = _row_tile(n_rows, n_cols)

    def body(w_ref, g_ref, m_ref, v_ref, d_ref, mo_ref, vo_ref):
        gv = g_ref[...]
        mn = ADAM_B1 * m_ref[...] + (1.0 - ADAM_B1) * gv
        vn = ADAM_B2 * v_ref[...] + (1.0 - ADAM_B2) * jnp.square(gv)
        m_hat = mn / (1.0 - ADAM_B1 ** ADAM_STEP)
        v_hat = vn / (1.0 - ADAM_B2 ** ADAM_STEP)
        d_ref[...] = -ADAM_LR * (m_hat / (jnp.sqrt(v_hat) + ADAM_EPS) + ADAM_WD * w_ref[...])
        mo_ref[...] = mn
        vo_ref[...] = vn

    row = pl.BlockSpec((tm, n_cols), lambda i: (i, 0))
    out = jax.ShapeDtypeStruct(w2.shape, f32)
    d, mn, vn = pl.pallas_call(body, grid=(n_rows // tm,), in_specs=[row] * 4, out_specs=[row] * 3, out_shape=[out] * 3,
                               compiler_params=_params("parallel"), name=name)(w2, g2, m2, v2)
    return d.reshape(shape), mn.reshape(shape), vn.reshape(shape)


def _add(a, b, name):
    a2, b2 = a.reshape(-1, a.shape[-1]), b.reshape(-1, b.shape[-1])
    n_rows, n_cols = a2.shape
    tm = _row_tile(n_rows, n_cols)

    def body(a_ref, b_ref, o_ref):
        o_ref[...] = (a_ref[...].astype(f32) + b_ref[...].astype(f32)).astype(o_ref.dtype)

    row = pl.BlockSpec((tm, n_cols), lambda i: (i, 0))
    out = pl.pallas_call(body, grid=(n_rows // tm,), in_specs=[row, row], out_specs=row,
                         out_shape=jax.ShapeDtypeStruct(a2.shape, a.dtype), compiler_params=_params("parallel"), name=name)(a2, b2)
    return out.reshape(a.shape)


def _sum_slabs(x, name):
    n = x.shape[0]
    shape = x.shape[1:]
    x3 = x.reshape(n, -1, shape[-1])
    n_rows, n_cols = x3.shape[1:]
    tm = _row_tile(n_rows, n_cols)

    def body(x_ref, o_ref):
        acc = x_ref[0].astype(f32)
        for s in range(1, n):
            acc = acc + x_ref[s].astype(f32)
        o_ref[...] = acc

    out = pl.pallas_call(
        body, grid=(n_rows // tm,), in_specs=[pl.BlockSpec((n, tm, n_cols), lambda i: (0, i, 0))],
        out_specs=pl.BlockSpec((tm, n_cols), lambda i: (i, 0)), out_shape=jax.ShapeDtypeStruct((n_rows, n_cols), f32),
        compiler_params=_params("parallel"), name=name)(x3)
    return out.reshape(shape)


def _mod_fwd(c_all, w_mod, b_cols, name):
    depth, d, cols = w_mod.shape
    tn = _tile(cols, 512)

    def body(c_ref, w_ref, b_ref, o_ref):
        cond = jax.nn.silu(c_ref[...])
        o_ref[...] = jnp.dot(cond, w_ref[...], preferred_element_type=f32, precision=lax.Precision.HIGHEST) + b_ref[...]

    return pl.pallas_call(
        body, grid=(depth, cols // tn),
        in_specs=[pl.BlockSpec((N_DEV, d), lambda l, j: (0, 0)), pl.BlockSpec((None, d, tn), lambda l, j: (l, 0, j)),
                  pl.BlockSpec((None, 1, tn), lambda l, j: (l, 0, j))],
        out_specs=pl.BlockSpec((None, N_DEV, tn), lambda l, j: (l, 0, j)),
        out_shape=jax.ShapeDtypeStruct((depth, N_DEV, cols), f32),
        compiler_params=_params("parallel", "parallel"), name=name)(c_all, w_mod, b_cols)


def _mod_wgrad(c_all_t, dmod_cols, name):
    d = c_all_t.shape[0]
    depth, _, cols = dmod_cols.shape
    tm = _tile(d, 256, SUBLANES)

    def body(c_ref, g_ref, o_ref):
        cond = jax.nn.silu(c_ref[...])
        o_ref[...] = jnp.dot(cond, g_ref[...], preferred_element_type=f32, precision=lax.Precision.HIGHEST)

    return pl.pallas_call(
        body, grid=(depth, d // tm),
        in_specs=[pl.BlockSpec((tm, N_DEV), lambda l, i: (i, 0)), pl.BlockSpec((None, N_DEV, cols), lambda l, i: (l, 0, 0))],
        out_specs=pl.BlockSpec((None, tm, cols), lambda l, i: (l, i, 0)),
        out_shape=jax.ShapeDtypeStruct((depth, d, cols), f32),
        compiler_params=_params("parallel", "parallel"), name=name)(c_all_t, dmod_cols)


def _position():
    return lax.axis_index("x"), lax.axis_index("y"), lax.axis_index("c")


def _linear(p):
    return 4 * p[0] + 2 * p[1] + p[2]


def _all_gather(xs, name):
    n = len(xs)

    def body(*refs):
        x_refs, o_refs = refs[:n], refs[n:2 * n]
        send_sems, recv_sems, local_sems = refs[2 * n:]
        x, y, c = _position()
        me, sibling = (x, y, c), (x, y, 1 - c)
        chips = [(1 - x, y), (x, 1 - y), (1 - x, 1 - y)]

        def copy(a, k, block, to, src=None):
            slab = o_refs[a].at[_linear(block)]
            return pltpu.make_async_remote_copy(
                src_ref=slab if src is None else src, dst_ref=slab, send_sem=send_sems.at[a, k], recv_sem=recv_sems.at[a, k],
                device_id=to, device_id_type=MESH)

        mine = [pltpu.make_async_copy(x_refs[a], o_refs[a].at[_linear(me)], local_sems.at[a]) for a in range(n)]
        first = []
        for a in range(n):
            mine[a].start()
            first.append(copy(a, 0, me, sibling, src=x_refs[a]))
            first += [copy(a, 1 + j, me, (*chip, c), src=x_refs[a]) for j, chip in enumerate(chips)]
        for cp in first:
            cp.start()
        passed = []
        for j, chip in enumerate(chips):
            for a in range(n):
                copy(a, 1 + j, (*chip, c), me).wait_recv()
                cp = copy(a, 4 + j, (*chip, c), sibling)
                cp.start()
                passed.append(cp)
        for a in range(n):
            copy(a, 0, sibling, me).wait_recv()
            for j, chip in enumerate(chips):
                copy(a, 4 + j, (*chip, 1 - c), me).wait_recv()
        for cp in first + passed:
            cp.wait_send()
        for cp in mine:
            cp.wait()

    return pl.pallas_call(
        body, in_specs=[ANY] * n, out_specs=[ANY] * n,
        out_shape=[jax.ShapeDtypeStruct((N_DEV,) + x.shape, x.dtype) for x in xs],
        scratch_shapes=[pltpu.SemaphoreType.DMA((n, 7)), pltpu.SemaphoreType.DMA((n, 7)), pltpu.SemaphoreType.DMA((n,))],
        name=name)(*xs)


def _pair_exchange(xs, name):
    n = len(xs)

    def body(*refs):
        x_refs, keep_refs, got_refs = refs[:n], refs[n:2 * n], refs[2 * n:3 * n]
        send_sems, recv_sems, local_sems = refs[3 * n:]
        x, y, c = _position()
        sibling = (x, y, 1 - c)
        local, sends = [], []
        for a in range(n):
            for p in range(N_CHIPS):
                local.append(pltpu.make_async_copy(x_refs[a].at[p, c], keep_refs[a].at[p], local_sems.at[a, p]))
                sends.append(pltpu.make_async_remote_copy(
                    src_ref=x_refs[a].at[p, 1 - c], dst_ref=got_refs[a].at[p], send_sem=send_sems.at[a, p],
                    recv_sem=recv_sems.at[a, p], device_id=sibling, device_id_type=MESH))
        for cp in sends + local:
            cp.start()
        for cp in sends:
            cp.wait_recv()
        for cp in sends:
            cp.wait_send()
        for cp in local:
            cp.wait()

    half = [jax.ShapeDtypeStruct((N_CHIPS,) + x.shape[2:], x.dtype) for x in xs]
    sems = pltpu.SemaphoreType.DMA((n, N_CHIPS))
    res = pl.pallas_call(body, in_specs=[ANY] * n, out_specs=[ANY] * (2 * n), out_shape=half + half,
                         scratch_shapes=[sems, sems, sems], name=name)(*xs)
    return res[:n], res[n:]


def _chip_exchange(xs, name):
    n = len(xs)

    def body(*refs):
        x_refs, o_refs = refs[:n], refs[n:2 * n]
        send_sems, recv_sems, local_sems = refs[2 * n:]
        x, y, c = _position()
        my_chip = 2 * x + y
        chips = [(1 - x, y), (x, 1 - y), (1 - x, 1 - y)]
        mine = [pltpu.make_async_copy(x_refs[a].at[my_chip], o_refs[a].at[my_chip], local_sems.at[a]) for a in range(n)]
        sends, arrivals = [], []
        for a in range(n):
            for k, (px, py) in enumerate(chips):
                theirs = 2 * px + py
                sends.append(pltpu.make_async_remote_copy(
                    src_ref=x_refs[a].at[theirs], dst_ref=o_refs[a].at[my_chip], send_sem=send_sems.at[a, k],
                    recv_sem=recv_sems.at[a, k], device_id=(px, py, c), device_id_type=MESH))
                arrivals.append(pltpu.make_async_remote_copy(
                    src_ref=x_refs[a].at[theirs], dst_ref=o_refs[a].at[theirs], send_sem=send_sems.at[a, k],
                    recv_sem=recv_sems.at[a, k], device_id=(px, py, c), device_id_type=MESH))
        for cp in sends + mine:
            cp.start()
        for cp in arrivals:
            cp.wait_recv()
        for cp in sends:
            cp.wait_send()
        for cp in mine:
            cp.wait()

    sems = pltpu.SemaphoreType.DMA((n, N_CHIPS - 1))
    return pl.pallas_call(body, in_specs=[ANY] * n, out_specs=[ANY] * n, out_shape=[jax.ShapeDtypeStruct(x.shape, x.dtype) for x in xs],
                          scratch_shapes=[sems, sems, pltpu.SemaphoreType.DMA((n,))], name=name)(*xs)


def _reduce_scatter(xs, name):
    kept, got = _pair_exchange([x.reshape((N_CHIPS, 2) + x.shape[1:]) for x in xs], f"{name}_pair")
    chip_sums = [_add(a, b, f"{name}_pair_sum{i}") for i, (a, b) in enumerate(zip(kept, got))]
    arrived = _chip_exchange(chip_sums, f"{name}_chips")
    return [_sum_slabs(g, f"{name}_sum{i}") for i, g in enumerate(arrived)]


def _column_group(j, tc, shape):
    col = lax.broadcasted_iota(jnp.int32, shape, 1)
    return (j * tc + col % tc) // SSM_STATE


def _bd_build(bt, n_groups, tc, name):
    n_rows = n_groups * SSM_GROUP
    two_s = bt.shape[1]

    def body(b_ref, o_ref):
        grp = _column_group(pl.program_id(0), tc, (n_rows, 2 * tc))
        row = lax.broadcasted_iota(jnp.int32, (n_rows, 2 * tc), 0)
        tiled = jnp.concatenate([b_ref[...]] * n_groups, axis=0)
        o_ref[...] = jnp.where(grp == row // SSM_GROUP, tiled, 0.0).astype(o_ref.dtype)

    return pl.pallas_call(
        body, grid=(two_s // (2 * tc),), in_specs=[pl.BlockSpec((SSM_GROUP, 2 * tc), lambda j: (0, j))],
        out_specs=pl.BlockSpec((n_rows, 2 * tc), lambda j: (0, j)), out_shape=jax.ShapeDtypeStruct((n_rows, two_s), bf16),
        compiler_params=_params("parallel"), name=name)(bt)


def _bd_fold(m, n_groups, tc, name):
    n_rows, two_s = m.shape

    def body(m_ref, o_ref):
        grp = _column_group(pl.program_id(0), tc, (SSM_GROUP, 2 * tc))
        acc = jnp.zeros((SSM_GROUP, 2 * tc), f32)
        for g in range(n_groups):
            acc = acc + jnp.where(grp == g, m_ref[g * SSM_GROUP:(g + 1) * SSM_GROUP, :], 0.0)
        o_ref[...] = acc

    return pl.pallas_call(
        body, grid=(two_s // (2 * tc),), in_specs=[pl.BlockSpec((n_rows, 2 * tc), lambda j: (0, j))],
        out_specs=pl.BlockSpec((SSM_GROUP, 2 * tc), lambda j: (0, j)), out_shape=jax.ShapeDtypeStruct((SSM_GROUP, two_s), f32),
        compiler_params=_params("parallel"), name=name)(m)


def _ssm_setup(p, tag):
    n_groups = p["ssm_a_re"].shape[0]
    s = n_groups * SSM_STATE
    tc = min(SCAN_TC, s)
    log_dt = p["ssm_log_dt"].reshape(n_groups, 1)
    br_t = p["ssm_b_re"].transpose(2, 0, 1)
    bi_t = p["ssm_b_im"].transpose(2, 0, 1)
    disc_in = (log_dt, p["ssm_a_re"], p["ssm_a_im"], br_t, bi_t)
    lr, li, bbr, bbi = _disc_fwd(*disc_in, name=f"ssm_disc{tag}")
    lam = _interleave(lr.reshape(1, s), li.reshape(1, s), tc)
    bb = _bd_build(_interleave(bbr.reshape(SSM_GROUP, s), bbi.reshape(SSM_GROUP, s), tc), n_groups, tc, f"ssm_in_map{tag}")
    cr_t = p["ssm_c_re"].transpose(1, 0, 2).reshape(SSM_GROUP, s)
    ci_t = p["ssm_c_im"].transpose(1, 0, 2).reshape(SSM_GROUP, s)
    cc = _bd_build(_interleave(cr_t, -ci_t, tc), n_groups, tc, f"ssm_out_map{tag}")
    return dict(disc_in=disc_in, lam=lam, bb=bb, cc=cc, tc=tc, n_groups=n_groups)


def _apply_w(a, w, key, out_dtype, name):
    return _mm(a, w[key], out_dtype, name, rhs_t=key in COL_SHARDED)


def _apply_wt(dy, w, key, out_dtype, name):
    return _mm(dy, w[key], out_dtype, name, rhs_t=key not in COL_SHARDED)


def _layer_fwd(x, mod, w, p, tag):
    n_rows, d = x.shape
    w4 = d // 4
    hpp = w4 // HEAD_DIM
    off_q, off_k, off_v, off_conv, off_gate = w4, 4 * w4, 7 * w4, 10 * w4, 13 * w4
    sh1, sc1, gt1, sh2, sc2, gt2 = mod
    vec = lambda a: a.reshape(1, -1)
    s = dict(x=x)
    (s["h"],) = _rowwise(_f_prenorm, [x], [vec(p["g_pre_mix"]), sc1, sh1], [bf16], 256, f"prenorm_mix{tag}")
    proj = s["proj"] = _apply_w(s["h"], w, "w_in", f32, f"in_proj{tag}")
    ssm = s["ssm"] = _ssm_setup(p, tag)
    s["u32"] = proj[:, :off_q]
    s["u"] = s["u32"].astype(bf16)
    xin = _mm(s["u"], ssm["bb"], f32, f"ssm_in{tag}")
    s["hf"], s["hb"] = _scan(xin, ssm["lam"], False, f"ssm_scan{tag}")
    s["yc"] = _mm(s["hb"], ssm["cc"], f32, f"ssm_out{tag}", rhs_t=True)
    (s["g"],) = _rowwise(_f_gelu, [s["yc"], s["u32"]], [vec(p["ssm_d"])], [bf16], 256, f"ssm_gelu{tag}")
    s["z"] = _apply_w(s["g"], w, "w_glu", f32, f"glu_proj{tag}")
    (s["s_ssm"],) = _rowwise(_f_glu, [s["g"], s["z"]], [vec(p["b_glu"])], [bf16], 256, f"glu{tag}")
    s["y_ssm"] = _apply_w(s["s_ssm"], w, "w_ssm_out", f32, f"ssm_proj{tag}")
    s["qb"] = _to_blocks(proj[:, off_q:off_k].astype(bf16), hpp)
    s["kb"] = _to_blocks(proj[:, off_k:off_v].astype(bf16), hpp)
    s["vb"] = _to_blocks(proj[:, off_v:off_conv].astype(bf16), hpp)
    s["ob"], s["lseb"] = _attn_fwd(s["qb"], s["kb"], s["vb"], hpp, f"attn{tag}")
    s["o_tok"] = _from_blocks(s["ob"], hpp)
    s["lse_x"] = [jnp.repeat(l, HEAD_DIM, axis=1) for l in _from_blocks(s["lseb"], hpp)]
    (s["attn_o"],) = _rowwise(_f_amerge, s["o_tok"] + s["lse_x"], [], [bf16], 256, f"attn_merge{tag}")
    s["y_attn"] = _apply_w(s["attn_o"], w, "w_attn_out", f32, f"attn_proj{tag}")
    s["cm"] = _convgate_fwd(proj, p["conv_mix_w"], off_conv, w4, f"convgate{tag}")
    s["y_conv"] = _apply_w(s["cm"], w, "w_conv_out", f32, f"conv_proj{tag}")
    s["pg"] = [proj[:, off_gate + i * d:off_gate + (i + 1) * d] for i in range(3)]
    s["bg"] = [p["b_gate"][i * d:(i + 1) * d].reshape(1, d) for i in range(3)]
    (s["merged"],) = _rowwise(_f_merge, s["pg"] + [s["y_ssm"], s["y_attn"], s["y_conv"]], s["bg"], [bf16], 128, f"merge{tag}")
    s["y"] = _apply_w(s["merged"], w, "w_o", f32, f"o_proj{tag}")
    (s["x1"],) = _rowwise(_f_postres, [x, s["y"]], [vec(p["g_post_mix"]), gt1], [f32], 256, f"postres_mix{tag}")
    (s["h2"],) = _rowwise(_f_prenorm, [s["x1"]], [vec(p["g_pre_ffn"]), sc2, sh2], [bf16], 256, f"prenorm_ffn{tag}")
    s["up"] = _apply_w(s["h2"], w, "w_up", f32, f"up_proj{tag}")
    s["act"] = _ffnconv_fwd(s["up"], p["ffn_conv_w"], f"ffnconv{tag}")
    s["y2"] = _apply_w(s["act"], w, "w_down", f32, f"down_proj{tag}")
    (x2,) = _rowwise(_f_postres, [s["x1"], s["y2"]], [vec(p["g_post_ffn"]), gt2], [f32], 256, f"postres_ffn{tag}")
    return x2, s


def _layer_bwd(dx2, s, mod, w, p, tag):
    x = s["x"]
    n_rows, d = x.shape
    w4 = d // 4
    hpp = w4 // HEAD_DIM
    off_conv = 10 * w4
    sh1, sc1, gt1, sh2, sc2, gt2 = mod
    vec = lambda a: a.reshape(1, -1)
    gw, gp = {}, {}
    dy2, d_gpf, d_gt2 = _rowwise_bwd(_f_postres, [s["x1"], s["y2"]], [vec(p["g_post_ffn"]), gt2], [[dx2]], [None, bf16], True,
                                     256, f"postres_ffn_bwd{tag}")
    gp["g_post_ffn"] = d_gpf
    dact = _apply_wt(dy2, w, "w_down", bf16, f"down_dx{tag}")
    gw["w_down"] = _mm(s["act"].T, dy2, bf16, f"down_dw{tag}")
    da, db, gp["ffn_conv_w"] = _ffnconv_bwd(s["up"], p["ffn_conv_w"], dact, f"ffnconv_bwd{tag}")
    dup = jnp.concatenate([da, db], axis=1)
    dh2 = _apply_wt(dup, w, "w_up", f32, f"up_dx{tag}")
    gw["w_up"] = _mm(s["h2"].T, dup, bf16, f"up_dw{tag}")
    dx1, d_g, d_sc2, d_sh2 = _rowwise_bwd(_f_prenorm, [s["x1"]], [vec(p["g_pre_ffn"]), sc2, sh2], [[dh2]], [f32], True,
                                          256, f"prenorm_ffn_bwd{tag}", add_to={0: dx2})
    gp["g_pre_ffn"] = d_g
    dy, d_gpm, d_gt1 = _rowwise_bwd(_f_postres, [x, s["y"]], [vec(p["g_post_mix"]), gt1], [[dx1]], [None, bf16], True,
                                    256, f"postres_mix_bwd{tag}")
    gp["g_post_mix"] = d_gpm
    dmerged = _apply_wt(dy, w, "w_o", f32, f"o_dx{tag}")
    gw["w_o"] = _mm(s["merged"].T, dy, bf16, f"o_dw{tag}")
    res = _rowwise_bwd(_f_merge, s["pg"] + [s["y_ssm"], s["y_attn"], s["y_conv"]], s["bg"], [[dmerged]], [bf16] * 6, True,
                       128, f"merge_bwd{tag}")
    dpg, (dys, dya, dyv), dbg = list(res[0:3]), res[3:6], res[6:9]
    gp["b_gate"] = jnp.concatenate(dbg, axis=1)
    ds_ssm = _apply_wt(dys, w, "w_ssm_out", f32, f"ssm_proj_dx{tag}")
    gw["w_ssm_out"] = _mm(s["s_ssm"].T, dys, bf16, f"ssm_proj_dw{tag}")
    dattn_o = _apply_wt(dya, w, "w_attn_out", f32, f"attn_proj_dx{tag}")
    gw["w_attn_out"] = _mm(s["attn_o"].T, dya, bf16, f"attn_proj_dw{tag}")
    dcm = _apply_wt(dyv, w, "w_conv_out", f32, f"conv_proj_dx{tag}")
    gw["w_conv_out"] = _mm(s["cm"].T, dyv, bf16, f"conv_proj_dw{tag}")
    ssm = s["ssm"]
    tc, n_groups = ssm["tc"], ssm["n_groups"]
    dg_a, dz, gp["b_glu"] = _rowwise_bwd(_f_glu, [s["g"], s["z"]], [vec(p["b_glu"])], [[ds_ssm]], [f32, bf16], True,
                                         256, f"glu_bwd{tag}")
    dg_b = _apply_wt(dz, w, "w_glu", f32, f"glu_proj_dx{tag}")
    gw["w_glu"] = _mm(s["g"].T, dz, bf16, f"glu_proj_dw{tag}")
    dyc, du_skip, gp["ssm_d"] = _rowwise_bwd(_f_gelu, [s["yc"], s["u32"]], [vec(p["ssm_d"])], [[dg_a, dg_b]], [bf16, f32], True,
                                             256, f"ssm_gelu_bwd{tag}")
    dh_state = _mm(dyc, ssm["cc"], f32, f"ssm_out_dx{tag}")
    g_cc = _bd_fold(_mm(dyc.T, s["hb"], f32, f"ssm_out_dw{tag}"), n_groups, tc, f"ssm_out_dw_fold{tag}")
    gf, gb = _scan(dh_state, ssm["lam"], True, f"ssm_scan_bwd{tag}")
    h_prev = jnp.concatenate([jnp.zeros((1, s["hf"].shape[1]), f32), s["hf"][:-1]], axis=0)
    d_lam = _lam_grad(gf, h_prev, f"ssm_lam_grad{tag}")
    du_x = _mm(gb, ssm["bb"], f32, f"ssm_in_dx{tag}", rhs_t=True)
    g_bb = _bd_fold(_mm(s["u"].T, gb, f32, f"ssm_in_dw{tag}"), n_groups, tc, f"ssm_in_dw_fold{tag}")
    (du,) = _rowwise(_f_add, [du_skip, du_x], [], [bf16], 256, f"ssm_du{tag}")
    d_lr, d_li = _deinterleave(d_lam, tc)
    g_bbr, g_bbi = _deinterleave(g_bb, tc)
    group_shape = (SSM_GROUP, n_groups, SSM_STATE)
    cots = (d_lr.reshape(n_groups, SSM_STATE), d_li.reshape(n_groups, SSM_STATE), g_bbr.reshape(group_shape), g_bbi.reshape(group_shape))
    d_ldt, d_ar, d_ai, d_br, d_bi = _disc_bwd(*ssm["disc_in"], cots, f"ssm_disc_bwd{tag}")
    gp["ssm_log_dt"], gp["ssm_a_re"], gp["ssm_a_im"] = d_ldt.reshape(-1), d_ar, d_ai
    gp["ssm_b_re"], gp["ssm_b_im"] = d_br.transpose(1, 2, 0), d_bi.transpose(1, 2, 0)
    g_ccr, g_cci = _deinterleave(g_cc, tc)
    gp["ssm_c_re"] = g_ccr.reshape(group_shape).transpose(1, 0, 2)
    gp["ssm_c_im"] = -g_cci.reshape(group_shape).transpose(1, 0, 2)
    res = _rowwise_bwd(_f_amerge, s["o_tok"] + s["lse_x"], [], [[dattn_o]], [f32] * 6, False, 256, f"attn_merge_bwd{tag}")
    do_b = _to_blocks(jnp.concatenate(res[0:3], axis=1), hpp)
    dl_b = _to_blocks(jnp.concatenate(res[3:6], axis=1), hpp)
    dqb, dkb, dvb = _attn_bwd(s["qb"], s["kb"], s["vb"], s["ob"], s["lseb"], do_b, dl_b, hpp, f"attn_bwd{tag}")
    dq, dk, dv = (jnp.concatenate(_from_blocks(t, hpp), axis=1) for t in (dqb, dkb, dvb))
    dcb, dcc, dch, gp["conv_mix_w"] = _convgate_bwd(s["proj"], p["conv_mix_w"], dcm, off_conv, w4, f"convgate_bwd{tag}")
    dproj = jnp.concatenate([du, dq, dk, dv, dcb, dcc, dch] + dpg, axis=1)
    dh = _apply_wt(dproj, w, "w_in", f32, f"in_dx{tag}")
    gw["w_in"] = _mm(s["h"].T, dproj, bf16, f"in_dw{tag}")
    dx0, d_g, d_sc1, d_sh1 = _rowwise_bwd(_f_prenorm, [x], [vec(p["g_pre_mix"]), sc1, sh1], [[dh]], [f32], True,
                                          256, f"prenorm_mix_bwd{tag}", add_to={0: dx1})
    gp["g_pre_mix"] = d_g
    return dx0, [d_sh1, d_sc1, d_gt1, d_sh2, d_sc2, d_gt2], gw, gp


COL_SHARDED = ("w_in", "w_up", "w_ssm_out", "w_attn_out", "w_conv_out")
ROW_SHARDED = ("w_down", "w_o", "w_glu")
SMALL_SHARDED = ("conv_mix_w", "ffn_conv_w")
REPLICATED = ("b_mod", "g_pre_mix", "g_post_mix", "g_pre_ffn", "g_post_ffn", "ssm_log_dt", "ssm_a_re", "ssm_a_im", "ssm_b_re",
              "ssm_b_im", "ssm_c_re", "ssm_c_im", "ssm_d", "b_glu", "b_gate")
WEIGHTS = ('w_mod', 'b_mod', 'g_pre_mix', 'g_post_mix', 'g_pre_ffn', 'g_post_ffn', 'w_in', 'ssm_log_dt', 'ssm_a_re', 'ssm_a_im',
           'ssm_b_re', 'ssm_b_im', 'ssm_c_re', 'ssm_c_im', 'ssm_d', 'w_glu', 'b_glu', 'conv_mix_w', 'w_ssm_out', 'w_attn_out',
           'w_conv_out', 'b_gate', 'w_o', 'w_up', 'ffn_conv_w', 'w_down')


def _full_cols(g):
    return g.transpose(1, 0, 2).reshape(g.shape[1], -1)


def _col_slabs(m):
    return m.reshape(m.shape[0], N_DEV, -1).transpose(1, 0, 2)


def _step(x, c, loss_target, wts, m_in, v_in):
    depth = wts["w_in"].shape[0]
    n_rows, d = x.shape[1], x.shape[2]
    me = _linear(_position())
    x2d, target2d = x[0], loss_target[0]

    (c_all,) = _all_gather([c], "gather_c")
    c_all = c_all.reshape(N_DEV, d)
    mod_cols = wts["w_mod"].shape[2]
    b_cols = lax.dynamic_slice_in_dim(wts["b_mod"], me * mod_cols, mod_cols, axis=1).reshape(depth, 1, mod_cols)
    mod_mine = _mod_fwd(c_all, wts["w_mod"], b_cols, "mod_fwd")
    (mod_all,) = _all_gather([mod_mine], "gather_mod")
    mod_me = lax.dynamic_index_in_dim(mod_all, me, axis=2, keepdims=False).transpose(1, 0, 2).reshape(depth, 6, 1, d)

    mats = []
    for l in range(depth):
        names = COL_SHARDED + ROW_SHARDED
        shards = [(wts[k][l].T if k in COL_SHARDED else wts[k][l]).astype(bf16) for k in names]
        got = _all_gather(shards, f"gather_w{l}")
        mats.append({k: g.reshape(-1, g.shape[2]) for k, g in zip(names, got)})
    (conv_mix_all, ffn_conv_all) = _all_gather([wts["conv_mix_w"], wts["ffn_conv_w"]], "gather_conv_w")
    small = []
    for l in range(depth):
        p = {k: wts[k][l] for k in REPLICATED if k != "b_mod"}
        p["conv_mix_w"] = _full_cols(conv_mix_all[:, l])
        p["ffn_conv_w"] = _full_cols(ffn_conv_all[:, l])
        small.append(p)

    saved = []
    h = x2d
    for l in range(depth):
        mod = [mod_me[l, i] for i in range(6)]
        h, s = _layer_fwd(h, mod, mats[l], small[l], f"_l{l}")
        saved.append(s)
    dy, loss_part = _loss_head(h, target2d, "loss_head")
    loss = lax.psum(loss_part[0, 0], ("x", "y", "c"))

    grads = {k: [None] * depth for k in WEIGHTS}
    dmod_rows = [None] * depth
    small_parts = [None] * depth
    for l in reversed(range(depth)):
        mod = [mod_me[l, i] for i in range(6)]
        dy, dmod, gw, gp = _layer_bwd(dy, saved[l], mod, mats[l], small[l], f"_l{l}")
        dmod_rows[l] = jnp.concatenate(dmod, axis=1)
        slabs = [(_col_slabs(gw[k]) if k in COL_SHARDED else gw[k].reshape(N_DEV, -1, gw[k].shape[1])) for k in COL_SHARDED + ROW_SHARDED]
        slabs += [_col_slabs(gp[k]) for k in SMALL_SHARDED]
        summed = _reduce_scatter(slabs, f"scatter_g{l}")
        for k, g in zip(COL_SHARDED + ROW_SHARDED + SMALL_SHARDED, summed):
            grads[k][l] = g
        small_parts[l] = gp
    grad_x = dy.reshape(x.shape)

    dmod_mine = jnp.concatenate(dmod_rows, axis=0)
    rep_names = [k for k in REPLICATED if k != "b_mod"]
    pieces = [dmod_mine.reshape(-1)] + [jnp.stack([small_parts[l][k].reshape(-1) for l in range(depth)]).reshape(-1) for k in rep_names]
    sizes = [int(pc.shape[0]) for pc in pieces]
    total = sum(sizes)
    padded = -(-total // (SUBLANES * LANES)) * (SUBLANES * LANES)
    pack = jnp.concatenate(pieces + [jnp.zeros((padded - total,), f32)]).reshape(-1, LANES)
    (pack_all,) = _all_gather([pack], "gather_small_grads")
    pack_sum = _sum_slabs(pack_all, "sum_small_grads").reshape(-1)
    offs = np.cumsum([0] + sizes)
    grads["b_mod"] = pack_sum[offs[0]:offs[1]].reshape(wts["b_mod"].shape)
    for i, k in enumerate(rep_names):
        grads[k] = pack_sum[offs[i + 1]:offs[i + 2]].reshape(wts[k].shape)
    dmod_all = pack_all.reshape(N_DEV, -1)[:, :sizes[0]].reshape(N_DEV, depth, 6 * d)
    dmod_cols = lax.dynamic_slice_in_dim(dmod_all, me * mod_cols, mod_cols, axis=2).transpose(1, 0, 2)
    grads["w_mod"] = _mod_wgrad(c_all.T, dmod_cols, "mod_wgrad")
    for k in COL_SHARDED + ROW_SHARDED + SMALL_SHARDED:
        grads[k] = jnp.stack(grads[k])

    delta, new_m, new_v = {}, {}, {}
    for k in WEIGHTS:
        delta[k], new_m[k], new_v[k] = _adamw(wts[k], grads[k], m_in[k], v_in[k], f"adamw_{k}")
    return (loss, grad_x, *[grads[k] for k in WEIGHTS], *[delta[k] for k in WEIGHTS], *[new_m[k] for k in WEIGHTS],
            *[new_v[k] for k in WEIGHTS])


def kernel(x, c, w_mod, b_mod, g_pre_mix, g_post_mix, g_pre_ffn, g_post_ffn, w_in, ssm_log_dt, ssm_a_re, ssm_a_im, ssm_b_re, ssm_b_im, ssm_c_re, ssm_c_im, ssm_d, w_glu, b_glu, conv_mix_w, w_ssm_out, w_attn_out, w_conv_out, b_gate, w_o, w_up, ffn_conv_w, w_down, loss_target, m_w_mod, m_b_mod, m_g_pre_mix, m_g_post_mix, m_g_pre_ffn, m_g_post_ffn, m_w_in, m_ssm_log_dt, m_ssm_a_re, m_ssm_a_im, m_ssm_b_re, m_ssm_b_im, m_ssm_c_re, m_ssm_c_im, m_ssm_d, m_w_glu, m_b_glu, m_conv_mix_w, m_w_ssm_out, m_w_attn_out, m_w_conv_out, m_b_gate, m_w_o, m_w_up, m_ffn_conv_w, m_w_down, v_w_mod, v_b_mod, v_g_pre_mix, v_g_post_mix, v_g_pre_ffn, v_g_post_ffn, v_w_in, v_ssm_log_dt, v_ssm_a_re, v_ssm_a_im, v_ssm_b_re, v_ssm_b_im, v_ssm_c_re, v_ssm_c_im, v_ssm_d, v_w_glu, v_b_glu, v_conv_mix_w, v_w_ssm_out, v_w_attn_out, v_w_conv_out, v_b_gate, v_w_o, v_w_up, v_ffn_conv_w, v_w_down):
    wts = dict(w_mod=w_mod, b_mod=b_mod, g_pre_mix=g_pre_mix, g_post_mix=g_post_mix, g_pre_ffn=g_pre_ffn, g_post_ffn=g_post_ffn, w_in=w_in, ssm_log_dt=ssm_log_dt, ssm_a_re=ssm_a_re, ssm_a_im=ssm_a_im, ssm_b_re=ssm_b_re, ssm_b_im=ssm_b_im, ssm_c_re=ssm_c_re, ssm_c_im=ssm_c_im, ssm_d=ssm_d, w_glu=w_glu, b_glu=b_glu, conv_mix_w=conv_mix_w, w_ssm_out=w_ssm_out, w_attn_out=w_attn_out, w_conv_out=w_conv_out, b_gate=b_gate, w_o=w_o, w_up=w_up, ffn_conv_w=ffn_conv_w, w_down=w_down)
    m_in = dict(w_mod=m_w_mod, b_mod=m_b_mod, g_pre_mix=m_g_pre_mix, g_post_mix=m_g_post_mix, g_pre_ffn=m_g_pre_ffn, g_post_ffn=m_g_post_ffn, w_in=m_w_in, ssm_log_dt=m_ssm_log_dt, ssm_a_re=m_ssm_a_re, ssm_a_im=m_ssm_a_im, ssm_b_re=m_ssm_b_re, ssm_b_im=m_ssm_b_im, ssm_c_re=m_ssm_c_re, ssm_c_im=m_ssm_c_im, ssm_d=m_ssm_d, w_glu=m_w_glu, b_glu=m_b_glu, conv_mix_w=m_conv_mix_w, w_ssm_out=m_w_ssm_out, w_attn_out=m_w_attn_out, w_conv_out=m_w_conv_out, b_gate=m_b_gate, w_o=m_w_o, w_up=m_w_up, ffn_conv_w=m_ffn_conv_w, w_down=m_w_down)
    v_in = dict(w_mod=v_w_mod, b_mod=v_b_mod, g_pre_mix=v_g_pre_mix, g_post_mix=v_g_post_mix, g_pre_ffn=v_g_pre_ffn, g_post_ffn=v_g_post_ffn, w_in=v_w_in, ssm_log_dt=v_ssm_log_dt, ssm_a_re=v_ssm_a_re, ssm_a_im=v_ssm_a_im, ssm_b_re=v_ssm_b_re, ssm_b_im=v_ssm_b_im, ssm_c_re=v_ssm_c_re, ssm_c_im=v_ssm_c_im, ssm_d=v_ssm_d, w_glu=v_w_glu, b_glu=v_b_glu, conv_mix_w=v_conv_mix_w, w_ssm_out=v_w_ssm_out, w_attn_out=v_w_attn_out, w_conv_out=v_w_conv_out, b_gate=v_b_gate, w_o=v_w_o, w_up=v_w_up, ffn_conv_w=v_ffn_conv_w, w_down=v_w_down)
    return _step(x, c, loss_target, wts, m_in, v_in)
```

```python
import functools
import math

import numpy as np
import jax
import jax.numpy as jnp
from jax import lax
from jax.experimental import pallas as pl
from jax.experimental.pallas import tpu as pltpu

f32 = jnp.float32
bf16 = jnp.bfloat16

N_DEV = 8
N_CHIPS = 4
V7X_VMEM_LIMIT_BYTES = 56 * 1024 * 1024
LANES = 128
SUBLANES = 8

RMS_EPS = 1e-6
NEG_INF = -1e30
HEAD_DIM = 64
SSM_GROUP = 16
SSM_STATE = 64
DSWA_PATTERNS = ((128, 1), (512, 4), (2048, 16))
QBLK = 128

ADAM_LR = 0.001
ADAM_B1 = 0.9
ADAM_B2 = 0.999
ADAM_EPS = 1e-08
ADAM_WD = 0.01
ADAM_STEP = 10

MESH = pl.DeviceIdType.MESH
ANY = pl.BlockSpec(memory_space=pl.ANY)


def _tile(n, pref, mult=LANES):
    t = min(pref, n) // mult * mult
    while t >= mult:
        if n % t == 0:
            return t
        t -= mult
    return n


def _params(*sem):
    return pltpu.CompilerParams(dimension_semantics=sem, vmem_limit_bytes=V7X_VMEM_LIMIT_BYTES)


def _mm(a, b, out_dtype, name, rhs_t=False):
    m, k = a.shape
    n, k2 = b.shape if rhs_t else b.shape[::-1]
    assert k == k2 and a.dtype == bf16 and b.dtype == bf16, (a.shape, b.shape, a.dtype, b.dtype)
    tm, tn, tk = _tile(m, 1024, SUBLANES), _tile(n, 512), _tile(k, 2048)
    nk = k // tk
    contract = (((1,), (1 if rhs_t else 0,)), ((), ()))

    def body(a_ref, b_ref, o_ref, acc_ref):
        kk = pl.program_id(2)

        @pl.when(kk == 0)
        def _():
            acc_ref[...] = jnp.zeros_like(acc_ref)

        acc_ref[...] += lax.dot_general(a_ref[...], b_ref[...], contract, preferred_element_type=f32)

        @pl.when(kk == nk - 1)
        def _():
            o_ref[...] = acc_ref[...].astype(o_ref.dtype)

    b_spec = pl.BlockSpec((tn, tk), lambda i, j, kk: (j, kk)) if rhs_t else pl.BlockSpec((tk, tn), lambda i, j, kk: (kk, j))
    return pl.pallas_call(
        body, grid=(m // tm, n // tn, nk),
        in_specs=[pl.BlockSpec((tm, tk), lambda i, j, kk: (i, kk)), b_spec],
        out_specs=pl.BlockSpec((tm, tn), lambda i, j, kk: (i, j)),
        out_shape=jax.ShapeDtypeStruct((m, n), out_dtype),
        scratch_shapes=[pltpu.VMEM((tm, tn), f32)],
        compiler_params=_params("parallel", "parallel", "arbitrary"), name=name)(a, b)


def _rowwise(fn, rows, vecs, out_dtypes, tm, name):
    n_rows = rows[0].shape[0]
    nr, nv = len(rows), len(vecs)
    outs = jax.eval_shape(fn, *[jax.ShapeDtypeStruct((tm, r.shape[1]), f32) for r in rows],
                          *[jax.ShapeDtypeStruct(v.shape, f32) for v in vecs])

    def body(*refs):
        vals = [r[...].astype(f32) for r in refs[:nr]] + [v[...] for v in refs[nr:nr + nv]]
        for o_ref, val in zip(refs[nr + nv:], fn(*vals)):
            o_ref[...] = val.astype(o_ref.dtype)

    res = pl.pallas_call(
        body, grid=(n_rows // tm,),
        in_specs=[pl.BlockSpec((tm, r.shape[1]), lambda i: (i, 0)) for r in rows]
        + [pl.BlockSpec(v.shape, lambda i: (0, 0)) for v in vecs],
        out_specs=[pl.BlockSpec((tm, o.shape[1]), lambda i: (i, 0)) for o in outs],
        out_shape=[jax.ShapeDtypeStruct((n_rows, o.shape[1]), dt) for o, dt in zip(outs, out_dtypes)],
        compiler_params=_params("parallel"), name=name)(*rows, *vecs)
    return res


def _rowwise_bwd(fn, rows, vecs, cots, row_dtypes, need_vecs, tm, name, add_to=None):
    n_rows = rows[0].shape[0]
    nr, nv = len(rows), len(vecs)
    add_to = add_to or {}
    flat_cots = [c for group in cots for c in group]
    add_keys = sorted(add_to)
    row_out = [i for i, dt in enumerate(row_dtypes) if dt is not None]
    n_in = nr + nv + len(flat_cots) + len(add_keys)

    def body(*refs):
        vals = [r[...].astype(f32) for r in refs[:nr]] + [v[...] for v in refs[nr:nr + nv]]
        pos = nr + nv
        cvals = []
        for group in cots:
            acc = refs[pos][...].astype(f32)
            for extra in range(1, len(group)):
                acc = acc + refs[pos + extra][...].astype(f32)
            pos += len(group)
            cvals.append(acc)
        adds = {key: refs[pos + j] for j, key in enumerate(add_keys)}
        _, vjp = jax.vjp(fn, *vals)
        grads = vjp(tuple(cvals))
        out_refs = refs[n_in:]
        for j, i in enumerate(row_out):
            g = grads[i]
            if i in adds:
                g = g + adds[i][...].astype(f32)
            out_refs[j][...] = g.astype(out_refs[j].dtype)
        if need_vecs:
            step = pl.program_id(0)
            for j in range(nv):
                acc_ref = out_refs[len(row_out) + j]

                @pl.when(step == 0)
                def _(acc_ref=acc_ref):
                    acc_ref[...] = jnp.zeros_like(acc_ref)

                acc_ref[...] += grads[nr + j]

    row_spec = lambda a: pl.BlockSpec((tm, a.shape[1]), lambda i: (i, 0))
    in_specs = ([row_spec(r) for r in rows] + [pl.BlockSpec(v.shape, lambda i: (0, 0)) for v in vecs]
                + [row_spec(c) for c in flat_cots] + [row_spec(add_to[key]) for key in add_keys])
    out_specs = [row_spec(rows[i]) for i in row_out]
    out_shape = [jax.ShapeDtypeStruct(rows[i].shape, row_dtypes[i]) for i in row_out]
    if need_vecs:
        out_specs += [pl.BlockSpec(v.shape, lambda i: (0, 0)) for v in vecs]
        out_shape += [jax.ShapeDtypeStruct(v.shape, f32) for v in vecs]
    return pl.pallas_call(
        body, grid=(n_rows // tm,), in_specs=in_specs, out_specs=out_specs, out_shape=out_shape,
        compiler_params=_params("arbitrary"), name=name)(*rows, *vecs, *flat_cots, *[add_to[key] for key in add_keys])


def _f_prenorm(x, g, sc, sh):
    y = x * lax.rsqrt(jnp.mean(x * x, axis=-1, keepdims=True) + RMS_EPS)
    return (y * g * (1.0 + sc) + sh,)


def _f_postres(x, y, g, gt):
    n = y * lax.rsqrt(jnp.mean(y * y, axis=-1, keepdims=True) + RMS_EPS)
    return (x + gt * (n * g),)


def _f_gelu(yc, u, d):
    return (jax.nn.gelu(yc + d * u),)


def _f_glu(g, z, b):
    return (g * jax.nn.sigmoid(z + b),)


def _f_merge(p0, p1, p2, ys, ya, yc, b0, b1, b2):
    return (jax.nn.sigmoid(p0 + b0) * ys + jax.nn.sigmoid(p1 + b1) * ya + jax.nn.sigmoid(p2 + b2) * yc,)


def _f_amerge(o0, o1, o2, l0, l1, l2):
    m = jnp.maximum(jnp.maximum(l0, l1), l2)
    e0, e1, e2 = jnp.exp(l0 - m), jnp.exp(l1 - m), jnp.exp(l2 - m)
    return ((e0 * o0 + e1 * o1 + e2 * o2) / (e0 + e1 + e2),)


def _f_add(a, b):
    return (a + b,)


def _shift_down(z, k, row):
    return jnp.where(row >= k, pltpu.roll(z, k, 0), 0.0)


def _shift_up(z, k, row):
    n = z.shape[0]
    return jnp.where(row < n - k, pltpu.roll(z, n - k, 0), 0.0)


def _conv3(z, w, row):
    return w[0:1, :] * z + w[1:2, :] * _shift_down(z, 1, row) + w[2:3, :] * _shift_down(z, 2, row)


def _conv3_bwd(z, w, dy, row):
    dz = w[0:1, :] * dy + w[1:2, :] * _shift_up(dy, 1, row) + w[2:3, :] * _shift_up(dy, 2, row)
    dw = [jnp.sum(dy * z, axis=0, keepdims=True),
          jnp.sum(dy * _shift_down(z, 1, row), axis=0, keepdims=True),
          jnp.sum(dy * _shift_down(z, 2, row), axis=0, keepdims=True)]
    return dz, dw


def _convgate_fwd(proj, w, off, width, name):
    n_rows = proj.shape[0]
    tc = _tile(width, 256)
    nb = width // tc
    base = off // tc

    def body(b_ref, c_ref, h_ref, w_ref, o_ref):
        row = lax.broadcasted_iota(jnp.int32, (n_rows, tc), 0)
        o_ref[...] = (b_ref[...] * _conv3(c_ref[...] * h_ref[...], w_ref[...], row)).astype(o_ref.dtype)

    col = lambda s: pl.BlockSpec((n_rows, tc), lambda j: (0, base + s * nb + j))
    return pl.pallas_call(
        body, grid=(nb,), in_specs=[col(0), col(1), col(2), pl.BlockSpec((3, tc), lambda j: (0, j))],
        out_specs=pl.BlockSpec((n_rows, tc), lambda j: (0, j)),
        out_shape=jax.ShapeDtypeStruct((n_rows, width), bf16),
        compiler_params=_params("parallel"), name=name)(proj, proj, proj, w)


def _convgate_bwd(proj, w, dcm, off, width, name):
    n_rows = proj.shape[0]
    tc = _tile(width, 128)
    nb = width // tc
    base = off // tc

    def body(b_ref, c_ref, h_ref, w_ref, d_ref, db_ref, dc_ref, dh_ref, dw_ref):
        row = lax.broadcasted_iota(jnp.int32, (n_rows, tc), 0)
        cb, cc, ch, wv, d = b_ref[...], c_ref[...], h_ref[...], w_ref[...], d_ref[...].astype(f32)
        z = cc * ch
        db_ref[...] = (d * _conv3(z, wv, row)).astype(db_ref.dtype)
        dz, dw = _conv3_bwd(z, wv, d * cb, row)
        dc_ref[...] = (dz * ch).astype(dc_ref.dtype)
        dh_ref[...] = (dz * cc).astype(dh_ref.dtype)
        for k in range(3):
            dw_ref[k:k + 1, :] = dw[k]

    col = lambda s: pl.BlockSpec((n_rows, tc), lambda j: (0, base + s * nb + j))
    own = pl.BlockSpec((n_rows, tc), lambda j: (0, j))
    wspec = pl.BlockSpec((3, tc), lambda j: (0, j))
    sec = jax.ShapeDtypeStruct((n_rows, width), bf16)
    return pl.pallas_call(
        body, grid=(nb,), in_specs=[col(0), col(1), col(2), wspec, own], out_specs=[own, own, own, wspec],
        out_shape=[sec, sec, sec, jax.ShapeDtypeStruct((3, width), f32)],
        compiler_params=_params("parallel"), name=name)(proj, proj, proj, w, dcm)


def _ffnconv_fwd(up, w, name):
    n_rows, two_f = up.shape
    half = two_f // 2
    tc = _tile(half, 256)
    nb = half // tc

    def body(a_ref, b_ref, wa_ref, wb_ref, o_ref):
        row = lax.broadcasted_iota(jnp.int32, (n_rows, tc), 0)
        ua = _conv3(a_ref[...], wa_ref[...], row)
        ub = _conv3(b_ref[...], wb_ref[...], row)
        o_ref[...] = (jax.nn.silu(ua) * ub).astype(o_ref.dtype)

    return pl.pallas_call(
        body, grid=(nb,),
        in_specs=[pl.BlockSpec((n_rows, tc), lambda j: (0, j)), pl.BlockSpec((n_rows, tc), lambda j: (0, nb + j)),
                  pl.BlockSpec((3, tc), lambda j: (0, j)), pl.BlockSpec((3, tc), lambda j: (0, nb + j))],
        out_specs=pl.BlockSpec((n_rows, tc), lambda j: (0, j)),
        out_shape=jax.ShapeDtypeStruct((n_rows, half), bf16),
        compiler_params=_params("parallel"), name=name)(up, up, w, w)


def _ffnconv_bwd(up, w, dact, name):
    n_rows, two_f = up.shape
    half = two_f // 2
    tc = _tile(half, 128)
    nb = half // tc

    def body(a_ref, b_ref, wa_ref, wb_ref, d_ref, da_ref, db_ref, dwa_ref, dwb_ref):
        row = lax.broadcasted_iota(jnp.int32, (n_rows, tc), 0)
        a, b, wa, wb, d = a_ref[...], b_ref[...], wa_ref[...], wb_ref[...], d_ref[...].astype(f32)
        ua = _conv3(a, wa, row)
        ub = _conv3(b, wb, row)
        sg = jax.nn.sigmoid(ua)
        d_ua = d * ub * (sg * (1.0 + ua * (1.0 - sg)))
        d_ub = d * (ua * sg)
        da, dwa = _conv3_bwd(a, wa, d_ua, row)
        db, dwb = _conv3_bwd(b, wb, d_ub, row)
        da_ref[...] = da.astype(da_ref.dtype)
        db_ref[...] = db.astype(db_ref.dtype)
        for k in range(3):
            dwa_ref[k:k + 1, :] = dwa[k]
            dwb_ref[k:k + 1, :] = dwb[k]

    lo = pl.BlockSpec((n_rows, tc), lambda j: (0, j))
    hi = pl.BlockSpec((n_rows, tc), lambda j: (0, nb + j))
    wlo = pl.BlockSpec((3, tc), lambda j: (0, j))
    whi = pl.BlockSpec((3, tc), lambda j: (0, nb + j))
    sec = jax.ShapeDtypeStruct((n_rows, half), bf16)
    wsec = jax.ShapeDtypeStruct((3, half), f32)
    da, db, dwa, dwb = pl.pallas_call(
        body, grid=(nb,), in_specs=[lo, hi, wlo, whi, lo], out_specs=[lo, lo, wlo, wlo],
        out_shape=[sec, sec, wsec, wsec], compiler_params=_params("parallel"), name=name)(up, up, w, w, dact)
    return da, db, jnp.concatenate([dwa, dwb], axis=1)


SCAN_TC = 256


def _cmul(ar, ai, br, bi):
    return ar * br - ai * bi, ar * bi + ai * br


def _scan(x, lam, adjoint, name):
    n_rows, two_s = x.shape
    tc = min(SCAN_TC, two_s // 2)
    n_tiles = two_s // (2 * tc)
    n_chunks = n_rows // SUBLANES

    def body(x_ref, lam_ref, hf_ref, hb_ref):
        lr = lam_ref[:, 0:tc]
        li = lam_ref[:, tc:2 * tc]
        if adjoint:
            li = -li
        row = lax.broadcasted_iota(jnp.int32, (SUBLANES, tc), 0)
        powers = [(lr, li)]
        for _ in range(SUBLANES - 1):
            powers.append(_cmul(powers[-1][0], powers[-1][1], lr, li))
        pr = jnp.zeros((SUBLANES, tc), f32)
        pi = jnp.zeros((SUBLANES, tc), f32)
        for t in range(SUBLANES):
            e = (SUBLANES - 1 - t) if adjoint else t
            pr = jnp.where(row == t, powers[e][0], pr)
            pi = jnp.where(row == t, powers[e][1], pi)
        last = 0 if adjoint else SUBLANES - 1

        def chunk(i, carry):
            cr, ci = carry
            c = (n_chunks - 1 - i) if adjoint else i
            rows = pl.ds(pl.multiple_of(c * SUBLANES, SUBLANES), SUBLANES)
            xr = x_ref[rows, 0:tc]
            xi = x_ref[rows, tc:2 * tc]
            for k in (1, 2, 4):
                ar, ai = powers[k - 1]
                if adjoint:
                    sr = jnp.where(row < SUBLANES - k, pltpu.roll(xr, SUBLANES - k, 0), 0.0)
                    si = jnp.where(row < SUBLANES - k, pltpu.roll(xi, SUBLANES - k, 0), 0.0)
                else:
                    sr = jnp.where(row >= k, pltpu.roll(xr, k, 0), 0.0)
                    si = jnp.where(row >= k, pltpu.roll(xi, k, 0), 0.0)
                xr, xi = xr + ar * sr - ai * si, xi + ar * si + ai * sr
            hr = xr + pr * cr - pi * ci
            hi = xi + pr * ci + pi * cr
            hf_ref[rows, 0:tc] = hr
            hf_ref[rows, tc:2 * tc] = hi
            hb_ref[rows, 0:tc] = hr.astype(bf16)
            hb_ref[rows, tc:2 * tc] = hi.astype(bf16)
            return (jnp.sum(jnp.where(row == last, hr, 0.0), axis=0, keepdims=True),
                    jnp.sum(jnp.where(row == last, hi, 0.0), axis=0, keepdims=True))

        zero = jnp.zeros((1, tc), f32)
        lax.fori_loop(0, n_chunks, chunk, (zero, zero))

    blk = pl.BlockSpec((n_rows, 2 * tc), lambda j: (0, j))
    return pl.pallas_call(
        body, grid=(n_tiles,), in_specs=[blk, pl.BlockSpec((1, 2 * tc), lambda j: (0, j))], out_specs=[blk, blk],
        out_shape=[jax.ShapeDtypeStruct(x.shape, f32), jax.ShapeDtypeStruct(x.shape, bf16)],
        compiler_params=_params("parallel"), name=name)(x, lam)


def _lam_grad(g, h, name):
    n_rows, two_s = g.shape
    tc = min(SCAN_TC, two_s // 2)

    def body(g_ref, h_ref, o_ref):
        row = lax.broadcasted_iota(jnp.int32, (n_rows, tc), 0)
        gr, gi = g_ref[:, 0:tc], g_ref[:, tc:2 * tc]
        hr, hi = _shift_down(h_ref[:, 0:tc], 1, row), _shift_down(h_ref[:, tc:2 * tc], 1, row)
        o_ref[:, 0:tc] = jnp.sum(gr * hr + gi * hi, axis=0, keepdims=True)
        o_ref[:, tc:2 * tc] = jnp.sum(gi * hr - gr * hi, axis=0, keepdims=True)

    blk = pl.BlockSpec((n_rows, 2 * tc), lambda j: (0, j))
    return pl.pallas_call(
        body, grid=(two_s // (2 * tc),), in_specs=[blk, blk], out_specs=pl.BlockSpec((1, 2 * tc), lambda j: (0, j)),
        out_shape=jax.ShapeDtypeStruct((1, two_s), f32), compiler_params=_params("parallel"), name=name)(g, h)


def _interleave(re, im, tc):
    lead = re.shape[:-1]
    s = re.shape[-1]
    return jnp.concatenate([re.reshape(*lead, s // tc, 1, tc), im.reshape(*lead, s // tc, 1, tc)], axis=-2).reshape(*lead, 2 * s)


def _deinterleave(z, tc):
    lead = z.shape[:-1]
    s = z.shape[-1] // 2
    z = z.reshape(*lead, s // tc, 2, tc)
    return z[..., 0, :].reshape(*lead, s), z[..., 1, :].reshape(*lead, s)


def _f_disc(log_dt, ar, ai, br, bi):
    dt = jnp.exp(log_dt)
    mag = jnp.exp(ar * dt)
    lr, li = mag * jnp.cos(ai * dt), mag * jnp.sin(ai * dt)
    den = ar * ar + ai * ai
    fr = ((lr - 1.0) * ar + li * ai) / den
    fi = (li * ar - (lr - 1.0) * ai) / den
    return lr, li, fr[None] * br - fi[None] * bi, fr[None] * bi + fi[None] * br


def _disc_fwd(log_dt, ar, ai, br, bi, name):
    def body(*refs):
        for o_ref, val in zip(refs[5:], _f_disc(*[r[...] for r in refs[:5]])):
            o_ref[...] = val

    sd = jax.ShapeDtypeStruct
    return pl.pallas_call(body, out_shape=[sd(ar.shape, f32), sd(ar.shape, f32), sd(br.shape, f32), sd(br.shape, f32)],
                          name=name)(log_dt, ar, ai, br, bi)


def _disc_bwd(log_dt, ar, ai, br, bi, cots, name):
    def body(*refs):
        _, vjp = jax.vjp(_f_disc, *[r[...] for r in refs[:5]])
        for o_ref, val in zip(refs[9:], vjp(tuple(r[...] for r in refs[5:9]))):
            o_ref[...] = val

    sd = jax.ShapeDtypeStruct
    return pl.pallas_call(
        body, out_shape=[sd(log_dt.shape, f32), sd(ar.shape, f32), sd(ar.shape, f32), sd(br.shape, f32), sd(br.shape, f32)],
        name=name)(log_dt, ar, ai, br, bi, *cots)


def _alibi_bias(n_heads_per_pattern):
    n_heads = len(DSWA_PATTERNS) * n_heads_per_pattern
    slopes = np.array([2.0 ** (-8.0 * (h + 1) / n_heads) for h in range(n_heads)], dtype=np.float32)
    qi = np.arange(QBLK)[:, None]
    kj = np.arange(QBLK)[None, :]
    cur, prev = [], []
    for h in range(n_heads):
        dil = DSWA_PATTERNS[h // n_heads_per_pattern][1]
        d_cur = qi - kj
        d_prev = QBLK + qi - kj
        cur.append(np.where(d_cur >= 0, -slopes[h] * (d_cur * dil).astype(np.float32), NEG_INF))
        prev.append(np.where(d_prev <= QBLK, -slopes[h] * (d_prev * dil).astype(np.float32), NEG_INF))
    return np.stack(cur).astype(np.float32), np.stack(prev).astype(np.float32)


def _blocks_per_residue(head, n_heads_per_pattern, n_blocks):
    pattern = head // n_heads_per_pattern
    out = n_blocks // DSWA_PATTERNS[-1][1]
    for p in range(len(DSWA_PATTERNS) - 2, -1, -1):
        out = jnp.where(pattern == p, n_blocks // DSWA_PATTERNS[p][1], out)
    return out


def _qk(a, b):
    return jnp.einsum('bqe,bke->bqk', a, b, preferred_element_type=f32)


def _pv(p, v):
    return jnp.einsum('bqk,bke->bqe', p.astype(bf16), v, preferred_element_type=f32)


def _ptq(p, a):
    return jnp.einsum('bqk,bqe->bke', p.astype(bf16), a, preferred_element_type=f32)


def _prev_block(t):
    return jnp.concatenate([jnp.zeros((1,) + t.shape[1:], t.dtype), t[:-1]], axis=0)


def _next_block(t):
    return jnp.concatenate([t[1:], jnp.zeros((1,) + t.shape[1:], t.dtype)], axis=0)


def _has_prev(head, hpp, n_blocks):
    blk = lax.broadcasted_iota(jnp.int32, (n_blocks, 1, 1), 0)
    return (blk & (_blocks_per_residue(head, hpp, n_blocks) - 1)) > 0


def _attn_fwd(q, k, v, hpp, name):
    n_heads, n_blocks = q.shape[0], q.shape[1]
    bias_cur, bias_prev = _alibi_bias(hpp)
    scale = HEAD_DIM ** -0.5

    def body(q_ref, k_ref, v_ref, bc_ref, bp_ref, o_ref, lse_ref):
        has_prev = _has_prev(pl.program_id(0), hpp, n_blocks)
        qv, kv, vv = q_ref[...], k_ref[...], v_ref[...]
        s_c = _qk(qv, kv) * scale + bc_ref[...][None]
        s_p = jnp.where(has_prev, _qk(qv, _prev_block(kv)) * scale + bp_ref[...][None], NEG_INF)
        m = jnp.maximum(jnp.max(s_c, axis=-1, keepdims=True), jnp.max(s_p, axis=-1, keepdims=True))
        p_c = jnp.exp(s_c - m)
        p_p = jnp.exp(s_p - m)
        l = jnp.sum(p_c, axis=-1, keepdims=True) + jnp.sum(p_p, axis=-1, keepdims=True)
        o_ref[...] = (_pv(p_c, vv) + _pv(p_p, _prev_block(vv))) / l
        lse_ref[...] = m + jnp.log(l)

    head = pl.BlockSpec((None, n_blocks, QBLK, HEAD_DIM), lambda a: (a, 0, 0, 0))
    bias = pl.BlockSpec((None, QBLK, QBLK), lambda a: (a, 0, 0))
    return pl.pallas_call(
        body, grid=(n_heads,), in_specs=[head, head, head, bias, bias],
        out_specs=[head, pl.BlockSpec((None, n_blocks, QBLK, 1), lambda a: (a, 0, 0, 0))],
        out_shape=[jax.ShapeDtypeStruct(q.shape, f32), jax.ShapeDtypeStruct(q.shape[:3] + (1,), f32)],
        compiler_params=_params("parallel"), name=name)(q, k, v, jnp.asarray(bias_cur), jnp.asarray(bias_prev))


def _attn_bwd(q, k, v, o, lse, do, dlse_x, hpp, name):
    n_heads, n_blocks = q.shape[0], q.shape[1]
    bias_cur, bias_prev = _alibi_bias(hpp)
    scale = HEAD_DIM ** -0.5

    def body(q_ref, k_ref, v_ref, o_ref, l_ref, do_ref, dl_ref, bc_ref, bp_ref, dq_ref, dk_ref, dv_ref):
        has_prev = _has_prev(pl.program_id(0), hpp, n_blocks)
        qv, kv, vv = q_ref[...], k_ref[...], v_ref[...]
        kp, vp = _prev_block(kv), _prev_block(vv)
        dov = do_ref[...]
        do_b = dov.astype(bf16)
        lse = l_ref[...]
        corr = jnp.sum(dl_ref[...], axis=-1, keepdims=True) - jnp.sum(dov * o_ref[...], axis=-1, keepdims=True)
        p_c = jnp.exp(_qk(qv, kv) * scale + bc_ref[...][None] - lse)
        ds_c = p_c * (_qk(do_b, vv) + corr)
        p_p = jnp.where(has_prev, jnp.exp(_qk(qv, kp) * scale + bp_ref[...][None] - lse), 0.0)
        ds_p = p_p * (_qk(do_b, vp) + corr)
        dq_ref[...] = ((_pv(ds_c, kv) + _pv(ds_p, kp)) * scale).astype(dq_ref.dtype)
        dk_ref[...] = ((_ptq(ds_c, qv) + _next_block(_ptq(ds_p, qv))) * scale).astype(dk_ref.dtype)
        dv_ref[...] = (_ptq(p_c, do_b) + _next_block(_ptq(p_p, do_b))).astype(dv_ref.dtype)

    head = pl.BlockSpec((None, n_blocks, QBLK, HEAD_DIM), lambda a: (a, 0, 0, 0))
    head1 = pl.BlockSpec((None, n_blocks, QBLK, 1), lambda a: (a, 0, 0, 0))
    bias = pl.BlockSpec((None, QBLK, QBLK), lambda a: (a, 0, 0))
    out = jax.ShapeDtypeStruct(q.shape, bf16)
    return pl.pallas_call(
        body, grid=(n_heads,), in_specs=[head, head, head, head, head1, head, head, bias, bias],
        out_specs=[head, head, head], out_shape=[out, out, out],
        compiler_params=_params("parallel"), name=name)(q, k, v, o, lse, do, dlse_x, jnp.asarray(bias_cur), jnp.asarray(bias_prev))


def _to_blocks(t, hpp):
    n_rows = t.shape[0]
    t = t.reshape(n_rows, len(DSWA_PATTERNS), hpp, HEAD_DIM)
    out = []
    for p, (_, dil) in enumerate(DSWA_PATTERNS):
        tp = t[:, p].reshape(n_rows // dil, dil, hpp, HEAD_DIM).transpose(2, 1, 0, 3)
        out.append(tp.reshape(hpp, n_rows // QBLK, QBLK, HEAD_DIM))
    return jnp.concatenate(out, axis=0)


def _from_blocks(t, hpp):
    n_blocks, last = t.shape[1], t.shape[3]
    n_rows = n_blocks * QBLK
    out = []
    for p, (_, dil) in enumerate(DSWA_PATTERNS):
        tp = t[p * hpp:(p + 1) * hpp].reshape(hpp, dil, n_rows // dil, last).transpose(2, 1, 0, 3)
        out.append(tp.reshape(n_rows, hpp * last))
    return out


def _loss_head(y, target, name):
    n_rows, d = y.shape
    tm = _tile(n_rows, 256, SUBLANES)

    def body(y_ref, t_ref, dy_ref, loss_ref):
        diff = y_ref[...] - t_ref[...]
        dy_ref[...] = diff * (1.0 / d)

        @pl.when(pl.program_id(0) == 0)
        def _():
            loss_ref[...] = jnp.zeros_like(loss_ref)

        loss_ref[...] += jnp.sum(jnp.sum(diff * diff, axis=1, keepdims=True), axis=0, keepdims=True) * (0.5 / d)

    row = pl.BlockSpec((tm, d), lambda i: (i, 0))
    return pl.pallas_call(
        body, grid=(n_rows // tm,), in_specs=[row, row], out_specs=[row, pl.BlockSpec((1, 1), lambda i: (0, 0))],
        out_shape=[jax.ShapeDtypeStruct(y.shape, f32), jax.ShapeDtypeStruct((1, 1), f32)],
        compiler_params=_params("arbitrary"), name=name)(y, target)


def _as2d(a):
    if a.ndim >= 2 and a.shape[-1] >= LANES:
        return a.reshape(-1, a.shape[-1])
    return a.reshape(-1, LANES) if a.size % LANES == 0 else a.reshape(1, -1)


ELEMENTWISE_BLOCK_BYTES = 2 << 20


def _row_tile(n_rows, n_cols):
    return _tile(n_rows, max(SUBLANES, ELEMENTWISE_BLOCK_BYTES // (4 * n_cols)), SUBLANES)


def _adamw(w, g, m, v, name):
    shape = w.shape
    w2, g2, m2, v2 = _as2d(w), _as2d(g), _as2d(m), _as2d(v)
    n_rows, n_cols = w2.shape
    tm = _row_tile(n_rows, n_cols)

    def body(w_ref, g_ref, m_ref, v_ref, d_ref, mo_ref, vo_ref):
        gv = g_ref[...]
        mn = ADAM_B1 * m_ref[...] + (1.0 - ADAM_B1) * gv
        vn = ADAM_B2 * v_ref[...] + (1.0 - ADAM_B2) * jnp.square(gv)
        m_hat = mn / (1.0 - ADAM_B1 ** ADAM_STEP)
        v_hat = vn / (1.0 - ADAM_B2 ** ADAM_STEP)
        d_ref[...] = -ADAM_LR * (m_hat / (jnp.sqrt(v_hat) + ADAM_EPS) + ADAM_WD * w_ref[...])
        mo_ref[...] = mn
        vo_ref[...] = vn

    row = pl.BlockSpec((tm, n_cols), lambda i: (i, 0))
    out = jax.ShapeDtypeStruct(w2.shape, f32)
    d, mn, vn = pl.pallas_call(body, grid=(n_rows // tm,), in_specs=[row] * 4, out_specs=[row] * 3, out_shape=[out] * 3,
                               compiler_params=_params("parallel"), name=name)(w2, g2, m2, v2)
    return d.reshape(shape), mn.reshape(shape), vn.reshape(shape)


def _sum_slabs(x, name):
    n = x.shape[0]
    shape = x.shape[1:]
    x3 = x.reshape(n, -1, shape[-1])
    n_rows, n_cols = x3.shape[1:]
    tm = _row_tile(n_rows, n_cols)

    def body(x_ref, o_ref):
        acc = x_ref[0].astype(f32)
        for s in range(1, n):
            acc = acc + x_ref[s].astype(f32)
        o_ref[...] = acc

    out = pl.pallas_call(
        body, grid=(n_rows // tm,), in_specs=[pl.BlockSpec((n, tm, n_cols), lambda i: (0, i, 0))],
        out_specs=pl.BlockSpec((tm, n_cols), lambda i: (i, 0)), out_shape=jax.ShapeDtypeStruct((n_rows, n_cols), f32),
        compiler_params=_params("parallel"), name=name)(x3)
    return out.reshape(shape)


def _mod_fwd(c_all, w_mod, b_cols, name):
    depth, d, cols = w_mod.shape
    tn = _tile(cols, 512)

    def body(c_ref, w_ref, b_ref, o_ref):
        cond = jax.nn.silu(c_ref[...])
        o_ref[...] = jnp.dot(cond, w_ref[...], preferred_element_type=f32, precision=lax.Precision.HIGHEST) + b_ref[...]

    return pl.pallas_call(
        body, grid=(depth, cols // tn),
        in_specs=[pl.BlockSpec((N_DEV, d), lambda l, j: (0, 0)), pl.BlockSpec((None, d, tn), lambda l, j: (l, 0, j)),
                  pl.BlockSpec((None, 1, tn), lambda l, j: (l, 0, j))],
        out_specs=pl.BlockSpec((None, N_DEV, tn), lambda l, j: (l, 0, j)),
        out_shape=jax.ShapeDtypeStruct((depth, N_DEV, cols), f32),
        compiler_params=_params("parallel", "parallel"), name=name)(c_all, w_mod, b_cols)


def _mod_wgrad(c_all_t, dmod_cols, name):
    d = c_all_t.shape[0]
    depth, _, cols = dmod_cols.shape
    tm = _tile(d, 256, SUBLANES)

    def body(c_ref, g_ref, o_ref):
        cond = jax.nn.silu(c_ref[...])
        o_ref[...] = jnp.dot(cond, g_ref[...], preferred_element_type=f32, precision=lax.Precision.HIGHEST)

    return pl.pallas_call(
        body, grid=(depth, d // tm),
        in_specs=[pl.BlockSpec((tm, N_DEV), lambda l, i: (i, 0)), pl.BlockSpec((None, N_DEV, cols), lambda l, i: (l, 0, 0))],
        out_specs=pl.BlockSpec((None, tm, cols), lambda l, i: (l, i, 0)),
        out_shape=jax.ShapeDtypeStruct((depth, d, cols), f32),
        compiler_params=_params("parallel", "parallel"), name=name)(c_all_t, dmod_cols)


def _position():
    return lax.axis_index("x"), lax.axis_index("y"), lax.axis_index("c")


def _linear(p):
    return 4 * p[0] + 2 * p[1] + p[2]


def _all_gather(xs, name):
    n = len(xs)

    def body(*refs):
        x_refs, o_refs = refs[:n], refs[n:2 * n]
        send_sems, recv_sems, local_sems = refs[2 * n:]
        x, y, c = _position()
        me, sibling = (x, y, c), (x, y, 1 - c)
        chips = [(1 - x, y), (x, 1 - y), (1 - x, 1 - y)]

        def copy(a, k, block, to, src=None):
            slab = o_refs[a].at[_linear(block)]
            return pltpu.make_async_remote_copy(
                src_ref=slab if src is None else src, dst_ref=slab, send_sem=send_sems.at[a, k], recv_sem=recv_sems.at[a, k],
                device_id=to, device_id_type=MESH)

        mine = [pltpu.make_async_copy(x_refs[a], o_refs[a].at[_linear(me)], local_sems.at[a]) for a in range(n)]
        first = []
        for a in range(n):
            mine[a].start()
            first.append(copy(a, 0, me, sibling, src=x_refs[a]))
            first += [copy(a, 1 + j, me, (*chip, c), src=x_refs[a]) for j, chip in enumerate(chips)]
        for cp in first:
            cp.start()
        passed = []
        for j, chip in enumerate(chips):
            for a in range(n):
                copy(a, 1 + j, (*chip, c), me).wait_recv()
                cp = copy(a, 4 + j, (*chip, c), sibling)
                cp.start()
                passed.append(cp)
        for a in range(n):
            copy(a, 0, sibling, me).wait_recv()
            for j, chip in enumerate(chips):
                copy(a, 4 + j, (*chip, 1 - c), me).wait_recv()
        for cp in first + passed:
            cp.wait_send()
        for cp in mine:
            cp.wait()

    return pl.pallas_call(
        body, in_specs=[ANY] * n, out_specs=[ANY] * n,
        out_shape=[jax.ShapeDtypeStruct((N_DEV,) + x.shape, x.dtype) for x in xs],
        scratch_shapes=[pltpu.SemaphoreType.DMA((n, 7)), pltpu.SemaphoreType.DMA((n, 7)), pltpu.SemaphoreType.DMA((n,))],
        name=name)(*xs)


def _pair_exchange(xs, name):
    n = len(xs)

    def body(*refs):
        x_refs, got_refs = refs[:n], refs[n:2 * n]
        send_sems, recv_sems = refs[2 * n:]
        x, y, c = _position()
        sends = []
        for a in range(n):
            for p in range(N_CHIPS):
                sends.append(pltpu.make_async_remote_copy(
                    src_ref=x_refs[a].at[p, 1 - c], dst_ref=got_refs[a].at[p], send_sem=send_sems.at[a, p],
                    recv_sem=recv_sems.at[a, p], device_id=(x, y, 1 - c), device_id_type=MESH))
        for cp in sends:
            cp.start()
        for cp in sends:
            cp.wait_recv()
        for cp in sends:
            cp.wait_send()

    sems = pltpu.SemaphoreType.DMA((n, N_CHIPS))
    return pl.pallas_call(body, in_specs=[ANY] * n, out_specs=[ANY] * n,
                          out_shape=[jax.ShapeDtypeStruct((N_CHIPS,) + x.shape[2:], x.dtype) for x in xs],
                          scratch_shapes=[sems, sems], name=name)(*xs)


def _chip_exchange(xs, name):
    n = len(xs)

    def body(*refs):
        x_refs, o_refs = refs[:n], refs[n:2 * n]
        send_sems, recv_sems = refs[2 * n:]
        x, y, c = _position()
        chips = [(1 - x, y), (x, 1 - y), (1 - x, 1 - y)]
        sends = []
        for a in range(n):
            for k, (px, py) in enumerate(chips):
                sends.append(pltpu.make_async_remote_copy(
                    src_ref=x_refs[a].at[2 * px + py], dst_ref=o_refs[a].at[k], send_sem=send_sems.at[a, k],
                    recv_sem=recv_sems.at[a, k], device_id=(px, py, c), device_id_type=MESH))
        for cp in sends:
            cp.start()
        for cp in sends:
            cp.wait_recv()
        for cp in sends:
            cp.wait_send()

    sems = pltpu.SemaphoreType.DMA((n, N_CHIPS - 1))
    return pl.pallas_call(body, in_specs=[ANY] * n, out_specs=[ANY] * n,
                          out_shape=[jax.ShapeDtypeStruct((N_CHIPS - 1,) + x.shape[1:], x.dtype) for x in xs],
                          scratch_shapes=[sems, sems], name=name)(*xs)


def _pick_add(picked, index, others, out_dtype, name):
    per_slab = picked.ndim == 4
    n_rows, n_cols = picked.shape[-2:]
    tm = _row_tile(n_rows, n_cols)
    n_other = 1 if per_slab else others.shape[0]

    def body(idx_ref, p_ref, o_ref, out_ref):
        acc = p_ref[...].astype(f32)
        if per_slab:
            acc = acc + o_ref[...].astype(f32)
        else:
            for k in range(n_other):
                acc = acc + o_ref[k].astype(f32)
        out_ref[...] = acc.astype(out_ref.dtype)

    if per_slab:
        grid = (picked.shape[0], n_rows // tm)
        in_specs = [pl.BlockSpec((None, None, tm, n_cols), lambda p, i, idx: (p, idx[0], i, 0)),
                    pl.BlockSpec((None, tm, n_cols), lambda p, i, idx: (p, i, 0))]
        out_specs = pl.BlockSpec((None, tm, n_cols), lambda p, i, idx: (p, i, 0))
        out_shape = jax.ShapeDtypeStruct((picked.shape[0], n_rows, n_cols), out_dtype)
    else:
        grid = (n_rows // tm,)
        in_specs = [pl.BlockSpec((None, tm, n_cols), lambda i, idx: (idx[0], i, 0)),
                    pl.BlockSpec((n_other, tm, n_cols), lambda i, idx: (0, i, 0))]
        out_specs = pl.BlockSpec((tm, n_cols), lambda i, idx: (i, 0))
        out_shape = jax.ShapeDtypeStruct((n_rows, n_cols), out_dtype)
    return pl.pallas_call(
        body, grid_spec=pltpu.PrefetchScalarGridSpec(num_scalar_prefetch=1, grid=grid, in_specs=in_specs, out_specs=out_specs),
        out_shape=out_shape, compiler_params=_params(*(["parallel"] * len(grid))), name=name)(
            jnp.reshape(index, (1,)).astype(jnp.int32), picked, others)


def _reduce_scatter(xs, name):
    x, y, c = _position()
    pairs = [v.reshape((N_CHIPS, 2) + v.shape[1:]) for v in xs]
    got = _pair_exchange(pairs, f"{name}_pair")
    chip_sums = [_pick_add(a, c, b, a.dtype, f"{name}_pair_sum{i}") for i, (a, b) in enumerate(zip(pairs, got))]
    arrived = _chip_exchange(chip_sums, f"{name}_chips")
    return [_pick_add(a, 2 * x + y, b, f32, f"{name}_sum{i}") for i, (a, b) in enumerate(zip(chip_sums, arrived))]


def _column_group(j, tc, shape):
    col = lax.broadcasted_iota(jnp.int32, shape, 1)
    return (j * tc + col % tc) // SSM_STATE


def _bd_build(bt, n_groups, tc, name):
    n_rows = n_groups * SSM_GROUP
    two_s = bt.shape[1]

    def body(b_ref, o_ref):
        grp = _column_group(pl.program_id(0), tc, (n_rows, 2 * tc))
        row = lax.broadcasted_iota(jnp.int32, (n_rows, 2 * tc), 0)
        tiled = jnp.concatenate([b_ref[...]] * n_groups, axis=0)
        o_ref[...] = jnp.where(grp == row // SSM_GROUP, tiled, 0.0).astype(o_ref.dtype)

    return pl.pallas_call(
        body, grid=(two_s // (2 * tc),), in_specs=[pl.BlockSpec((SSM_GROUP, 2 * tc), lambda j: (0, j))],
        out_specs=pl.BlockSpec((n_rows, 2 * tc), lambda j: (0, j)), out_shape=jax.ShapeDtypeStruct((n_rows, two_s), bf16),
        compiler_params=_params("parallel"), name=name)(bt)


def _bd_fold(m, n_groups, tc, name):
    n_rows, two_s = m.shape

    def body(m_ref, o_ref):
        grp = _column_group(pl.program_id(0), tc, (SSM_GROUP, 2 * tc))
        acc = jnp.zeros((SSM_GROUP, 2 * tc), f32)
        for g in range(n_groups):
            acc = acc + jnp.where(grp == g, m_ref[g * SSM_GROUP:(g + 1) * SSM_GROUP, :], 0.0)
        o_ref[...] = acc

    return pl.pallas_call(
        body, grid=(two_s // (2 * tc),), in_specs=[pl.BlockSpec((n_rows, 2 * tc), lambda j: (0, j))],
        out_specs=pl.BlockSpec((SSM_GROUP, 2 * tc), lambda j: (0, j)), out_shape=jax.ShapeDtypeStruct((SSM_GROUP, two_s), f32),
        compiler_params=_params("parallel"), name=name)(m)


def _ssm_setup(p, tag):
    n_groups = p["ssm_a_re"].shape[0]
    s = n_groups * SSM_STATE
    tc = min(SCAN_TC, s)
    log_dt = p["ssm_log_dt"].reshape(n_groups, 1)
    br_t = p["ssm_b_re"].transpose(2, 0, 1)
    bi_t = p["ssm_b_im"].transpose(2, 0, 1)
    disc_in = (log_dt, p["ssm_a_re"], p["ssm_a_im"], br_t, bi_t)
    lr, li, bbr, bbi = _disc_fwd(*disc_in, name=f"ssm_disc{tag}")
    lam = _interleave(lr.reshape(1, s), li.reshape(1, s), tc)
    bb = _bd_build(_interleave(bbr.reshape(SSM_GROUP, s), bbi.reshape(SSM_GROUP, s), tc), n_groups, tc, f"ssm_in_map{tag}")
    cr_t = p["ssm_c_re"].transpose(1, 0, 2).reshape(SSM_GROUP, s)
    ci_t = p["ssm_c_im"].transpose(1, 0, 2).reshape(SSM_GROUP, s)
    cc = _bd_build(_interleave(cr_t, -ci_t, tc), n_groups, tc, f"ssm_out_map{tag}")
    return dict(disc_in=disc_in, lam=lam, bb=bb, cc=cc, tc=tc, n_groups=n_groups)


def _apply_w(a, w, key, out_dtype, name):
    return _mm(a, w[key], out_dtype, name, rhs_t=key in COL_SHARDED)


def _apply_wt(dy, w, key, out_dtype, name):
    return _mm(dy, w[key], out_dtype, name, rhs_t=key not in COL_SHARDED)


def _layer_fwd(x, mod, w, p, tag):
    n_rows, d = x.shape
    w4 = d // 4
    hpp = w4 // HEAD_DIM
    off_q, off_k, off_v, off_conv, off_gate = w4, 4 * w4, 7 * w4, 10 * w4, 13 * w4
    sh1, sc1, gt1, sh2, sc2, gt2 = mod
    vec = lambda a: a.reshape(1, -1)
    s = dict(x=x)
    (s["h"],) = _rowwise(_f_prenorm, [x], [vec(p["g_pre_mix"]), sc1, sh1], [bf16], 256, f"prenorm_mix{tag}")
    proj = s["proj"] = _apply_w(s["h"], w, "w_in", f32, f"in_proj{tag}")
    ssm = s["ssm"] = _ssm_setup(p, tag)
    s["u32"] = proj[:, :off_q]
    s["u"] = s["u32"].astype(bf16)
    xin = _mm(s["u"], ssm["bb"], f32, f"ssm_in{tag}")
    s["hf"], s["hb"] = _scan(xin, ssm["lam"], False, f"ssm_scan{tag}")
    s["yc"] = _mm(s["hb"], ssm["cc"], f32, f"ssm_out{tag}", rhs_t=True)
    (s["g"],) = _rowwise(_f_gelu, [s["yc"], s["u32"]], [vec(p["ssm_d"])], [bf16], 256, f"ssm_gelu{tag}")
    s["z"] = _apply_w(s["g"], w, "w_glu", f32, f"glu_proj{tag}")
    (s["s_ssm"],) = _rowwise(_f_glu, [s["g"], s["z"]], [vec(p["b_glu"])], [bf16], 256, f"glu{tag}")
    s["y_ssm"] = _apply_w(s["s_ssm"], w, "w_ssm_out", f32, f"ssm_proj{tag}")
    s["qb"] = _to_blocks(proj[:, off_q:off_k].astype(bf16), hpp)
    s["kb"] = _to_blocks(proj[:, off_k:off_v].astype(bf16), hpp)
    s["vb"] = _to_blocks(proj[:, off_v:off_conv].astype(bf16), hpp)
    s["ob"], s["lseb"] = _attn_fwd(s["qb"], s["kb"], s["vb"], hpp, f"attn{tag}")
    s["o_tok"] = _from_blocks(s["ob"], hpp)
    s["lse_x"] = [jnp.repeat(l, HEAD_DIM, axis=1) for l in _from_blocks(s["lseb"], hpp)]
    (s["attn_o"],) = _rowwise(_f_amerge, s["o_tok"] + s["lse_x"], [], [bf16], 256, f"attn_merge{tag}")
    s["y_attn"] = _apply_w(s["attn_o"], w, "w_attn_out", f32, f"attn_proj{tag}")
    s["cm"] = _convgate_fwd(proj, p["conv_mix_w"], off_conv, w4, f"convgate{tag}")
    s["y_conv"] = _apply_w(s["cm"], w, "w_conv_out", f32, f"conv_proj{tag}")
    s["pg"] = [proj[:, off_gate + i * d:off_gate + (i + 1) * d] for i in range(3)]
    s["bg"] = [p["b_gate"][i * d:(i + 1) * d].reshape(1, d) for i in range(3)]
    (s["merged"],) = _rowwise(_f_merge, s["pg"] + [s["y_ssm"], s["y_attn"], s["y_conv"]], s["bg"], [bf16], 128, f"merge{tag}")
    s["y"] = _apply_w(s["merged"], w, "w_o", f32, f"o_proj{tag}")
    (s["x1"],) = _rowwise(_f_postres, [x, s["y"]], [vec(p["g_post_mix"]), gt1], [f32], 256, f"postres_mix{tag}")
    (s["h2"],) = _rowwise(_f_prenorm, [s["x1"]], [vec(p["g_pre_ffn"]), sc2, sh2], [bf16], 256, f"prenorm_ffn{tag}")
    s["up"] = _apply_w(s["h2"], w, "w_up", f32, f"up_proj{tag}")
    s["act"] = _ffnconv_fwd(s["up"], p["ffn_conv_w"], f"ffnconv{tag}")
    s["y2"] = _apply_w(s["act"], w, "w_down", f32, f"down_proj{tag}")
    (x2,) = _rowwise(_f_postres, [s["x1"], s["y2"]], [vec(p["g_post_ffn"]), gt2], [f32], 256, f"postres_ffn{tag}")
    return x2, s


def _layer_bwd(dx2, s, mod, w, p, tag):
    x = s["x"]
    n_rows, d = x.shape
    w4 = d // 4
    hpp = w4 // HEAD_DIM
    off_conv = 10 * w4
    sh1, sc1, gt1, sh2, sc2, gt2 = mod
    vec = lambda a: a.reshape(1, -1)
    gw, gp = {}, {}
    dy2, d_gpf, d_gt2 = _rowwise_bwd(_f_postres, [s["x1"], s["y2"]], [vec(p["g_post_ffn"]), gt2], [[dx2]], [None, bf16], True,
                                     256, f"postres_ffn_bwd{tag}")
    gp["g_post_ffn"] = d_gpf
    dact = _apply_wt(dy2, w, "w_down", bf16, f"down_dx{tag}")
    gw["w_down"] = _mm(s["act"].T, dy2, bf16, f"down_dw{tag}")
    da, db, gp["ffn_conv_w"] = _ffnconv_bwd(s["up"], p["ffn_conv_w"], dact, f"ffnconv_bwd{tag}")
    dup = jnp.concatenate([da, db], axis=1)
    dh2 = _apply_wt(dup, w, "w_up", f32, f"up_dx{tag}")
    gw["w_up"] = _mm(s["h2"].T, dup, bf16, f"up_dw{tag}")
    dx1, d_g, d_sc2, d_sh2 = _rowwise_bwd(_f_prenorm, [s["x1"]], [vec(p["g_pre_ffn"]), sc2, sh2], [[dh2]], [f32], True,
                                          256, f"prenorm_ffn_bwd{tag}", add_to={0: dx2})
    gp["g_pre_ffn"] = d_g
    dy, d_gpm, d_gt1 = _rowwise_bwd(_f_postres, [x, s["y"]], [vec(p["g_post_mix"]), gt1], [[dx1]], [None, bf16], True,
                                    256, f"postres_mix_bwd{tag}")
    gp["g_post_mix"] = d_gpm
    dmerged = _apply_wt(dy, w, "w_o", f32, f"o_dx{tag}")
    gw["w_o"] = _mm(s["merged"].T, dy, bf16, f"o_dw{tag}")
    res = _rowwise_bwd(_f_merge, s["pg"] + [s["y_ssm"], s["y_attn"], s["y_conv"]], s["bg"], [[dmerged]], [bf16] * 6, True,
                       128, f"merge_bwd{tag}")
    dpg, (dys, dya, dyv), dbg = list(res[0:3]), res[3:6], res[6:9]
    gp["b_gate"] = jnp.concatenate(dbg, axis=1)
    ds_ssm = _apply_wt(dys, w, "w_ssm_out", f32, f"ssm_proj_dx{tag}")
    gw["w_ssm_out"] = _mm(s["s_ssm"].T, dys, bf16, f"ssm_proj_dw{tag}")
    dattn_o = _apply_wt(dya, w, "w_attn_out", f32, f"attn_proj_dx{tag}")
    gw["w_attn_out"] = _mm(s["attn_o"].T, dya, bf16, f"attn_proj_dw{tag}")
    dcm = _apply_wt(dyv, w, "w_conv_out", f32, f"conv_proj_dx{tag}")
    gw["w_conv_out"] = _mm(s["cm"].T, dyv, bf16, f"conv_proj_dw{tag}")
    ssm = s["ssm"]
    tc, n_groups = ssm["tc"], ssm["n_groups"]
    dg_a, dz, gp["b_glu"] = _rowwise_bwd(_f_glu, [s["g"], s["z"]], [vec(p["b_glu"])], [[ds_ssm]], [f32, bf16], True,
                                         256, f"glu_bwd{tag}")
    dg_b = _apply_wt(dz, w, "w_glu", f32, f"glu_proj_dx{tag}")
    gw["w_glu"] = _mm(s["g"].T, dz, bf16, f"glu_proj_dw{tag}")
    dyc, du_skip, gp["ssm_d"] = _rowwise_bwd(_f_gelu, [s["yc"], s["u32"]], [vec(p["ssm_d"])], [[dg_a, dg_b]], [bf16, f32], True,
                                             256, f"ssm_gelu_bwd{tag}")
    dh_state = _mm(dyc, ssm["cc"], f32, f"ssm_out_dx{tag}")
    g_cc = _bd_fold(_mm(dyc.T, s["hb"], f32, f"ssm_out_dw{tag}"), n_groups, tc, f"ssm_out_dw_fold{tag}")
    gf, gb = _scan(dh_state, ssm["lam"], True, f"ssm_scan_bwd{tag}")
    d_lam = _lam_grad(gf, s["hf"], f"ssm_lam_grad{tag}")
    du_x = _mm(gb, ssm["bb"], f32, f"ssm_in_dx{tag}", rhs_t=True)
    g_bb = _bd_fold(_mm(s["u"].T, gb, f32, f"ssm_in_dw{tag}"), n_groups, tc, f"ssm_in_dw_fold{tag}")
    (du,) = _rowwise(_f_add, [du_skip, du_x], [], [bf16], 256, f"ssm_du{tag}")
    d_lr, d_li = _deinterleave(d_lam, tc)
    g_bbr, g_bbi = _deinterleave(g_bb, tc)
    group_shape = (SSM_GROUP, n_groups, SSM_STATE)
    cots = (d_lr.reshape(n_groups, SSM_STATE), d_li.reshape(n_groups, SSM_STATE), g_bbr.reshape(group_shape), g_bbi.reshape(group_shape))
    d_ldt, d_ar, d_ai, d_br, d_bi = _disc_bwd(*ssm["disc_in"], cots, f"ssm_disc_bwd{tag}")
    gp["ssm_log_dt"], gp["ssm_a_re"], gp["ssm_a_im"] = d_ldt.reshape(-1), d_ar, d_ai
    gp["ssm_b_re"], gp["ssm_b_im"] = d_br.transpose(1, 2, 0), d_bi.transpose(1, 2, 0)
    g_ccr, g_cci = _deinterleave(g_cc, tc)
    gp["ssm_c_re"] = g_ccr.reshape(group_shape).transpose(1, 0, 2)
    gp["ssm_c_im"] = -g_cci.reshape(group_shape).transpose(1, 0, 2)
    res = _rowwise_bwd(_f_amerge, s["o_tok"] + s["lse_x"], [], [[dattn_o]], [f32] * 6, False, 256, f"attn_merge_bwd{tag}")
    do_b = _to_blocks(jnp.concatenate(res[0:3], axis=1), hpp)
    dl_b = _to_blocks(jnp.concatenate(res[3:6], axis=1), hpp)
    dqb, dkb, dvb = _attn_bwd(s["qb"], s["kb"], s["vb"], s["ob"], s["lseb"], do_b, dl_b, hpp, f"attn_bwd{tag}")
    dq, dk, dv = (jnp.concatenate(_from_blocks(t, hpp), axis=1) for t in (dqb, dkb, dvb))
    dcb, dcc, dch, gp["conv_mix_w"] = _convgate_bwd(s["proj"], p["conv_mix_w"], dcm, off_conv, w4, f"convgate_bwd{tag}")
    dproj = jnp.concatenate([du, dq, dk, dv, dcb, dcc, dch] + dpg, axis=1)
    dh = _apply_wt(dproj, w, "w_in", f32, f"in_dx{tag}")
    gw["w_in"] = _mm(s["h"].T, dproj, bf16, f"in_dw{tag}")
    dx0, d_g, d_sc1, d_sh1 = _rowwise_bwd(_f_prenorm, [x], [vec(p["g_pre_mix"]), sc1, sh1], [[dh]], [f32], True,
                                          256, f"prenorm_mix_bwd{tag}", add_to={0: dx1})
    gp["g_pre_mix"] = d_g
    return dx0, [d_sh1, d_sc1, d_gt1, d_sh2, d_sc2, d_gt2], gw, gp


COL_SHARDED = ("w_in", "w_up", "w_ssm_out", "w_attn_out", "w_conv_out")
ROW_SHARDED = ("w_down", "w_o", "w_glu")
SMALL_SHARDED = ("conv_mix_w", "ffn_conv_w")
REPLICATED = ("b_mod", "g_pre_mix", "g_post_mix", "g_pre_ffn", "g_post_ffn", "ssm_log_dt", "ssm_a_re", "ssm_a_im", "ssm_b_re",
              "ssm_b_im", "ssm_c_re", "ssm_c_im", "ssm_d", "b_glu", "b_gate")
WEIGHTS = ('w_mod', 'b_mod', 'g_pre_mix', 'g_post_mix', 'g_pre_ffn', 'g_post_ffn', 'w_in', 'ssm_log_dt', 'ssm_a_re', 'ssm_a_im',
           'ssm_b_re', 'ssm_b_im', 'ssm_c_re', 'ssm_c_im', 'ssm_d', 'w_glu', 'b_glu', 'conv_mix_w', 'w_ssm_out', 'w_attn_out',
           'w_conv_out', 'b_gate', 'w_o', 'w_up', 'ffn_conv_w', 'w_down')


def _full_cols(g):
    return g.transpose(1, 0, 2).reshape(g.shape[1], -1)


def _col_slabs(m):
    return m.reshape(m.shape[0], N_DEV, -1).transpose(1, 0, 2)


def _step(x, c, loss_target, wts, m_in, v_in):
    depth = wts["w_in"].shape[0]
    n_rows, d = x.shape[1], x.shape[2]
    me = _linear(_position())
    x2d, target2d = x[0], loss_target[0]

    (c_all,) = _all_gather([c], "gather_c")
    c_all = c_all.reshape(N_DEV, d)
    mod_cols = wts["w_mod"].shape[2]
    b_cols = lax.dynamic_slice_in_dim(wts["b_mod"], me * mod_cols, mod_cols, axis=1).reshape(depth, 1, mod_cols)
    mod_mine = _mod_fwd(c_all, wts["w_mod"], b_cols, "mod_fwd")
    (mod_all,) = _all_gather([mod_mine], "gather_mod")
    mod_me = lax.dynamic_index_in_dim(mod_all, me, axis=2, keepdims=False).transpose(1, 0, 2).reshape(depth, 6, 1, d)

    mats = []
    for l in range(depth):
        names = COL_SHARDED + ROW_SHARDED
        shards = [(wts[k][l].T if k in COL_SHARDED else wts[k][l]).astype(bf16) for k in names]
        got = _all_gather(shards, f"gather_w{l}")
        mats.append({k: g.reshape(-1, g.shape[2]) for k, g in zip(names, got)})
    (conv_mix_all, ffn_conv_all) = _all_gather([wts["conv_mix_w"], wts["ffn_conv_w"]], "gather_conv_w")
    small = []
    for l in range(depth):
        p = {k: wts[k][l] for k in REPLICATED if k != "b_mod"}
        p["conv_mix_w"] = _full_cols(conv_mix_all[:, l])
        p["ffn_conv_w"] = _full_cols(ffn_conv_all[:, l])
        small.append(p)

    saved = []
    h = x2d
    for l in range(depth):
        mod = [mod_me[l, i] for i in range(6)]
        h, s = _layer_fwd(h, mod, mats[l], small[l], f"_l{l}")
        saved.append(s)
    dy, loss_part = _loss_head(h, target2d, "loss_head")
    loss = lax.psum(loss_part[0, 0], ("x", "y", "c"))

    grads = {k: [None] * depth for k in WEIGHTS}
    dmod_rows = [None] * depth
    small_parts = [None] * depth
    for l in reversed(range(depth)):
        mod = [mod_me[l, i] for i in range(6)]
        dy, dmod, gw, gp = _layer_bwd(dy, saved[l], mod, mats[l], small[l], f"_l{l}")
        dmod_rows[l] = jnp.concatenate(dmod, axis=1)
        slabs = [(_col_slabs(gw[k]) if k in COL_SHARDED else gw[k].reshape(N_DEV, -1, gw[k].shape[1])) for k in COL_SHARDED + ROW_SHARDED]
        slabs += [_col_slabs(gp[k]) for k in SMALL_SHARDED]
        summed = _reduce_scatter(slabs, f"scatter_g{l}")
        for k, g in zip(COL_SHARDED + ROW_SHARDED + SMALL_SHARDED, summed):
            grads[k][l] = g
        small_parts[l] = gp
    grad_x = dy.reshape(x.shape)

    dmod_mine = jnp.concatenate(dmod_rows, axis=0)
    rep_names = [k for k in REPLICATED if k != "b_mod"]
    pieces = [dmod_mine.reshape(-1)] + [jnp.stack([small_parts[l][k].reshape(-1) for l in range(depth)]).reshape(-1) for k in rep_names]
    sizes = [int(pc.shape[0]) for pc in pieces]
    total = sum(sizes)
    padded = -(-total // (SUBLANES * LANES)) * (SUBLANES * LANES)
    pack = jnp.concatenate(pieces + [jnp.zeros((padded - total,), f32)]).reshape(-1, LANES)
    (pack_all,) = _all_gather([pack], "gather_small_grads")
    pack_sum = _sum_slabs(pack_all, "sum_small_grads").reshape(-1)
    offs = np.cumsum([0] + sizes)
    grads["b_mod"] = pack_sum[offs[0]:offs[1]].reshape(wts["b_mod"].shape)
    for i, k in enumerate(rep_names):
        grads[k] = pack_sum[offs[i + 1]:offs[i + 2]].reshape(wts[k].shape)
    dmod_all = pack_all.reshape(N_DEV, -1)[:, :sizes[0]].reshape(N_DEV, depth, 6 * d)
    dmod_cols = lax.dynamic_slice_in_dim(dmod_all, me * mod_cols, mod_cols, axis=2).transpose(1, 0, 2)
    grads["w_mod"] = _mod_wgrad(c_all.T, dmod_cols, "mod_wgrad")
    for k in COL_SHARDED + ROW_SHARDED + SMALL_SHARDED:
        grads[k] = jnp.stack(grads[k])

    delta, new_m, new_v = {}, {}, {}
    for k in WEIGHTS:
        delta[k], new_m[k], new_v[k] = _adamw(wts[k], grads[k], m_in[k], v_in[k], f"adamw_{k}")
    return (loss, grad_x, *[grads[k] for k in WEIGHTS], *[delta[k] for k in WEIGHTS], *[new_m[k] for k in WEIGHTS],
            *[new_v[k] for k in WEIGHTS])


def kernel(x, c, w_mod, b_mod, g_pre_mix, g_post_mix, g_pre_ffn, g_post_ffn, w_in, ssm_log_dt, ssm_a_re, ssm_a_im, ssm_b_re, ssm_b_im, ssm_c_re, ssm_c_im, ssm_d, w_glu, b_glu, conv_mix_w, w_ssm_out, w_attn_out, w_conv_out, b_gate, w_o, w_up, ffn_conv_w, w_down, loss_target, m_w_mod, m_b_mod, m_g_pre_mix, m_g_post_mix, m_g_pre_ffn, m_g_post_ffn, m_w_in, m_ssm_log_dt, m_ssm_a_re, m_ssm_a_im, m_ssm_b_re, m_ssm_b_im, m_ssm_c_re, m_ssm_c_im, m_ssm_d, m_w_glu, m_b_glu, m_conv_mix_w, m_w_ssm_out, m_w_attn_out, m_w_conv_out, m_b_gate, m_w_o, m_w_up, m_ffn_conv_w, m_w_down, v_w_mod, v_b_mod, v_g_pre_mix, v_g_post_mix, v_g_pre_ffn, v_g_post_ffn, v_w_in, v_ssm_log_dt, v_ssm_a_re, v_ssm_a_im, v_ssm_b_re, v_ssm_b_im, v_ssm_c_re, v_ssm_c_im, v_ssm_d, v_w_glu, v_b_glu, v_conv_mix_w, v_w_ssm_out, v_w_attn_out, v_w_conv_out, v_b_gate, v_w_o, v_w_up, v_ffn_conv_w, v_w_down):
    wts = dict(w_mod=w_mod, b_mod=b_mod, g_pre_mix=g_pre_mix, g_post_mix=g_post_mix, g_pre_ffn=g_pre_ffn, g_post_ffn=g_post_ffn, w_in=w_in, ssm_log_dt=ssm_log_dt, ssm_a_re=ssm_a_re, ssm_a_im=ssm_a_im, ssm_b_re=ssm_b_re, ssm_b_im=ssm_b_im, ssm_c_re=ssm_c_re, ssm_c_im=ssm_c_im, ssm_d=ssm_d, w_glu=w_glu, b_glu=b_glu, conv_mix_w=conv_mix_w, w_ssm_out=w_ssm_out, w_attn_out=w_attn_out, w_conv_out=w_conv_out, b_gate=b_gate, w_o=w_o, w_up=w_up, ffn_conv_w=ffn_conv_w, w_down=w_down)
    m_in = dict(w_mod=m_w_mod, b_mod=m_b_mod, g_pre_mix=m_g_pre_mix, g_post_mix=m_g_post_mix, g_pre_ffn=m_g_pre_ffn, g_post_ffn=m_g_post_ffn, w_in=m_w_in, ssm_log_dt=m_ssm_log_dt, ssm_a_re=m_ssm_a_re, ssm_a_im=m_ssm_a_im, ssm_b_re=m_ssm_b_re, ssm_b_im=m_ssm_b_im, ssm_c_re=m_ssm_c_re, ssm_c_im=m_ssm_c_im, ssm_d=m_ssm_d, w_glu=m_w_glu, b_glu=m_b_glu, conv_mix_w=m_conv_mix_w, w_ssm_out=m_w_ssm_out, w_attn_out=m_w_attn_out, w_conv_out=m_w_conv_out, b_gate=m_b_gate, w_o=m_w_o, w_up=m_w_up, ffn_conv_w=m_ffn_conv_w, w_down=m_w_down)
    v_in = dict(w_mod=v_w_mod, b_mod=v_b_mod, g_pre_mix=v_g_pre_mix, g_post_mix=v_g_post_mix, g_pre_ffn=v_g_pre_ffn, g_post_ffn=v_g_post_ffn, w_in=v_w_in, ssm_log_dt=v_ssm_log_dt, ssm_a_re=v_ssm_a_re, ssm_a_im=v_ssm_a_im, ssm_b_re=v_ssm_b_re, ssm_b_im=v_ssm_b_im, ssm_c_re=v_ssm_c_re, ssm_c_im=v_ssm_c_im, ssm_d=v_ssm_d, w_glu=v_w_glu, b_glu=v_b_glu, conv_mix_w=v_conv_mix_w, w_ssm_out=v_w_ssm_out, w_attn_out=v_w_attn_out, w_conv_out=v_w_conv_out, b_gate=v_b_gate, w_o=v_w_o, w_up=v_w_up, ffn_conv_w=v_ffn_conv_w, w_down=v_w_down)
    return _step(x, c, loss_target, wts, m_in, v_in)
```

```python
import functools
import math

import numpy as np
import jax
import jax.numpy as jnp
from jax import lax
from jax.experimental import pallas as pl
from jax.experimental.pallas import tpu as pltpu

f32 = jnp.float32
bf16 = jnp.bfloat16

N_DEV = 8
N_CHIPS = 4
V7X_VMEM_LIMIT_BYTES = 56 * 1024 * 1024
LANES = 128
SUBLANES = 8
BF16_ROWS = 16

RMS_EPS = 1e-6
NEG_INF = -1e30
HEAD_DIM = 64
SSM_GROUP = 16
SSM_STATE = 64
DSWA_PATTERNS = ((128, 1), (512, 4), (2048, 16))
QBLK = 128

ADAM_LR = 0.001
ADAM_B1 = 0.9
ADAM_B2 = 0.999
ADAM_EPS = 1e-08
ADAM_WD = 0.01
ADAM_STEP = 10

MESH = pl.DeviceIdType.MESH
ANY = pl.BlockSpec(memory_space=pl.ANY)


def _tile(n, pref, mult=LANES):
    t = min(pref, n) // mult * mult
    while t >= mult:
        if n % t == 0:
            return t
        t -= mult
    return n


def _params(*sem):
    return pltpu.CompilerParams(dimension_semantics=sem, vmem_limit_bytes=V7X_VMEM_LIMIT_BYTES)


def _mm(a, b, out_dtype, name, rhs_t=False, lhs_t=False, carry=()):
    k, m = a.shape if lhs_t else a.shape[::-1]
    n, k2 = b.shape if rhs_t else b.shape[::-1]
    assert k == k2 and a.dtype == bf16 and b.dtype == bf16, (a.shape, b.shape, a.dtype, b.dtype)
    tm, tn, tk = _tile(m, 1024, LANES if lhs_t else SUBLANES), _tile(n, 512), _tile(k, 2048)
    grid = (m // tm, n // tn, k // tk)
    contract = (((0 if lhs_t else 1,), (1 if rhs_t else 0,)), ((), ()))
    nc = len(carry)
    srcs = [job[0] for job in carry]
    ranges = [(job[1], job[2]) for job in carry]
    intos = [(i, job[3]) for i, job in enumerate(carry) if job[3] is not None]
    n_in = 2 + nc + len(intos)

    def body(*refs):
        a_ref, b_ref = refs[:2]
        o_ref = refs[n_in]
        acc_ref = refs[n_in + 1 + nc]
        steps = [pl.program_id(axis) for axis in range(3)]
        if nc:
            copies = _chip_copies(refs[2:2 + nc], refs[n_in + 1:n_in + 1 + nc], refs[n_in + 2 + nc], refs[n_in + 3 + nc], ranges)

            @pl.when((steps[0] == 0) & (steps[1] == 0) & (steps[2] == 0))
            def _():
                for cp in copies:
                    cp.start()

        @pl.when(steps[2] == 0)
        def _():
            acc_ref[...] = jnp.zeros_like(acc_ref)

        acc_ref[...] += lax.dot_general(a_ref[...], b_ref[...], contract, preferred_element_type=f32)

        @pl.when(steps[2] == grid[2] - 1)
        def _():
            o_ref[...] = acc_ref[...].astype(o_ref.dtype)

        if nc:
            @pl.when((steps[0] == grid[0] - 1) & (steps[1] == grid[1] - 1) & (steps[2] == grid[2] - 1))
            def _():
                for cp in copies:
                    cp.wait_recv()
                for cp in copies:
                    cp.wait_send()

    a_spec = pl.BlockSpec((tk, tm), lambda i, j, kk: (kk, i)) if lhs_t else pl.BlockSpec((tm, tk), lambda i, j, kk: (i, kk))
    b_spec = pl.BlockSpec((tn, tk), lambda i, j, kk: (j, kk)) if rhs_t else pl.BlockSpec((tk, tn), lambda i, j, kk: (kk, j))
    sems = pltpu.SemaphoreType.DMA((max(nc, 1), N_CHIPS - 1))
    res = pl.pallas_call(
        body, grid=grid,
        in_specs=[a_spec, b_spec] + [ANY] * (nc + len(intos)),
        out_specs=[pl.BlockSpec((tm, tn), lambda i, j, kk: (i, j))] + [ANY] * nc,
        out_shape=[jax.ShapeDtypeStruct((m, n), out_dtype)]
        + [jax.ShapeDtypeStruct((N_CHIPS - 1,) + x.shape[1:], x.dtype) for x in srcs],
        scratch_shapes=[pltpu.VMEM((tm, tn), f32)] + ([sems, sems] if nc else []),
        input_output_aliases={2 + nc + j: 1 + i for j, (i, _) in enumerate(intos)},
        compiler_params=_params(*(("arbitrary",) * 3 if nc else ("parallel", "parallel", "arbitrary"))), name=name)(
            a, b, *srcs, *[buf for _, buf in intos])
    return (res[0], list(res[1:])) if nc else res[0]


def _rowwise(fn, rows, vecs, out_dtypes, tm, name):
    n_rows = rows[0].shape[0]
    nr, nv = len(rows), len(vecs)
    outs = jax.eval_shape(fn, *[jax.ShapeDtypeStruct((tm, r.shape[1]), f32) for r in rows],
                          *[jax.ShapeDtypeStruct(v.shape, f32) for v in vecs])

    def body(*refs):
        vals = [r[...].astype(f32) for r in refs[:nr]] + [v[...] for v in refs[nr:nr + nv]]
        for o_ref, val in zip(refs[nr + nv:], fn(*vals)):
            o_ref[...] = val.astype(o_ref.dtype)

    res = pl.pallas_call(
        body, grid=(n_rows // tm,),
        in_specs=[pl.BlockSpec((tm, r.shape[1]), lambda i: (i, 0)) for r in rows]
        + [pl.BlockSpec(v.shape, lambda i: (0, 0)) for v in vecs],
        out_specs=[pl.BlockSpec((tm, o.shape[1]), lambda i: (i, 0)) for o in outs],
        out_shape=[jax.ShapeDtypeStruct((n_rows, o.shape[1]), dt) for o, dt in zip(outs, out_dtypes)],
        compiler_params=_params("parallel"), name=name)(*rows, *vecs)
    return res


def _rowwise_bwd(fn, rows, vecs, cots, row_dtypes, need_vecs, tm, name, add_to=None):
    n_rows = rows[0].shape[0]
    nr, nv = len(rows), len(vecs)
    add_to = add_to or {}
    flat_cots = [c for group in cots for c in group]
    add_keys = sorted(add_to)
    row_out = [i for i, dt in enumerate(row_dtypes) if dt is not None]
    n_in = nr + nv + len(flat_cots) + len(add_keys)

    def body(*refs):
        vals = [r[...].astype(f32) for r in refs[:nr]] + [v[...] for v in refs[nr:nr + nv]]
        pos = nr + nv
        cvals = []
        for group in cots:
            acc = refs[pos][...].astype(f32)
            for extra in range(1, len(group)):
                acc = acc + refs[pos + extra][...].astype(f32)
            pos += len(group)
            cvals.append(acc)
        adds = {key: refs[pos + j] for j, key in enumerate(add_keys)}
        _, vjp = jax.vjp(fn, *vals)
        grads = vjp(tuple(cvals))
        out_refs = refs[n_in:]
        for j, i in enumerate(row_out):
            g = grads[i]
            if i in adds:
                g = g + adds[i][...].astype(f32)
            out_refs[j][...] = g.astype(out_refs[j].dtype)
        if need_vecs:
            step = pl.program_id(0)
            for j in range(nv):
                acc_ref = out_refs[len(row_out) + j]

                @pl.when(step == 0)
                def _(acc_ref=acc_ref):
                    acc_ref[...] = jnp.zeros_like(acc_ref)

                acc_ref[...] += grads[nr + j]

    row_spec = lambda a: pl.BlockSpec((tm, a.shape[1]), lambda i: (i, 0))
    in_specs = ([row_spec(r) for r in rows] + [pl.BlockSpec(v.shape, lambda i: (0, 0)) for v in vecs]
                + [row_spec(c) for c in flat_cots] + [row_spec(add_to[key]) for key in add_keys])
    out_specs = [row_spec(rows[i]) for i in row_out]
    out_shape = [jax.ShapeDtypeStruct(rows[i].shape, row_dtypes[i]) for i in row_out]
    if need_vecs:
        out_specs += [pl.BlockSpec(v.shape, lambda i: (0, 0)) for v in vecs]
        out_shape += [jax.ShapeDtypeStruct(v.shape, f32) for v in vecs]
    return pl.pallas_call(
        body, grid=(n_rows // tm,), in_specs=in_specs, out_specs=out_specs, out_shape=out_shape,
        compiler_params=_params("arbitrary"), name=name)(*rows, *vecs, *flat_cots, *[add_to[key] for key in add_keys])


def _f_prenorm(x, g, sc, sh):
    y = x * lax.rsqrt(jnp.mean(x * x, axis=-1, keepdims=True) + RMS_EPS)
    return (y * g * (1.0 + sc) + sh,)


def _f_postres(x, y, g, gt):
    n = y * lax.rsqrt(jnp.mean(y * y, axis=-1, keepdims=True) + RMS_EPS)
    return (x + gt * (n * g),)


def _f_gelu(yc, u, d):
    return (jax.nn.gelu(yc + d * u),)


def _f_glu(g, z, b):
    return (g * jax.nn.sigmoid(z + b),)


def _f_merge(p0, p1, p2, ys, ya, yc, b0, b1, b2):
    return (jax.nn.sigmoid(p0 + b0) * ys + jax.nn.sigmoid(p1 + b1) * ya + jax.nn.sigmoid(p2 + b2) * yc,)


def _f_amerge(o0, o1, o2, l0, l1, l2):
    m = jnp.maximum(jnp.maximum(l0, l1), l2)
    e0, e1, e2 = jnp.exp(l0 - m), jnp.exp(l1 - m), jnp.exp(l2 - m)
    return ((e0 * o0 + e1 * o1 + e2 * o2) / (e0 + e1 + e2),)


def _f_add(a, b):
    return (a + b,)


def _shift_down(z, k, row):
    return jnp.where(row >= k, pltpu.roll(z, k, 0), 0.0)


def _shift_up(z, k, row):
    n = z.shape[0]
    return jnp.where(row < n - k, pltpu.roll(z, n - k, 0), 0.0)


def _conv3(z, w, row):
    return w[0:1, :] * z + w[1:2, :] * _shift_down(z, 1, row) + w[2:3, :] * _shift_down(z, 2, row)


def _conv3_bwd(z, w, dy, row):
    dz = w[0:1, :] * dy + w[1:2, :] * _shift_up(dy, 1, row) + w[2:3, :] * _shift_up(dy, 2, row)
    dw = [jnp.sum(dy * z, axis=0, keepdims=True),
          jnp.sum(dy * _shift_down(z, 1, row), axis=0, keepdims=True),
          jnp.sum(dy * _shift_down(z, 2, row), axis=0, keepdims=True)]
    return dz, dw


def _convgate_fwd(proj, w, off, width, name):
    n_rows = proj.shape[0]
    tc = _tile(width, 256)
    nb = width // tc
    base = off // tc

    def body(b_ref, c_ref, h_ref, w_ref, o_ref):
        row = lax.broadcasted_iota(jnp.int32, (n_rows, tc), 0)
        o_ref[...] = (b_ref[...] * _conv3(c_ref[...] * h_ref[...], w_ref[...], row)).astype(o_ref.dtype)

    col = lambda s: pl.BlockSpec((n_rows, tc), lambda j: (0, base + s * nb + j))
    return pl.pallas_call(
        body, grid=(nb,), in_specs=[col(0), col(1), col(2), pl.BlockSpec((3, tc), lambda j: (0, j))],
        out_specs=pl.BlockSpec((n_rows, tc), lambda j: (0, j)),
        out_shape=jax.ShapeDtypeStruct((n_rows, width), bf16),
        compiler_params=_params("parallel"), name=name)(proj, proj, proj, w)


def _convgate_bwd(proj, w, dcm, off, width, name):
    n_rows = proj.shape[0]
    tc = _tile(width, 128)
    nb = width // tc
    base = off // tc

    def body(b_ref, c_ref, h_ref, w_ref, d_ref, db_ref, dc_ref, dh_ref, dw_ref):
        row = lax.broadcasted_iota(jnp.int32, (n_rows, tc), 0)
        cb, cc, ch, wv, d = b_ref[...], c_ref[...], h_ref[...], w_ref[...], d_ref[...].astype(f32)
        z = cc * ch
        db_ref[...] = (d * _conv3(z, wv, row)).astype(db_ref.dtype)
        dz, dw = _conv3_bwd(z, wv, d * cb, row)
        dc_ref[...] = (dz * ch).astype(dc_ref.dtype)
        dh_ref[...] = (dz * cc).astype(dh_ref.dtype)
        for k in range(3):
            dw_ref[k:k + 1, :] = dw[k]

    col = lambda s: pl.BlockSpec((n_rows, tc), lambda j: (0, base + s * nb + j))
    own = pl.BlockSpec((n_rows, tc), lambda j: (0, j))
    wspec = pl.BlockSpec((3, tc), lambda j: (0, j))
    sec = jax.ShapeDtypeStruct((n_rows, width), bf16)
    return pl.pallas_call(
        body, grid=(nb,), in_specs=[col(0), col(1), col(2), wspec, own], out_specs=[own, own, own, wspec],
        out_shape=[sec, sec, sec, jax.ShapeDtypeStruct((3, width), f32)],
        compiler_params=_params("parallel"), name=name)(proj, proj, proj, w, dcm)


def _ffnconv_fwd(up, w, name):
    n_rows, two_f = up.shape
    half = two_f // 2
    tc = _tile(half, 256)
    nb = half // tc

    def body(a_ref, b_ref, wa_ref, wb_ref, o_ref):
        row = lax.broadcasted_iota(jnp.int32, (n_rows, tc), 0)
        ua = _conv3(a_ref[...], wa_ref[...], row)
        ub = _conv3(b_ref[...], wb_ref[...], row)
        o_ref[...] = (jax.nn.silu(ua) * ub).astype(o_ref.dtype)

    return pl.pallas_call(
        body, grid=(nb,),
        in_specs=[pl.BlockSpec((n_rows, tc), lambda j: (0, j)), pl.BlockSpec((n_rows, tc), lambda j: (0, nb + j)),
                  pl.BlockSpec((3, tc), lambda j: (0, j)), pl.BlockSpec((3, tc), lambda j: (0, nb + j))],
        out_specs=pl.BlockSpec((n_rows, tc), lambda j: (0, j)),
        out_shape=jax.ShapeDtypeStruct((n_rows, half), bf16),
        compiler_params=_params("parallel"), name=name)(up, up, w, w)


def _ffnconv_bwd(up, w, dact, name):
    n_rows, two_f = up.shape
    half = two_f // 2
    tc = _tile(half, 128)
    nb = half // tc

    def body(a_ref, b_ref, wa_ref, wb_ref, d_ref, da_ref, db_ref, dwa_ref, dwb_ref):
        row = lax.broadcasted_iota(jnp.int32, (n_rows, tc), 0)
        a, b, wa, wb, d = a_ref[...], b_ref[...], wa_ref[...], wb_ref[...], d_ref[...].astype(f32)
        ua = _conv3(a, wa, row)
        ub = _conv3(b, wb, row)
        sg = jax.nn.sigmoid(ua)
        d_ua = d * ub * (sg * (1.0 + ua * (1.0 - sg)))
        d_ub = d * (ua * sg)
        da, dwa = _conv3_bwd(a, wa, d_ua, row)
        db, dwb = _conv3_bwd(b, wb, d_ub, row)
        da_ref[...] = da.astype(da_ref.dtype)
        db_ref[...] = db.astype(db_ref.dtype)
        for k in range(3):
            dwa_ref[k:k + 1, :] = dwa[k]
            dwb_ref[k:k + 1, :] = dwb[k]

    lo = pl.BlockSpec((n_rows, tc), lambda j: (0, j))
    hi = pl.BlockSpec((n_rows, tc), lambda j: (0, nb + j))
    wlo = pl.BlockSpec((3, tc), lambda j: (0, j))
    whi = pl.BlockSpec((3, tc), lambda j: (0, nb + j))
    sec = jax.ShapeDtypeStruct((n_rows, half), bf16)
    wsec = jax.ShapeDtypeStruct((3, half), f32)
    da, db, dwa, dwb = pl.pallas_call(
        body, grid=(nb,), in_specs=[lo, hi, wlo, whi, lo], out_specs=[lo, lo, wlo, wlo],
        out_shape=[sec, sec, wsec, wsec], compiler_params=_params("parallel"), name=name)(up, up, w, w, dact)
    return da, db, jnp.concatenate([dwa, dwb], axis=1)


SCAN_TC = 256


def _cmul(ar, ai, br, bi):
    return ar * br - ai * bi, ar * bi + ai * br


def _scan(x, lam, adjoint, name):
    n_rows, two_s = x.shape
    tc = min(SCAN_TC, two_s // 2)
    n_tiles = two_s // (2 * tc)
    n_chunks = n_rows // SUBLANES

    def body(x_ref, lam_ref, hf_ref, hb_ref):
        lr = lam_ref[:, 0:tc]
        li = lam_ref[:, tc:2 * tc]
        if adjoint:
            li = -li
        row = lax.broadcasted_iota(jnp.int32, (SUBLANES, tc), 0)
        powers = [(lr, li)]
        for _ in range(SUBLANES - 1):
            powers.append(_cmul(powers[-1][0], powers[-1][1], lr, li))
        pr = jnp.zeros((SUBLANES, tc), f32)
        pi = jnp.zeros((SUBLANES, tc), f32)
        for t in range(SUBLANES):
            e = (SUBLANES - 1 - t) if adjoint else t
            pr = jnp.where(row == t, powers[e][0], pr)
            pi = jnp.where(row == t, powers[e][1], pi)
        last = 0 if adjoint else SUBLANES - 1

        def chunk(i, carry):
            cr, ci = carry
            c = (n_chunks - 1 - i) if adjoint else i
            rows = pl.ds(pl.multiple_of(c * SUBLANES, SUBLANES), SUBLANES)
            xr = x_ref[rows, 0:tc]
            xi = x_ref[rows, tc:2 * tc]
            for k in (1, 2, 4):
                ar, ai = powers[k - 1]
                if adjoint:
                    sr = jnp.where(row < SUBLANES - k, pltpu.roll(xr, SUBLANES - k, 0), 0.0)
                    si = jnp.where(row < SUBLANES - k, pltpu.roll(xi, SUBLANES - k, 0), 0.0)
                else:
                    sr = jnp.where(row >= k, pltpu.roll(xr, k, 0), 0.0)
                    si = jnp.where(row >= k, pltpu.roll(xi, k, 0), 0.0)
                xr, xi = xr + ar * sr - ai * si, xi + ar * si + ai * sr
            hr = xr + pr * cr - pi * ci
            hi = xi + pr * ci + pi * cr
            hf_ref[rows, 0:tc] = hr
            hf_ref[rows, tc:2 * tc] = hi
            hb_ref[rows, 0:tc] = hr.astype(bf16)
            hb_ref[rows, tc:2 * tc] = hi.astype(bf16)
            return (jnp.sum(jnp.where(row == last, hr, 0.0), axis=0, keepdims=True),
                    jnp.sum(jnp.where(row == last, hi, 0.0), axis=0, keepdims=True))

        zero = jnp.zeros((1, tc), f32)
        lax.fori_loop(0, n_chunks, chunk, (zero, zero))

    blk = pl.BlockSpec((n_rows, 2 * tc), lambda j: (0, j))
    return pl.pallas_call(
        body, grid=(n_tiles,), in_specs=[blk, pl.BlockSpec((1, 2 * tc), lambda j: (0, j))], out_specs=[blk, blk],
        out_shape=[jax.ShapeDtypeStruct(x.shape, f32), jax.ShapeDtypeStruct(x.shape, bf16)],
        compiler_params=_params("parallel"), name=name)(x, lam)


def _lam_grad(g, h, name):
    n_rows, two_s = g.shape
    tc = min(SCAN_TC, two_s // 2)

    def body(g_ref, h_ref, o_ref):
        row = lax.broadcasted_iota(jnp.int32, (n_rows, tc), 0)
        gr, gi = g_ref[:, 0:tc], g_ref[:, tc:2 * tc]
        hr, hi = _shift_down(h_ref[:, 0:tc], 1, row), _shift_down(h_ref[:, tc:2 * tc], 1, row)
        o_ref[:, 0:tc] = jnp.sum(gr * hr + gi * hi, axis=0, keepdims=True)
        o_ref[:, tc:2 * tc] = jnp.sum(gi * hr - gr * hi, axis=0, keepdims=True)

    blk = pl.BlockSpec((n_rows, 2 * tc), lambda j: (0, j))
    return pl.pallas_call(
        body, grid=(two_s // (2 * tc),), in_specs=[blk, blk], out_specs=pl.BlockSpec((1, 2 * tc), lambda j: (0, j)),
        out_shape=jax.ShapeDtypeStruct((1, two_s), f32), compiler_params=_params("parallel"), name=name)(g, h)


def _interleave(re, im, tc):
    lead = re.shape[:-1]
    s = re.shape[-1]
    return jnp.concatenate([re.reshape(*lead, s // tc, 1, tc), im.reshape(*lead, s // tc, 1, tc)], axis=-2).reshape(*lead, 2 * s)


def _deinterleave(z, tc):
    lead = z.shape[:-1]
    s = z.shape[-1] // 2
    z = z.reshape(*lead, s // tc, 2, tc)
    return z[..., 0, :].reshape(*lead, s), z[..., 1, :].reshape(*lead, s)


def _f_disc(log_dt, ar, ai, br, bi):
    dt = jnp.exp(log_dt)
    mag = jnp.exp(ar * dt)
    lr, li = mag * jnp.cos(ai * dt), mag * jnp.sin(ai * dt)
    den = ar * ar + ai * ai
    fr = ((lr - 1.0) * ar + li * ai) / den
    fi = (li * ar - (lr - 1.0) * ai) / den
    return lr, li, fr[None] * br - fi[None] * bi, fr[None] * bi + fi[None] * br


def _disc_fwd(log_dt, ar, ai, br, bi, name):
    def body(*refs):
        for o_ref, val in zip(refs[5:], _f_disc(*[r[...] for r in refs[:5]])):
            o_ref[...] = val

    sd = jax.ShapeDtypeStruct
    return pl.pallas_call(body, out_shape=[sd(ar.shape, f32), sd(ar.shape, f32), sd(br.shape, f32), sd(br.shape, f32)],
                          name=name)(log_dt, ar, ai, br, bi)


def _disc_bwd(log_dt, ar, ai, br, bi, cots, name):
    def body(*refs):
        _, vjp = jax.vjp(_f_disc, *[r[...] for r in refs[:5]])
        for o_ref, val in zip(refs[9:], vjp(tuple(r[...] for r in refs[5:9]))):
            o_ref[...] = val

    sd = jax.ShapeDtypeStruct
    return pl.pallas_call(
        body, out_shape=[sd(log_dt.shape, f32), sd(ar.shape, f32), sd(ar.shape, f32), sd(br.shape, f32), sd(br.shape, f32)],
        name=name)(log_dt, ar, ai, br, bi, *cots)


def _alibi_bias(n_heads_per_pattern):
    n_heads = len(DSWA_PATTERNS) * n_heads_per_pattern
    slopes = np.array([2.0 ** (-8.0 * (h + 1) / n_heads) for h in range(n_heads)], dtype=np.float32)
    qi = np.arange(QBLK)[:, None]
    kj = np.arange(QBLK)[None, :]
    cur, prev = [], []
    for h in range(n_heads):
        dil = DSWA_PATTERNS[h // n_heads_per_pattern][1]
        d_cur = qi - kj
        d_prev = QBLK + qi - kj
        cur.append(np.where(d_cur >= 0, -slopes[h] * (d_cur * dil).astype(np.float32), NEG_INF))
        prev.append(np.where(d_prev <= QBLK, -slopes[h] * (d_prev * dil).astype(np.float32), NEG_INF))
    return np.stack(cur).astype(np.float32), np.stack(prev).astype(np.float32)


def _blocks_per_residue(head, n_heads_per_pattern, n_blocks):
    pattern = head // n_heads_per_pattern
    out = n_blocks // DSWA_PATTERNS[-1][1]
    for p in range(len(DSWA_PATTERNS) - 2, -1, -1):
        out = jnp.where(pattern == p, n_blocks // DSWA_PATTERNS[p][1], out)
    return out


def _qk(a, b):
    return jnp.einsum('bqe,bke->bqk', a, b, preferred_element_type=f32)


def _pv(p, v):
    return jnp.einsum('bqk,bke->bqe', p.astype(bf16), v, preferred_element_type=f32)


def _ptq(p, a):
    return jnp.einsum('bqk,bqe->bke', p.astype(bf16), a, preferred_element_type=f32)


def _prev_block(t):
    return jnp.concatenate([jnp.zeros((1,) + t.shape[1:], t.dtype), t[:-1]], axis=0)


def _next_block(t):
    return jnp.concatenate([t[1:], jnp.zeros((1,) + t.shape[1:], t.dtype)], axis=0)


def _has_prev(head, hpp, n_blocks):
    blk = lax.broadcasted_iota(jnp.int32, (n_blocks, 1, 1), 0)
    return (blk & (_blocks_per_residue(head, hpp, n_blocks) - 1)) > 0


def _attn_fwd(q, k, v, hpp, name):
    n_heads, n_blocks = q.shape[0], q.shape[1]
    bias_cur, bias_prev = _alibi_bias(hpp)
    scale = HEAD_DIM ** -0.5

    def body(q_ref, k_ref, v_ref, bc_ref, bp_ref, o_ref, lse_ref):
        has_prev = _has_prev(pl.program_id(0), hpp, n_blocks)
        qv, kv, vv = q_ref[...], k_ref[...], v_ref[...]
        s_c = _qk(qv, kv) * scale + bc_ref[...][None]
        s_p = jnp.where(has_prev, _qk(qv, _prev_block(kv)) * scale + bp_ref[...][None], NEG_INF)
        m = jnp.maximum(jnp.max(s_c, axis=-1, keepdims=True), jnp.max(s_p, axis=-1, keepdims=True))
        p_c = jnp.exp(s_c - m)
        p_p = jnp.exp(s_p - m)
        l = jnp.sum(p_c, axis=-1, keepdims=True) + jnp.sum(p_p, axis=-1, keepdims=True)
        o_ref[...] = (_pv(p_c, vv) + _pv(p_p, _prev_block(vv))) / l
        lse_ref[...] = m + jnp.log(l)

    head = pl.BlockSpec((None, n_blocks, QBLK, HEAD_DIM), lambda a: (a, 0, 0, 0))
    bias = pl.BlockSpec((None, QBLK, QBLK), lambda a: (a, 0, 0))
    return pl.pallas_call(
        body, grid=(n_heads,), in_specs=[head, head, head, bias, bias],
        out_specs=[head, pl.BlockSpec((None, n_blocks, QBLK, 1), lambda a: (a, 0, 0, 0))],
        out_shape=[jax.ShapeDtypeStruct(q.shape, f32), jax.ShapeDtypeStruct(q.shape[:3] + (1,), f32)],
        compiler_params=_params("parallel"), name=name)(q, k, v, jnp.asarray(bias_cur), jnp.asarray(bias_prev))


def _attn_bwd(q, k, v, o, lse, do, dlse_x, hpp, name):
    n_heads, n_blocks = q.shape[0], q.shape[1]
    bias_cur, bias_prev = _alibi_bias(hpp)
    scale = HEAD_DIM ** -0.5

    def body(q_ref, k_ref, v_ref, o_ref, l_ref, do_ref, dl_ref, bc_ref, bp_ref, dq_ref, dk_ref, dv_ref):
        has_prev = _has_prev(pl.program_id(0), hpp, n_blocks)
        qv, kv, vv = q_ref[...], k_ref[...], v_ref[...]
        kp, vp = _prev_block(kv), _prev_block(vv)
        dov = do_ref[...]
        do_b = dov.astype(bf16)
        lse = l_ref[...]
        corr = jnp.sum(dl_ref[...], axis=-1, keepdims=True) - jnp.sum(dov * o_ref[...], axis=-1, keepdims=True)
        p_c = jnp.exp(_qk(qv, kv) * scale + bc_ref[...][None] - lse)
        ds_c = p_c * (_qk(do_b, vv) + corr)
        p_p = jnp.where(has_prev, jnp.exp(_qk(qv, kp) * scale + bp_ref[...][None] - lse), 0.0)
        ds_p = p_p * (_qk(do_b, vp) + corr)
        dq_ref[...] = ((_pv(ds_c, kv) + _pv(ds_p, kp)) * scale).astype(dq_ref.dtype)
        dk_ref[...] = ((_ptq(ds_c, qv) + _next_block(_ptq(ds_p, qv))) * scale).astype(dk_ref.dtype)
        dv_ref[...] = (_ptq(p_c, do_b) + _next_block(_ptq(p_p, do_b))).astype(dv_ref.dtype)

    head = pl.BlockSpec((None, n_blocks, QBLK, HEAD_DIM), lambda a: (a, 0, 0, 0))
    head1 = pl.BlockSpec((None, n_blocks, QBLK, 1), lambda a: (a, 0, 0, 0))
    bias = pl.BlockSpec((None, QBLK, QBLK), lambda a: (a, 0, 0))
    out = jax.ShapeDtypeStruct(q.shape, bf16)
    return pl.pallas_call(
        body, grid=(n_heads,), in_specs=[head, head, head, head, head1, head, head, bias, bias],
        out_specs=[head, head, head], out_shape=[out, out, out],
        compiler_params=_params("parallel"), name=name)(q, k, v, o, lse, do, dlse_x, jnp.asarray(bias_cur), jnp.asarray(bias_prev))


def _to_blocks(t, hpp):
    n_rows = t.shape[0]
    t = t.reshape(n_rows, len(DSWA_PATTERNS), hpp, HEAD_DIM)
    out = []
    for p, (_, dil) in enumerate(DSWA_PATTERNS):
        tp = t[:, p].reshape(n_rows // dil, dil, hpp, HEAD_DIM).transpose(2, 1, 0, 3)
        out.append(tp.reshape(hpp, n_rows // QBLK, QBLK, HEAD_DIM))
    return jnp.concatenate(out, axis=0)


def _from_blocks(t, hpp):
    n_blocks, last = t.shape[1], t.shape[3]
    n_rows = n_blocks * QBLK
    out = []
    for p, (_, dil) in enumerate(DSWA_PATTERNS):
        tp = t[p * hpp:(p + 1) * hpp].reshape(hpp, dil, n_rows // dil, last).transpose(2, 1, 0, 3)
        out.append(tp.reshape(n_rows, hpp * last))
    return out


def _loss_head(y, target, name):
    n_rows, d = y.shape
    tm = _tile(n_rows, 256, SUBLANES)

    def body(y_ref, t_ref, dy_ref, loss_ref):
        diff = y_ref[...] - t_ref[...]
        dy_ref[...] = diff * (1.0 / d)

        @pl.when(pl.program_id(0) == 0)
        def _():
            loss_ref[...] = jnp.zeros_like(loss_ref)

        loss_ref[...] += jnp.sum(jnp.sum(diff * diff, axis=1, keepdims=True), axis=0, keepdims=True) * (0.5 / d)

    row = pl.BlockSpec((tm, d), lambda i: (i, 0))
    return pl.pallas_call(
        body, grid=(n_rows // tm,), in_specs=[row, row], out_specs=[row, pl.BlockSpec((1, 1), lambda i: (0, 0))],
        out_shape=[jax.ShapeDtypeStruct(y.shape, f32), jax.ShapeDtypeStruct((1, 1), f32)],
        compiler_params=_params("arbitrary"), name=name)(y, target)


def _as2d(a):
    if a.ndim >= 2 and a.shape[-1] >= LANES:
        return a.reshape(-1, a.shape[-1])
    return a.reshape(-1, LANES) if a.size % LANES == 0 else a.reshape(1, -1)


ELEMENTWISE_BLOCK_BYTES = 2 << 20


def _row_tile(n_rows, n_cols):
    return _tile(n_rows, max(SUBLANES, ELEMENTWISE_BLOCK_BYTES // (4 * n_cols)), SUBLANES)


def _adamw(w, g, m, v, name):
    shape = w.shape
    w2, g2, m2, v2 = _as2d(w), _as2d(g), _as2d(m), _as2d(v)
    n_rows, n_cols = w2.shape
    tm = _row_tile(n_rows, n_cols)

    def body(w_ref, g_ref, m_ref, v_ref, d_ref, mo_ref, vo_ref):
        gv = g_ref[...]
        mn = ADAM_B1 * m_ref[...] + (1.0 - ADAM_B1) * gv
        vn = ADAM_B2 * v_ref[...] + (1.0 - ADAM_B2) * jnp.square(gv)
        m_hat = mn / (1.0 - ADAM_B1 ** ADAM_STEP)
        v_hat = vn / (1.0 - ADAM_B2 ** ADAM_STEP)
        d_ref[...] = -ADAM_LR * (m_hat / (jnp.sqrt(v_hat) + ADAM_EPS) + ADAM_WD * w_ref[...])
        mo_ref[...] = mn
        vo_ref[...] = vn

    row = pl.BlockSpec((tm, n_cols), lambda i: (i, 0))
    out = jax.ShapeDtypeStruct(w2.shape, f32)
    d, mn, vn = pl.pallas_call(body, grid=(n_rows // tm,), in_specs=[row] * 4, out_specs=[row] * 3, out_shape=[out] * 3,
                               compiler_params=_params("parallel"), name=name)(w2, g2, m2, v2)
    return d.reshape(shape), mn.reshape(shape), vn.reshape(shape)


def _sum_slabs(x, name):
    n = x.shape[0]
    shape = x.shape[1:]
    x3 = x.reshape(n, -1, shape[-1])
    n_rows, n_cols = x3.shape[1:]
    tm = _row_tile(n_rows, n_cols)

    def body(x_ref, o_ref):
        acc = x_ref[0].astype(f32)
        for s in range(1, n):
            acc = acc + x_ref[s].astype(f32)
        o_ref[...] = acc

    out = pl.pallas_call(
        body, grid=(n_rows // tm,), in_specs=[pl.BlockSpec((n, tm, n_cols), lambda i: (0, i, 0))],
        out_specs=pl.BlockSpec((tm, n_cols), lambda i: (i, 0)), out_shape=jax.ShapeDtypeStruct((n_rows, n_cols), f32),
        compiler_params=_params("parallel"), name=name)(x3)
    return out.reshape(shape)


def _mod_fwd(c_all, w_mod, b_cols, name):
    depth, d, cols = w_mod.shape
    tn = _tile(cols, 512)

    def body(c_ref, w_ref, b_ref, o_ref):
        cond = jax.nn.silu(c_ref[...])
        o_ref[...] = jnp.dot(cond, w_ref[...], preferred_element_type=f32, precision=lax.Precision.HIGHEST) + b_ref[...]

    return pl.pallas_call(
        body, grid=(depth, cols // tn),
        in_specs=[pl.BlockSpec((N_DEV, d), lambda l, j: (0, 0)), pl.BlockSpec((None, d, tn), lambda l, j: (l, 0, j)),
                  pl.BlockSpec((None, 1, tn), lambda l, j: (l, 0, j))],
        out_specs=pl.BlockSpec((None, N_DEV, tn), lambda l, j: (l, 0, j)),
        out_shape=jax.ShapeDtypeStruct((depth, N_DEV, cols), f32),
        compiler_params=_params("parallel", "parallel"), name=name)(c_all, w_mod, b_cols)


def _mod_wgrad(c_all_t, dmod_cols, name):
    d = c_all_t.shape[0]
    depth, _, cols = dmod_cols.shape
    tm = _tile(d, 256, SUBLANES)

    def body(c_ref, g_ref, o_ref):
        cond = jax.nn.silu(c_ref[...])
        o_ref[...] = jnp.dot(cond, g_ref[...], preferred_element_type=f32, precision=lax.Precision.HIGHEST)

    return pl.pallas_call(
        body, grid=(depth, d // tm),
        in_specs=[pl.BlockSpec((tm, N_DEV), lambda l, i: (i, 0)), pl.BlockSpec((None, N_DEV, cols), lambda l, i: (l, 0, 0))],
        out_specs=pl.BlockSpec((None, tm, cols), lambda l, i: (l, i, 0)),
        out_shape=jax.ShapeDtypeStruct((depth, d, cols), f32),
        compiler_params=_params("parallel", "parallel"), name=name)(c_all_t, dmod_cols)


def _position():
    return lax.axis_index("x"), lax.axis_index("y"), lax.axis_index("c")


def _linear(p):
    return 4 * p[0] + 2 * p[1] + p[2]


def _all_gather(xs, name):
    n = len(xs)

    def body(*refs):
        x_refs, o_refs = refs[:n], refs[n:2 * n]
        send_sems, recv_sems, local_sems = refs[2 * n:]
        x, y, c = _position()
        me, sibling = (x, y, c), (x, y, 1 - c)
        chips = [(1 - x, y), (x, 1 - y), (1 - x, 1 - y)]

        def copy(a, k, block, to, src=None):
            slab = o_refs[a].at[_linear(block)]
            return pltpu.make_async_remote_copy(
                src_ref=slab if src is None else src, dst_ref=slab, send_sem=send_sems.at[a, k], recv_sem=recv_sems.at[a, k],
                device_id=to, device_id_type=MESH)

        mine = [pltpu.make_async_copy(x_refs[a], o_refs[a].at[_linear(me)], local_sems.at[a]) for a in range(n)]
        first = []
        for a in range(n):
            mine[a].start()
            first.append(copy(a, 0, me, sibling, src=x_refs[a]))
            first += [copy(a, 1 + j, me, (*chip, c), src=x_refs[a]) for j, chip in enumerate(chips)]
        for cp in first:
            cp.start()
        passed = []
        for j, chip in enumerate(chips):
            for a in range(n):
                copy(a, 1 + j, (*chip, c), me).wait_recv()
                cp = copy(a, 4 + j, (*chip, c), sibling)
                cp.start()
                passed.append(cp)
        for a in range(n):
            copy(a, 0, sibling, me).wait_recv()
            for j, chip in enumerate(chips):
                copy(a, 4 + j, (*chip, 1 - c), me).wait_recv()
        for cp in first + passed:
            cp.wait_send()
        for cp in mine:
            cp.wait()

    return pl.pallas_call(
        body, in_specs=[ANY] * n, out_specs=[ANY] * n,
        out_shape=[jax.ShapeDtypeStruct((N_DEV,) + x.shape, x.dtype) for x in xs],
        scratch_shapes=[pltpu.SemaphoreType.DMA((n, 7)), pltpu.SemaphoreType.DMA((n, 7)), pltpu.SemaphoreType.DMA((n,))],
        name=name)(*xs)


def _pair_exchange(xs, name):
    n = len(xs)

    def body(*refs):
        x_refs, got_refs = refs[:n], refs[n:2 * n]
        send_sems, recv_sems = refs[2 * n:]
        x, y, c = _position()
        sends = []
        for a in range(n):
            for p in range(N_CHIPS):
                sends.append(pltpu.make_async_remote_copy(
                    src_ref=x_refs[a].at[p, 1 - c], dst_ref=got_refs[a].at[p], send_sem=send_sems.at[a, p],
                    recv_sem=recv_sems.at[a, p], device_id=(x, y, 1 - c), device_id_type=MESH))
        for cp in sends:
            cp.start()
        for cp in sends:
            cp.wait_recv()
        for cp in sends:
            cp.wait_send()

    sems = pltpu.SemaphoreType.DMA((n, N_CHIPS))
    return pl.pallas_call(body, in_specs=[ANY] * n, out_specs=[ANY] * n,
                          out_shape=[jax.ShapeDtypeStruct((N_CHIPS,) + x.shape[2:], x.dtype) for x in xs],
                          scratch_shapes=[sems, sems], name=name)(*xs)


def _chip_copies(x_refs, o_refs, send_sems, recv_sems, ranges=None):
    x, y, c = _position()
    copies = []
    for a in range(len(x_refs)):
        rows = pl.ds(0, x_refs[a].shape[1]) if ranges is None else pl.ds(ranges[a][0], ranges[a][1] - ranges[a][0])
        for k, (px, py) in enumerate([(1 - x, y), (x, 1 - y), (1 - x, 1 - y)]):
            copies.append(pltpu.make_async_remote_copy(
                src_ref=x_refs[a].at[2 * px + py, rows], dst_ref=o_refs[a].at[k, rows], send_sem=send_sems.at[a, k],
                recv_sem=recv_sems.at[a, k], device_id=(px, py, c), device_id_type=MESH))
    return copies


def _chip_exchange(xs, name):
    n = len(xs)

    def body(*refs):
        sends = _chip_copies(refs[:n], refs[n:2 * n], refs[2 * n], refs[2 * n + 1])
        for cp in sends:
            cp.start()
        for cp in sends:
            cp.wait_recv()
        for cp in sends:
            cp.wait_send()

    sems = pltpu.SemaphoreType.DMA((n, N_CHIPS - 1))
    return pl.pallas_call(body, in_specs=[ANY] * n, out_specs=[ANY] * n,
                          out_shape=[jax.ShapeDtypeStruct((N_CHIPS - 1,) + x.shape[1:], x.dtype) for x in xs],
                          scratch_shapes=[sems, sems], name=name)(*xs)


def _pick_add(picked, index, others, out_dtype, name):
    per_slab = picked.ndim == 4
    n_rows, n_cols = picked.shape[-2:]
    tm = _row_tile(n_rows, n_cols)
    n_other = 1 if per_slab else others.shape[0]

    def body(idx_ref, p_ref, o_ref, out_ref):
        acc = p_ref[...].astype(f32)
        if per_slab:
            acc = acc + o_ref[...].astype(f32)
        else:
            for k in range(n_other):
                acc = acc + o_ref[k].astype(f32)
        out_ref[...] = acc.astype(out_ref.dtype)

    if per_slab:
        grid = (picked.shape[0], n_rows // tm)
        in_specs = [pl.BlockSpec((None, None, tm, n_cols), lambda p, i, idx: (p, idx[0], i, 0)),
                    pl.BlockSpec((None, tm, n_cols), lambda p, i, idx: (p, i, 0))]
        out_specs = pl.BlockSpec((None, tm, n_cols), lambda p, i, idx: (p, i, 0))
        out_shape = jax.ShapeDtypeStruct((picked.shape[0], n_rows, n_cols), out_dtype)
    else:
        grid = (n_rows // tm,)
        in_specs = [pl.BlockSpec((None, tm, n_cols), lambda i, idx: (idx[0], i, 0)),
                    pl.BlockSpec((n_other, tm, n_cols), lambda i, idx: (0, i, 0))]
        out_specs = pl.BlockSpec((tm, n_cols), lambda i, idx: (i, 0))
        out_shape = jax.ShapeDtypeStruct((n_rows, n_cols), out_dtype)
    return pl.pallas_call(
        body, grid_spec=pltpu.PrefetchScalarGridSpec(num_scalar_prefetch=1, grid=grid, in_specs=in_specs, out_specs=out_specs),
        out_shape=out_shape, compiler_params=_params(*(["parallel"] * len(grid))), name=name)(
            jnp.reshape(index, (1,)).astype(jnp.int32), picked, others)


def _chip_sums(xs, name):
    c = lax.axis_index("c")
    pairs = [v.reshape((N_CHIPS, 2) + v.shape[1:]) for v in xs]
    got = _pair_exchange(pairs, f"{name}_pair")
    return [_pick_add(a, c, b, a.dtype, f"{name}_pair_sum{i}") for i, (a, b) in enumerate(zip(pairs, got))]


def _total_sums(chip_sums, arrived, name):
    my_chip = 2 * lax.axis_index("x") + lax.axis_index("y")
    return [_pick_add(a, my_chip, b, f32, f"{name}_sum{i}") for i, (a, b) in enumerate(zip(chip_sums, arrived))]


def _column_group(j, tc, shape):
    col = lax.broadcasted_iota(jnp.int32, shape, 1)
    return (j * tc + col % tc) // SSM_STATE


def _bd_build(bt, n_groups, tc, name):
    n_rows = n_groups * SSM_GROUP
    two_s = bt.shape[1]

    def body(b_ref, o_ref):
        grp = _column_group(pl.program_id(0), tc, (n_rows, 2 * tc))
        row = lax.broadcasted_iota(jnp.int32, (n_rows, 2 * tc), 0)
        tiled = jnp.concatenate([b_ref[...]] * n_groups, axis=0)
        o_ref[...] = jnp.where(grp == row // SSM_GROUP, tiled, 0.0).astype(o_ref.dtype)

    return pl.pallas_call(
        body, grid=(two_s // (2 * tc),), in_specs=[pl.BlockSpec((SSM_GROUP, 2 * tc), lambda j: (0, j))],
        out_specs=pl.BlockSpec((n_rows, 2 * tc), lambda j: (0, j)), out_shape=jax.ShapeDtypeStruct((n_rows, two_s), bf16),
        compiler_params=_params("parallel"), name=name)(bt)


def _bd_fold(m, n_groups, tc, name):
    n_rows, two_s = m.shape

    def body(m_ref, o_ref):
        grp = _column_group(pl.program_id(0), tc, (SSM_GROUP, 2 * tc))
        acc = jnp.zeros((SSM_GROUP, 2 * tc), f32)
        for g in range(n_groups):
            acc = acc + jnp.where(grp == g, m_ref[g * SSM_GROUP:(g + 1) * SSM_GROUP, :], 0.0)
        o_ref[...] = acc

    return pl.pallas_call(
        body, grid=(two_s // (2 * tc),), in_specs=[pl.BlockSpec((n_rows, 2 * tc), lambda j: (0, j))],
        out_specs=pl.BlockSpec((SSM_GROUP, 2 * tc), lambda j: (0, j)), out_shape=jax.ShapeDtypeStruct((SSM_GROUP, two_s), f32),
        compiler_params=_params("parallel"), name=name)(m)


def _ssm_setup(p, tag):
    n_groups = p["ssm_a_re"].shape[0]
    s = n_groups * SSM_STATE
    tc = min(SCAN_TC, s)
    log_dt = p["ssm_log_dt"].reshape(n_groups, 1)
    br_t = p["ssm_b_re"].transpose(2, 0, 1)
    bi_t = p["ssm_b_im"].transpose(2, 0, 1)
    disc_in = (log_dt, p["ssm_a_re"], p["ssm_a_im"], br_t, bi_t)
    lr, li, bbr, bbi = _disc_fwd(*disc_in, name=f"ssm_disc{tag}")
    lam = _interleave(lr.reshape(1, s), li.reshape(1, s), tc)
    bb = _bd_build(_interleave(bbr.reshape(SSM_GROUP, s), bbi.reshape(SSM_GROUP, s), tc), n_groups, tc, f"ssm_in_map{tag}")
    cr_t = p["ssm_c_re"].transpose(1, 0, 2).reshape(SSM_GROUP, s)
    ci_t = p["ssm_c_im"].transpose(1, 0, 2).reshape(SSM_GROUP, s)
    cc = _bd_build(_interleave(cr_t, -ci_t, tc), n_groups, tc, f"ssm_out_map{tag}")
    return dict(disc_in=disc_in, lam=lam, bb=bb, cc=cc, tc=tc, n_groups=n_groups)


def _apply_w(a, w, key, out_dtype, name):
    return _mm(a, w[key], out_dtype, name, rhs_t=key in COL_SHARDED)


def _apply_wt(dy, w, key, out_dtype, name, carry=()):
    return _mm(dy, w[key], out_dtype, name, rhs_t=key not in COL_SHARDED, carry=carry)


class _Cargo:
    def __init__(self, chip_sums):
        self.chip_sums = chip_sums
        self.arrived = {}

    def hosted(self, call, jobs):
        if self.chip_sums is None:
            return call(())
        carry = []
        for key, part, parts in jobs:
            n_rows = self.chip_sums[key].shape[1]
            step = n_rows if parts == 1 else -(-n_rows // parts // BF16_ROWS) * BF16_ROWS
            carry.append((self.chip_sums[key], part * step, min(n_rows, (part + 1) * step), self.arrived.get(key)))
        out, got = call(carry)
        for (key, _, _), buf in zip(jobs, got):
            self.arrived[key] = buf
        return out


def _layer_fwd(x, mod, w, p, tag):
    n_rows, d = x.shape
    w4 = d // 4
    hpp = w4 // HEAD_DIM
    off_q, off_k, off_v, off_conv, off_gate = w4, 4 * w4, 7 * w4, 10 * w4, 13 * w4
    sh1, sc1, gt1, sh2, sc2, gt2 = mod
    vec = lambda a: a.reshape(1, -1)
    s = dict(x=x)
    (s["h"],) = _rowwise(_f_prenorm, [x], [vec(p["g_pre_mix"]), sc1, sh1], [bf16], 256, f"prenorm_mix{tag}")
    proj = s["proj"] = _apply_w(s["h"], w, "w_in", f32, f"in_proj{tag}")
    ssm = s["ssm"] = _ssm_setup(p, tag)
    s["u32"] = proj[:, :off_q]
    s["u"] = s["u32"].astype(bf16)
    xin = _mm(s["u"], ssm["bb"], f32, f"ssm_in{tag}")
    s["hf"], s["hb"] = _scan(xin, ssm["lam"], False, f"ssm_scan{tag}")
    s["yc"] = _mm(s["hb"], ssm["cc"], f32, f"ssm_out{tag}", rhs_t=True)
    (s["g"],) = _rowwise(_f_gelu, [s["yc"], s["u32"]], [vec(p["ssm_d"])], [bf16], 256, f"ssm_gelu{tag}")
    s["z"] = _apply_w(s["g"], w, "w_glu", f32, f"glu_proj{tag}")
    (s["s_ssm"],) = _rowwise(_f_glu, [s["g"], s["z"]], [vec(p["b_glu"])], [bf16], 256, f"glu{tag}")
    s["y_ssm"] = _apply_w(s["s_ssm"], w, "w_ssm_out", f32, f"ssm_proj{tag}")
    s["qb"] = _to_blocks(proj[:, off_q:off_k].astype(bf16), hpp)
    s["kb"] = _to_blocks(proj[:, off_k:off_v].astype(bf16), hpp)
    s["vb"] = _to_blocks(proj[:, off_v:off_conv].astype(bf16), hpp)
    s["ob"], s["lseb"] = _attn_fwd(s["qb"], s["kb"], s["vb"], hpp, f"attn{tag}")
    s["o_tok"] = _from_blocks(s["ob"], hpp)
    s["lse_x"] = [jnp.repeat(l, HEAD_DIM, axis=1) for l in _from_blocks(s["lseb"], hpp)]
    (s["attn_o"],) = _rowwise(_f_amerge, s["o_tok"] + s["lse_x"], [], [bf16], 256, f"attn_merge{tag}")
    s["y_attn"] = _apply_w(s["attn_o"], w, "w_attn_out", f32, f"attn_proj{tag}")
    s["cm"] = _convgate_fwd(proj, p["conv_mix_w"], off_conv, w4, f"convgate{tag}")
    s["y_conv"] = _apply_w(s["cm"], w, "w_conv_out", f32, f"conv_proj{tag}")
    s["pg"] = [proj[:, off_gate + i * d:off_gate + (i + 1) * d] for i in range(3)]
    s["bg"] = [p["b_gate"][i * d:(i + 1) * d].reshape(1, d) for i in range(3)]
    (s["merged"],) = _rowwise(_f_merge, s["pg"] + [s["y_ssm"], s["y_attn"], s["y_conv"]], s["bg"], [bf16], 128, f"merge{tag}")
    s["y"] = _apply_w(s["merged"], w, "w_o", f32, f"o_proj{tag}")
    (s["x1"],) = _rowwise(_f_postres, [x, s["y"]], [vec(p["g_post_mix"]), gt1], [f32], 256, f"postres_mix{tag}")
    (s["h2"],) = _rowwise(_f_prenorm, [s["x1"]], [vec(p["g_pre_ffn"]), sc2, sh2], [bf16], 256, f"prenorm_ffn{tag}")
    s["up"] = _apply_w(s["h2"], w, "w_up", f32, f"up_proj{tag}")
    s["act"] = _ffnconv_fwd(s["up"], p["ffn_conv_w"], f"ffnconv{tag}")
    s["y2"] = _apply_w(s["act"], w, "w_down", f32, f"down_proj{tag}")
    (x2,) = _rowwise(_f_postres, [s["x1"], s["y2"]], [vec(p["g_post_ffn"]), gt2], [f32], 256, f"postres_ffn{tag}")
    return x2, s


def _layer_bwd(dx2, s, mod, w, p, tag, cargo):
    x = s["x"]
    n_rows, d = x.shape
    w4 = d // 4
    hpp = w4 // HEAD_DIM
    off_conv = 10 * w4
    sh1, sc1, gt1, sh2, sc2, gt2 = mod
    vec = lambda a: a.reshape(1, -1)
    gw, gp = {}, {}
    dy2, d_gpf, d_gt2 = _rowwise_bwd(_f_postres, [s["x1"], s["y2"]], [vec(p["g_post_ffn"]), gt2], [[dx2]], [None, bf16], True,
                                     256, f"postres_ffn_bwd{tag}")
    gp["g_post_ffn"] = d_gpf
    dact = cargo.hosted(lambda carry: _apply_wt(dy2, w, "w_down", bf16, f"down_dx{tag}", carry), [("w_down", 0, 2)])
    gw["w_down"] = cargo.hosted(lambda carry: _mm(s["act"].T, dy2, bf16, f"down_dw{tag}", carry=carry), [("w_down", 1, 2)])
    da, db, gp["ffn_conv_w"] = _ffnconv_bwd(s["up"], p["ffn_conv_w"], dact, f"ffnconv_bwd{tag}")
    dup = jnp.concatenate([da, db], axis=1)
    dh2 = cargo.hosted(lambda carry: _apply_wt(dup, w, "w_up", f32, f"up_dx{tag}", carry), [("w_in", 0, 2)])
    gw["w_up"] = cargo.hosted(lambda carry: _mm(s["h2"].T, dup, bf16, f"up_dw{tag}", carry=carry), [("w_up", 0, 2)])
    dx1, d_g, d_sc2, d_sh2 = _rowwise_bwd(_f_prenorm, [s["x1"]], [vec(p["g_pre_ffn"]), sc2, sh2], [[dh2]], [f32], True,
                                          256, f"prenorm_ffn_bwd{tag}", add_to={0: dx2})
    gp["g_pre_ffn"] = d_g
    dy, d_gpm, d_gt1 = _rowwise_bwd(_f_postres, [x, s["y"]], [vec(p["g_post_mix"]), gt1], [[dx1]], [None, bf16], True,
                                    256, f"postres_mix_bwd{tag}")
    gp["g_post_mix"] = d_gpm
    dmerged = cargo.hosted(lambda carry: _apply_wt(dy, w, "w_o", f32, f"o_dx{tag}", carry), [("w_o", 0, 1)])
    gw["w_o"] = cargo.hosted(lambda carry: _mm(s["merged"].T, dy, bf16, f"o_dw{tag}", carry=carry),
                             [(k, 0, 1) for k in LIGHT_SHARDED])
    res = _rowwise_bwd(_f_merge, s["pg"] + [s["y_ssm"], s["y_attn"], s["y_conv"]], s["bg"], [[dmerged]], [bf16] * 6, True,
                       128, f"merge_bwd{tag}")
    dpg, (dys, dya, dyv), dbg = list(res[0:3]), res[3:6], res[6:9]
    gp["b_gate"] = jnp.concatenate(dbg, axis=1)
    ds_ssm = _apply_wt(dys, w, "w_ssm_out", f32, f"ssm_proj_dx{tag}")
    gw["w_ssm_out"] = _mm(s["s_ssm"].T, dys, bf16, f"ssm_proj_dw{tag}")
    dattn_o = _apply_wt(dya, w, "w_attn_out", f32, f"attn_proj_dx{tag}")
    gw["w_attn_out"] = _mm(s["attn_o"].T, dya, bf16, f"attn_proj_dw{tag}")
    dcm = _apply_wt(dyv, w, "w_conv_out", f32, f"conv_proj_dx{tag}")
    gw["w_conv_out"] = _mm(s["cm"].T, dyv, bf16, f"conv_proj_dw{tag}")
    ssm = s["ssm"]
    tc, n_groups = ssm["tc"], ssm["n_groups"]
    dg_a, dz, gp["b_glu"] = _rowwise_bwd(_f_glu, [s["g"], s["z"]], [vec(p["b_glu"])], [[ds_ssm]], [f32, bf16], True,
                                         256, f"glu_bwd{tag}")
    dg_b = _apply_wt(dz, w, "w_glu", f32, f"glu_proj_dx{tag}")
    gw["w_glu"] = _mm(s["g"].T, dz, bf16, f"glu_proj_dw{tag}")
    dyc, du_skip, gp["ssm_d"] = _rowwise_bwd(_f_gelu, [s["yc"], s["u32"]], [vec(p["ssm_d"])], [[dg_a, dg_b]], [bf16, f32], True,
                                             256, f"ssm_gelu_bwd{tag}")
    dh_state = _mm(dyc, ssm["cc"], f32, f"ssm_out_dx{tag}")
    g_cc = _bd_fold(_mm(dyc, s["hb"], f32, f"ssm_out_dw{tag}", lhs_t=True), n_groups, tc, f"ssm_out_dw_fold{tag}")
    gf, gb = _scan(dh_state, ssm["lam"], True, f"ssm_scan_bwd{tag}")
    d_lam = _lam_grad(gf, s["hf"], f"ssm_lam_grad{tag}")
    du_x = _mm(gb, ssm["bb"], f32, f"ssm_in_dx{tag}", rhs_t=True)
    g_bb = _bd_fold(_mm(s["u"], gb, f32, f"ssm_in_dw{tag}", lhs_t=True), n_groups, tc, f"ssm_in_dw_fold{tag}")
    (du,) = _rowwise(_f_add, [du_skip, du_x], [], [bf16], 256, f"ssm_du{tag}")
    d_lr, d_li = _deinterleave(d_lam, tc)
    g_bbr, g_bbi = _deinterleave(g_bb, tc)
    group_shape = (SSM_GROUP, n_groups, SSM_STATE)
    cots = (d_lr.reshape(n_groups, SSM_STATE), d_li.reshape(n_groups, SSM_STATE), g_bbr.reshape(group_shape), g_bbi.reshape(group_shape))
    d_ldt, d_ar, d_ai, d_br, d_bi = _disc_bwd(*ssm["disc_in"], cots, f"ssm_disc_bwd{tag}")
    gp["ssm_log_dt"], gp["ssm_a_re"], gp["ssm_a_im"] = d_ldt.reshape(-1), d_ar, d_ai
    gp["ssm_b_re"], gp["ssm_b_im"] = d_br.transpose(1, 2, 0), d_bi.transpose(1, 2, 0)
    g_ccr, g_cci = _deinterleave(g_cc, tc)
    gp["ssm_c_re"] = g_ccr.reshape(group_shape).transpose(1, 0, 2)
    gp["ssm_c_im"] = -g_cci.reshape(group_shape).transpose(1, 0, 2)
    res = _rowwise_bwd(_f_amerge, s["o_tok"] + s["lse_x"], [], [[dattn_o]], [f32] * 6, False, 256, f"attn_merge_bwd{tag}")
    do_b = _to_blocks(jnp.concatenate(res[0:3], axis=1), hpp)
    dl_b = _to_blocks(jnp.concatenate(res[3:6], axis=1), hpp)
    dqb, dkb, dvb = _attn_bwd(s["qb"], s["kb"], s["vb"], s["ob"], s["lseb"], do_b, dl_b, hpp, f"attn_bwd{tag}")
    dq, dk, dv = (jnp.concatenate(_from_blocks(t, hpp), axis=1) for t in (dqb, dkb, dvb))
    dcb, dcc, dch, gp["conv_mix_w"] = _convgate_bwd(s["proj"], p["conv_mix_w"], dcm, off_conv, w4, f"convgate_bwd{tag}")
    dproj = jnp.concatenate([du, dq, dk, dv, dcb, dcc, dch] + dpg, axis=1)
    dh = cargo.hosted(lambda carry: _apply_wt(dproj, w, "w_in", f32, f"in_dx{tag}", carry), [("w_in", 1, 2)])
    gw["w_in"] = cargo.hosted(lambda carry: _mm(s["h"].T, dproj, bf16, f"in_dw{tag}", carry=carry), [("w_up", 1, 2)])
    dx0, d_g, d_sc1, d_sh1 = _rowwise_bwd(_f_prenorm, [x], [vec(p["g_pre_mix"]), sc1, sh1], [[dh]], [f32], True,
                                          256, f"prenorm_mix_bwd{tag}", add_to={0: dx1})
    gp["g_pre_mix"] = d_g
    return dx0, [d_sh1, d_sc1, d_gt1, d_sh2, d_sc2, d_gt2], gw, gp


COL_SHARDED = ("w_in", "w_up", "w_ssm_out", "w_attn_out", "w_conv_out")
ROW_SHARDED = ("w_down", "w_o", "w_glu")
SMALL_SHARDED = ("conv_mix_w", "ffn_conv_w")
LIGHT_SHARDED = ("w_ssm_out", "w_attn_out", "w_conv_out", "w_glu") + SMALL_SHARDED
REPLICATED = ("b_mod", "g_pre_mix", "g_post_mix", "g_pre_ffn", "g_post_ffn", "ssm_log_dt", "ssm_a_re", "ssm_a_im", "ssm_b_re",
              "ssm_b_im", "ssm_c_re", "ssm_c_im", "ssm_d", "b_glu", "b_gate")
WEIGHTS = ('w_mod', 'b_mod', 'g_pre_mix', 'g_post_mix', 'g_pre_ffn', 'g_post_ffn', 'w_in', 'ssm_log_dt', 'ssm_a_re', 'ssm_a_im',
           'ssm_b_re', 'ssm_b_im', 'ssm_c_re', 'ssm_c_im', 'ssm_d', 'w_glu', 'b_glu', 'conv_mix_w', 'w_ssm_out', 'w_attn_out',
           'w_conv_out', 'b_gate', 'w_o', 'w_up', 'ffn_conv_w', 'w_down')


def _full_cols(g):
    return g.transpose(1, 0, 2).reshape(g.shape[1], -1)


def _col_slabs(m):
    return m.reshape(m.shape[0], N_DEV, -1).transpose(1, 0, 2)


def _step(x, c, loss_target, wts, m_in, v_in):
    depth = wts["w_in"].shape[0]
    n_rows, d = x.shape[1], x.shape[2]
    me = _linear(_position())
    x2d, target2d = x[0], loss_target[0]

    (c_all,) = _all_gather([c], "gather_c")
    c_all = c_all.reshape(N_DEV, d)
    mod_cols = wts["w_mod"].shape[2]
    b_cols = lax.dynamic_slice_in_dim(wts["b_mod"], me * mod_cols, mod_cols, axis=1).reshape(depth, 1, mod_cols)
    mod_mine = _mod_fwd(c_all, wts["w_mod"], b_cols, "mod_fwd")
    (mod_all,) = _all_gather([mod_mine], "gather_mod")
    mod_me = lax.dynamic_index_in_dim(mod_all, me, axis=2, keepdims=False).transpose(1, 0, 2).reshape(depth, 6, 1, d)

    mats = []
    for l in range(depth):
        names = COL_SHARDED + ROW_SHARDED
        shards = [(wts[k][l].T if k in COL_SHARDED else wts[k][l]).astype(bf16) for k in names]
        got = _all_gather(shards, f"gather_w{l}")
        mats.append({k: g.reshape(-1, g.shape[2]) for k, g in zip(names, got)})
    (conv_mix_all, ffn_conv_all) = _all_gather([wts["conv_mix_w"], wts["ffn_conv_w"]], "gather_conv_w")
    small = []
    for l in range(depth):
        p = {k: wts[k][l] for k in REPLICATED if k != "b_mod"}
        p["conv_mix_w"] = _full_cols(conv_mix_all[:, l])
        p["ffn_conv_w"] = _full_cols(ffn_conv_all[:, l])
        small.append(p)

    saved = []
    h = x2d
    for l in range(depth):
        mod = [mod_me[l, i] for i in range(6)]
        h, s = _layer_fwd(h, mod, mats[l], small[l], f"_l{l}")
        saved.append(s)
    dy, loss_part = _loss_head(h, target2d, "loss_head")
    loss = lax.psum(loss_part[0, 0], ("x", "y", "c"))

    grads = {k: [None] * depth for k in WEIGHTS}
    dmod_rows = [None] * depth
    small_parts = [None] * depth
    sharded = COL_SHARDED + ROW_SHARDED + SMALL_SHARDED
    waiting = None
    for l in reversed(range(depth)):
        mod = [mod_me[l, i] for i in range(6)]
        cargo = _Cargo(waiting)
        dy, dmod, gw, gp = _layer_bwd(dy, saved[l], mod, mats[l], small[l], f"_l{l}", cargo)
        if waiting is not None:
            summed = _total_sums([waiting[k] for k in sharded], [cargo.arrived[k] for k in sharded], f"scatter_g{l + 1}")
            for k, g in zip(sharded, summed):
                grads[k][l + 1] = g
        dmod_rows[l] = jnp.concatenate(dmod, axis=1)
        slabs = [(_col_slabs(gw[k]) if k in COL_SHARDED else gw[k].reshape(N_DEV, -1, gw[k].shape[1])) for k in COL_SHARDED + ROW_SHARDED]
        slabs += [_col_slabs(gp[k]) for k in SMALL_SHARDED]
        waiting = dict(zip(sharded, _chip_sums(slabs, f"scatter_g{l}")))
        small_parts[l] = gp
    last = [waiting[k] for k in sharded]
    for k, g in zip(sharded, _total_sums(last, _chip_exchange(last, "scatter_g0_chips"), "scatter_g0")):
        grads[k][0] = g
    grad_x = dy.reshape(x.shape)

    dmod_mine = jnp.concatenate(dmod_rows, axis=0)
    rep_names = [k for k in REPLICATED if k != "b_mod"]
    pieces = [dmod_mine.reshape(-1)] + [jnp.stack([small_parts[l][k].reshape(-1) for l in range(depth)]).reshape(-1) for k in rep_names]
    sizes = [int(pc.shape[0]) for pc in pieces]
    total = sum(sizes)
    padded = -(-total // (SUBLANES * LANES)) * (SUBLANES * LANES)
    pack = jnp.concatenate(pieces + [jnp.zeros((padded - total,), f32)]).reshape(-1, LANES)
    (pack_all,) = _all_gather([pack], "gather_small_grads")
    pack_sum = _sum_slabs(pack_all, "sum_small_grads").reshape(-1)
    offs = np.cumsum([0] + sizes)
    grads["b_mod"] = pack_sum[offs[0]:offs[1]].reshape(wts["b_mod"].shape)
    for i, k in enumerate(rep_names):
        grads[k] = pack_sum[offs[i + 1]:offs[i + 2]].reshape(wts[k].shape)
    dmod_all = pack_all.reshape(N_DEV, -1)[:, :sizes[0]].reshape(N_DEV, depth, 6 * d)
    dmod_cols = lax.dynamic_slice_in_dim(dmod_all, me * mod_cols, mod_cols, axis=2).transpose(1, 0, 2)
    grads["w_mod"] = _mod_wgrad(c_all.T, dmod_cols, "mod_wgrad")
    for k in COL_SHARDED + ROW_SHARDED + SMALL_SHARDED:
        grads[k] = jnp.stack(grads[k])

    delta, new_m, new_v = {}, {}, {}
    for k in WEIGHTS:
        delta[k], new_m[k], new_v[k] = _adamw(wts[k], grads[k], m_in[k], v_in[k], f"adamw_{k}")
    return (loss, grad_x, *[grads[k] for k in WEIGHTS], *[delta[k] for k in WEIGHTS], *[new_m[k] for k in WEIGHTS],
            *[new_v[k] for k in WEIGHTS])


def kernel(x, c, w_mod, b_mod, g_pre_mix, g_post_mix, g_pre_ffn, g_post_ffn, w_in, ssm_log_dt, ssm_a_re, ssm_a_im, ssm_b_re, ssm_b_im, ssm_c_re, ssm_c_im, ssm_d, w_glu, b_glu, conv_mix_w, w_ssm_out, w_attn_out, w_conv_out, b_gate, w_o, w_up, ffn_conv_w, w_down, loss_target, m_w_mod, m_b_mod, m_g_pre_mix, m_g_post_mix, m_g_pre_ffn, m_g_post_ffn, m_w_in, m_ssm_log_dt, m_ssm_a_re, m_ssm_a_im, m_ssm_b_re, m_ssm_b_im, m_ssm_c_re, m_ssm_c_im, m_ssm_d, m_w_glu, m_b_glu, m_conv_mix_w, m_w_ssm_out, m_w_attn_out, m_w_conv_out, m_b_gate, m_w_o, m_w_up, m_ffn_conv_w, m_w_down, v_w_mod, v_b_mod, v_g_pre_mix, v_g_post_mix, v_g_pre_ffn, v_g_post_ffn, v_w_in, v_ssm_log_dt, v_ssm_a_re, v_ssm_a_im, v_ssm_b_re, v_ssm_b_im, v_ssm_c_re, v_ssm_c_im, v_ssm_d, v_w_glu, v_b_glu, v_conv_mix_w, v_w_ssm_out, v_w_attn_out, v_w_conv_out, v_b_gate, v_w_o, v_w_up, v_ffn_conv_w, v_w_down):
    wts = dict(w_mod=w_mod, b_mod=b_mod, g_pre_mix=g_pre_mix, g_post_mix=g_post_mix, g_pre_ffn=g_pre_ffn, g_post_ffn=g_post_ffn, w_in=w_in, ssm_log_dt=ssm_log_dt, ssm_a_re=ssm_a_re, ssm_a_im=ssm_a_im, ssm_b_re=ssm_b_re, ssm_b_im=ssm_b_im, ssm_c_re=ssm_c_re, ssm_c_im=ssm_c_im, ssm_d=ssm_d, w_glu=w_glu, b_glu=b_glu, conv_mix_w=conv_mix_w, w_ssm_out=w_ssm_out, w_attn_out=w_attn_out, w_conv_out=w_conv_out, b_gate=b_gate, w_o=w_o, w_up=w_up, ffn_conv_w=ffn_conv_w, w_down=w_down)
    m_in = dict(w_mod=m_w_mod, b_mod=m_b_mod, g_pre_mix=m_g_pre_mix, g_post_mix=m_g_post_mix, g_pre_ffn=m_g_pre_ffn, g_post_ffn=m_g_post_ffn, w_in=m_w_in, ssm_log_dt=m_ssm_log_dt, ssm_a_re=m_ssm_a_re, ssm_a_im=m_ssm_a_im, ssm_b_re=m_ssm_b_re, ssm_b_im=m_ssm_b_im, ssm_c_re=m_ssm_c_re, ssm_c_im=m_ssm_c_im, ssm_d=m_ssm_d, w_glu=m_w_glu, b_glu=m_b_glu, conv_mix_w=m_conv_mix_w, w_ssm_out=m_w_ssm_out, w_attn_out=m_w_attn_out, w_conv_out=m_w_conv_out, b_gate=m_b_gate, w_o=m_w_o, w_up=m_w_up, ffn_conv_w=m_ffn_conv_w, w_down=m_w_down)
    v_in = dict(w_mod=v_w_mod, b_mod=v_b_mod, g_pre_mix=v_g_pre_mix, g_post_mix=v_g_post_mix, g_pre_ffn=v_g_pre_ffn, g_post_ffn=v_g_post_ffn, w_in=v_w_in, ssm_log_dt=v_ssm_log_dt, ssm_a_re=v_ssm_a_re, ssm_a_im=v_ssm_a_im, ssm_b_re=v_ssm_b_re, ssm_b_im=v_ssm_b_im, ssm_c_re=v_ssm_c_re, ssm_c_im=v_ssm_c_im, ssm_d=v_ssm_d, w_glu=v_w_glu, b_glu=v_b_glu, conv_mix_w=v_conv_mix_w, w_ssm_out=v_w_ssm_out, w_attn_out=v_w_attn_out, w_conv_out=v_w_conv_out, b_gate=v_b_gate, w_o=v_w_o, w_up=v_w_up, ffn_conv_w=v_ffn_conv_w, w_down=v_w_down)
    return _step(x, c, loss_target, wts, m_in, v_in)
```

```python
import functools
import math

import numpy as np
import jax
import jax.numpy as jnp
from jax import lax
from jax.experimental import pallas as pl
from jax.experimental.pallas import tpu as pltpu

f32 = jnp.float32
bf16 = jnp.bfloat16

N_DEV = 8
N_CHIPS = 4
V7X_VMEM_LIMIT_BYTES = 56 * 1024 * 1024
LANES = 128
SUBLANES = 8
BF16_ROWS = 16

RMS_EPS = 1e-6
NEG_INF = -1e30
HEAD_DIM = 64
SSM_GROUP = 16
SSM_STATE = 64
DSWA_PATTERNS = ((128, 1), (512, 4), (2048, 16))
QBLK = 128

ADAM_LR = 0.001
ADAM_B1 = 0.9
ADAM_B2 = 0.999
ADAM_EPS = 1e-08
ADAM_WD = 0.01
ADAM_STEP = 10

MESH = pl.DeviceIdType.MESH
ANY = pl.BlockSpec(memory_space=pl.ANY)


def _tile(n, pref, mult=LANES):
    t = min(pref, n) // mult * mult
    while t >= mult:
        if n % t == 0:
            return t
        t -= mult
    return n


def _params(*sem):
    return pltpu.CompilerParams(dimension_semantics=sem, vmem_limit_bytes=V7X_VMEM_LIMIT_BYTES)


def _mm(a, b, out_dtype, name, rhs_t=False, lhs_t=False, carry=()):
    k, m = a.shape if lhs_t else a.shape[::-1]
    n, k2 = b.shape if rhs_t else b.shape[::-1]
    assert k == k2 and a.dtype == bf16 and b.dtype == bf16, (a.shape, b.shape, a.dtype, b.dtype)
    tm, tn, tk = _tile(m, 1024, LANES if lhs_t else SUBLANES), _tile(n, 512), _tile(k, 2048)
    grid = (m // tm, n // tn, k // tk)
    contract = (((0 if lhs_t else 1,), (1 if rhs_t else 0,)), ((), ()))
    nc = len(carry)
    srcs = [job[0] for job in carry]
    ranges = [(job[1], job[2]) for job in carry]
    intos = [(i, job[3]) for i, job in enumerate(carry) if job[3] is not None]
    n_in = 2 + nc + len(intos)

    def body(*refs):
        a_ref, b_ref = refs[:2]
        o_ref = refs[n_in]
        acc_ref = refs[n_in + 1 + nc]
        steps = [pl.program_id(axis) for axis in range(3)]
        if nc:
            copies = _chip_copies(refs[2:2 + nc], refs[n_in + 1:n_in + 1 + nc], refs[n_in + 2 + nc], refs[n_in + 3 + nc], ranges)

            @pl.when((steps[0] == 0) & (steps[1] == 0) & (steps[2] == 0))
            def _():
                for cp in copies:
                    cp.start()

        @pl.when(steps[2] == 0)
        def _():
            acc_ref[...] = jnp.zeros_like(acc_ref)

        acc_ref[...] += lax.dot_general(a_ref[...], b_ref[...], contract, preferred_element_type=f32)

        @pl.when(steps[2] == grid[2] - 1)
        def _():
            o_ref[...] = acc_ref[...].astype(o_ref.dtype)

        if nc:
            @pl.when((steps[0] == grid[0] - 1) & (steps[1] == grid[1] - 1) & (steps[2] == grid[2] - 1))
            def _():
                for cp in copies:
                    cp.wait_recv()
                for cp in copies:
                    cp.wait_send()

    a_spec = pl.BlockSpec((tk, tm), lambda i, j, kk: (kk, i)) if lhs_t else pl.BlockSpec((tm, tk), lambda i, j, kk: (i, kk))
    b_spec = pl.BlockSpec((tn, tk), lambda i, j, kk: (j, kk)) if rhs_t else pl.BlockSpec((tk, tn), lambda i, j, kk: (kk, j))
    sems = pltpu.SemaphoreType.DMA((max(nc, 1), N_CHIPS - 1))
    res = pl.pallas_call(
        body, grid=grid,
        in_specs=[a_spec, b_spec] + [ANY] * (nc + len(intos)),
        out_specs=[pl.BlockSpec((tm, tn), lambda i, j, kk: (i, j))] + [ANY] * nc,
        out_shape=[jax.ShapeDtypeStruct((m, n), out_dtype)]
        + [jax.ShapeDtypeStruct((N_CHIPS - 1,) + x.shape[1:], x.dtype) for x in srcs],
        scratch_shapes=[pltpu.VMEM((tm, tn), f32)] + ([sems, sems] if nc else []),
        input_output_aliases={2 + nc + j: 1 + i for j, (i, _) in enumerate(intos)},
        compiler_params=_params(*(("arbitrary",) * 3 if nc else ("parallel", "parallel", "arbitrary"))), name=name)(
            a, b, *srcs, *[buf for _, buf in intos])
    return (res[0], list(res[1:])) if nc else res[0]


def _rowwise(fn, rows, vecs, out_dtypes, tm, name):
    n_rows = rows[0].shape[0]
    nr, nv = len(rows), len(vecs)
    outs = jax.eval_shape(fn, *[jax.ShapeDtypeStruct((tm, r.shape[1]), f32) for r in rows],
                          *[jax.ShapeDtypeStruct(v.shape, f32) for v in vecs])

    def body(*refs):
        vals = [r[...].astype(f32) for r in refs[:nr]] + [v[...] for v in refs[nr:nr + nv]]
        for o_ref, val in zip(refs[nr + nv:], fn(*vals)):
            o_ref[...] = val.astype(o_ref.dtype)

    res = pl.pallas_call(
        body, grid=(n_rows // tm,),
        in_specs=[pl.BlockSpec((tm, r.shape[1]), lambda i: (i, 0)) for r in rows]
        + [pl.BlockSpec(v.shape, lambda i: (0, 0)) for v in vecs],
        out_specs=[pl.BlockSpec((tm, o.shape[1]), lambda i: (i, 0)) for o in outs],
        out_shape=[jax.ShapeDtypeStruct((n_rows, o.shape[1]), dt) for o, dt in zip(outs, out_dtypes)],
        compiler_params=_params("parallel"), name=name)(*rows, *vecs)
    return res


def _rowwise_bwd(fn, rows, vecs, cots, row_dtypes, need_vecs, tm, name, add_to=None):
    n_rows = rows[0].shape[0]
    nr, nv = len(rows), len(vecs)
    add_to = add_to or {}
    flat_cots = [c for group in cots for c in group]
    add_keys = sorted(add_to)
    row_out = [i for i, dt in enumerate(row_dtypes) if dt is not None]
    n_in = nr + nv + len(flat_cots) + len(add_keys)

    def body(*refs):
        vals = [r[...].astype(f32) for r in refs[:nr]] + [v[...] for v in refs[nr:nr + nv]]
        pos = nr + nv
        cvals = []
        for group in cots:
            acc = refs[pos][...].astype(f32)
            for extra in range(1, len(group)):
                acc = acc + refs[pos + extra][...].astype(f32)
            pos += len(group)
            cvals.append(acc)
        adds = {key: refs[pos + j] for j, key in enumerate(add_keys)}
        _, vjp = jax.vjp(fn, *vals)
        grads = vjp(tuple(cvals))
        out_refs = refs[n_in:]
        for j, i in enumerate(row_out):
            g = grads[i]
            if i in adds:
                g = g + adds[i][...].astype(f32)
            out_refs[j][...] = g.astype(out_refs[j].dtype)
        if need_vecs:
            step = pl.program_id(0)
            for j in range(nv):
                acc_ref = out_refs[len(row_out) + j]

                @pl.when(step == 0)
                def _(acc_ref=acc_ref):
                    acc_ref[...] = jnp.zeros_like(acc_ref)

                acc_ref[...] += grads[nr + j]

    row_spec = lambda a: pl.BlockSpec((tm, a.shape[1]), lambda i: (i, 0))
    in_specs = ([row_spec(r) for r in rows] + [pl.BlockSpec(v.shape, lambda i: (0, 0)) for v in vecs]
                + [row_spec(c) for c in flat_cots] + [row_spec(add_to[key]) for key in add_keys])
    out_specs = [row_spec(rows[i]) for i in row_out]
    out_shape = [jax.ShapeDtypeStruct(rows[i].shape, row_dtypes[i]) for i in row_out]
    if need_vecs:
        out_specs += [pl.BlockSpec(v.shape, lambda i: (0, 0)) for v in vecs]
        out_shape += [jax.ShapeDtypeStruct(v.shape, f32) for v in vecs]
    return pl.pallas_call(
        body, grid=(n_rows // tm,), in_specs=in_specs, out_specs=out_specs, out_shape=out_shape,
        compiler_params=_params("arbitrary"), name=name)(*rows, *vecs, *flat_cots, *[add_to[key] for key in add_keys])


def _f_prenorm(x, g, sc, sh):
    y = x * lax.rsqrt(jnp.mean(x * x, axis=-1, keepdims=True) + RMS_EPS)
    return (y * g * (1.0 + sc) + sh,)


def _f_postres(x, y, g, gt):
    n = y * lax.rsqrt(jnp.mean(y * y, axis=-1, keepdims=True) + RMS_EPS)
    return (x + gt * (n * g),)


def _f_gelu(yc, u, d):
    return (jax.nn.gelu(yc + d * u),)


def _f_glu(g, z, b):
    return (g * jax.nn.sigmoid(z + b),)


def _f_merge(p0, p1, p2, ys, ya, yc, b0, b1, b2):
    return (jax.nn.sigmoid(p0 + b0) * ys + jax.nn.sigmoid(p1 + b1) * ya + jax.nn.sigmoid(p2 + b2) * yc,)


def _f_amerge(o0, o1, o2, l0, l1, l2):
    m = jnp.maximum(jnp.maximum(l0, l1), l2)
    e0, e1, e2 = jnp.exp(l0 - m), jnp.exp(l1 - m), jnp.exp(l2 - m)
    return ((e0 * o0 + e1 * o1 + e2 * o2) / (e0 + e1 + e2),)


def _f_add(a, b):
    return (a + b,)


def _shift_down(z, k, row):
    return jnp.where(row >= k, pltpu.roll(z, k, 0), 0.0)


def _shift_up(z, k, row):
    n = z.shape[0]
    return jnp.where(row < n - k, pltpu.roll(z, n - k, 0), 0.0)


def _conv3(z, w, row):
    return w[0:1, :] * z + w[1:2, :] * _shift_down(z, 1, row) + w[2:3, :] * _shift_down(z, 2, row)


def _conv3_bwd(z, w, dy, row):
    dz = w[0:1, :] * dy + w[1:2, :] * _shift_up(dy, 1, row) + w[2:3, :] * _shift_up(dy, 2, row)
    dw = [jnp.sum(dy * z, axis=0, keepdims=True),
          jnp.sum(dy * _shift_down(z, 1, row), axis=0, keepdims=True),
          jnp.sum(dy * _shift_down(z, 2, row), axis=0, keepdims=True)]
    return dz, dw


def _convgate_fwd(proj, w, off, width, name):
    n_rows = proj.shape[0]
    tc = _tile(width, 256)
    nb = width // tc
    base = off // tc

    def body(b_ref, c_ref, h_ref, w_ref, o_ref):
        row = lax.broadcasted_iota(jnp.int32, (n_rows, tc), 0)
        o_ref[...] = (b_ref[...] * _conv3(c_ref[...] * h_ref[...], w_ref[...], row)).astype(o_ref.dtype)

    col = lambda s: pl.BlockSpec((n_rows, tc), lambda j: (0, base + s * nb + j))
    return pl.pallas_call(
        body, grid=(nb,), in_specs=[col(0), col(1), col(2), pl.BlockSpec((3, tc), lambda j: (0, j))],
        out_specs=pl.BlockSpec((n_rows, tc), lambda j: (0, j)),
        out_shape=jax.ShapeDtypeStruct((n_rows, width), bf16),
        compiler_params=_params("parallel"), name=name)(proj, proj, proj, w)


def _convgate_bwd(proj, w, dcm, off, width, name):
    n_rows = proj.shape[0]
    tc = _tile(width, 128)
    nb = width // tc
    base = off // tc

    def body(b_ref, c_ref, h_ref, w_ref, d_ref, db_ref, dc_ref, dh_ref, dw_ref):
        row = lax.broadcasted_iota(jnp.int32, (n_rows, tc), 0)
        cb, cc, ch, wv, d = b_ref[...], c_ref[...], h_ref[...], w_ref[...], d_ref[...].astype(f32)
        z = cc * ch
        db_ref[...] = (d * _conv3(z, wv, row)).astype(db_ref.dtype)
        dz, dw = _conv3_bwd(z, wv, d * cb, row)
        dc_ref[...] = (dz * ch).astype(dc_ref.dtype)
        dh_ref[...] = (dz * cc).astype(dh_ref.dtype)
        for k in range(3):
            dw_ref[k:k + 1, :] = dw[k]

    col = lambda s: pl.BlockSpec((n_rows, tc), lambda j: (0, base + s * nb + j))
    own = pl.BlockSpec((n_rows, tc), lambda j: (0, j))
    wspec = pl.BlockSpec((3, tc), lambda j: (0, j))
    sec = jax.ShapeDtypeStruct((n_rows, width), bf16)
    return pl.pallas_call(
        body, grid=(nb,), in_specs=[col(0), col(1), col(2), wspec, own], out_specs=[own, own, own, wspec],
        out_shape=[sec, sec, sec, jax.ShapeDtypeStruct((3, width), f32)],
        compiler_params=_params("parallel"), name=name)(proj, proj, proj, w, dcm)


def _ffnconv_fwd(up, w, name):
    n_rows, two_f = up.shape
    half = two_f // 2
    tc = _tile(half, 256)
    nb = half // tc

    def body(a_ref, b_ref, wa_ref, wb_ref, o_ref):
        row = lax.broadcasted_iota(jnp.int32, (n_rows, tc), 0)
        ua = _conv3(a_ref[...], wa_ref[...], row)
        ub = _conv3(b_ref[...], wb_ref[...], row)
        o_ref[...] = (jax.nn.silu(ua) * ub).astype(o_ref.dtype)

    return pl.pallas_call(
        body, grid=(nb,),
        in_specs=[pl.BlockSpec((n_rows, tc), lambda j: (0, j)), pl.BlockSpec((n_rows, tc), lambda j: (0, nb + j)),
                  pl.BlockSpec((3, tc), lambda j: (0, j)), pl.BlockSpec((3, tc), lambda j: (0, nb + j))],
        out_specs=pl.BlockSpec((n_rows, tc), lambda j: (0, j)),
        out_shape=jax.ShapeDtypeStruct((n_rows, half), bf16),
        compiler_params=_params("parallel"), name=name)(up, up, w, w)


def _ffnconv_bwd(up, w, dact, name):
    n_rows, two_f = up.shape
    half = two_f // 2
    tc = _tile(half, 128)
    nb = half // tc

    def body(a_ref, b_ref, wa_ref, wb_ref, d_ref, da_ref, db_ref, dwa_ref, dwb_ref):
        row = lax.broadcasted_iota(jnp.int32, (n_rows, tc), 0)
        a, b, wa, wb, d = a_ref[...], b_ref[...], wa_ref[...], wb_ref[...], d_ref[...].astype(f32)
        ua = _conv3(a, wa, row)
        ub = _conv3(b, wb, row)
        sg = jax.nn.sigmoid(ua)
        d_ua = d * ub * (sg * (1.0 + ua * (1.0 - sg)))
        d_ub = d * (ua * sg)
        da, dwa = _conv3_bwd(a, wa, d_ua, row)
        db, dwb = _conv3_bwd(b, wb, d_ub, row)
        da_ref[...] = da.astype(da_ref.dtype)
        db_ref[...] = db.astype(db_ref.dtype)
        for k in range(3):
            dwa_ref[k:k + 1, :] = dwa[k]
            dwb_ref[k:k + 1, :] = dwb[k]

    lo = pl.BlockSpec((n_rows, tc), lambda j: (0, j))
    hi = pl.BlockSpec((n_rows, tc), lambda j: (0, nb + j))
    wlo = pl.BlockSpec((3, tc), lambda j: (0, j))
    whi = pl.BlockSpec((3, tc), lambda j: (0, nb + j))
    sec = jax.ShapeDtypeStruct((n_rows, half), bf16)
    wsec = jax.ShapeDtypeStruct((3, half), f32)
    da, db, dwa, dwb = pl.pallas_call(
        body, grid=(nb,), in_specs=[lo, hi, wlo, whi, lo], out_specs=[lo, lo, wlo, wlo],
        out_shape=[sec, sec, wsec, wsec], compiler_params=_params("parallel"), name=name)(up, up, w, w, dact)
    return da, db, jnp.concatenate([dwa, dwb], axis=1)


SCAN_TC = 256


def _cmul(ar, ai, br, bi):
    return ar * br - ai * bi, ar * bi + ai * br


def _scan(x, lam, adjoint, name):
    n_rows, two_s = x.shape
    tc = min(SCAN_TC, two_s // 2)
    n_tiles = two_s // (2 * tc)
    n_chunks = n_rows // SUBLANES

    def body(x_ref, lam_ref, hf_ref, hb_ref):
        lr = lam_ref[:, 0:tc]
        li = lam_ref[:, tc:2 * tc]
        if adjoint:
            li = -li
        row = lax.broadcasted_iota(jnp.int32, (SUBLANES, tc), 0)
        powers = [(lr, li)]
        for _ in range(SUBLANES - 1):
            powers.append(_cmul(powers[-1][0], powers[-1][1], lr, li))
        pr = jnp.zeros((SUBLANES, tc), f32)
        pi = jnp.zeros((SUBLANES, tc), f32)
        for t in range(SUBLANES):
            e = (SUBLANES - 1 - t) if adjoint else t
            pr = jnp.where(row == t, powers[e][0], pr)
            pi = jnp.where(row == t, powers[e][1], pi)
        last = 0 if adjoint else SUBLANES - 1

        def chunk(i, carry):
            cr, ci = carry
            c = (n_chunks - 1 - i) if adjoint else i
            rows = pl.ds(pl.multiple_of(c * SUBLANES, SUBLANES), SUBLANES)
            xr = x_ref[rows, 0:tc]
            xi = x_ref[rows, tc:2 * tc]
            for k in (1, 2, 4):
                ar, ai = powers[k - 1]
                if adjoint:
                    sr = jnp.where(row < SUBLANES - k, pltpu.roll(xr, SUBLANES - k, 0), 0.0)
                    si = jnp.where(row < SUBLANES - k, pltpu.roll(xi, SUBLANES - k, 0), 0.0)
                else:
                    sr = jnp.where(row >= k, pltpu.roll(xr, k, 0), 0.0)
                    si = jnp.where(row >= k, pltpu.roll(xi, k, 0), 0.0)
                xr, xi = xr + ar * sr - ai * si, xi + ar * si + ai * sr
            hr = xr + pr * cr - pi * ci
            hi = xi + pr * ci + pi * cr
            hf_ref[rows, 0:tc] = hr
            hf_ref[rows, tc:2 * tc] = hi
            hb_ref[rows, 0:tc] = hr.astype(bf16)
            hb_ref[rows, tc:2 * tc] = hi.astype(bf16)
            return (jnp.sum(jnp.where(row == last, hr, 0.0), axis=0, keepdims=True),
                    jnp.sum(jnp.where(row == last, hi, 0.0), axis=0, keepdims=True))

        zero = jnp.zeros((1, tc), f32)
        lax.fori_loop(0, n_chunks, chunk, (zero, zero))

    blk = pl.BlockSpec((n_rows, 2 * tc), lambda j: (0, j))
    return pl.pallas_call(
        body, grid=(n_tiles,), in_specs=[blk, pl.BlockSpec((1, 2 * tc), lambda j: (0, j))], out_specs=[blk, blk],
        out_shape=[jax.ShapeDtypeStruct(x.shape, f32), jax.ShapeDtypeStruct(x.shape, bf16)],
        compiler_params=_params("parallel"), name=name)(x, lam)


def _lam_grad(g, h, name):
    n_rows, two_s = g.shape
    tc = min(SCAN_TC, two_s // 2)

    def body(g_ref, h_ref, o_ref):
        row = lax.broadcasted_iota(jnp.int32, (n_rows, tc), 0)
        gr, gi = g_ref[:, 0:tc], g_ref[:, tc:2 * tc]
        hr, hi = _shift_down(h_ref[:, 0:tc], 1, row), _shift_down(h_ref[:, tc:2 * tc], 1, row)
        o_ref[:, 0:tc] = jnp.sum(gr * hr + gi * hi, axis=0, keepdims=True)
        o_ref[:, tc:2 * tc] = jnp.sum(gi * hr - gr * hi, axis=0, keepdims=True)

    blk = pl.BlockSpec((n_rows, 2 * tc), lambda j: (0, j))
    return pl.pallas_call(
        body, grid=(two_s // (2 * tc),), in_specs=[blk, blk], out_specs=pl.BlockSpec((1, 2 * tc), lambda j: (0, j)),
        out_shape=jax.ShapeDtypeStruct((1, two_s), f32), compiler_params=_params("parallel"), name=name)(g, h)


def _interleave(re, im, tc):
    lead = re.shape[:-1]
    s = re.shape[-1]
    return jnp.concatenate([re.reshape(*lead, s // tc, 1, tc), im.reshape(*lead, s // tc, 1, tc)], axis=-2).reshape(*lead, 2 * s)


def _deinterleave(z, tc):
    lead = z.shape[:-1]
    s = z.shape[-1] // 2
    z = z.reshape(*lead, s // tc, 2, tc)
    return z[..., 0, :].reshape(*lead, s), z[..., 1, :].reshape(*lead, s)


def _f_disc(log_dt, ar, ai, br, bi):
    dt = jnp.exp(log_dt)
    mag = jnp.exp(ar * dt)
    lr, li = mag * jnp.cos(ai * dt), mag * jnp.sin(ai * dt)
    den = ar * ar + ai * ai
    fr = ((lr - 1.0) * ar + li * ai) / den
    fi = (li * ar - (lr - 1.0) * ai) / den
    return lr, li, fr[None] * br - fi[None] * bi, fr[None] * bi + fi[None] * br


def _disc_fwd(log_dt, ar, ai, br, bi, name):
    def body(*refs):
        for o_ref, val in zip(refs[5:], _f_disc(*[r[...] for r in refs[:5]])):
            o_ref[...] = val

    sd = jax.ShapeDtypeStruct
    return pl.pallas_call(body, out_shape=[sd(ar.shape, f32), sd(ar.shape, f32), sd(br.shape, f32), sd(br.shape, f32)],
                          name=name)(log_dt, ar, ai, br, bi)


def _disc_bwd(log_dt, ar, ai, br, bi, cots, name):
    def body(*refs):
        _, vjp = jax.vjp(_f_disc, *[r[...] for r in refs[:5]])
        for o_ref, val in zip(refs[9:], vjp(tuple(r[...] for r in refs[5:9]))):
            o_ref[...] = val

    sd = jax.ShapeDtypeStruct
    return pl.pallas_call(
        body, out_shape=[sd(log_dt.shape, f32), sd(ar.shape, f32), sd(ar.shape, f32), sd(br.shape, f32), sd(br.shape, f32)],
        name=name)(log_dt, ar, ai, br, bi, *cots)


def _alibi_bias(n_heads_per_pattern):
    n_heads = len(DSWA_PATTERNS) * n_heads_per_pattern
    slopes = np.array([2.0 ** (-8.0 * (h + 1) / n_heads) for h in range(n_heads)], dtype=np.float32)
    qi = np.arange(QBLK)[:, None]
    kj = np.arange(QBLK)[None, :]
    cur, prev = [], []
    for h in range(n_heads):
        dil = DSWA_PATTERNS[h // n_heads_per_pattern][1]
        d_cur = qi - kj
        d_prev = QBLK + qi - kj
        cur.append(np.where(d_cur >= 0, -slopes[h] * (d_cur * dil).astype(np.float32), NEG_INF))
        prev.append(np.where(d_prev <= QBLK, -slopes[h] * (d_prev * dil).astype(np.float32), NEG_INF))
    return np.stack(cur).astype(np.float32), np.stack(prev).astype(np.float32)


def _blocks_per_residue(head, n_heads_per_pattern, n_blocks):
    pattern = head // n_heads_per_pattern
    out = n_blocks // DSWA_PATTERNS[-1][1]
    for p in range(len(DSWA_PATTERNS) - 2, -1, -1):
        out = jnp.where(pattern == p, n_blocks // DSWA_PATTERNS[p][1], out)
    return out


def _qk(a, b):
    return jnp.einsum('bqe,bke->bqk', a, b, preferred_element_type=f32)


def _pv(p, v):
    return jnp.einsum('bqk,bke->bqe', p.astype(bf16), v, preferred_element_type=f32)


def _ptq(p, a):
    return jnp.einsum('bqk,bqe->bke', p.astype(bf16), a, preferred_element_type=f32)


def _prev_block(t):
    return jnp.concatenate([jnp.zeros((1,) + t.shape[1:], t.dtype), t[:-1]], axis=0)


def _next_block(t):
    return jnp.concatenate([t[1:], jnp.zeros((1,) + t.shape[1:], t.dtype)], axis=0)


def _has_prev(head, hpp, n_blocks):
    blk = lax.broadcasted_iota(jnp.int32, (n_blocks, 1, 1), 0)
    return (blk & (_blocks_per_residue(head, hpp, n_blocks) - 1)) > 0


def _attn_fwd(q, k, v, hpp, name):
    n_heads, n_blocks = q.shape[0], q.shape[1]
    bias_cur, bias_prev = _alibi_bias(hpp)
    scale = HEAD_DIM ** -0.5

    def body(q_ref, k_ref, v_ref, bc_ref, bp_ref, o_ref, lse_ref):
        has_prev = _has_prev(pl.program_id(0), hpp, n_blocks)
        qv, kv, vv = q_ref[...], k_ref[...], v_ref[...]
        s_c = _qk(qv, kv) * scale + bc_ref[...][None]
        s_p = jnp.where(has_prev, _qk(qv, _prev_block(kv)) * scale + bp_ref[...][None], NEG_INF)
        m = jnp.maximum(jnp.max(s_c, axis=-1, keepdims=True), jnp.max(s_p, axis=-1, keepdims=True))
        p_c = jnp.exp(s_c - m)
        p_p = jnp.exp(s_p - m)
        l = jnp.sum(p_c, axis=-1, keepdims=True) + jnp.sum(p_p, axis=-1, keepdims=True)
        o_ref[...] = (_pv(p_c, vv) + _pv(p_p, _prev_block(vv))) / l
        lse_ref[...] = m + jnp.log(l)

    head = pl.BlockSpec((None, n_blocks, QBLK, HEAD_DIM), lambda a: (a, 0, 0, 0))
    bias = pl.BlockSpec((None, QBLK, QBLK), lambda a: (a, 0, 0))
    return pl.pallas_call(
        body, grid=(n_heads,), in_specs=[head, head, head, bias, bias],
        out_specs=[head, pl.BlockSpec((None, n_blocks, QBLK, 1), lambda a: (a, 0, 0, 0))],
        out_shape=[jax.ShapeDtypeStruct(q.shape, f32), jax.ShapeDtypeStruct(q.shape[:3] + (1,), f32)],
        compiler_params=_params("parallel"), name=name)(q, k, v, jnp.asarray(bias_cur), jnp.asarray(bias_prev))


def _attn_bwd(q, k, v, o, lse, do, dlse_x, hpp, name):
    n_heads, n_blocks = q.shape[0], q.shape[1]
    bias_cur, bias_prev = _alibi_bias(hpp)
    scale = HEAD_DIM ** -0.5

    def body(q_ref, k_ref, v_ref, o_ref, l_ref, do_ref, dl_ref, bc_ref, bp_ref, dq_ref, dk_ref, dv_ref):
        has_prev = _has_prev(pl.program_id(0), hpp, n_blocks)
        qv, kv, vv = q_ref[...], k_ref[...], v_ref[...]
        kp, vp = _prev_block(kv), _prev_block(vv)
        dov = do_ref[...]
        do_b = dov.astype(bf16)
        lse = l_ref[...]
        corr = jnp.sum(dl_ref[...], axis=-1, keepdims=True) - jnp.sum(dov * o_ref[...], axis=-1, keepdims=True)
        p_c = jnp.exp(_qk(qv, kv) * scale + bc_ref[...][None] - lse)
        ds_c = p_c * (_qk(do_b, vv) + corr)
        p_p = jnp.where(has_prev, jnp.exp(_qk(qv, kp) * scale + bp_ref[...][None] - lse), 0.0)
        ds_p = p_p * (_qk(do_b, vp) + corr)
        dq_ref[...] = ((_pv(ds_c, kv) + _pv(ds_p, kp)) * scale).astype(dq_ref.dtype)
        dk_ref[...] = ((_ptq(ds_c, qv) + _next_block(_ptq(ds_p, qv))) * scale).astype(dk_ref.dtype)
        dv_ref[...] = (_ptq(p_c, do_b) + _next_block(_ptq(p_p, do_b))).astype(dv_ref.dtype)

    head = pl.BlockSpec((None, n_blocks, QBLK, HEAD_DIM), lambda a: (a, 0, 0, 0))
    head1 = pl.BlockSpec((None, n_blocks, QBLK, 1), lambda a: (a, 0, 0, 0))
    bias = pl.BlockSpec((None, QBLK, QBLK), lambda a: (a, 0, 0))
    out = jax.ShapeDtypeStruct(q.shape, bf16)
    return pl.pallas_call(
        body, grid=(n_heads,), in_specs=[head, head, head, head, head1, head, head, bias, bias],
        out_specs=[head, head, head], out_shape=[out, out, out],
        compiler_params=_params("parallel"), name=name)(q, k, v, o, lse, do, dlse_x, jnp.asarray(bias_cur), jnp.asarray(bias_prev))


def _to_blocks(t, hpp):
    n_rows = t.shape[0]
    t = t.reshape(n_rows, len(DSWA_PATTERNS), hpp, HEAD_DIM)
    out = []
    for p, (_, dil) in enumerate(DSWA_PATTERNS):
        tp = t[:, p].reshape(n_rows // dil, dil, hpp, HEAD_DIM).transpose(2, 1, 0, 3)
        out.append(tp.reshape(hpp, n_rows // QBLK, QBLK, HEAD_DIM))
    return jnp.concatenate(out, axis=0)


def _from_blocks(t, hpp):
    n_blocks, last = t.shape[1], t.shape[3]
    n_rows = n_blocks * QBLK
    out = []
    for p, (_, dil) in enumerate(DSWA_PATTERNS):
        tp = t[p * hpp:(p + 1) * hpp].reshape(hpp, dil, n_rows // dil, last).transpose(2, 1, 0, 3)
        out.append(tp.reshape(n_rows, hpp * last))
    return out


def _loss_head(y, target, name):
    n_rows, d = y.shape
    tm = _tile(n_rows, 256, SUBLANES)

    def body(y_ref, t_ref, dy_ref, loss_ref):
        diff = y_ref[...] - t_ref[...]
        dy_ref[...] = diff * (1.0 / d)

        @pl.when(pl.program_id(0) == 0)
        def _():
            loss_ref[...] = jnp.zeros_like(loss_ref)

        loss_ref[...] += jnp.sum(jnp.sum(diff * diff, axis=1, keepdims=True), axis=0, keepdims=True) * (0.5 / d)

    row = pl.BlockSpec((tm, d), lambda i: (i, 0))
    return pl.pallas_call(
        body, grid=(n_rows // tm,), in_specs=[row, row], out_specs=[row, pl.BlockSpec((1, 1), lambda i: (0, 0))],
        out_shape=[jax.ShapeDtypeStruct(y.shape, f32), jax.ShapeDtypeStruct((1, 1), f32)],
        compiler_params=_params("arbitrary"), name=name)(y, target)


def _as2d(a):
    if a.ndim >= 2 and a.shape[-1] >= LANES:
        return a.reshape(-1, a.shape[-1])
    return a.reshape(-1, LANES) if a.size % LANES == 0 else a.reshape(1, -1)


ELEMENTWISE_BLOCK_BYTES = 2 << 20


def _row_tile(n_rows, n_cols):
    return _tile(n_rows, max(SUBLANES, ELEMENTWISE_BLOCK_BYTES // (4 * n_cols)), SUBLANES)


def _adamw(w, g, m, v, name):
    shape = w.shape
    w2, g2, m2, v2 = _as2d(w), _as2d(g), _as2d(m), _as2d(v)
    n_rows, n_cols = w2.shape
    tm = _row_tile(n_rows, n_cols)

    def body(w_ref, g_ref, m_ref, v_ref, d_ref, mo_ref, vo_ref):
        gv = g_ref[...]
        mn = ADAM_B1 * m_ref[...] + (1.0 - ADAM_B1) * gv
        vn = ADAM_B2 * v_ref[...] + (1.0 - ADAM_B2) * jnp.square(gv)
        m_hat = mn / (1.0 - ADAM_B1 ** ADAM_STEP)
        v_hat = vn / (1.0 - ADAM_B2 ** ADAM_STEP)
        d_ref[...] = -ADAM_LR * (m_hat / (jnp.sqrt(v_hat) + ADAM_EPS) + ADAM_WD * w_ref[...])
        mo_ref[...] = mn
        vo_ref[...] = vn

    row = pl.BlockSpec((tm, n_cols), lambda i: (i, 0))
    out = jax.ShapeDtypeStruct(w2.shape, f32)
    d, mn, vn = pl.pallas_call(body, grid=(n_rows // tm,), in_specs=[row] * 4, out_specs=[row] * 3, out_shape=[out] * 3,
                               compiler_params=_params("parallel"), name=name)(w2, g2, m2, v2)
    return d.reshape(shape), mn.reshape(shape), vn.reshape(shape)


def _sum_slabs(x, name):
    n = x.shape[0]
    shape = x.shape[1:]
    x3 = x.reshape(n, -1, shape[-1])
    n_rows, n_cols = x3.shape[1:]
    tm = _row_tile(n_rows, n_cols)

    def body(x_ref, o_ref):
        acc = x_ref[0].astype(f32)
        for s in range(1, n):
            acc = acc + x_ref[s].astype(f32)
        o_ref[...] = acc

    out = pl.pallas_call(
        body, grid=(n_rows // tm,), in_specs=[pl.BlockSpec((n, tm, n_cols), lambda i: (0, i, 0))],
        out_specs=pl.BlockSpec((tm, n_cols), lambda i: (i, 0)), out_shape=jax.ShapeDtypeStruct((n_rows, n_cols), f32),
        compiler_params=_params("parallel"), name=name)(x3)
    return out.reshape(shape)


def _mod_fwd(c_all, w_mod, b_cols, name):
    depth, d, cols = w_mod.shape
    tn = _tile(cols, 512)

    def body(c_ref, w_ref, b_ref, o_ref):
        cond = jax.nn.silu(c_ref[...])
        o_ref[...] = jnp.dot(cond, w_ref[...], preferred_element_type=f32, precision=lax.Precision.HIGHEST) + b_ref[...]

    return pl.pallas_call(
        body, grid=(depth, cols // tn),
        in_specs=[pl.BlockSpec((N_DEV, d), lambda l, j: (0, 0)), pl.BlockSpec((None, d, tn), lambda l, j: (l, 0, j)),
                  pl.BlockSpec((None, 1, tn), lambda l, j: (l, 0, j))],
        out_specs=pl.BlockSpec((None, N_DEV, tn), lambda l, j: (l, 0, j)),
        out_shape=jax.ShapeDtypeStruct((depth, N_DEV, cols), f32),
        compiler_params=_params("parallel", "parallel"), name=name)(c_all, w_mod, b_cols)


def _mod_wgrad(c_all_t, dmod_cols, name):
    d = c_all_t.shape[0]
    depth, _, cols = dmod_cols.shape
    tm = _tile(d, 256, SUBLANES)

    def body(c_ref, g_ref, o_ref):
        cond = jax.nn.silu(c_ref[...])
        o_ref[...] = jnp.dot(cond, g_ref[...], preferred_element_type=f32, precision=lax.Precision.HIGHEST)

    return pl.pallas_call(
        body, grid=(depth, d // tm),
        in_specs=[pl.BlockSpec((tm, N_DEV), lambda l, i: (i, 0)), pl.BlockSpec((None, N_DEV, cols), lambda l, i: (l, 0, 0))],
        out_specs=pl.BlockSpec((None, tm, cols), lambda l, i: (l, i, 0)),
        out_shape=jax.ShapeDtypeStruct((depth, d, cols), f32),
        compiler_params=_params("parallel", "parallel"), name=name)(c_all_t, dmod_cols)


def _position():
    return lax.axis_index("x"), lax.axis_index("y"), lax.axis_index("c")


def _linear(p):
    return 4 * p[0] + 2 * p[1] + p[2]


def _all_gather(xs, name):
    n = len(xs)

    def body(*refs):
        x_refs, o_refs = refs[:n], refs[n:2 * n]
        send_sems, recv_sems, local_sems = refs[2 * n:]
        x, y, c = _position()
        me, sibling = (x, y, c), (x, y, 1 - c)
        chips = [(1 - x, y), (x, 1 - y), (1 - x, 1 - y)]

        def copy(a, k, block, to, src=None):
            slab = o_refs[a].at[_linear(block)]
            return pltpu.make_async_remote_copy(
                src_ref=slab if src is None else src, dst_ref=slab, send_sem=send_sems.at[a, k], recv_sem=recv_sems.at[a, k],
                device_id=to, device_id_type=MESH)

        mine = [pltpu.make_async_copy(x_refs[a], o_refs[a].at[_linear(me)], local_sems.at[a]) for a in range(n)]
        first = []
        for a in range(n):
            mine[a].start()
            first.append(copy(a, 0, me, sibling, src=x_refs[a]))
            first += [copy(a, 1 + j, me, (*chip, c), src=x_refs[a]) for j, chip in enumerate(chips)]
        for cp in first:
            cp.start()
        passed = []
        for j, chip in enumerate(chips):
            for a in range(n):
                copy(a, 1 + j, (*chip, c), me).wait_recv()
                cp = copy(a, 4 + j, (*chip, c), sibling)
                cp.start()
                passed.append(cp)
        for a in range(n):
            copy(a, 0, sibling, me).wait_recv()
            for j, chip in enumerate(chips):
                copy(a, 4 + j, (*chip, 1 - c), me).wait_recv()
        for cp in first + passed:
            cp.wait_send()
        for cp in mine:
            cp.wait()

    return pl.pallas_call(
        body, in_specs=[ANY] * n, out_specs=[ANY] * n,
        out_shape=[jax.ShapeDtypeStruct((N_DEV,) + x.shape, x.dtype) for x in xs],
        scratch_shapes=[pltpu.SemaphoreType.DMA((n, 7)), pltpu.SemaphoreType.DMA((n, 7)), pltpu.SemaphoreType.DMA((n,))],
        name=name)(*xs)


def _all_gather_relayed(xs, name):
    n = len(xs)
    halves = [x.shape[0] // 2 for x in xs]

    def body(*refs):
        x_refs, o_refs = refs[:n], refs[n:2 * n]
        send_sems, recv_sems = refs[2 * n:]
        x, y, c = _position()
        across_x, across_y, across_xy = (1 - x, y, c), (x, 1 - y, c), (1 - x, 1 - y, c)
        sibling = (x, y, 1 - c)

        def copy(a, k, slab, to, rows=None, src=None):
            dst = o_refs[a].at[_linear(slab)] if rows is None else o_refs[a].at[_linear(slab), rows]
            return pltpu.make_async_remote_copy(
                src_ref=dst if src is None else src, dst_ref=dst, send_sem=send_sems.at[a, k], recv_sem=recv_sems.at[a, k],
                device_id=to, device_id_type=MESH)

        def other_core(p):
            return (p[0], p[1], 1 - c)

        me = (x, y, c)
        started = []
        for a in range(n):
            started += [copy(a, 0, me, sibling, src=x_refs[a]), copy(a, 1, me, across_x, src=x_refs[a]),
                        copy(a, 2, me, across_y, src=x_refs[a])]
        for cp in started:
            cp.start()
        later = []
        for a in range(n):
            first, rest = pl.ds(0, halves[a]), pl.ds(halves[a], xs[a].shape[0] - halves[a])
            copy(a, 1, across_x, me).wait_recv()
            later += [copy(a, 3, across_x, across_y, rows=first), copy(a, 5, across_x, sibling)]
            later[-2].start()
            later[-1].start()
            copy(a, 2, across_y, me).wait_recv()
            later += [copy(a, 4, across_y, across_x, rows=rest), copy(a, 6, across_y, sibling)]
            later[-2].start()
            later[-1].start()
        for a in range(n):
            first, rest = pl.ds(0, halves[a]), pl.ds(halves[a], xs[a].shape[0] - halves[a])
            copy(a, 3, across_xy, me, rows=first).wait_recv()
            copy(a, 4, across_xy, me, rows=rest).wait_recv()
            later.append(copy(a, 7, across_xy, sibling))
            later[-1].start()
        for a in range(n):
            copy(a, 0, sibling, me).wait_recv()
            for k, slab in ((5, across_x), (6, across_y), (7, across_xy)):
                copy(a, k, other_core(slab), me).wait_recv()
        for cp in started + later:
            cp.wait_send()

    sems = pltpu.SemaphoreType.DMA((n, 8))
    return pl.pallas_call(
        body, in_specs=[ANY] * n, out_specs=[ANY] * n,
        out_shape=[jax.ShapeDtypeStruct((N_DEV,) + x.shape, x.dtype) for x in xs],
        scratch_shapes=[sems, sems], name=name)(*xs)


def _pair_exchange(xs, name):
    n = len(xs)

    def body(*refs):
        x_refs, got_refs = refs[:n], refs[n:2 * n]
        send_sems, recv_sems = refs[2 * n:]
        x, y, c = _position()
        sends = []
        for a in range(n):
            for p in range(N_CHIPS):
                sends.append(pltpu.make_async_remote_copy(
                    src_ref=x_refs[a].at[p, 1 - c], dst_ref=got_refs[a].at[p], send_sem=send_sems.at[a, p],
                    recv_sem=recv_sems.at[a, p], device_id=(x, y, 1 - c), device_id_type=MESH))
        for cp in sends:
            cp.start()
        for cp in sends:
            cp.wait_recv()
        for cp in sends:
            cp.wait_send()

    sems = pltpu.SemaphoreType.DMA((n, N_CHIPS))
    return pl.pallas_call(body, in_specs=[ANY] * n, out_specs=[ANY] * n,
                          out_shape=[jax.ShapeDtypeStruct((N_CHIPS,) + x.shape[2:], x.dtype) for x in xs],
                          scratch_shapes=[sems, sems], name=name)(*xs)


def _chip_copies(x_refs, o_refs, send_sems, recv_sems, ranges=None):
    x, y, c = _position()
    copies = []
    for a in range(len(x_refs)):
        rows = pl.ds(0, x_refs[a].shape[1]) if ranges is None else pl.ds(ranges[a][0], ranges[a][1] - ranges[a][0])
        for k, (px, py) in enumerate([(1 - x, y), (x, 1 - y), (1 - x, 1 - y)]):
            copies.append(pltpu.make_async_remote_copy(
                src_ref=x_refs[a].at[2 * px + py, rows], dst_ref=o_refs[a].at[k, rows], send_sem=send_sems.at[a, k],
                recv_sem=recv_sems.at[a, k], device_id=(px, py, c), device_id_type=MESH))
    return copies


def _chip_exchange(xs, name):
    n = len(xs)

    def body(*refs):
        sends = _chip_copies(refs[:n], refs[n:2 * n], refs[2 * n], refs[2 * n + 1])
        for cp in sends:
            cp.start()
        for cp in sends:
            cp.wait_recv()
        for cp in sends:
            cp.wait_send()

    sems = pltpu.SemaphoreType.DMA((n, N_CHIPS - 1))
    return pl.pallas_call(body, in_specs=[ANY] * n, out_specs=[ANY] * n,
                          out_shape=[jax.ShapeDtypeStruct((N_CHIPS - 1,) + x.shape[1:], x.dtype) for x in xs],
                          scratch_shapes=[sems, sems], name=name)(*xs)


def _pick_add(picked, index, others, out_dtype, name):
    per_slab = picked.ndim == 4
    n_rows, n_cols = picked.shape[-2:]
    tm = _row_tile(n_rows, n_cols)
    n_other = 1 if per_slab else others.shape[0]

    def body(idx_ref, p_ref, o_ref, out_ref):
        acc = p_ref[...].astype(f32)
        if per_slab:
            acc = acc + o_ref[...].astype(f32)
        else:
            for k in range(n_other):
                acc = acc + o_ref[k].astype(f32)
        out_ref[...] = acc.astype(out_ref.dtype)

    if per_slab:
        grid = (picked.shape[0], n_rows // tm)
        in_specs = [pl.BlockSpec((None, None, tm, n_cols), lambda p, i, idx: (p, idx[0], i, 0)),
                    pl.BlockSpec((None, tm, n_cols), lambda p, i, idx: (p, i, 0))]
        out_specs = pl.BlockSpec((None, tm, n_cols), lambda p, i, idx: (p, i, 0))
        out_shape = jax.ShapeDtypeStruct((picked.shape[0], n_rows, n_cols), out_dtype)
    else:
        grid = (n_rows // tm,)
        in_specs = [pl.BlockSpec((None, tm, n_cols), lambda i, idx: (idx[0], i, 0)),
                    pl.BlockSpec((n_other, tm, n_cols), lambda i, idx: (0, i, 0))]
        out_specs = pl.BlockSpec((tm, n_cols), lambda i, idx: (i, 0))
        out_shape = jax.ShapeDtypeStruct((n_rows, n_cols), out_dtype)
    return pl.pallas_call(
        body, grid_spec=pltpu.PrefetchScalarGridSpec(num_scalar_prefetch=1, grid=grid, in_specs=in_specs, out_specs=out_specs),
        out_shape=out_shape, compiler_params=_params(*(["parallel"] * len(grid))), name=name)(
            jnp.reshape(index, (1,)).astype(jnp.int32), picked, others)


def _chip_sums(xs, name):
    c = lax.axis_index("c")
    pairs = [v.reshape((N_CHIPS, 2) + v.shape[1:]) for v in xs]
    got = _pair_exchange(pairs, f"{name}_pair")
    return [_pick_add(a, c, b, a.dtype, f"{name}_pair_sum{i}") for i, (a, b) in enumerate(zip(pairs, got))]


def _total_sums(chip_sums, arrived, name):
    my_chip = 2 * lax.axis_index("x") + lax.axis_index("y")
    return [_pick_add(a, my_chip, b, f32, f"{name}_sum{i}") for i, (a, b) in enumerate(zip(chip_sums, arrived))]


def _column_group(j, tc, shape):
    col = lax.broadcasted_iota(jnp.int32, shape, 1)
    return (j * tc + col % tc) // SSM_STATE


def _bd_build(bt, n_groups, tc, name):
    n_rows = n_groups * SSM_GROUP
    two_s = bt.shape[1]

    def body(b_ref, o_ref):
        grp = _column_group(pl.program_id(0), tc, (n_rows, 2 * tc))
        row = lax.broadcasted_iota(jnp.int32, (n_rows, 2 * tc), 0)
        tiled = jnp.concatenate([b_ref[...]] * n_groups, axis=0)
        o_ref[...] = jnp.where(grp == row // SSM_GROUP, tiled, 0.0).astype(o_ref.dtype)

    return pl.pallas_call(
        body, grid=(two_s // (2 * tc),), in_specs=[pl.BlockSpec((SSM_GROUP, 2 * tc), lambda j: (0, j))],
        out_specs=pl.BlockSpec((n_rows, 2 * tc), lambda j: (0, j)), out_shape=jax.ShapeDtypeStruct((n_rows, two_s), bf16),
        compiler_params=_params("parallel"), name=name)(bt)


def _bd_fold(m, n_groups, tc, name):
    n_rows, two_s = m.shape

    def body(m_ref, o_ref):
        grp = _column_group(pl.program_id(0), tc, (SSM_GROUP, 2 * tc))
        acc = jnp.zeros((SSM_GROUP, 2 * tc), f32)
        for g in range(n_groups):
            acc = acc + jnp.where(grp == g, m_ref[g * SSM_GROUP:(g + 1) * SSM_GROUP, :], 0.0)
        o_ref[...] = acc

    return pl.pallas_call(
        body, grid=(two_s // (2 * tc),), in_specs=[pl.BlockSpec((n_rows, 2 * tc), lambda j: (0, j))],
        out_specs=pl.BlockSpec((SSM_GROUP, 2 * tc), lambda j: (0, j)), out_shape=jax.ShapeDtypeStruct((SSM_GROUP, two_s), f32),
        compiler_params=_params("parallel"), name=name)(m)


def _ssm_setup(p, tag):
    n_groups = p["ssm_a_re"].shape[0]
    s = n_groups * SSM_STATE
    tc = min(SCAN_TC, s)
    log_dt = p["ssm_log_dt"].reshape(n_groups, 1)
    br_t = p["ssm_b_re"].transpose(2, 0, 1)
    bi_t = p["ssm_b_im"].transpose(2, 0, 1)
    disc_in = (log_dt, p["ssm_a_re"], p["ssm_a_im"], br_t, bi_t)
    lr, li, bbr, bbi = _disc_fwd(*disc_in, name=f"ssm_disc{tag}")
    lam = _interleave(lr.reshape(1, s), li.reshape(1, s), tc)
    bb = _bd_build(_interleave(bbr.reshape(SSM_GROUP, s), bbi.reshape(SSM_GROUP, s), tc), n_groups, tc, f"ssm_in_map{tag}")
    cr_t = p["ssm_c_re"].transpose(1, 0, 2).reshape(SSM_GROUP, s)
    ci_t = p["ssm_c_im"].transpose(1, 0, 2).reshape(SSM_GROUP, s)
    cc = _bd_build(_interleave(cr_t, -ci_t, tc), n_groups, tc, f"ssm_out_map{tag}")
    return dict(disc_in=disc_in, lam=lam, bb=bb, cc=cc, tc=tc, n_groups=n_groups)


def _apply_w(a, w, key, out_dtype, name):
    return _mm(a, w[key], out_dtype, name, rhs_t=key in COL_SHARDED)


def _apply_wt(dy, w, key, out_dtype, name, carry=()):
    return _mm(dy, w[key], out_dtype, name, rhs_t=key not in COL_SHARDED, carry=carry)


def _weight_grad(a, dy, key, name, carry=()):
    lhs, rhs = (dy, a) if key in COL_SHARDED else (a, dy)
    return _mm(lhs, rhs, bf16, name, lhs_t=True, carry=carry)


class _Cargo:
    def __init__(self, chip_sums):
        self.chip_sums = chip_sums
        self.arrived = {}

    def hosted(self, call, jobs):
        if self.chip_sums is None:
            return call(())
        carry = []
        for key, part, parts in jobs:
            n_rows = self.chip_sums[key].shape[1]
            step = n_rows if parts == 1 else -(-n_rows // parts // BF16_ROWS) * BF16_ROWS
            carry.append((self.chip_sums[key], part * step, min(n_rows, (part + 1) * step), self.arrived.get(key)))
        out, got = call(carry)
        for (key, _, _), buf in zip(jobs, got):
            self.arrived[key] = buf
        return out


def _layer_fwd(x, mod, w, p, tag):
    n_rows, d = x.shape
    w4 = d // 4
    hpp = w4 // HEAD_DIM
    off_q, off_k, off_v, off_conv, off_gate = w4, 4 * w4, 7 * w4, 10 * w4, 13 * w4
    sh1, sc1, gt1, sh2, sc2, gt2 = mod
    vec = lambda a: a.reshape(1, -1)
    s = dict(x=x)
    (s["h"],) = _rowwise(_f_prenorm, [x], [vec(p["g_pre_mix"]), sc1, sh1], [bf16], 256, f"prenorm_mix{tag}")
    proj = s["proj"] = _apply_w(s["h"], w, "w_in", f32, f"in_proj{tag}")
    ssm = s["ssm"] = _ssm_setup(p, tag)
    s["u32"] = proj[:, :off_q]
    s["u"] = s["u32"].astype(bf16)
    xin = _mm(s["u"], ssm["bb"], f32, f"ssm_in{tag}")
    s["hf"], s["hb"] = _scan(xin, ssm["lam"], False, f"ssm_scan{tag}")
    s["yc"] = _mm(s["hb"], ssm["cc"], f32, f"ssm_out{tag}", rhs_t=True)
    (s["g"],) = _rowwise(_f_gelu, [s["yc"], s["u32"]], [vec(p["ssm_d"])], [bf16], 256, f"ssm_gelu{tag}")
    s["z"] = _apply_w(s["g"], w, "w_glu", f32, f"glu_proj{tag}")
    (s["s_ssm"],) = _rowwise(_f_glu, [s["g"], s["z"]], [vec(p["b_glu"])], [bf16], 256, f"glu{tag}")
    s["y_ssm"] = _apply_w(s["s_ssm"], w, "w_ssm_out", f32, f"ssm_proj{tag}")
    s["qb"] = _to_blocks(proj[:, off_q:off_k].astype(bf16), hpp)
    s["kb"] = _to_blocks(proj[:, off_k:off_v].astype(bf16), hpp)
    s["vb"] = _to_blocks(proj[:, off_v:off_conv].astype(bf16), hpp)
    s["ob"], s["lseb"] = _attn_fwd(s["qb"], s["kb"], s["vb"], hpp, f"attn{tag}")
    s["o_tok"] = _from_blocks(s["ob"], hpp)
    s["lse_x"] = [jnp.repeat(l, HEAD_DIM, axis=1) for l in _from_blocks(s["lseb"], hpp)]
    (s["attn_o"],) = _rowwise(_f_amerge, s["o_tok"] + s["lse_x"], [], [bf16], 256, f"attn_merge{tag}")
    s["y_attn"] = _apply_w(s["attn_o"], w, "w_attn_out", f32, f"attn_proj{tag}")
    s["cm"] = _convgate_fwd(proj, p["conv_mix_w"], off_conv, w4, f"convgate{tag}")
    s["y_conv"] = _apply_w(s["cm"], w, "w_conv_out", f32, f"conv_proj{tag}")
    s["pg"] = [proj[:, off_gate + i * d:off_gate + (i + 1) * d] for i in range(3)]
    s["bg"] = [p["b_gate"][i * d:(i + 1) * d].reshape(1, d) for i in range(3)]
    (s["merged"],) = _rowwise(_f_merge, s["pg"] + [s["y_ssm"], s["y_attn"], s["y_conv"]], s["bg"], [bf16], 128, f"merge{tag}")
    s["y"] = _apply_w(s["merged"], w, "w_o", f32, f"o_proj{tag}")
    (s["x1"],) = _rowwise(_f_postres, [x, s["y"]], [vec(p["g_post_mix"]), gt1], [f32], 256, f"postres_mix{tag}")
    (s["h2"],) = _rowwise(_f_prenorm, [s["x1"]], [vec(p["g_pre_ffn"]), sc2, sh2], [bf16], 256, f"prenorm_ffn{tag}")
    s["up"] = _apply_w(s["h2"], w, "w_up", f32, f"up_proj{tag}")
    s["act"] = _ffnconv_fwd(s["up"], p["ffn_conv_w"], f"ffnconv{tag}")
    s["y2"] = _apply_w(s["act"], w, "w_down", f32, f"down_proj{tag}")
    (x2,) = _rowwise(_f_postres, [s["x1"], s["y2"]], [vec(p["g_post_ffn"]), gt2], [f32], 256, f"postres_ffn{tag}")
    return x2, s


def _layer_bwd(dx2, s, mod, w, p, tag, cargo):
    x = s["x"]
    n_rows, d = x.shape
    w4 = d // 4
    hpp = w4 // HEAD_DIM
    off_conv = 10 * w4
    sh1, sc1, gt1, sh2, sc2, gt2 = mod
    vec = lambda a: a.reshape(1, -1)
    gw, gp = {}, {}
    dy2, d_gpf, d_gt2 = _rowwise_bwd(_f_postres, [s["x1"], s["y2"]], [vec(p["g_post_ffn"]), gt2], [[dx2]], [None, bf16], True,
                                     256, f"postres_ffn_bwd{tag}")
    gp["g_post_ffn"] = d_gpf
    dact = cargo.hosted(lambda carry: _apply_wt(dy2, w, "w_down", bf16, f"down_dx{tag}", carry), [("w_down", 0, 2)])
    gw["w_down"] = cargo.hosted(lambda carry: _weight_grad(s["act"], dy2, "w_down", f"down_dw{tag}", carry), [("w_down", 1, 2)])
    da, db, gp["ffn_conv_w"] = _ffnconv_bwd(s["up"], p["ffn_conv_w"], dact, f"ffnconv_bwd{tag}")
    dup = jnp.concatenate([da, db], axis=1)
    dh2 = cargo.hosted(lambda carry: _apply_wt(dup, w, "w_up", f32, f"up_dx{tag}", carry), [("w_in", 0, 2)])
    gw["w_up"] = cargo.hosted(lambda carry: _weight_grad(s["h2"], dup, "w_up", f"up_dw{tag}", carry), [("w_up", 0, 2)])
    dx1, d_g, d_sc2, d_sh2 = _rowwise_bwd(_f_prenorm, [s["x1"]], [vec(p["g_pre_ffn"]), sc2, sh2], [[dh2]], [f32], True,
                                          256, f"prenorm_ffn_bwd{tag}", add_to={0: dx2})
    gp["g_pre_ffn"] = d_g
    dy, d_gpm, d_gt1 = _rowwise_bwd(_f_postres, [x, s["y"]], [vec(p["g_post_mix"]), gt1], [[dx1]], [None, bf16], True,
                                    256, f"postres_mix_bwd{tag}")
    gp["g_post_mix"] = d_gpm
    dmerged = cargo.hosted(lambda carry: _apply_wt(dy, w, "w_o", f32, f"o_dx{tag}", carry), [("w_o", 0, 1)])
    gw["w_o"] = cargo.hosted(lambda carry: _weight_grad(s["merged"], dy, "w_o", f"o_dw{tag}", carry),
                             [(k, 0, 1) for k in LIGHT_SHARDED])
    res = _rowwise_bwd(_f_merge, s["pg"] + [s["y_ssm"], s["y_attn"], s["y_conv"]], s["bg"], [[dmerged]], [bf16] * 6, True,
                       128, f"merge_bwd{tag}")
    dpg, (dys, dya, dyv), dbg = list(res[0:3]), res[3:6], res[6:9]
    gp["b_gate"] = jnp.concatenate(dbg, axis=1)
    ds_ssm = _apply_wt(dys, w, "w_ssm_out", f32, f"ssm_proj_dx{tag}")
    gw["w_ssm_out"] = _weight_grad(s["s_ssm"], dys, "w_ssm_out", f"ssm_proj_dw{tag}")
    dattn_o = _apply_wt(dya, w, "w_attn_out", f32, f"attn_proj_dx{tag}")
    gw["w_attn_out"] = _weight_grad(s["attn_o"], dya, "w_attn_out", f"attn_proj_dw{tag}")
    dcm = _apply_wt(dyv, w, "w_conv_out", f32, f"conv_proj_dx{tag}")
    gw["w_conv_out"] = _weight_grad(s["cm"], dyv, "w_conv_out", f"conv_proj_dw{tag}")
    ssm = s["ssm"]
    tc, n_groups = ssm["tc"], ssm["n_groups"]
    dg_a, dz, gp["b_glu"] = _rowwise_bwd(_f_glu, [s["g"], s["z"]], [vec(p["b_glu"])], [[ds_ssm]], [f32, bf16], True,
                                         256, f"glu_bwd{tag}")
    dg_b = _apply_wt(dz, w, "w_glu", f32, f"glu_proj_dx{tag}")
    gw["w_glu"] = _weight_grad(s["g"], dz, "w_glu", f"glu_proj_dw{tag}")
    dyc, du_skip, gp["ssm_d"] = _rowwise_bwd(_f_gelu, [s["yc"], s["u32"]], [vec(p["ssm_d"])], [[dg_a, dg_b]], [bf16, f32], True,
                                             256, f"ssm_gelu_bwd{tag}")
    dh_state = _mm(dyc, ssm["cc"], f32, f"ssm_out_dx{tag}")
    g_cc = _bd_fold(_mm(dyc, s["hb"], f32, f"ssm_out_dw{tag}", lhs_t=True), n_groups, tc, f"ssm_out_dw_fold{tag}")
    gf, gb = _scan(dh_state, ssm["lam"], True, f"ssm_scan_bwd{tag}")
    d_lam = _lam_grad(gf, s["hf"], f"ssm_lam_grad{tag}")
    du_x = _mm(gb, ssm["bb"], f32, f"ssm_in_dx{tag}", rhs_t=True)
    g_bb = _bd_fold(_mm(s["u"], gb, f32, f"ssm_in_dw{tag}", lhs_t=True), n_groups, tc, f"ssm_in_dw_fold{tag}")
    (du,) = _rowwise(_f_add, [du_skip, du_x], [], [bf16], 256, f"ssm_du{tag}")
    d_lr, d_li = _deinterleave(d_lam, tc)
    g_bbr, g_bbi = _deinterleave(g_bb, tc)
    group_shape = (SSM_GROUP, n_groups, SSM_STATE)
    cots = (d_lr.reshape(n_groups, SSM_STATE), d_li.reshape(n_groups, SSM_STATE), g_bbr.reshape(group_shape), g_bbi.reshape(group_shape))
    d_ldt, d_ar, d_ai, d_br, d_bi = _disc_bwd(*ssm["disc_in"], cots, f"ssm_disc_bwd{tag}")
    gp["ssm_log_dt"], gp["ssm_a_re"], gp["ssm_a_im"] = d_ldt.reshape(-1), d_ar, d_ai
    gp["ssm_b_re"], gp["ssm_b_im"] = d_br.transpose(1, 2, 0), d_bi.transpose(1, 2, 0)
    g_ccr, g_cci = _deinterleave(g_cc, tc)
    gp["ssm_c_re"] = g_ccr.reshape(group_shape).transpose(1, 0, 2)
    gp["ssm_c_im"] = -g_cci.reshape(group_shape).transpose(1, 0, 2)
    res = _rowwise_bwd(_f_amerge, s["o_tok"] + s["lse_x"], [], [[dattn_o]], [f32] * 6, False, 256, f"attn_merge_bwd{tag}")
    do_b = _to_blocks(jnp.concatenate(res[0:3], axis=1), hpp)
    dl_b = _to_blocks(jnp.concatenate(res[3:6], axis=1), hpp)
    dqb, dkb, dvb = _attn_bwd(s["qb"], s["kb"], s["vb"], s["ob"], s["lseb"], do_b, dl_b, hpp, f"attn_bwd{tag}")
    dq, dk, dv = (jnp.concatenate(_from_blocks(t, hpp), axis=1) for t in (dqb, dkb, dvb))
    dcb, dcc, dch, gp["conv_mix_w"] = _convgate_bwd(s["proj"], p["conv_mix_w"], dcm, off_conv, w4, f"convgate_bwd{tag}")
    dproj = jnp.concatenate([du, dq, dk, dv, dcb, dcc, dch] + dpg, axis=1)
    dh = cargo.hosted(lambda carry: _apply_wt(dproj, w, "w_in", f32, f"in_dx{tag}", carry), [("w_in", 1, 2)])
    gw["w_in"] = cargo.hosted(lambda carry: _weight_grad(s["h"], dproj, "w_in", f"in_dw{tag}", carry), [("w_up", 1, 2)])
    dx0, d_g, d_sc1, d_sh1 = _rowwise_bwd(_f_prenorm, [x], [vec(p["g_pre_mix"]), sc1, sh1], [[dh]], [f32], True,
                                          256, f"prenorm_mix_bwd{tag}", add_to={0: dx1})
    gp["g_pre_mix"] = d_g
    return dx0, [d_sh1, d_sc1, d_gt1, d_sh2, d_sc2, d_gt2], gw, gp


COL_SHARDED = ("w_in", "w_up", "w_ssm_out", "w_attn_out", "w_conv_out")
ROW_SHARDED = ("w_down", "w_o", "w_glu")
SMALL_SHARDED = ("conv_mix_w", "ffn_conv_w")
LIGHT_SHARDED = ("w_ssm_out", "w_attn_out", "w_conv_out", "w_glu") + SMALL_SHARDED
REPLICATED = ("b_mod", "g_pre_mix", "g_post_mix", "g_pre_ffn", "g_post_ffn", "ssm_log_dt", "ssm_a_re", "ssm_a_im", "ssm_b_re",
              "ssm_b_im", "ssm_c_re", "ssm_c_im", "ssm_d", "b_glu", "b_gate")
WEIGHTS = ('w_mod', 'b_mod', 'g_pre_mix', 'g_post_mix', 'g_pre_ffn', 'g_post_ffn', 'w_in', 'ssm_log_dt', 'ssm_a_re', 'ssm_a_im',
           'ssm_b_re', 'ssm_b_im', 'ssm_c_re', 'ssm_c_im', 'ssm_d', 'w_glu', 'b_glu', 'conv_mix_w', 'w_ssm_out', 'w_attn_out',
           'w_conv_out', 'b_gate', 'w_o', 'w_up', 'ffn_conv_w', 'w_down')


def _full_cols(g):
    return g.transpose(1, 0, 2).reshape(g.shape[1], -1)


def _col_slabs(m):
    return m.reshape(m.shape[0], N_DEV, -1).transpose(1, 0, 2)


def _step(x, c, loss_target, wts, m_in, v_in):
    depth = wts["w_in"].shape[0]
    n_rows, d = x.shape[1], x.shape[2]
    me = _linear(_position())
    x2d, target2d = x[0], loss_target[0]

    (c_all,) = _all_gather([c], "gather_c")
    c_all = c_all.reshape(N_DEV, d)
    mod_cols = wts["w_mod"].shape[2]
    b_cols = lax.dynamic_slice_in_dim(wts["b_mod"], me * mod_cols, mod_cols, axis=1).reshape(depth, 1, mod_cols)
    mod_mine = _mod_fwd(c_all, wts["w_mod"], b_cols, "mod_fwd")
    (mod_all,) = _all_gather([mod_mine], "gather_mod")
    mod_me = lax.dynamic_index_in_dim(mod_all, me, axis=2, keepdims=False).transpose(1, 0, 2).reshape(depth, 6, 1, d)

    mats = []
    for l in range(depth):
        names = COL_SHARDED + ROW_SHARDED
        shards = [(wts[k][l].T if k in COL_SHARDED else wts[k][l]).astype(bf16) for k in names]
        got = _all_gather_relayed(shards, f"gather_w{l}")
        got = [lax.dynamic_update_index_in_dim(g, shard, me, axis=0) for g, shard in zip(got, shards)]
        mats.append({k: g.reshape(-1, g.shape[2]) for k, g in zip(names, got)})
    (conv_mix_all, ffn_conv_all) = _all_gather([wts["conv_mix_w"], wts["ffn_conv_w"]], "gather_conv_w")
    small = []
    for l in range(depth):
        p = {k: wts[k][l] for k in REPLICATED if k != "b_mod"}
        p["conv_mix_w"] = _full_cols(conv_mix_all[:, l])
        p["ffn_conv_w"] = _full_cols(ffn_conv_all[:, l])
        small.append(p)

    saved = []
    h = x2d
    for l in range(depth):
        mod = [mod_me[l, i] for i in range(6)]
        h, s = _layer_fwd(h, mod, mats[l], small[l], f"_l{l}")
        saved.append(s)
    dy, loss_part = _loss_head(h, target2d, "loss_head")
    loss = lax.psum(loss_part[0, 0], ("x", "y", "c"))

    grads = {k: [None] * depth for k in WEIGHTS}
    dmod_rows = [None] * depth
    small_parts = [None] * depth
    sharded = COL_SHARDED + ROW_SHARDED + SMALL_SHARDED
    waiting = None
    for l in reversed(range(depth)):
        mod = [mod_me[l, i] for i in range(6)]
        cargo = _Cargo(waiting)
        dy, dmod, gw, gp = _layer_bwd(dy, saved[l], mod, mats[l], small[l], f"_l{l}", cargo)
        if waiting is not None:
            summed = _total_sums([waiting[k] for k in sharded], [cargo.arrived[k] for k in sharded], f"scatter_g{l + 1}")
            for k, g in zip(sharded, summed):
                grads[k][l + 1] = g
        dmod_rows[l] = jnp.concatenate(dmod, axis=1)
        slabs = [gw[k].reshape(N_DEV, -1, gw[k].shape[1]) for k in COL_SHARDED + ROW_SHARDED]
        slabs += [_col_slabs(gp[k]) for k in SMALL_SHARDED]
        waiting = dict(zip(sharded, _chip_sums(slabs, f"scatter_g{l}")))
        small_parts[l] = gp
    last = [waiting[k] for k in sharded]
    for k, g in zip(sharded, _total_sums(last, _chip_exchange(last, "scatter_g0_chips"), "scatter_g0")):
        grads[k][0] = g
    grad_x = dy.reshape(x.shape)

    dmod_mine = jnp.concatenate(dmod_rows, axis=0)
    rep_names = [k for k in REPLICATED if k != "b_mod"]
    pieces = [dmod_mine.reshape(-1)] + [jnp.stack([small_parts[l][k].reshape(-1) for l in range(depth)]).reshape(-1) for k in rep_names]
    sizes = [int(pc.shape[0]) for pc in pieces]
    total = sum(sizes)
    padded = -(-total // (SUBLANES * LANES)) * (SUBLANES * LANES)
    pack = jnp.concatenate(pieces + [jnp.zeros((padded - total,), f32)]).reshape(-1, LANES)
    (pack_all,) = _all_gather([pack], "gather_small_grads")
    pack_sum = _sum_slabs(pack_all, "sum_small_grads").reshape(-1)
    offs = np.cumsum([0] + sizes)
    grads["b_mod"] = pack_sum[offs[0]:offs[1]].reshape(wts["b_mod"].shape)
    for i, k in enumerate(rep_names):
        grads[k] = pack_sum[offs[i + 1]:offs[i + 2]].reshape(wts[k].shape)
    dmod_all = pack_all.reshape(N_DEV, -1)[:, :sizes[0]].reshape(N_DEV, depth, 6 * d)
    dmod_cols = lax.dynamic_slice_in_dim(dmod_all, me * mod_cols, mod_cols, axis=2).transpose(1, 0, 2)
    grads["w_mod"] = _mod_wgrad(c_all.T, dmod_cols, "mod_wgrad")
    for k in COL_SHARDED + ROW_SHARDED + SMALL_SHARDED:
        grads[k] = jnp.stack([g.T for g in grads[k]] if k in COL_SHARDED else grads[k])

    delta, new_m, new_v = {}, {}, {}
    for k in WEIGHTS:
        delta[k], new_m[k], new_v[k] = _adamw(wts[k], grads[k], m_in[k], v_in[k], f"adamw_{k}")
    return (loss, grad_x, *[grads[k] for k in WEIGHTS], *[delta[k] for k in WEIGHTS], *[new_m[k] for k in WEIGHTS],
            *[new_v[k] for k in WEIGHTS])


def kernel(x, c, w_mod, b_mod, g_pre_mix, g_post_mix, g_pre_ffn, g_post_ffn, w_in, ssm_log_dt, ssm_a_re, ssm_a_im, ssm_b_re, ssm_b_im, ssm_c_re, ssm_c_im, ssm_d, w_glu, b_glu, conv_mix_w, w_ssm_out, w_attn_out, w_conv_out, b_gate, w_o, w_up, ffn_conv_w, w_down, loss_target, m_w_mod, m_b_mod, m_g_pre_mix, m_g_post_mix, m_g_pre_ffn, m_g_post_ffn, m_w_in, m_ssm_log_dt, m_ssm_a_re, m_ssm_a_im, m_ssm_b_re, m_ssm_b_im, m_ssm_c_re, m_ssm_c_im, m_ssm_d, m_w_glu, m_b_glu, m_conv_mix_w, m_w_ssm_out, m_w_attn_out, m_w_conv_out, m_b_gate, m_w_o, m_w_up, m_ffn_conv_w, m_w_down, v_w_mod, v_b_mod, v_g_pre_mix, v_g_post_mix, v_g_pre_ffn, v_g_post_ffn, v_w_in, v_ssm_log_dt, v_ssm_a_re, v_ssm_a_im, v_ssm_b_re, v_ssm_b_im, v_ssm_c_re, v_ssm_c_im, v_ssm_d, v_w_glu, v_b_glu, v_conv_mix_w, v_w_ssm_out, v_w_attn_out, v_w_conv_out, v_b_gate, v_w_o, v_w_up, v_ffn_conv_w, v_w_down):
    wts = dict(w_mod=w_mod, b_mod=b_mod, g_pre_mix=g_pre_mix, g_post_mix=g_post_mix, g_pre_ffn=g_pre_ffn, g_post_ffn=g_post_ffn, w_in=w_in, ssm_log_dt=ssm_log_dt, ssm_a_re=ssm_a_re, ssm_a_im=ssm_a_im, ssm_b_re=ssm_b_re, ssm_b_im=ssm_b_im, ssm_c_re=ssm_c_re, ssm_c_im=ssm_c_im, ssm_d=ssm_d, w_glu=w_glu, b_glu=b_glu, conv_mix_w=conv_mix_w, w_ssm_out=w_ssm_out, w_attn_out=w_attn_out, w_conv_out=w_conv_out, b_gate=b_gate, w_o=w_o, w_up=w_up, ffn_conv_w=ffn_conv_w, w_down=w_down)
    m_in = dict(w_mod=m_w_mod, b_mod=m_b_mod, g_pre_mix=m_g_pre_mix, g_post_mix=m_g_post_mix, g_pre_ffn=m_g_pre_ffn, g_post_ffn=m_g_post_ffn, w_in=m_w_in, ssm_log_dt=m_ssm_log_dt, ssm_a_re=m_ssm_a_re, ssm_a_im=m_ssm_a_im, ssm_b_re=m_ssm_b_re, ssm_b_im=m_ssm_b_im, ssm_c_re=m_ssm_c_re, ssm_c_im=m_ssm_c_im, ssm_d=m_ssm_d, w_glu=m_w_glu, b_glu=m_b_glu, conv_mix_w=m_conv_mix_w, w_ssm_out=m_w_ssm_out, w_attn_out=m_w_attn_out, w_conv_out=m_w_conv_out, b_gate=m_b_gate, w_o=m_w_o, w_up=m_w_up, ffn_conv_w=m_ffn_conv_w, w_down=m_w_down)
    v_in = dict(w_mod=v_w_mod, b_mod=v_b_mod, g_pre_mix=v_g_pre_mix, g_post_mix=v_g_post_mix, g_pre_ffn=v_g_pre_ffn, g_post_ffn=v_g_post_ffn, w_in=v_w_in, ssm_log_dt=v_ssm_log_dt, ssm_a_re=v_ssm_a_re, ssm_a_im=v_ssm_a_im, ssm_b_re=v_ssm_b_re, ssm_b_im=v_ssm_b_im, ssm_c_re=v_ssm_c_re, ssm_c_im=v_ssm_c_im, ssm_d=v_ssm_d, w_glu=v_w_glu, b_glu=v_b_glu, conv_mix_w=v_conv_mix_w, w_ssm_out=v_w_ssm_out, w_attn_out=v_w_attn_out, w_conv_out=v_w_conv_out, b_gate=v_b_gate, w_o=v_w_o, w_up=v_w_up, ffn_conv_w=v_ffn_conv_w, w_down=v_w_down)
    return _step(x, c, loss_target, wts, m_in, v_in)
```

```python
import functools
import math

import numpy as np
import jax
import jax.numpy as jnp
from jax import lax
from jax.experimental import pallas as pl
from jax.experimental.pallas import tpu as pltpu

f32 = jnp.float32
bf16 = jnp.bfloat16

N_DEV = 8
N_CHIPS = 4
V7X_VMEM_LIMIT_BYTES = 56 * 1024 * 1024
LANES = 128
SUBLANES = 8
BF16_ROWS = 16

RMS_EPS = 1e-6
NEG_INF = -1e30
HEAD_DIM = 64
SSM_GROUP = 16
SSM_STATE = 64
DSWA_PATTERNS = ((128, 1), (512, 4), (2048, 16))
QBLK = 128

ADAM_LR = 0.001
ADAM_B1 = 0.9
ADAM_B2 = 0.999
ADAM_EPS = 1e-08
ADAM_WD = 0.01
ADAM_STEP = 10

MESH = pl.DeviceIdType.MESH
ANY = pl.BlockSpec(memory_space=pl.ANY)


def _tile(n, pref, mult=LANES):
    t = min(pref, n) // mult * mult
    while t >= mult:
        if n % t == 0:
            return t
        t -= mult
    return n


def _params(*sem):
    return pltpu.CompilerParams(dimension_semantics=sem, vmem_limit_bytes=V7X_VMEM_LIMIT_BYTES)


def _mm(a, b, out_dtype, name, rhs_t=False, lhs_t=False, carry=()):
    k, m = a.shape if lhs_t else a.shape[::-1]
    n, k2 = b.shape if rhs_t else b.shape[::-1]
    assert k == k2 and a.dtype == bf16 and b.dtype == bf16, (a.shape, b.shape, a.dtype, b.dtype)
    tm, tn, tk = _tile(m, 1024, LANES if lhs_t else SUBLANES), _tile(n, 512), _tile(k, 2048)
    grid = (m // tm, n // tn, k // tk)
    contract = (((0 if lhs_t else 1,), (1 if rhs_t else 0,)), ((), ()))
    nc = len(carry)
    srcs = [job[0] for job in carry]
    ranges = [(job[1], job[2]) for job in carry]
    intos = [(i, job[3]) for i, job in enumerate(carry) if job[3] is not None]
    n_in = 2 + nc + len(intos)

    def body(*refs):
        a_ref, b_ref = refs[:2]
        o_ref = refs[n_in]
        acc_ref = refs[n_in + 1 + nc]
        steps = [pl.program_id(axis) for axis in range(3)]
        if nc:
            copies = _chip_copies(refs[2:2 + nc], refs[n_in + 1:n_in + 1 + nc], refs[n_in + 2 + nc], refs[n_in + 3 + nc], ranges)

            @pl.when((steps[0] == 0) & (steps[1] == 0) & (steps[2] == 0))
            def _():
                for cp in copies:
                    cp.start()

        @pl.when(steps[2] == 0)
        def _():
            acc_ref[...] = jnp.zeros_like(acc_ref)

        acc_ref[...] += lax.dot_general(a_ref[...], b_ref[...], contract, preferred_element_type=f32)

        @pl.when(steps[2] == grid[2] - 1)
        def _():
            o_ref[...] = acc_ref[...].astype(o_ref.dtype)

        if nc:
            @pl.when((steps[0] == grid[0] - 1) & (steps[1] == grid[1] - 1) & (steps[2] == grid[2] - 1))
            def _():
                for cp in copies:
                    cp.wait_recv()
                for cp in copies:
                    cp.wait_send()

    a_spec = pl.BlockSpec((tk, tm), lambda i, j, kk: (kk, i)) if lhs_t else pl.BlockSpec((tm, tk), lambda i, j, kk: (i, kk))
    b_spec = pl.BlockSpec((tn, tk), lambda i, j, kk: (j, kk)) if rhs_t else pl.BlockSpec((tk, tn), lambda i, j, kk: (kk, j))
    sems = pltpu.SemaphoreType.DMA((max(nc, 1), N_CHIPS - 1))
    res = pl.pallas_call(
        body, grid=grid,
        in_specs=[a_spec, b_spec] + [ANY] * (nc + len(intos)),
        out_specs=[pl.BlockSpec((tm, tn), lambda i, j, kk: (i, j))] + [ANY] * nc,
        out_shape=[jax.ShapeDtypeStruct((m, n), out_dtype)]
        + [jax.ShapeDtypeStruct((N_CHIPS - 1,) + x.shape[1:], x.dtype) for x in srcs],
        scratch_shapes=[pltpu.VMEM((tm, tn), f32)] + ([sems, sems] if nc else []),
        input_output_aliases={2 + nc + j: 1 + i for j, (i, _) in enumerate(intos)},
        compiler_params=_params(*(("arbitrary",) * 3 if nc else ("parallel", "parallel", "arbitrary"))), name=name)(
            a, b, *srcs, *[buf for _, buf in intos])
    return (res[0], list(res[1:])) if nc else res[0]


def _mm_sections(a, b_t, sections, name):
    m, k = a.shape
    n = b_t.shape[0]
    assert b_t.shape[1] == k and sum(w for w, _ in sections) == n
    tn = _tile(math.gcd(*[w for w, _ in sections]), 512)
    tm, tk = _tile(m, 1024, SUBLANES), _tile(k, 2048)
    grid = (m // tm, n // tn, k // tk)
    bounds, lo = [], 0
    for w, _ in sections:
        bounds.append((lo, lo + w // tn))
        lo += w // tn

    def body(a_ref, b_ref, *rest):
        o_refs, acc_ref = rest[:-1], rest[-1]
        j, kk = pl.program_id(1), pl.program_id(2)

        @pl.when(kk == 0)
        def _():
            acc_ref[...] = jnp.zeros_like(acc_ref)

        acc_ref[...] += lax.dot_general(a_ref[...], b_ref[...], (((1,), (1,)), ((), ())), preferred_element_type=f32)
        for o_ref, (first, end) in zip(o_refs, bounds):
            @pl.when((kk == grid[2] - 1) & (j >= first) & (j < end))
            def _(o_ref=o_ref):
                o_ref[...] = acc_ref[...].astype(o_ref.dtype)

    def out_spec(first, end):
        return pl.BlockSpec((tm, tn), lambda i, j, kk: (i, jnp.clip(j - first, 0, end - first - 1)))

    return pl.pallas_call(
        body, grid=grid,
        in_specs=[pl.BlockSpec((tm, tk), lambda i, j, kk: (i, kk)), pl.BlockSpec((tn, tk), lambda i, j, kk: (j, kk))],
        out_specs=[out_spec(first, end) for first, end in bounds],
        out_shape=[jax.ShapeDtypeStruct((m, w), dt) for w, dt in sections],
        scratch_shapes=[pltpu.VMEM((tm, tn), f32)],
        compiler_params=_params("arbitrary", "arbitrary", "arbitrary"), name=name)(a, b_t)


def _rowwise(fn, rows, vecs, out_dtypes, tm, name):
    n_rows = rows[0].shape[0]
    nr, nv = len(rows), len(vecs)
    outs = jax.eval_shape(fn, *[jax.ShapeDtypeStruct((tm, r.shape[1]), f32) for r in rows],
                          *[jax.ShapeDtypeStruct(v.shape, f32) for v in vecs])

    def body(*refs):
        vals = [r[...].astype(f32) for r in refs[:nr]] + [v[...] for v in refs[nr:nr + nv]]
        for o_ref, val in zip(refs[nr + nv:], fn(*vals)):
            o_ref[...] = val.astype(o_ref.dtype)

    res = pl.pallas_call(
        body, grid=(n_rows // tm,),
        in_specs=[pl.BlockSpec((tm, r.shape[1]), lambda i: (i, 0)) for r in rows]
        + [pl.BlockSpec(v.shape, lambda i: (0, 0)) for v in vecs],
        out_specs=[pl.BlockSpec((tm, o.shape[1]), lambda i: (i, 0)) for o in outs],
        out_shape=[jax.ShapeDtypeStruct((n_rows, o.shape[1]), dt) for o, dt in zip(outs, out_dtypes)],
        compiler_params=_params("parallel"), name=name)(*rows, *vecs)
    return res


def _rowwise_bwd(fn, rows, vecs, cots, row_dtypes, need_vecs, tm, name, add_to=None):
    n_rows = rows[0].shape[0]
    nr, nv = len(rows), len(vecs)
    add_to = add_to or {}
    flat_cots = [c for group in cots for c in group]
    add_keys = sorted(add_to)
    row_out = [i for i, dt in enumerate(row_dtypes) if dt is not None]
    n_in = nr + nv + len(flat_cots) + len(add_keys)

    def body(*refs):
        vals = [r[...].astype(f32) for r in refs[:nr]] + [v[...] for v in refs[nr:nr + nv]]
        pos = nr + nv
        cvals = []
        for group in cots:
            acc = refs[pos][...].astype(f32)
            for extra in range(1, len(group)):
                acc = acc + refs[pos + extra][...].astype(f32)
            pos += len(group)
            cvals.append(acc)
        adds = {key: refs[pos + j] for j, key in enumerate(add_keys)}
        _, vjp = jax.vjp(fn, *vals)
        grads = vjp(tuple(cvals))
        out_refs = refs[n_in:]
        for j, i in enumerate(row_out):
            g = grads[i]
            if i in adds:
                g = g + adds[i][...].astype(f32)
            out_refs[j][...] = g.astype(out_refs[j].dtype)
        if need_vecs:
            step = pl.program_id(0)
            for j in range(nv):
                acc_ref = out_refs[len(row_out) + j]

                @pl.when(step == 0)
                def _(acc_ref=acc_ref):
                    acc_ref[...] = jnp.zeros_like(acc_ref)

                acc_ref[...] += grads[nr + j]

    row_spec = lambda a: pl.BlockSpec((tm, a.shape[1]), lambda i: (i, 0))
    in_specs = ([row_spec(r) for r in rows] + [pl.BlockSpec(v.shape, lambda i: (0, 0)) for v in vecs]
                + [row_spec(c) for c in flat_cots] + [row_spec(add_to[key]) for key in add_keys])
    out_specs = [row_spec(rows[i]) for i in row_out]
    out_shape = [jax.ShapeDtypeStruct(rows[i].shape, row_dtypes[i]) for i in row_out]
    if need_vecs:
        out_specs += [pl.BlockSpec(v.shape, lambda i: (0, 0)) for v in vecs]
        out_shape += [jax.ShapeDtypeStruct(v.shape, f32) for v in vecs]
    return pl.pallas_call(
        body, grid=(n_rows // tm,), in_specs=in_specs, out_specs=out_specs, out_shape=out_shape,
        compiler_params=_params("arbitrary"), name=name)(*rows, *vecs, *flat_cots, *[add_to[key] for key in add_keys])


def _f_prenorm(x, g, sc, sh):
    y = x * lax.rsqrt(jnp.mean(x * x, axis=-1, keepdims=True) + RMS_EPS)
    return (y * g * (1.0 + sc) + sh,)


def _f_postres(x, y, g, gt):
    n = y * lax.rsqrt(jnp.mean(y * y, axis=-1, keepdims=True) + RMS_EPS)
    return (x + gt * (n * g),)


def _f_gelu(yc, u, d):
    return (jax.nn.gelu(yc + d * u),)


def _f_glu(g, z, b):
    return (g * jax.nn.sigmoid(z + b),)


def _f_merge(p0, p1, p2, ys, ya, yc, b0, b1, b2):
    return (jax.nn.sigmoid(p0 + b0) * ys + jax.nn.sigmoid(p1 + b1) * ya + jax.nn.sigmoid(p2 + b2) * yc,)


def _f_amerge(o0, o1, o2, l0, l1, l2):
    m = jnp.maximum(jnp.maximum(l0, l1), l2)
    e0, e1, e2 = jnp.exp(l0 - m), jnp.exp(l1 - m), jnp.exp(l2 - m)
    return ((e0 * o0 + e1 * o1 + e2 * o2) / (e0 + e1 + e2),)


def _f_add(a, b):
    return (a + b,)


def _shift_down(z, k, row):
    return jnp.where(row >= k, pltpu.roll(z, k, 0), 0.0)


def _shift_up(z, k, row):
    n = z.shape[0]
    return jnp.where(row < n - k, pltpu.roll(z, n - k, 0), 0.0)


def _conv3(z, w, row):
    return w[0:1, :] * z + w[1:2, :] * _shift_down(z, 1, row) + w[2:3, :] * _shift_down(z, 2, row)


def _conv3_bwd(z, w, dy, row):
    dz = w[0:1, :] * dy + w[1:2, :] * _shift_up(dy, 1, row) + w[2:3, :] * _shift_up(dy, 2, row)
    dw = [jnp.sum(dy * z, axis=0, keepdims=True),
          jnp.sum(dy * _shift_down(z, 1, row), axis=0, keepdims=True),
          jnp.sum(dy * _shift_down(z, 2, row), axis=0, keepdims=True)]
    return dz, dw


def _convgate_fwd(proj, w, off, width, name):
    n_rows = proj.shape[0]
    tc = _tile(width, 256)
    nb = width // tc
    base = off // tc

    def body(b_ref, c_ref, h_ref, w_ref, o_ref):
        row = lax.broadcasted_iota(jnp.int32, (n_rows, tc), 0)
        o_ref[...] = (b_ref[...] * _conv3(c_ref[...] * h_ref[...], w_ref[...], row)).astype(o_ref.dtype)

    col = lambda s: pl.BlockSpec((n_rows, tc), lambda j: (0, base + s * nb + j))
    return pl.pallas_call(
        body, grid=(nb,), in_specs=[col(0), col(1), col(2), pl.BlockSpec((3, tc), lambda j: (0, j))],
        out_specs=pl.BlockSpec((n_rows, tc), lambda j: (0, j)),
        out_shape=jax.ShapeDtypeStruct((n_rows, width), bf16),
        compiler_params=_params("parallel"), name=name)(proj, proj, proj, w)


def _convgate_bwd(proj, w, dcm, off, width, name):
    n_rows = proj.shape[0]
    tc = _tile(width, 128)
    nb = width // tc
    base = off // tc

    def body(b_ref, c_ref, h_ref, w_ref, d_ref, db_ref, dc_ref, dh_ref, dw_ref):
        row = lax.broadcasted_iota(jnp.int32, (n_rows, tc), 0)
        cb, cc, ch, wv, d = b_ref[...], c_ref[...], h_ref[...], w_ref[...], d_ref[...].astype(f32)
        z = cc * ch
        db_ref[...] = (d * _conv3(z, wv, row)).astype(db_ref.dtype)
        dz, dw = _conv3_bwd(z, wv, d * cb, row)
        dc_ref[...] = (dz * ch).astype(dc_ref.dtype)
        dh_ref[...] = (dz * cc).astype(dh_ref.dtype)
        for k in range(3):
            dw_ref[k:k + 1, :] = dw[k]

    col = lambda s: pl.BlockSpec((n_rows, tc), lambda j: (0, base + s * nb + j))
    own = pl.BlockSpec((n_rows, tc), lambda j: (0, j))
    wspec = pl.BlockSpec((3, tc), lambda j: (0, j))
    sec = jax.ShapeDtypeStruct((n_rows, width), bf16)
    return pl.pallas_call(
        body, grid=(nb,), in_specs=[col(0), col(1), col(2), wspec, own], out_specs=[own, own, own, wspec],
        out_shape=[sec, sec, sec, jax.ShapeDtypeStruct((3, width), f32)],
        compiler_params=_params("parallel"), name=name)(proj, proj, proj, w, dcm)


def _ffnconv_fwd(up, w, name):
    n_rows, two_f = up.shape
    half = two_f // 2
    tc = _tile(half, 256)
    nb = half // tc

    def body(a_ref, b_ref, wa_ref, wb_ref, o_ref):
        row = lax.broadcasted_iota(jnp.int32, (n_rows, tc), 0)
        ua = _conv3(a_ref[...], wa_ref[...], row)
        ub = _conv3(b_ref[...], wb_ref[...], row)
        o_ref[...] = (jax.nn.silu(ua) * ub).astype(o_ref.dtype)

    return pl.pallas_call(
        body, grid=(nb,),
        in_specs=[pl.BlockSpec((n_rows, tc), lambda j: (0, j)), pl.BlockSpec((n_rows, tc), lambda j: (0, nb + j)),
                  pl.BlockSpec((3, tc), lambda j: (0, j)), pl.BlockSpec((3, tc), lambda j: (0, nb + j))],
        out_specs=pl.BlockSpec((n_rows, tc), lambda j: (0, j)),
        out_shape=jax.ShapeDtypeStruct((n_rows, half), bf16),
        compiler_params=_params("parallel"), name=name)(up, up, w, w)


def _ffnconv_bwd(up, w, dact, name):
    n_rows, two_f = up.shape
    half = two_f // 2
    tc = _tile(half, 128)
    nb = half // tc

    def body(a_ref, b_ref, wa_ref, wb_ref, d_ref, da_ref, db_ref, dwa_ref, dwb_ref):
        row = lax.broadcasted_iota(jnp.int32, (n_rows, tc), 0)
        a, b, wa, wb, d = a_ref[...], b_ref[...], wa_ref[...], wb_ref[...], d_ref[...].astype(f32)
        ua = _conv3(a, wa, row)
        ub = _conv3(b, wb, row)
        sg = jax.nn.sigmoid(ua)
        d_ua = d * ub * (sg * (1.0 + ua * (1.0 - sg)))
        d_ub = d * (ua * sg)
        da, dwa = _conv3_bwd(a, wa, d_ua, row)
        db, dwb = _conv3_bwd(b, wb, d_ub, row)
        da_ref[...] = da.astype(da_ref.dtype)
        db_ref[...] = db.astype(db_ref.dtype)
        for k in range(3):
            dwa_ref[k:k + 1, :] = dwa[k]
            dwb_ref[k:k + 1, :] = dwb[k]

    lo = pl.BlockSpec((n_rows, tc), lambda j: (0, j))
    hi = pl.BlockSpec((n_rows, tc), lambda j: (0, nb + j))
    wlo = pl.BlockSpec((3, tc), lambda j: (0, j))
    whi = pl.BlockSpec((3, tc), lambda j: (0, nb + j))
    sec = jax.ShapeDtypeStruct((n_rows, half), bf16)
    wsec = jax.ShapeDtypeStruct((3, half), f32)
    da, db, dwa, dwb = pl.pallas_call(
        body, grid=(nb,), in_specs=[lo, hi, wlo, whi, lo], out_specs=[lo, lo, wlo, wlo],
        out_shape=[sec, sec, wsec, wsec], compiler_params=_params("parallel"), name=name)(up, up, w, w, dact)
    return da, db, jnp.concatenate([dwa, dwb], axis=1)


SCAN_TC = 256


def _cmul(ar, ai, br, bi):
    return ar * br - ai * bi, ar * bi + ai * br


def _scan(x, lam, adjoint, name):
    n_rows, two_s = x.shape
    tc = min(SCAN_TC, two_s // 2)
    n_tiles = two_s // (2 * tc)
    n_chunks = n_rows // SUBLANES

    def body(x_ref, lam_ref, hf_ref, hb_ref):
        lr = lam_ref[:, 0:tc]
        li = lam_ref[:, tc:2 * tc]
        if adjoint:
            li = -li
        row = lax.broadcasted_iota(jnp.int32, (SUBLANES, tc), 0)
        powers = [(lr, li)]
        for _ in range(SUBLANES - 1):
            powers.append(_cmul(powers[-1][0], powers[-1][1], lr, li))
        pr = jnp.zeros((SUBLANES, tc), f32)
        pi = jnp.zeros((SUBLANES, tc), f32)
        for t in range(SUBLANES):
            e = (SUBLANES - 1 - t) if adjoint else t
            pr = jnp.where(row == t, powers[e][0], pr)
            pi = jnp.where(row == t, powers[e][1], pi)
        last = 0 if adjoint else SUBLANES - 1

        def chunk(i, carry):
            cr, ci = carry
            c = (n_chunks - 1 - i) if adjoint else i
            rows = pl.ds(pl.multiple_of(c * SUBLANES, SUBLANES), SUBLANES)
            xr = x_ref[rows, 0:tc]
            xi = x_ref[rows, tc:2 * tc]
            for k in (1, 2, 4):
                ar, ai = powers[k - 1]
                if adjoint:
                    sr = jnp.where(row < SUBLANES - k, pltpu.roll(xr, SUBLANES - k, 0), 0.0)
                    si = jnp.where(row < SUBLANES - k, pltpu.roll(xi, SUBLANES - k, 0), 0.0)
                else:
                    sr = jnp.where(row >= k, pltpu.roll(xr, k, 0), 0.0)
                    si = jnp.where(row >= k, pltpu.roll(xi, k, 0), 0.0)
                xr, xi = xr + ar * sr - ai * si, xi + ar * si + ai * sr
            hr = xr + pr * cr - pi * ci
            hi = xi + pr * ci + pi * cr
            hf_ref[rows, 0:tc] = hr
            hf_ref[rows, tc:2 * tc] = hi
            hb_ref[rows, 0:tc] = hr.astype(bf16)
            hb_ref[rows, tc:2 * tc] = hi.astype(bf16)
            return (jnp.sum(jnp.where(row == last, hr, 0.0), axis=0, keepdims=True),
                    jnp.sum(jnp.where(row == last, hi, 0.0), axis=0, keepdims=True))

        zero = jnp.zeros((1, tc), f32)
        lax.fori_loop(0, n_chunks, chunk, (zero, zero))

    blk = pl.BlockSpec((n_rows, 2 * tc), lambda j: (0, j))
    return pl.pallas_call(
        body, grid=(n_tiles,), in_specs=[blk, pl.BlockSpec((1, 2 * tc), lambda j: (0, j))], out_specs=[blk, blk],
        out_shape=[jax.ShapeDtypeStruct(x.shape, f32), jax.ShapeDtypeStruct(x.shape, bf16)],
        compiler_params=_params("parallel"), name=name)(x, lam)


def _lam_grad(g, h, name):
    n_rows, two_s = g.shape
    tc = min(SCAN_TC, two_s // 2)

    def body(g_ref, h_ref, o_ref):
        row = lax.broadcasted_iota(jnp.int32, (n_rows, tc), 0)
        gr, gi = g_ref[:, 0:tc], g_ref[:, tc:2 * tc]
        hr, hi = _shift_down(h_ref[:, 0:tc], 1, row), _shift_down(h_ref[:, tc:2 * tc], 1, row)
        o_ref[:, 0:tc] = jnp.sum(gr * hr + gi * hi, axis=0, keepdims=True)
        o_ref[:, tc:2 * tc] = jnp.sum(gi * hr - gr * hi, axis=0, keepdims=True)

    blk = pl.BlockSpec((n_rows, 2 * tc), lambda j: (0, j))
    return pl.pallas_call(
        body, grid=(two_s // (2 * tc),), in_specs=[blk, blk], out_specs=pl.BlockSpec((1, 2 * tc), lambda j: (0, j)),
        out_shape=jax.ShapeDtypeStruct((1, two_s), f32), compiler_params=_params("parallel"), name=name)(g, h)


def _interleave(re, im, tc):
    lead = re.shape[:-1]
    s = re.shape[-1]
    return jnp.concatenate([re.reshape(*lead, s // tc, 1, tc), im.reshape(*lead, s // tc, 1, tc)], axis=-2).reshape(*lead, 2 * s)


def _deinterleave(z, tc):
    lead = z.shape[:-1]
    s = z.shape[-1] // 2
    z = z.reshape(*lead, s // tc, 2, tc)
    return z[..., 0, :].reshape(*lead, s), z[..., 1, :].reshape(*lead, s)


def _f_disc(log_dt, ar, ai, br, bi):
    dt = jnp.exp(log_dt)
    mag = jnp.exp(ar * dt)
    lr, li = mag * jnp.cos(ai * dt), mag * jnp.sin(ai * dt)
    den = ar * ar + ai * ai
    fr = ((lr - 1.0) * ar + li * ai) / den
    fi = (li * ar - (lr - 1.0) * ai) / den
    return lr, li, fr[None] * br - fi[None] * bi, fr[None] * bi + fi[None] * br


def _disc_fwd(log_dt, ar, ai, br, bi, name):
    def body(*refs):
        for o_ref, val in zip(refs[5:], _f_disc(*[r[...] for r in refs[:5]])):
            o_ref[...] = val

    sd = jax.ShapeDtypeStruct
    return pl.pallas_call(body, out_shape=[sd(ar.shape, f32), sd(ar.shape, f32), sd(br.shape, f32), sd(br.shape, f32)],
                          name=name)(log_dt, ar, ai, br, bi)


def _disc_bwd(log_dt, ar, ai, br, bi, cots, name):
    def body(*refs):
        _, vjp = jax.vjp(_f_disc, *[r[...] for r in refs[:5]])
        for o_ref, val in zip(refs[9:], vjp(tuple(r[...] for r in refs[5:9]))):
            o_ref[...] = val

    sd = jax.ShapeDtypeStruct
    return pl.pallas_call(
        body, out_shape=[sd(log_dt.shape, f32), sd(ar.shape, f32), sd(ar.shape, f32), sd(br.shape, f32), sd(br.shape, f32)],
        name=name)(log_dt, ar, ai, br, bi, *cots)


def _alibi_bias(n_heads_per_pattern):
    n_heads = len(DSWA_PATTERNS) * n_heads_per_pattern
    slopes = np.array([2.0 ** (-8.0 * (h + 1) / n_heads) for h in range(n_heads)], dtype=np.float32)
    qi = np.arange(QBLK)[:, None]
    kj = np.arange(QBLK)[None, :]
    cur, prev = [], []
    for h in range(n_heads):
        dil = DSWA_PATTERNS[h // n_heads_per_pattern][1]
        d_cur = qi - kj
        d_prev = QBLK + qi - kj
        cur.append(np.where(d_cur >= 0, -slopes[h] * (d_cur * dil).astype(np.float32), NEG_INF))
        prev.append(np.where(d_prev <= QBLK, -slopes[h] * (d_prev * dil).astype(np.float32), NEG_INF))
    return np.stack(cur).astype(np.float32), np.stack(prev).astype(np.float32)


def _blocks_per_residue(head, n_heads_per_pattern, n_blocks):
    pattern = head // n_heads_per_pattern
    out = n_blocks // DSWA_PATTERNS[-1][1]
    for p in range(len(DSWA_PATTERNS) - 2, -1, -1):
        out = jnp.where(pattern == p, n_blocks // DSWA_PATTERNS[p][1], out)
    return out


def _qk(a, b):
    return jnp.einsum('bqe,bke->bqk', a, b, preferred_element_type=f32)


def _pv(p, v):
    return jnp.einsum('bqk,bke->bqe', p.astype(bf16), v, preferred_element_type=f32)


def _ptq(p, a):
    return jnp.einsum('bqk,bqe->bke', p.astype(bf16), a, preferred_element_type=f32)


def _prev_block(t):
    return jnp.concatenate([jnp.zeros((1,) + t.shape[1:], t.dtype), t[:-1]], axis=0)


def _next_block(t):
    return jnp.concatenate([t[1:], jnp.zeros((1,) + t.shape[1:], t.dtype)], axis=0)


def _has_prev(head, hpp, n_blocks):
    blk = lax.broadcasted_iota(jnp.int32, (n_blocks, 1, 1), 0)
    return (blk & (_blocks_per_residue(head, hpp, n_blocks) - 1)) > 0


def _attn_fwd(q, k, v, hpp, name):
    n_heads, n_blocks = q.shape[0], q.shape[1]
    bias_cur, bias_prev = _alibi_bias(hpp)
    scale = HEAD_DIM ** -0.5

    def body(q_ref, k_ref, v_ref, bc_ref, bp_ref, o_ref, lse_ref):
        has_prev = _has_prev(pl.program_id(0), hpp, n_blocks)
        qv, kv, vv = q_ref[...], k_ref[...], v_ref[...]
        s_c = _qk(qv, kv) * scale + bc_ref[...][None]
        s_p = jnp.where(has_prev, _qk(qv, _prev_block(kv)) * scale + bp_ref[...][None], NEG_INF)
        m = jnp.maximum(jnp.max(s_c, axis=-1, keepdims=True), jnp.max(s_p, axis=-1, keepdims=True))
        p_c = jnp.exp(s_c - m)
        p_p = jnp.exp(s_p - m)
        l = jnp.sum(p_c, axis=-1, keepdims=True) + jnp.sum(p_p, axis=-1, keepdims=True)
        o_ref[...] = (_pv(p_c, vv) + _pv(p_p, _prev_block(vv))) / l
        lse_ref[...] = m + jnp.log(l)

    head = pl.BlockSpec((None, n_blocks, QBLK, HEAD_DIM), lambda a: (a, 0, 0, 0))
    bias = pl.BlockSpec((None, QBLK, QBLK), lambda a: (a, 0, 0))
    return pl.pallas_call(
        body, grid=(n_heads,), in_specs=[head, head, head, bias, bias],
        out_specs=[head, pl.BlockSpec((None, n_blocks, QBLK, 1), lambda a: (a, 0, 0, 0))],
        out_shape=[jax.ShapeDtypeStruct(q.shape, f32), jax.ShapeDtypeStruct(q.shape[:3] + (1,), f32)],
        compiler_params=_params("parallel"), name=name)(q, k, v, jnp.asarray(bias_cur), jnp.asarray(bias_prev))


def _attn_bwd(q, k, v, o, lse, do, dlse_x, hpp, name):
    n_heads, n_blocks = q.shape[0], q.shape[1]
    bias_cur, bias_prev = _alibi_bias(hpp)
    scale = HEAD_DIM ** -0.5

    def body(q_ref, k_ref, v_ref, o_ref, l_ref, do_ref, dl_ref, bc_ref, bp_ref, dq_ref, dk_ref, dv_ref):
        has_prev = _has_prev(pl.program_id(0), hpp, n_blocks)
        qv, kv, vv = q_ref[...], k_ref[...], v_ref[...]
        kp, vp = _prev_block(kv), _prev_block(vv)
        dov = do_ref[...]
        do_b = dov.astype(bf16)
        lse = l_ref[...]
        corr = jnp.sum(dl_ref[...], axis=-1, keepdims=True) - jnp.sum(dov * o_ref[...], axis=-1, keepdims=True)
        p_c = jnp.exp(_qk(qv, kv) * scale + bc_ref[...][None] - lse)
        ds_c = p_c * (_qk(do_b, vv) + corr)
        p_p = jnp.where(has_prev, jnp.exp(_qk(qv, kp) * scale + bp_ref[...][None] - lse), 0.0)
        ds_p = p_p * (_qk(do_b, vp) + corr)
        dq_ref[...] = ((_pv(ds_c, kv) + _pv(ds_p, kp)) * scale).astype(dq_ref.dtype)
        dk_ref[...] = ((_ptq(ds_c, qv) + _next_block(_ptq(ds_p, qv))) * scale).astype(dk_ref.dtype)
        dv_ref[...] = (_ptq(p_c, do_b) + _next_block(_ptq(p_p, do_b))).astype(dv_ref.dtype)

    head = pl.BlockSpec((None, n_blocks, QBLK, HEAD_DIM), lambda a: (a, 0, 0, 0))
    head1 = pl.BlockSpec((None, n_blocks, QBLK, 1), lambda a: (a, 0, 0, 0))
    bias = pl.BlockSpec((None, QBLK, QBLK), lambda a: (a, 0, 0))
    out = jax.ShapeDtypeStruct(q.shape, bf16)
    return pl.pallas_call(
        body, grid=(n_heads,), in_specs=[head, head, head, head, head1, head, head, bias, bias],
        out_specs=[head, head, head], out_shape=[out, out, out],
        compiler_params=_params("parallel"), name=name)(q, k, v, o, lse, do, dlse_x, jnp.asarray(bias_cur), jnp.asarray(bias_prev))


def _to_blocks(t, hpp):
    n_rows = t.shape[0]
    t = t.reshape(n_rows, len(DSWA_PATTERNS), hpp, HEAD_DIM)
    out = []
    for p, (_, dil) in enumerate(DSWA_PATTERNS):
        tp = t[:, p].reshape(n_rows // dil, dil, hpp, HEAD_DIM).transpose(2, 1, 0, 3)
        out.append(tp.reshape(hpp, n_rows // QBLK, QBLK, HEAD_DIM))
    return jnp.concatenate(out, axis=0)


def _from_blocks(t, hpp):
    n_blocks, last = t.shape[1], t.shape[3]
    n_rows = n_blocks * QBLK
    out = []
    for p, (_, dil) in enumerate(DSWA_PATTERNS):
        tp = t[p * hpp:(p + 1) * hpp].reshape(hpp, dil, n_rows // dil, last).transpose(2, 1, 0, 3)
        out.append(tp.reshape(n_rows, hpp * last))
    return out


def _loss_head(y, target, name):
    n_rows, d = y.shape
    tm = _tile(n_rows, 256, SUBLANES)

    def body(y_ref, t_ref, dy_ref, loss_ref):
        diff = y_ref[...] - t_ref[...]
        dy_ref[...] = diff * (1.0 / d)

        @pl.when(pl.program_id(0) == 0)
        def _():
            loss_ref[...] = jnp.zeros_like(loss_ref)

        loss_ref[...] += jnp.sum(jnp.sum(diff * diff, axis=1, keepdims=True), axis=0, keepdims=True) * (0.5 / d)

    row = pl.BlockSpec((tm, d), lambda i: (i, 0))
    return pl.pallas_call(
        body, grid=(n_rows // tm,), in_specs=[row, row], out_specs=[row, pl.BlockSpec((1, 1), lambda i: (0, 0))],
        out_shape=[jax.ShapeDtypeStruct(y.shape, f32), jax.ShapeDtypeStruct((1, 1), f32)],
        compiler_params=_params("arbitrary"), name=name)(y, target)


def _as2d(a):
    if a.ndim >= 2 and a.shape[-1] >= LANES:
        return a.reshape(-1, a.shape[-1])
    return a.reshape(-1, LANES) if a.size % LANES == 0 else a.reshape(1, -1)


ELEMENTWISE_BLOCK_BYTES = 2 << 20


def _row_tile(n_rows, n_cols):
    return _tile(n_rows, max(SUBLANES, ELEMENTWISE_BLOCK_BYTES // (4 * n_cols)), SUBLANES)


def _adamw(w, g, m, v, name):
    shape = w.shape
    w2, g2, m2, v2 = _as2d(w), _as2d(g), _as2d(m), _as2d(v)
    n_rows, n_cols = w2.shape
    tm = _row_tile(n_rows, n_cols)

    def body(w_ref, g_ref, m_ref, v_ref, d_ref, mo_ref, vo_ref):
        gv = g_ref[...]
        mn = ADAM_B1 * m_ref[...] + (1.0 - ADAM_B1) * gv
        vn = ADAM_B2 * v_ref[...] + (1.0 - ADAM_B2) * jnp.square(gv)
        m_hat = mn / (1.0 - ADAM_B1 ** ADAM_STEP)
        v_hat = vn / (1.0 - ADAM_B2 ** ADAM_STEP)
        d_ref[...] = -ADAM_LR * (m_hat / (jnp.sqrt(v_hat) + ADAM_EPS) + ADAM_WD * w_ref[...])
        mo_ref[...] = mn
        vo_ref[...] = vn

    row = pl.BlockSpec((tm, n_cols), lambda i: (i, 0))
    out = jax.ShapeDtypeStruct(w2.shape, f32)
    d, mn, vn = pl.pallas_call(body, grid=(n_rows // tm,), in_specs=[row] * 4, out_specs=[row] * 3, out_shape=[out] * 3,
                               compiler_params=_params("parallel"), name=name)(w2, g2, m2, v2)
    return d.reshape(shape), mn.reshape(shape), vn.reshape(shape)


def _sum_slabs(x, name):
    n = x.shape[0]
    shape = x.shape[1:]
    x3 = x.reshape(n, -1, shape[-1])
    n_rows, n_cols = x3.shape[1:]
    tm = _row_tile(n_rows, n_cols)

    def body(x_ref, o_ref):
        acc = x_ref[0].astype(f32)
        for s in range(1, n):
            acc = acc + x_ref[s].astype(f32)
        o_ref[...] = acc

    out = pl.pallas_call(
        body, grid=(n_rows // tm,), in_specs=[pl.BlockSpec((n, tm, n_cols), lambda i: (0, i, 0))],
        out_specs=pl.BlockSpec((tm, n_cols), lambda i: (i, 0)), out_shape=jax.ShapeDtypeStruct((n_rows, n_cols), f32),
        compiler_params=_params("parallel"), name=name)(x3)
    return out.reshape(shape)


def _mod_fwd(c_all, w_mod, b_cols, name):
    depth, d, cols = w_mod.shape
    tn = _tile(cols, 512)

    def body(c_ref, w_ref, b_ref, o_ref):
        cond = jax.nn.silu(c_ref[...])
        o_ref[...] = jnp.dot(cond, w_ref[...], preferred_element_type=f32, precision=lax.Precision.HIGHEST) + b_ref[...]

    return pl.pallas_call(
        body, grid=(depth, cols // tn),
        in_specs=[pl.BlockSpec((N_DEV, d), lambda l, j: (0, 0)), pl.BlockSpec((None, d, tn), lambda l, j: (l, 0, j)),
                  pl.BlockSpec((None, 1, tn), lambda l, j: (l, 0, j))],
        out_specs=pl.BlockSpec((None, N_DEV, tn), lambda l, j: (l, 0, j)),
        out_shape=jax.ShapeDtypeStruct((depth, N_DEV, cols), f32),
        compiler_params=_params("parallel", "parallel"), name=name)(c_all, w_mod, b_cols)


def _mod_wgrad(c_all_t, dmod_cols, name):
    d = c_all_t.shape[0]
    depth, _, cols = dmod_cols.shape
    tm = _tile(d, 256, SUBLANES)

    def body(c_ref, g_ref, o_ref):
        cond = jax.nn.silu(c_ref[...])
        o_ref[...] = jnp.dot(cond, g_ref[...], preferred_element_type=f32, precision=lax.Precision.HIGHEST)

    return pl.pallas_call(
        body, grid=(depth, d // tm),
        in_specs=[pl.BlockSpec((tm, N_DEV), lambda l, i: (i, 0)), pl.BlockSpec((None, N_DEV, cols), lambda l, i: (l, 0, 0))],
        out_specs=pl.BlockSpec((None, tm, cols), lambda l, i: (l, i, 0)),
        out_shape=jax.ShapeDtypeStruct((depth, d, cols), f32),
        compiler_params=_params("parallel", "parallel"), name=name)(c_all_t, dmod_cols)


def _position():
    return lax.axis_index("x"), lax.axis_index("y"), lax.axis_index("c")


def _linear(p):
    return 4 * p[0] + 2 * p[1] + p[2]


def _all_gather(xs, name):
    n = len(xs)

    def body(*refs):
        x_refs, o_refs = refs[:n], refs[n:2 * n]
        send_sems, recv_sems, local_sems = refs[2 * n:]
        x, y, c = _position()
        me, sibling = (x, y, c), (x, y, 1 - c)
        chips = [(1 - x, y), (x, 1 - y), (1 - x, 1 - y)]

        def copy(a, k, block, to, src=None):
            slab = o_refs[a].at[_linear(block)]
            return pltpu.make_async_remote_copy(
                src_ref=slab if src is None else src, dst_ref=slab, send_sem=send_sems.at[a, k], recv_sem=recv_sems.at[a, k],
                device_id=to, device_id_type=MESH)

        mine = [pltpu.make_async_copy(x_refs[a], o_refs[a].at[_linear(me)], local_sems.at[a]) for a in range(n)]
        first = []
        for a in range(n):
            mine[a].start()
            first.append(copy(a, 0, me, sibling, src=x_refs[a]))
            first += [copy(a, 1 + j, me, (*chip, c), src=x_refs[a]) for j, chip in enumerate(chips)]
        for cp in first:
            cp.start()
        passed = []
        for j, chip in enumerate(chips):
            for a in range(n):
                copy(a, 1 + j, (*chip, c), me).wait_recv()
                cp = copy(a, 4 + j, (*chip, c), sibling)
                cp.start()
                passed.append(cp)
        for a in range(n):
            copy(a, 0, sibling, me).wait_recv()
            for j, chip in enumerate(chips):
                copy(a, 4 + j, (*chip, 1 - c), me).wait_recv()
        for cp in first + passed:
            cp.wait_send()
        for cp in mine:
            cp.wait()

    return pl.pallas_call(
        body, in_specs=[ANY] * n, out_specs=[ANY] * n,
        out_shape=[jax.ShapeDtypeStruct((N_DEV,) + x.shape, x.dtype) for x in xs],
        scratch_shapes=[pltpu.SemaphoreType.DMA((n, 7)), pltpu.SemaphoreType.DMA((n, 7)), pltpu.SemaphoreType.DMA((n,))],
        name=name)(*xs)


def _all_gather_relayed(xs, name):
    n = len(xs)
    halves = [x.shape[0] // 2 for x in xs]

    def body(*refs):
        x_refs, o_refs = refs[:n], refs[n:2 * n]
        send_sems, recv_sems = refs[2 * n:]
        x, y, c = _position()
        across_x, across_y, across_xy = (1 - x, y, c), (x, 1 - y, c), (1 - x, 1 - y, c)
        sibling = (x, y, 1 - c)

        def copy(a, k, slab, to, rows=None, src=None):
            dst = o_refs[a].at[_linear(slab)] if rows is None else o_refs[a].at[_linear(slab), rows]
            return pltpu.make_async_remote_copy(
                src_ref=dst if src is None else src, dst_ref=dst, send_sem=send_sems.at[a, k], recv_sem=recv_sems.at[a, k],
                device_id=to, device_id_type=MESH)

        def other_core(p):
            return (p[0], p[1], 1 - c)

        me = (x, y, c)
        started = []
        for a in range(n):
            started += [copy(a, 0, me, sibling, src=x_refs[a]), copy(a, 1, me, across_x, src=x_refs[a]),
                        copy(a, 2, me, across_y, src=x_refs[a])]
        for cp in started:
            cp.start()
        later = []
        for a in range(n):
            first, rest = pl.ds(0, halves[a]), pl.ds(halves[a], xs[a].shape[0] - halves[a])
            copy(a, 1, across_x, me).wait_recv()
            later += [copy(a, 3, across_x, across_y, rows=first), copy(a, 5, across_x, sibling)]
            later[-2].start()
            later[-1].start()
            copy(a, 2, across_y, me).wait_recv()
            later += [copy(a, 4, across_y, across_x, rows=rest), copy(a, 6, across_y, sibling)]
            later[-2].start()
            later[-1].start()
        for a in range(n):
            first, rest = pl.ds(0, halves[a]), pl.ds(halves[a], xs[a].shape[0] - halves[a])
            copy(a, 3, across_xy, me, rows=first).wait_recv()
            copy(a, 4, across_xy, me, rows=rest).wait_recv()
            later.append(copy(a, 7, across_xy, sibling))
            later[-1].start()
        for a in range(n):
            copy(a, 0, sibling, me).wait_recv()
            for k, slab in ((5, across_x), (6, across_y), (7, across_xy)):
                copy(a, k, other_core(slab), me).wait_recv()
        for cp in started + later:
            cp.wait_send()

    sems = pltpu.SemaphoreType.DMA((n, 8))
    return pl.pallas_call(
        body, in_specs=[ANY] * n, out_specs=[ANY] * n,
        out_shape=[jax.ShapeDtypeStruct((N_DEV,) + x.shape, x.dtype) for x in xs],
        scratch_shapes=[sems, sems], name=name)(*xs)


def _pair_exchange(xs, name):
    n = len(xs)

    def body(*refs):
        x_refs, got_refs = refs[:n], refs[n:2 * n]
        send_sems, recv_sems = refs[2 * n:]
        x, y, c = _position()
        sends = []
        for a in range(n):
            for p in range(N_CHIPS):
                sends.append(pltpu.make_async_remote_copy(
                    src_ref=x_refs[a].at[p, 1 - c], dst_ref=got_refs[a].at[p], send_sem=send_sems.at[a, p],
                    recv_sem=recv_sems.at[a, p], device_id=(x, y, 1 - c), device_id_type=MESH))
        for cp in sends:
            cp.start()
        for cp in sends:
            cp.wait_recv()
        for cp in sends:
            cp.wait_send()

    sems = pltpu.SemaphoreType.DMA((n, N_CHIPS))
    return pl.pallas_call(body, in_specs=[ANY] * n, out_specs=[ANY] * n,
                          out_shape=[jax.ShapeDtypeStruct((N_CHIPS,) + x.shape[2:], x.dtype) for x in xs],
                          scratch_shapes=[sems, sems], name=name)(*xs)


def _chip_copies(x_refs, o_refs, send_sems, recv_sems, ranges=None):
    x, y, c = _position()
    copies = []
    for a in range(len(x_refs)):
        rows = pl.ds(0, x_refs[a].shape[1]) if ranges is None else pl.ds(ranges[a][0], ranges[a][1] - ranges[a][0])
        for k, (px, py) in enumerate([(1 - x, y), (x, 1 - y), (1 - x, 1 - y)]):
            copies.append(pltpu.make_async_remote_copy(
                src_ref=x_refs[a].at[2 * px + py, rows], dst_ref=o_refs[a].at[k, rows], send_sem=send_sems.at[a, k],
                recv_sem=recv_sems.at[a, k], device_id=(px, py, c), device_id_type=MESH))
    return copies


def _chip_exchange(xs, name):
    n = len(xs)

    def body(*refs):
        sends = _chip_copies(refs[:n], refs[n:2 * n], refs[2 * n], refs[2 * n + 1])
        for cp in sends:
            cp.start()
        for cp in sends:
            cp.wait_recv()
        for cp in sends:
            cp.wait_send()

    sems = pltpu.SemaphoreType.DMA((n, N_CHIPS - 1))
    return pl.pallas_call(body, in_specs=[ANY] * n, out_specs=[ANY] * n,
                          out_shape=[jax.ShapeDtypeStruct((N_CHIPS - 1,) + x.shape[1:], x.dtype) for x in xs],
                          scratch_shapes=[sems, sems], name=name)(*xs)


def _pick_add(picked, index, others, out_dtype, name):
    per_slab = picked.ndim == 4
    n_rows, n_cols = picked.shape[-2:]
    tm = _row_tile(n_rows, n_cols)
    n_other = 1 if per_slab else others.shape[0]

    def body(idx_ref, p_ref, o_ref, out_ref):
        acc = p_ref[...].astype(f32)
        if per_slab:
            acc = acc + o_ref[...].astype(f32)
        else:
            for k in range(n_other):
                acc = acc + o_ref[k].astype(f32)
        out_ref[...] = acc.astype(out_ref.dtype)

    if per_slab:
        grid = (picked.shape[0], n_rows // tm)
        in_specs = [pl.BlockSpec((None, None, tm, n_cols), lambda p, i, idx: (p, idx[0], i, 0)),
                    pl.BlockSpec((None, tm, n_cols), lambda p, i, idx: (p, i, 0))]
        out_specs = pl.BlockSpec((None, tm, n_cols), lambda p, i, idx: (p, i, 0))
        out_shape = jax.ShapeDtypeStruct((picked.shape[0], n_rows, n_cols), out_dtype)
    else:
        grid = (n_rows // tm,)
        in_specs = [pl.BlockSpec((None, tm, n_cols), lambda i, idx: (idx[0], i, 0)),
                    pl.BlockSpec((n_other, tm, n_cols), lambda i, idx: (0, i, 0))]
        out_specs = pl.BlockSpec((tm, n_cols), lambda i, idx: (i, 0))
        out_shape = jax.ShapeDtypeStruct((n_rows, n_cols), out_dtype)
    return pl.pallas_call(
        body, grid_spec=pltpu.PrefetchScalarGridSpec(num_scalar_prefetch=1, grid=grid, in_specs=in_specs, out_specs=out_specs),
        out_shape=out_shape, compiler_params=_params(*(["parallel"] * len(grid))), name=name)(
            jnp.reshape(index, (1,)).astype(jnp.int32), picked, others)


def _chip_sums(xs, name):
    c = lax.axis_index("c")
    pairs = [v.reshape((N_CHIPS, 2) + v.shape[1:]) for v in xs]
    got = _pair_exchange(pairs, f"{name}_pair")
    return [_pick_add(a, c, b, a.dtype, f"{name}_pair_sum{i}") for i, (a, b) in enumerate(zip(pairs, got))]


def _total_sums(chip_sums, arrived, name):
    my_chip = 2 * lax.axis_index("x") + lax.axis_index("y")
    return [_pick_add(a, my_chip, b, f32, f"{name}_sum{i}") for i, (a, b) in enumerate(zip(chip_sums, arrived))]


def _column_group(j, tc, shape):
    col = lax.broadcasted_iota(jnp.int32, shape, 1)
    return (j * tc + col % tc) // SSM_STATE


def _bd_build(bt, n_groups, tc, name):
    n_rows = n_groups * SSM_GROUP
    two_s = bt.shape[1]

    def body(b_ref, o_ref):
        grp = _column_group(pl.program_id(0), tc, (n_rows, 2 * tc))
        row = lax.broadcasted_iota(jnp.int32, (n_rows, 2 * tc), 0)
        tiled = jnp.concatenate([b_ref[...]] * n_groups, axis=0)
        o_ref[...] = jnp.where(grp == row // SSM_GROUP, tiled, 0.0).astype(o_ref.dtype)

    return pl.pallas_call(
        body, grid=(two_s // (2 * tc),), in_specs=[pl.BlockSpec((SSM_GROUP, 2 * tc), lambda j: (0, j))],
        out_specs=pl.BlockSpec((n_rows, 2 * tc), lambda j: (0, j)), out_shape=jax.ShapeDtypeStruct((n_rows, two_s), bf16),
        compiler_params=_params("parallel"), name=name)(bt)


def _bd_fold(m, n_groups, tc, name):
    n_rows, two_s = m.shape

    def body(m_ref, o_ref):
        grp = _column_group(pl.program_id(0), tc, (SSM_GROUP, 2 * tc))
        acc = jnp.zeros((SSM_GROUP, 2 * tc), f32)
        for g in range(n_groups):
            acc = acc + jnp.where(grp == g, m_ref[g * SSM_GROUP:(g + 1) * SSM_GROUP, :], 0.0)
        o_ref[...] = acc

    return pl.pallas_call(
        body, grid=(two_s // (2 * tc),), in_specs=[pl.BlockSpec((n_rows, 2 * tc), lambda j: (0, j))],
        out_specs=pl.BlockSpec((SSM_GROUP, 2 * tc), lambda j: (0, j)), out_shape=jax.ShapeDtypeStruct((SSM_GROUP, two_s), f32),
        compiler_params=_params("parallel"), name=name)(m)


def _ssm_setup(p, tag):
    n_groups = p["ssm_a_re"].shape[0]
    s = n_groups * SSM_STATE
    tc = min(SCAN_TC, s)
    log_dt = p["ssm_log_dt"].reshape(n_groups, 1)
    br_t = p["ssm_b_re"].transpose(2, 0, 1)
    bi_t = p["ssm_b_im"].transpose(2, 0, 1)
    disc_in = (log_dt, p["ssm_a_re"], p["ssm_a_im"], br_t, bi_t)
    lr, li, bbr, bbi = _disc_fwd(*disc_in, name=f"ssm_disc{tag}")
    lam = _interleave(lr.reshape(1, s), li.reshape(1, s), tc)
    bb = _bd_build(_interleave(bbr.reshape(SSM_GROUP, s), bbi.reshape(SSM_GROUP, s), tc), n_groups, tc, f"ssm_in_map{tag}")
    cr_t = p["ssm_c_re"].transpose(1, 0, 2).reshape(SSM_GROUP, s)
    ci_t = p["ssm_c_im"].transpose(1, 0, 2).reshape(SSM_GROUP, s)
    cc = _bd_build(_interleave(cr_t, -ci_t, tc), n_groups, tc, f"ssm_out_map{tag}")
    return dict(disc_in=disc_in, lam=lam, bb=bb, cc=cc, tc=tc, n_groups=n_groups)


def _apply_w(a, w, key, out_dtype, name):
    return _mm(a, w[key], out_dtype, name, rhs_t=key in COL_SHARDED)


def _apply_wt(dy, w, key, out_dtype, name, carry=()):
    return _mm(dy, w[key], out_dtype, name, rhs_t=key not in COL_SHARDED, carry=carry)


def _weight_grad(a, dy, key, name, carry=()):
    lhs, rhs = (dy, a) if key in COL_SHARDED else (a, dy)
    return _mm(lhs, rhs, bf16, name, lhs_t=True, carry=carry)


class _Cargo:
    def __init__(self, chip_sums):
        self.chip_sums = chip_sums
        self.arrived = {}

    def hosted(self, call, jobs):
        if self.chip_sums is None:
            return call(())
        carry = []
        for key, part, parts in jobs:
            n_rows = self.chip_sums[key].shape[1]
            step = n_rows if parts == 1 else -(-n_rows // parts // BF16_ROWS) * BF16_ROWS
            carry.append((self.chip_sums[key], part * step, min(n_rows, (part + 1) * step), self.arrived.get(key)))
        out, got = call(carry)
        for (key, _, _), buf in zip(jobs, got):
            self.arrived[key] = buf
        return out


def _layer_fwd(x, mod, w, p, tag):
    n_rows, d = x.shape
    w4 = d // 4
    hpp = w4 // HEAD_DIM
    sh1, sc1, gt1, sh2, sc2, gt2 = mod
    vec = lambda a: a.reshape(1, -1)
    s = dict(x=x)
    (s["h"],) = _rowwise(_f_prenorm, [x], [vec(p["g_pre_mix"]), sc1, sh1], [bf16], 256, f"prenorm_mix{tag}")
    sections = [(w4, f32), (9 * w4, bf16), (3 * w4, f32), (d, f32), (d, f32), (d, f32)]
    s["u32"], qkv, s["conv"], *s["pg"] = _mm_sections(s["h"], w["w_in"], sections, f"in_proj{tag}")
    ssm = s["ssm"] = _ssm_setup(p, tag)
    s["u"] = s["u32"].astype(bf16)
    xin = _mm(s["u"], ssm["bb"], f32, f"ssm_in{tag}")
    s["hf"], s["hb"] = _scan(xin, ssm["lam"], False, f"ssm_scan{tag}")
    s["yc"] = _mm(s["hb"], ssm["cc"], f32, f"ssm_out{tag}", rhs_t=True)
    (s["g"],) = _rowwise(_f_gelu, [s["yc"], s["u32"]], [vec(p["ssm_d"])], [bf16], 256, f"ssm_gelu{tag}")
    s["z"] = _apply_w(s["g"], w, "w_glu", f32, f"glu_proj{tag}")
    (s["s_ssm"],) = _rowwise(_f_glu, [s["g"], s["z"]], [vec(p["b_glu"])], [bf16], 256, f"glu{tag}")
    s["y_ssm"] = _apply_w(s["s_ssm"], w, "w_ssm_out", f32, f"ssm_proj{tag}")
    s["qb"], s["kb"], s["vb"] = (_to_blocks(qkv[:, i * 3 * w4:(i + 1) * 3 * w4], hpp) for i in range(3))
    s["ob"], s["lseb"] = _attn_fwd(s["qb"], s["kb"], s["vb"], hpp, f"attn{tag}")
    s["o_tok"] = _from_blocks(s["ob"], hpp)
    s["lse_x"] = [jnp.repeat(l, HEAD_DIM, axis=1) for l in _from_blocks(s["lseb"], hpp)]
    (s["attn_o"],) = _rowwise(_f_amerge, s["o_tok"] + s["lse_x"], [], [bf16], 256, f"attn_merge{tag}")
    s["y_attn"] = _apply_w(s["attn_o"], w, "w_attn_out", f32, f"attn_proj{tag}")
    s["cm"] = _convgate_fwd(s["conv"], p["conv_mix_w"], 0, w4, f"convgate{tag}")
    s["y_conv"] = _apply_w(s["cm"], w, "w_conv_out", f32, f"conv_proj{tag}")
    s["bg"] = [p["b_gate"][i * d:(i + 1) * d].reshape(1, d) for i in range(3)]
    (s["merged"],) = _rowwise(_f_merge, s["pg"] + [s["y_ssm"], s["y_attn"], s["y_conv"]], s["bg"], [bf16], 128, f"merge{tag}")
    s["y"] = _apply_w(s["merged"], w, "w_o", f32, f"o_proj{tag}")
    (s["x1"],) = _rowwise(_f_postres, [x, s["y"]], [vec(p["g_post_mix"]), gt1], [f32], 256, f"postres_mix{tag}")
    (s["h2"],) = _rowwise(_f_prenorm, [s["x1"]], [vec(p["g_pre_ffn"]), sc2, sh2], [bf16], 256, f"prenorm_ffn{tag}")
    s["up"] = _apply_w(s["h2"], w, "w_up", f32, f"up_proj{tag}")
    s["act"] = _ffnconv_fwd(s["up"], p["ffn_conv_w"], f"ffnconv{tag}")
    s["y2"] = _apply_w(s["act"], w, "w_down", f32, f"down_proj{tag}")
    (x2,) = _rowwise(_f_postres, [s["x1"], s["y2"]], [vec(p["g_post_ffn"]), gt2], [f32], 256, f"postres_ffn{tag}")
    return x2, s


def _layer_bwd(dx2, s, mod, w, p, tag, cargo):
    x = s["x"]
    n_rows, d = x.shape
    w4 = d // 4
    hpp = w4 // HEAD_DIM
    sh1, sc1, gt1, sh2, sc2, gt2 = mod
    vec = lambda a: a.reshape(1, -1)
    gw, gp = {}, {}
    dy2, d_gpf, d_gt2 = _rowwise_bwd(_f_postres, [s["x1"], s["y2"]], [vec(p["g_post_ffn"]), gt2], [[dx2]], [None, bf16], True,
                                     256, f"postres_ffn_bwd{tag}")
    gp["g_post_ffn"] = d_gpf
    dact = cargo.hosted(lambda carry: _apply_wt(dy2, w, "w_down", bf16, f"down_dx{tag}", carry), [("w_down", 0, 2)])
    gw["w_down"] = cargo.hosted(lambda carry: _weight_grad(s["act"], dy2, "w_down", f"down_dw{tag}", carry), [("w_down", 1, 2)])
    da, db, gp["ffn_conv_w"] = _ffnconv_bwd(s["up"], p["ffn_conv_w"], dact, f"ffnconv_bwd{tag}")
    dup = jnp.concatenate([da, db], axis=1)
    dh2 = cargo.hosted(lambda carry: _apply_wt(dup, w, "w_up", f32, f"up_dx{tag}", carry), [("w_in", 0, 2)])
    gw["w_up"] = cargo.hosted(lambda carry: _weight_grad(s["h2"], dup, "w_up", f"up_dw{tag}", carry), [("w_up", 0, 2)])
    dx1, d_g, d_sc2, d_sh2 = _rowwise_bwd(_f_prenorm, [s["x1"]], [vec(p["g_pre_ffn"]), sc2, sh2], [[dh2]], [f32], True,
                                          256, f"prenorm_ffn_bwd{tag}", add_to={0: dx2})
    gp["g_pre_ffn"] = d_g
    dy, d_gpm, d_gt1 = _rowwise_bwd(_f_postres, [x, s["y"]], [vec(p["g_post_mix"]), gt1], [[dx1]], [None, bf16], True,
                                    256, f"postres_mix_bwd{tag}")
    gp["g_post_mix"] = d_gpm
    dmerged = cargo.hosted(lambda carry: _apply_wt(dy, w, "w_o", f32, f"o_dx{tag}", carry), [("w_o", 0, 1)])
    gw["w_o"] = cargo.hosted(lambda carry: _weight_grad(s["merged"], dy, "w_o", f"o_dw{tag}", carry),
                             [(k, 0, 1) for k in LIGHT_SHARDED])
    res = _rowwise_bwd(_f_merge, s["pg"] + [s["y_ssm"], s["y_attn"], s["y_conv"]], s["bg"], [[dmerged]], [bf16] * 6, True,
                       128, f"merge_bwd{tag}")
    dpg, (dys, dya, dyv), dbg = list(res[0:3]), res[3:6], res[6:9]
    gp["b_gate"] = jnp.concatenate(dbg, axis=1)
    ds_ssm = _apply_wt(dys, w, "w_ssm_out", f32, f"ssm_proj_dx{tag}")
    gw["w_ssm_out"] = _weight_grad(s["s_ssm"], dys, "w_ssm_out", f"ssm_proj_dw{tag}")
    dattn_o = _apply_wt(dya, w, "w_attn_out", f32, f"attn_proj_dx{tag}")
    gw["w_attn_out"] = _weight_grad(s["attn_o"], dya, "w_attn_out", f"attn_proj_dw{tag}")
    dcm = _apply_wt(dyv, w, "w_conv_out", f32, f"conv_proj_dx{tag}")
    gw["w_conv_out"] = _weight_grad(s["cm"], dyv, "w_conv_out", f"conv_proj_dw{tag}")
    ssm = s["ssm"]
    tc, n_groups = ssm["tc"], ssm["n_groups"]
    dg_a, dz, gp["b_glu"] = _rowwise_bwd(_f_glu, [s["g"], s["z"]], [vec(p["b_glu"])], [[ds_ssm]], [f32, bf16], True,
                                         256, f"glu_bwd{tag}")
    dg_b = _apply_wt(dz, w, "w_glu", f32, f"glu_proj_dx{tag}")
    gw["w_glu"] = _weight_grad(s["g"], dz, "w_glu", f"glu_proj_dw{tag}")
    dyc, du_skip, gp["ssm_d"] = _rowwise_bwd(_f_gelu, [s["yc"], s["u32"]], [vec(p["ssm_d"])], [[dg_a, dg_b]], [bf16, f32], True,
                                             256, f"ssm_gelu_bwd{tag}")
    dh_state = _mm(dyc, ssm["cc"], f32, f"ssm_out_dx{tag}")
    g_cc = _bd_fold(_mm(dyc, s["hb"], f32, f"ssm_out_dw{tag}", lhs_t=True), n_groups, tc, f"ssm_out_dw_fold{tag}")
    gf, gb = _scan(dh_state, ssm["lam"], True, f"ssm_scan_bwd{tag}")
    d_lam = _lam_grad(gf, s["hf"], f"ssm_lam_grad{tag}")
    du_x = _mm(gb, ssm["bb"], f32, f"ssm_in_dx{tag}", rhs_t=True)
    g_bb = _bd_fold(_mm(s["u"], gb, f32, f"ssm_in_dw{tag}", lhs_t=True), n_groups, tc, f"ssm_in_dw_fold{tag}")
    (du,) = _rowwise(_f_add, [du_skip, du_x], [], [bf16], 256, f"ssm_du{tag}")
    d_lr, d_li = _deinterleave(d_lam, tc)
    g_bbr, g_bbi = _deinterleave(g_bb, tc)
    group_shape = (SSM_GROUP, n_groups, SSM_STATE)
    cots = (d_lr.reshape(n_groups, SSM_STATE), d_li.reshape(n_groups, SSM_STATE), g_bbr.reshape(group_shape), g_bbi.reshape(group_shape))
    d_ldt, d_ar, d_ai, d_br, d_bi = _disc_bwd(*ssm["disc_in"], cots, f"ssm_disc_bwd{tag}")
    gp["ssm_log_dt"], gp["ssm_a_re"], gp["ssm_a_im"] = d_ldt.reshape(-1), d_ar, d_ai
    gp["ssm_b_re"], gp["ssm_b_im"] = d_br.transpose(1, 2, 0), d_bi.transpose(1, 2, 0)
    g_ccr, g_cci = _deinterleave(g_cc, tc)
    gp["ssm_c_re"] = g_ccr.reshape(group_shape).transpose(1, 0, 2)
    gp["ssm_c_im"] = -g_cci.reshape(group_shape).transpose(1, 0, 2)
    res = _rowwise_bwd(_f_amerge, s["o_tok"] + s["lse_x"], [], [[dattn_o]], [f32] * 6, False, 256, f"attn_merge_bwd{tag}")
    do_b = _to_blocks(jnp.concatenate(res[0:3], axis=1), hpp)
    dl_b = _to_blocks(jnp.concatenate(res[3:6], axis=1), hpp)
    dqb, dkb, dvb = _attn_bwd(s["qb"], s["kb"], s["vb"], s["ob"], s["lseb"], do_b, dl_b, hpp, f"attn_bwd{tag}")
    dq, dk, dv = (jnp.concatenate(_from_blocks(t, hpp), axis=1) for t in (dqb, dkb, dvb))
    dcb, dcc, dch, gp["conv_mix_w"] = _convgate_bwd(s["conv"], p["conv_mix_w"], dcm, 0, w4, f"convgate_bwd{tag}")
    dproj = jnp.concatenate([du, dq, dk, dv, dcb, dcc, dch] + dpg, axis=1)
    dh = cargo.hosted(lambda carry: _apply_wt(dproj, w, "w_in", f32, f"in_dx{tag}", carry), [("w_in", 1, 2)])
    gw["w_in"] = cargo.hosted(lambda carry: _weight_grad(s["h"], dproj, "w_in", f"in_dw{tag}", carry), [("w_up", 1, 2)])
    dx0, d_g, d_sc1, d_sh1 = _rowwise_bwd(_f_prenorm, [x], [vec(p["g_pre_mix"]), sc1, sh1], [[dh]], [f32], True,
                                          256, f"prenorm_mix_bwd{tag}", add_to={0: dx1})
    gp["g_pre_mix"] = d_g
    return dx0, [d_sh1, d_sc1, d_gt1, d_sh2, d_sc2, d_gt2], gw, gp


COL_SHARDED = ("w_in", "w_up", "w_ssm_out", "w_attn_out", "w_conv_out")
ROW_SHARDED = ("w_down", "w_o", "w_glu")
SMALL_SHARDED = ("conv_mix_w", "ffn_conv_w")
LIGHT_SHARDED = ("w_ssm_out", "w_attn_out", "w_conv_out", "w_glu") + SMALL_SHARDED
REPLICATED = ("b_mod", "g_pre_mix", "g_post_mix", "g_pre_ffn", "g_post_ffn", "ssm_log_dt", "ssm_a_re", "ssm_a_im", "ssm_b_re",
              "ssm_b_im", "ssm_c_re", "ssm_c_im", "ssm_d", "b_glu", "b_gate")
WEIGHTS = ('w_mod', 'b_mod', 'g_pre_mix', 'g_post_mix', 'g_pre_ffn', 'g_post_ffn', 'w_in', 'ssm_log_dt', 'ssm_a_re', 'ssm_a_im',
           'ssm_b_re', 'ssm_b_im', 'ssm_c_re', 'ssm_c_im', 'ssm_d', 'w_glu', 'b_glu', 'conv_mix_w', 'w_ssm_out', 'w_attn_out',
           'w_conv_out', 'b_gate', 'w_o', 'w_up', 'ffn_conv_w', 'w_down')


def _full_cols(g):
    return g.transpose(1, 0, 2).reshape(g.shape[1], -1)


def _col_slabs(m):
    return m.reshape(m.shape[0], N_DEV, -1).transpose(1, 0, 2)


def _step(x, c, loss_target, wts, m_in, v_in):
    depth = wts["w_in"].shape[0]
    n_rows, d = x.shape[1], x.shape[2]
    me = _linear(_position())
    x2d, target2d = x[0], loss_target[0]

    (c_all,) = _all_gather([c], "gather_c")
    c_all = c_all.reshape(N_DEV, d)
    mod_cols = wts["w_mod"].shape[2]
    b_cols = lax.dynamic_slice_in_dim(wts["b_mod"], me * mod_cols, mod_cols, axis=1).reshape(depth, 1, mod_cols)
    mod_mine = _mod_fwd(c_all, wts["w_mod"], b_cols, "mod_fwd")
    (mod_all,) = _all_gather([mod_mine], "gather_mod")
    mod_me = lax.dynamic_index_in_dim(mod_all, me, axis=2, keepdims=False).transpose(1, 0, 2).reshape(depth, 6, 1, d)

    mats = []
    for l in range(depth):
        names = COL_SHARDED + ROW_SHARDED
        shards = [(wts[k][l].T if k in COL_SHARDED else wts[k][l]).astype(bf16) for k in names]
        got = _all_gather_relayed(shards, f"gather_w{l}")
        got = [lax.dynamic_update_index_in_dim(g, shard, me, axis=0) for g, shard in zip(got, shards)]
        mats.append({k: g.reshape(-1, g.shape[2]) for k, g in zip(names, got)})
    (conv_mix_all, ffn_conv_all) = _all_gather([wts["conv_mix_w"], wts["ffn_conv_w"]], "gather_conv_w")
    small = []
    for l in range(depth):
        p = {k: wts[k][l] for k in REPLICATED if k != "b_mod"}
        p["conv_mix_w"] = _full_cols(conv_mix_all[:, l])
        p["ffn_conv_w"] = _full_cols(ffn_conv_all[:, l])
        small.append(p)

    saved = []
    h = x2d
    for l in range(depth):
        mod = [mod_me[l, i] for i in range(6)]
        h, s = _layer_fwd(h, mod, mats[l], small[l], f"_l{l}")
        saved.append(s)
    dy, loss_part = _loss_head(h, target2d, "loss_head")
    loss = lax.psum(loss_part[0, 0], ("x", "y", "c"))

    grads = {k: [None] * depth for k in WEIGHTS}
    dmod_rows = [None] * depth
    small_parts = [None] * depth
    sharded = COL_SHARDED + ROW_SHARDED + SMALL_SHARDED
    waiting = None
    for l in reversed(range(depth)):
        mod = [mod_me[l, i] for i in range(6)]
        cargo = _Cargo(waiting)
        dy, dmod, gw, gp = _layer_bwd(dy, saved[l], mod, mats[l], small[l], f"_l{l}", cargo)
        if waiting is not None:
            summed = _total_sums([waiting[k] for k in sharded], [cargo.arrived[k] for k in sharded], f"scatter_g{l + 1}")
            for k, g in zip(sharded, summed):
                grads[k][l + 1] = g
        dmod_rows[l] = jnp.concatenate(dmod, axis=1)
        slabs = [gw[k].reshape(N_DEV, -1, gw[k].shape[1]) for k in COL_SHARDED + ROW_SHARDED]
        slabs += [_col_slabs(gp[k]) for k in SMALL_SHARDED]
        waiting = dict(zip(sharded, _chip_sums(slabs, f"scatter_g{l}")))
        small_parts[l] = gp
    last = [waiting[k] for k in sharded]
    for k, g in zip(sharded, _total_sums(last, _chip_exchange(last, "scatter_g0_chips"), "scatter_g0")):
        grads[k][0] = g
    grad_x = dy.reshape(x.shape)

    dmod_mine = jnp.concatenate(dmod_rows, axis=0)
    rep_names = [k for k in REPLICATED if k != "b_mod"]
    pieces = [dmod_mine.reshape(-1)] + [jnp.stack([small_parts[l][k].reshape(-1) for l in range(depth)]).reshape(-1) for k in rep_names]
    sizes = [int(pc.shape[0]) for pc in pieces]
    total = sum(sizes)
    padded = -(-total // (SUBLANES * LANES)) * (SUBLANES * LANES)
    pack = jnp.concatenate(pieces + [jnp.zeros((padded - total,), f32)]).reshape(-1, LANES)
    (pack_all,) = _all_gather([pack], "gather_small_grads")
    pack_sum = _sum_slabs(pack_all, "sum_small_grads").reshape(-1)
    offs = np.cumsum([0] + sizes)
    grads["b_mod"] = pack_sum[offs[0]:offs[1]].reshape(wts["b_mod"].shape)
    for i, k in enumerate(rep_names):
        grads[k] = pack_sum[offs[i + 1]:offs[i + 2]].reshape(wts[k].shape)
    dmod_all = pack_all.reshape(N_DEV, -1)[:, :sizes[0]].reshape(N_DEV, depth, 6 * d)
    dmod_cols = lax.dynamic_slice_in_dim(dmod_all, me * mod_cols, mod_cols, axis=2).transpose(1, 0, 2)
    grads["w_mod"] = _mod_wgrad(c_all.T, dmod_cols, "mod_wgrad")
    for k in COL_SHARDED + ROW_SHARDED + SMALL_SHARDED:
        grads[k] = jnp.stack([g.T for g in grads[k]] if k in COL_SHARDED else grads[k])

    delta, new_m, new_v = {}, {}, {}
    for k in WEIGHTS:
        delta[k], new_m[k], new_v[k] = _adamw(wts[k], grads[k], m_in[k], v_in[k], f"adamw_{k}")
    return (loss, grad_x, *[grads[k] for k in WEIGHTS], *[delta[k] for k in WEIGHTS], *[new_m[k] for k in WEIGHTS],
            *[new_v[k] for k in WEIGHTS])


def kernel(x, c, w_mod, b_mod, g_pre_mix, g_post_mix, g_pre_ffn, g_post_ffn, w_in, ssm_log_dt, ssm_a_re, ssm_a_im, ssm_b_re, ssm_b_im, ssm_c_re, ssm_c_im, ssm_d, w_glu, b_glu, conv_mix_w, w_ssm_out, w_attn_out, w_conv_out, b_gate, w_o, w_up, ffn_conv_w, w_down, loss_target, m_w_mod, m_b_mod, m_g_pre_mix, m_g_post_mix, m_g_pre_ffn, m_g_post_ffn, m_w_in, m_ssm_log_dt, m_ssm_a_re, m_ssm_a_im, m_ssm_b_re, m_ssm_b_im, m_ssm_c_re, m_ssm_c_im, m_ssm_d, m_w_glu, m_b_glu, m_conv_mix_w, m_w_ssm_out, m_w_attn_out, m_w_conv_out, m_b_gate, m_w_o, m_w_up, m_ffn_conv_w, m_w_down, v_w_mod, v_b_mod, v_g_pre_mix, v_g_post_mix, v_g_pre_ffn, v_g_post_ffn, v_w_in, v_ssm_log_dt, v_ssm_a_re, v_ssm_a_im, v_ssm_b_re, v_ssm_b_im, v_ssm_c_re, v_ssm_c_im, v_ssm_d, v_w_glu, v_b_glu, v_conv_mix_w, v_w_ssm_out, v_w_attn_out, v_w_conv_out, v_b_gate, v_w_o, v_w_up, v_ffn_conv_w, v_w_down):
    wts = dict(w_mod=w_mod, b_mod=b_mod, g_pre_mix=g_pre_mix, g_post_mix=g_post_mix, g_pre_ffn=g_pre_ffn, g_post_ffn=g_post_ffn, w_in=w_in, ssm_log_dt=ssm_log_dt, ssm_a_re=ssm_a_re, ssm_a_im=ssm_a_im, ssm_b_re=ssm_b_re, ssm_b_im=ssm_b_im, ssm_c_re=ssm_c_re, ssm_c_im=ssm_c_im, ssm_d=ssm_d, w_glu=w_glu, b_glu=b_glu, conv_mix_w=conv_mix_w, w_ssm_out=w_ssm_out, w_attn_out=w_attn_out, w_conv_out=w_conv_out, b_gate=b_gate, w_o=w_o, w_up=w_up, ffn_conv_w=ffn_conv_w, w_down=w_down)
    m_in = dict(w_mod=m_w_mod, b_mod=m_b_mod, g_pre_mix=m_g_pre_mix, g_post_mix=m_g_post_mix, g_pre_ffn=m_g_pre_ffn, g_post_ffn=m_g_post_ffn, w_in=m_w_in, ssm_log_dt=m_ssm_log_dt, ssm_a_re=m_ssm_a_re, ssm_a_im=m_ssm_a_im, ssm_b_re=m_ssm_b_re, ssm_b_im=m_ssm_b_im, ssm_c_re=m_ssm_c_re, ssm_c_im=m_ssm_c_im, ssm_d=m_ssm_d, w_glu=m_w_glu, b_glu=m_b_glu, conv_mix_w=m_conv_mix_w, w_ssm_out=m_w_ssm_out, w_attn_out=m_w_attn_out, w_conv_out=m_w_conv_out, b_gate=m_b_gate, w_o=m_w_o, w_up=m_w_up, ffn_conv_w=m_ffn_conv_w, w_down=m_w_down)
    v_in = dict(w_mod=v_w_mod, b_mod=v_b_mod, g_pre_mix=v_g_pre_mix, g_post_mix=v_g_post_mix, g_pre_ffn=v_g_pre_ffn, g_post_ffn=v_g_post_ffn, w_in=v_w_in, ssm_log_dt=v_ssm_log_dt, ssm_a_re=v_ssm_a_re, ssm_a_im=v_ssm_a_im, ssm_b_re=v_ssm_b_re, ssm_b_im=v_ssm_b_im, ssm_c_re=v_ssm_c_re, ssm_c_im=v_ssm_c_im, ssm_d=v_ssm_d, w_glu=v_w_glu, b_glu=v_b_glu, conv_mix_w=v_conv_mix_w, w_ssm_out=v_w_ssm_out, w_attn_out=v_w_attn_out, w_conv_out=v_w_conv_out, b_gate=v_b_gate, w_o=v_w_o, w_up=v_w_up, ffn_conv_w=v_ffn_conv_w, w_down=v_w_down)
    return _step(x, c, loss_target, wts, m_in, v_in)
```

```python
import functools
import math

import numpy as np
import jax
import jax.numpy as jnp
from jax import lax
from jax.experimental import pallas as pl
from jax.experimental.pallas import tpu as pltpu

f32 = jnp.float32
bf16 = jnp.bfloat16

N_DEV = 8
N_CHIPS = 4
V7X_VMEM_LIMIT_BYTES = 56 * 1024 * 1024
LANES = 128
SUBLANES = 8
BF16_ROWS = 16

RMS_EPS = 1e-6
NEG_INF = -1e30
HEAD_DIM = 64
SSM_GROUP = 16
SSM_STATE = 64
DSWA_PATTERNS = ((128, 1), (512, 4), (2048, 16))
QBLK = 128

ADAM_LR = 0.001
ADAM_B1 = 0.9
ADAM_B2 = 0.999
ADAM_EPS = 1e-08
ADAM_WD = 0.01
ADAM_STEP = 10

MESH = pl.DeviceIdType.MESH
ANY = pl.BlockSpec(memory_space=pl.ANY)


def _tile(n, pref, mult=LANES):
    t = min(pref, n) // mult * mult
    while t >= mult:
        if n % t == 0:
            return t
        t -= mult
    return n


def _params(*sem):
    return pltpu.CompilerParams(dimension_semantics=sem, vmem_limit_bytes=V7X_VMEM_LIMIT_BYTES)


def _mm(a, b, out_dtype, name, rhs_t=False, lhs_t=False, carry=()):
    k, m = a.shape if lhs_t else a.shape[::-1]
    n, k2 = b.shape if rhs_t else b.shape[::-1]
    assert k == k2 and a.dtype == bf16 and b.dtype == bf16, (a.shape, b.shape, a.dtype, b.dtype)
    tm, tn, tk = _tile(m, 1024, LANES if lhs_t else SUBLANES), _tile(n, 512), _tile(k, 2048)
    grid = (m // tm, n // tn, k // tk)
    contract = (((0 if lhs_t else 1,), (1 if rhs_t else 0,)), ((), ()))
    nc = len(carry)
    srcs = [job[0] for job in carry]
    ranges = [(job[1], job[2]) for job in carry]
    intos = [(i, job[3]) for i, job in enumerate(carry) if job[3] is not None]
    n_in = 2 + nc + len(intos)

    def body(*refs):
        a_ref, b_ref = refs[:2]
        o_ref = refs[n_in]
        acc_ref = refs[n_in + 1 + nc]
        steps = [pl.program_id(axis) for axis in range(3)]
        if nc:
            copies = _chip_copies(refs[2:2 + nc], refs[n_in + 1:n_in + 1 + nc], refs[n_in + 2 + nc], refs[n_in + 3 + nc], ranges)

            @pl.when((steps[0] == 0) & (steps[1] == 0) & (steps[2] == 0))
            def _():
                for cp in copies:
                    cp.start()

        @pl.when(steps[2] == 0)
        def _():
            acc_ref[...] = jnp.zeros_like(acc_ref)

        acc_ref[...] += lax.dot_general(a_ref[...], b_ref[...], contract, preferred_element_type=f32)

        @pl.when(steps[2] == grid[2] - 1)
        def _():
            o_ref[...] = acc_ref[...].astype(o_ref.dtype)

        if nc:
            @pl.when((steps[0] == grid[0] - 1) & (steps[1] == grid[1] - 1) & (steps[2] == grid[2] - 1))
            def _():
                for cp in copies:
                    cp.wait_recv()
                for cp in copies:
                    cp.wait_send()

    a_spec = pl.BlockSpec((tk, tm), lambda i, j, kk: (kk, i)) if lhs_t else pl.BlockSpec((tm, tk), lambda i, j, kk: (i, kk))
    b_spec = pl.BlockSpec((tn, tk), lambda i, j, kk: (j, kk)) if rhs_t else pl.BlockSpec((tk, tn), lambda i, j, kk: (kk, j))
    sems = pltpu.SemaphoreType.DMA((max(nc, 1), N_CHIPS - 1))
    res = pl.pallas_call(
        body, grid=grid,
        in_specs=[a_spec, b_spec] + [ANY] * (nc + len(intos)),
        out_specs=[pl.BlockSpec((tm, tn), lambda i, j, kk: (i, j))] + [ANY] * nc,
        out_shape=[jax.ShapeDtypeStruct((m, n), out_dtype)]
        + [jax.ShapeDtypeStruct((N_CHIPS - 1,) + x.shape[1:], x.dtype) for x in srcs],
        scratch_shapes=[pltpu.VMEM((tm, tn), f32)] + ([sems, sems] if nc else []),
        input_output_aliases={2 + nc + j: 1 + i for j, (i, _) in enumerate(intos)},
        compiler_params=_params(*(("arbitrary",) * 3 if nc else ("parallel", "parallel", "arbitrary"))), name=name)(
            a, b, *srcs, *[buf for _, buf in intos])
    return (res[0], list(res[1:])) if nc else res[0]


def _mm_sections(a, b_t, sections, name):
    m, k = a.shape
    n = b_t.shape[0]
    assert b_t.shape[1] == k and sum(w for w, _ in sections) == n
    tn = _tile(math.gcd(*[w for w, _ in sections]), 512)
    tm, tk = _tile(m, 1024, SUBLANES), _tile(k, 2048)
    grid = (m // tm, n // tn, k // tk)
    bounds, lo = [], 0
    for w, _ in sections:
        bounds.append((lo, lo + w // tn))
        lo += w // tn

    def body(a_ref, b_ref, *rest):
        o_refs, acc_ref = rest[:-1], rest[-1]
        j, kk = pl.program_id(1), pl.program_id(2)

        @pl.when(kk == 0)
        def _():
            acc_ref[...] = jnp.zeros_like(acc_ref)

        acc_ref[...] += lax.dot_general(a_ref[...], b_ref[...], (((1,), (1,)), ((), ())), preferred_element_type=f32)
        for o_ref, (first, end) in zip(o_refs, bounds):
            @pl.when((kk == grid[2] - 1) & (j >= first) & (j < end))
            def _(o_ref=o_ref):
                o_ref[...] = acc_ref[...].astype(o_ref.dtype)

    def out_spec(first, end):
        return pl.BlockSpec((tm, tn), lambda i, j, kk: (i, jnp.clip(j - first, 0, end - first - 1)))

    return pl.pallas_call(
        body, grid=grid,
        in_specs=[pl.BlockSpec((tm, tk), lambda i, j, kk: (i, kk)), pl.BlockSpec((tn, tk), lambda i, j, kk: (j, kk))],
        out_specs=[out_spec(first, end) for first, end in bounds],
        out_shape=[jax.ShapeDtypeStruct((m, w), dt) for w, dt in sections],
        scratch_shapes=[pltpu.VMEM((tm, tn), f32)],
        compiler_params=_params("arbitrary", "arbitrary", "arbitrary"), name=name)(a, b_t)


def _rowwise(fn, rows, vecs, out_dtypes, tm, name):
    n_rows = rows[0].shape[0]
    nr, nv = len(rows), len(vecs)
    outs = jax.eval_shape(fn, *[jax.ShapeDtypeStruct((tm, r.shape[1]), f32) for r in rows],
                          *[jax.ShapeDtypeStruct(v.shape, f32) for v in vecs])

    def body(*refs):
        vals = [r[...].astype(f32) for r in refs[:nr]] + [v[...] for v in refs[nr:nr + nv]]
        for o_ref, val in zip(refs[nr + nv:], fn(*vals)):
            o_ref[...] = val.astype(o_ref.dtype)

    res = pl.pallas_call(
        body, grid=(n_rows // tm,),
        in_specs=[pl.BlockSpec((tm, r.shape[1]), lambda i: (i, 0)) for r in rows]
        + [pl.BlockSpec(v.shape, lambda i: (0, 0)) for v in vecs],
        out_specs=[pl.BlockSpec((tm, o.shape[1]), lambda i: (i, 0)) for o in outs],
        out_shape=[jax.ShapeDtypeStruct((n_rows, o.shape[1]), dt) for o, dt in zip(outs, out_dtypes)],
        compiler_params=_params("parallel"), name=name)(*rows, *vecs)
    return res


def _rowwise_bwd(fn, rows, vecs, cots, row_dtypes, need_vecs, tm, name, add_to=None):
    n_rows = rows[0].shape[0]
    nr, nv = len(rows), len(vecs)
    add_to = add_to or {}
    flat_cots = [c for group in cots for c in group]
    add_keys = sorted(add_to)
    row_out = [i for i, dt in enumerate(row_dtypes) if dt is not None]
    n_in = nr + nv + len(flat_cots) + len(add_keys)

    def body(*refs):
        vals = [r[...].astype(f32) for r in refs[:nr]] + [v[...] for v in refs[nr:nr + nv]]
        pos = nr + nv
        cvals = []
        for group in cots:
            acc = refs[pos][...].astype(f32)
            for extra in range(1, len(group)):
                acc = acc + refs[pos + extra][...].astype(f32)
            pos += len(group)
            cvals.append(acc)
        adds = {key: refs[pos + j] for j, key in enumerate(add_keys)}
        _, vjp = jax.vjp(fn, *vals)
        grads = vjp(tuple(cvals))
        out_refs = refs[n_in:]
        for j, i in enumerate(row_out):
            g = grads[i]
            if i in adds:
                g = g + adds[i][...].astype(f32)
            out_refs[j][...] = g.astype(out_refs[j].dtype)
        if need_vecs:
            step = pl.program_id(0)
            for j in range(nv):
                acc_ref = out_refs[len(row_out) + j]

                @pl.when(step == 0)
                def _(acc_ref=acc_ref):
                    acc_ref[...] = jnp.zeros_like(acc_ref)

                acc_ref[...] += grads[nr + j]

    row_spec = lambda a: pl.BlockSpec((tm, a.shape[1]), lambda i: (i, 0))
    in_specs = ([row_spec(r) for r in rows] + [pl.BlockSpec(v.shape, lambda i: (0, 0)) for v in vecs]
                + [row_spec(c) for c in flat_cots] + [row_spec(add_to[key]) for key in add_keys])
    out_specs = [row_spec(rows[i]) for i in row_out]
    out_shape = [jax.ShapeDtypeStruct(rows[i].shape, row_dtypes[i]) for i in row_out]
    if need_vecs:
        out_specs += [pl.BlockSpec(v.shape, lambda i: (0, 0)) for v in vecs]
        out_shape += [jax.ShapeDtypeStruct(v.shape, f32) for v in vecs]
    return pl.pallas_call(
        body, grid=(n_rows // tm,), in_specs=in_specs, out_specs=out_specs, out_shape=out_shape,
        compiler_params=_params("arbitrary"), name=name)(*rows, *vecs, *flat_cots, *[add_to[key] for key in add_keys])


def _f_prenorm(x, g, sc, sh):
    y = x * lax.rsqrt(jnp.mean(x * x, axis=-1, keepdims=True) + RMS_EPS)
    return (y * g * (1.0 + sc) + sh,)


def _f_postres(x, y, g, gt):
    n = y * lax.rsqrt(jnp.mean(y * y, axis=-1, keepdims=True) + RMS_EPS)
    return (x + gt * (n * g),)


def _f_gelu(yc, u, d):
    return (jax.nn.gelu(yc + d * u),)


def _f_glu(g, z, b):
    return (g * jax.nn.sigmoid(z + b),)


def _f_merge(p0, p1, p2, ys, ya, yc, b0, b1, b2):
    return (jax.nn.sigmoid(p0 + b0) * ys + jax.nn.sigmoid(p1 + b1) * ya + jax.nn.sigmoid(p2 + b2) * yc,)


def _f_amerge(o0, o1, o2, l0, l1, l2):
    m = jnp.maximum(jnp.maximum(l0, l1), l2)
    e0, e1, e2 = jnp.exp(l0 - m), jnp.exp(l1 - m), jnp.exp(l2 - m)
    return ((e0 * o0 + e1 * o1 + e2 * o2) / (e0 + e1 + e2),)


def _f_add(a, b):
    return (a + b,)


def _shift_down(z, k, row):
    return jnp.where(row >= k, pltpu.roll(z, k, 0), 0.0)


def _shift_up(z, k, row):
    n = z.shape[0]
    return jnp.where(row < n - k, pltpu.roll(z, n - k, 0), 0.0)


def _conv3(z, w, row):
    return w[0:1, :] * z + w[1:2, :] * _shift_down(z, 1, row) + w[2:3, :] * _shift_down(z, 2, row)


def _conv3_bwd(z, w, dy, row):
    dz = w[0:1, :] * dy + w[1:2, :] * _shift_up(dy, 1, row) + w[2:3, :] * _shift_up(dy, 2, row)
    dw = [jnp.sum(dy * z, axis=0, keepdims=True),
          jnp.sum(dy * _shift_down(z, 1, row), axis=0, keepdims=True),
          jnp.sum(dy * _shift_down(z, 2, row), axis=0, keepdims=True)]
    return dz, dw


def _convgate_fwd(proj, w, off, width, name):
    n_rows = proj.shape[0]
    tc = _tile(width, 256)
    nb = width // tc
    base = off // tc

    def body(b_ref, c_ref, h_ref, w_ref, o_ref):
        row = lax.broadcasted_iota(jnp.int32, (n_rows, tc), 0)
        o_ref[...] = (b_ref[...] * _conv3(c_ref[...] * h_ref[...], w_ref[...], row)).astype(o_ref.dtype)

    col = lambda s: pl.BlockSpec((n_rows, tc), lambda j: (0, base + s * nb + j))
    return pl.pallas_call(
        body, grid=(nb,), in_specs=[col(0), col(1), col(2), pl.BlockSpec((3, tc), lambda j: (0, j))],
        out_specs=pl.BlockSpec((n_rows, tc), lambda j: (0, j)),
        out_shape=jax.ShapeDtypeStruct((n_rows, width), bf16),
        compiler_params=_params("parallel"), name=name)(proj, proj, proj, w)


def _convgate_bwd(proj, w, dcm, off, width, name):
    n_rows = proj.shape[0]
    tc = _tile(width, 128)
    nb = width // tc
    base = off // tc

    def body(b_ref, c_ref, h_ref, w_ref, d_ref, db_ref, dc_ref, dh_ref, dw_ref):
        row = lax.broadcasted_iota(jnp.int32, (n_rows, tc), 0)
        cb, cc, ch, wv, d = b_ref[...], c_ref[...], h_ref[...], w_ref[...], d_ref[...].astype(f32)
        z = cc * ch
        db_ref[...] = (d * _conv3(z, wv, row)).astype(db_ref.dtype)
        dz, dw = _conv3_bwd(z, wv, d * cb, row)
        dc_ref[...] = (dz * ch).astype(dc_ref.dtype)
        dh_ref[...] = (dz * cc).astype(dh_ref.dtype)
        for k in range(3):
            dw_ref[k:k + 1, :] = dw[k]

    col = lambda s: pl.BlockSpec((n_rows, tc), lambda j: (0, base + s * nb + j))
    own = pl.BlockSpec((n_rows, tc), lambda j: (0, j))
    wspec = pl.BlockSpec((3, tc), lambda j: (0, j))
    sec = jax.ShapeDtypeStruct((n_rows, width), bf16)
    return pl.pallas_call(
        body, grid=(nb,), in_specs=[col(0), col(1), col(2), wspec, own], out_specs=[own, own, own, wspec],
        out_shape=[sec, sec, sec, jax.ShapeDtypeStruct((3, width), f32)],
        compiler_params=_params("parallel"), name=name)(proj, proj, proj, w, dcm)


def _ffnconv_fwd(up, w, name):
    n_rows, two_f = up.shape
    half = two_f // 2
    tc = _tile(half, 256)
    nb = half // tc

    def body(a_ref, b_ref, wa_ref, wb_ref, o_ref):
        row = lax.broadcasted_iota(jnp.int32, (n_rows, tc), 0)
        ua = _conv3(a_ref[...], wa_ref[...], row)
        ub = _conv3(b_ref[...], wb_ref[...], row)
        o_ref[...] = (jax.nn.silu(ua) * ub).astype(o_ref.dtype)

    return pl.pallas_call(
        body, grid=(nb,),
        in_specs=[pl.BlockSpec((n_rows, tc), lambda j: (0, j)), pl.BlockSpec((n_rows, tc), lambda j: (0, nb + j)),
                  pl.BlockSpec((3, tc), lambda j: (0, j)), pl.BlockSpec((3, tc), lambda j: (0, nb + j))],
        out_specs=pl.BlockSpec((n_rows, tc), lambda j: (0, j)),
        out_shape=jax.ShapeDtypeStruct((n_rows, half), bf16),
        compiler_params=_params("parallel"), name=name)(up, up, w, w)


def _ffnconv_bwd(up, w, dact, name):
    n_rows, two_f = up.shape
    half = two_f // 2
    tc = _tile(half, 128)
    nb = half // tc

    def body(a_ref, b_ref, wa_ref, wb_ref, d_ref, da_ref, db_ref, dwa_ref, dwb_ref):
        row = lax.broadcasted_iota(jnp.int32, (n_rows, tc), 0)
        a, b, wa, wb, d = a_ref[...], b_ref[...], wa_ref[...], wb_ref[...], d_ref[...].astype(f32)
        ua = _conv3(a, wa, row)
        ub = _conv3(b, wb, row)
        sg = jax.nn.sigmoid(ua)
        d_ua = d * ub * (sg * (1.0 + ua * (1.0 - sg)))
        d_ub = d * (ua * sg)
        da, dwa = _conv3_bwd(a, wa, d_ua, row)
        db, dwb = _conv3_bwd(b, wb, d_ub, row)
        da_ref[...] = da.astype(da_ref.dtype)
        db_ref[...] = db.astype(db_ref.dtype)
        for k in range(3):
            dwa_ref[k:k + 1, :] = dwa[k]
            dwb_ref[k:k + 1, :] = dwb[k]

    lo = pl.BlockSpec((n_rows, tc), lambda j: (0, j))
    hi = pl.BlockSpec((n_rows, tc), lambda j: (0, nb + j))
    wlo = pl.BlockSpec((3, tc), lambda j: (0, j))
    whi = pl.BlockSpec((3, tc), lambda j: (0, nb + j))
    sec = jax.ShapeDtypeStruct((n_rows, half), bf16)
    wsec = jax.ShapeDtypeStruct((3, half), f32)
    da, db, dwa, dwb = pl.pallas_call(
        body, grid=(nb,), in_specs=[lo, hi, wlo, whi, lo], out_specs=[lo, lo, wlo, wlo],
        out_shape=[sec, sec, wsec, wsec], compiler_params=_params("parallel"), name=name)(up, up, w, w, dact)
    return da, db, jnp.concatenate([dwa, dwb], axis=1)


SCAN_TC = 256


def _cmul(ar, ai, br, bi):
    return ar * br - ai * bi, ar * bi + ai * br


def _scan(x, lam, adjoint, name):
    n_rows, two_s = x.shape
    tc = min(SCAN_TC, two_s // 2)
    n_tiles = two_s // (2 * tc)
    n_chunks = n_rows // SUBLANES

    def body(x_ref, lam_ref, hf_ref, hb_ref):
        lr = lam_ref[:, 0:tc]
        li = lam_ref[:, tc:2 * tc]
        if adjoint:
            li = -li
        row = lax.broadcasted_iota(jnp.int32, (SUBLANES, tc), 0)
        powers = [(lr, li)]
        for _ in range(SUBLANES - 1):
            powers.append(_cmul(powers[-1][0], powers[-1][1], lr, li))
        pr = jnp.zeros((SUBLANES, tc), f32)
        pi = jnp.zeros((SUBLANES, tc), f32)
        for t in range(SUBLANES):
            e = (SUBLANES - 1 - t) if adjoint else t
            pr = jnp.where(row == t, powers[e][0], pr)
            pi = jnp.where(row == t, powers[e][1], pi)
        last = 0 if adjoint else SUBLANES - 1

        def chunk(i, carry):
            cr, ci = carry
            c = (n_chunks - 1 - i) if adjoint else i
            rows = pl.ds(pl.multiple_of(c * SUBLANES, SUBLANES), SUBLANES)
            xr = x_ref[rows, 0:tc]
            xi = x_ref[rows, tc:2 * tc]
            for k in (1, 2, 4):
                ar, ai = powers[k - 1]
                if adjoint:
                    sr = jnp.where(row < SUBLANES - k, pltpu.roll(xr, SUBLANES - k, 0), 0.0)
                    si = jnp.where(row < SUBLANES - k, pltpu.roll(xi, SUBLANES - k, 0), 0.0)
                else:
                    sr = jnp.where(row >= k, pltpu.roll(xr, k, 0), 0.0)
                    si = jnp.where(row >= k, pltpu.roll(xi, k, 0), 0.0)
                xr, xi = xr + ar * sr - ai * si, xi + ar * si + ai * sr
            hr = xr + pr * cr - pi * ci
            hi = xi + pr * ci + pi * cr
            hf_ref[rows, 0:tc] = hr
            hf_ref[rows, tc:2 * tc] = hi
            hb_ref[rows, 0:tc] = hr.astype(bf16)
            hb_ref[rows, tc:2 * tc] = hi.astype(bf16)
            return (jnp.sum(jnp.where(row == last, hr, 0.0), axis=0, keepdims=True),
                    jnp.sum(jnp.where(row == last, hi, 0.0), axis=0, keepdims=True))

        zero = jnp.zeros((1, tc), f32)
        lax.fori_loop(0, n_chunks, chunk, (zero, zero))

    blk = pl.BlockSpec((n_rows, 2 * tc), lambda j: (0, j))
    return pl.pallas_call(
        body, grid=(n_tiles,), in_specs=[blk, pl.BlockSpec((1, 2 * tc), lambda j: (0, j))], out_specs=[blk, blk],
        out_shape=[jax.ShapeDtypeStruct(x.shape, f32), jax.ShapeDtypeStruct(x.shape, bf16)],
        compiler_params=_params("parallel"), name=name)(x, lam)


def _lam_grad(g, h, name):
    n_rows, two_s = g.shape
    tc = min(SCAN_TC, two_s // 2)

    def body(g_ref, h_ref, o_ref):
        row = lax.broadcasted_iota(jnp.int32, (n_rows, tc), 0)
        gr, gi = g_ref[:, 0:tc], g_ref[:, tc:2 * tc]
        hr, hi = _shift_down(h_ref[:, 0:tc], 1, row), _shift_down(h_ref[:, tc:2 * tc], 1, row)
        o_ref[:, 0:tc] = jnp.sum(gr * hr + gi * hi, axis=0, keepdims=True)
        o_ref[:, tc:2 * tc] = jnp.sum(gi * hr - gr * hi, axis=0, keepdims=True)

    blk = pl.BlockSpec((n_rows, 2 * tc), lambda j: (0, j))
    return pl.pallas_call(
        body, grid=(two_s // (2 * tc),), in_specs=[blk, blk], out_specs=pl.BlockSpec((1, 2 * tc), lambda j: (0, j)),
        out_shape=jax.ShapeDtypeStruct((1, two_s), f32), compiler_params=_params("parallel"), name=name)(g, h)


def _interleave(re, im, tc):
    lead = re.shape[:-1]
    s = re.shape[-1]
    return jnp.concatenate([re.reshape(*lead, s // tc, 1, tc), im.reshape(*lead, s // tc, 1, tc)], axis=-2).reshape(*lead, 2 * s)


def _deinterleave(z, tc):
    lead = z.shape[:-1]
    s = z.shape[-1] // 2
    z = z.reshape(*lead, s // tc, 2, tc)
    return z[..., 0, :].reshape(*lead, s), z[..., 1, :].reshape(*lead, s)


def _f_disc(log_dt, ar, ai, br, bi):
    dt = jnp.exp(log_dt)
    mag = jnp.exp(ar * dt)
    lr, li = mag * jnp.cos(ai * dt), mag * jnp.sin(ai * dt)
    den = ar * ar + ai * ai
    fr = ((lr - 1.0) * ar + li * ai) / den
    fi = (li * ar - (lr - 1.0) * ai) / den
    return lr, li, fr[None] * br - fi[None] * bi, fr[None] * bi + fi[None] * br


def _disc_fwd(log_dt, ar, ai, br, bi, name):
    def body(*refs):
        for o_ref, val in zip(refs[5:], _f_disc(*[r[...] for r in refs[:5]])):
            o_ref[...] = val

    sd = jax.ShapeDtypeStruct
    return pl.pallas_call(body, out_shape=[sd(ar.shape, f32), sd(ar.shape, f32), sd(br.shape, f32), sd(br.shape, f32)],
                          name=name)(log_dt, ar, ai, br, bi)


def _disc_bwd(log_dt, ar, ai, br, bi, cots, name):
    def body(*refs):
        _, vjp = jax.vjp(_f_disc, *[r[...] for r in refs[:5]])
        for o_ref, val in zip(refs[9:], vjp(tuple(r[...] for r in refs[5:9]))):
            o_ref[...] = val

    sd = jax.ShapeDtypeStruct
    return pl.pallas_call(
        body, out_shape=[sd(log_dt.shape, f32), sd(ar.shape, f32), sd(ar.shape, f32), sd(br.shape, f32), sd(br.shape, f32)],
        name=name)(log_dt, ar, ai, br, bi, *cots)


def _alibi_bias(n_heads_per_pattern):
    n_heads = len(DSWA_PATTERNS) * n_heads_per_pattern
    slopes = np.array([2.0 ** (-8.0 * (h + 1) / n_heads) for h in range(n_heads)], dtype=np.float32)
    qi = np.arange(QBLK)[:, None]
    kj = np.arange(QBLK)[None, :]
    cur, prev = [], []
    for h in range(n_heads):
        dil = DSWA_PATTERNS[h // n_heads_per_pattern][1]
        d_cur = qi - kj
        d_prev = QBLK + qi - kj
        cur.append(np.where(d_cur >= 0, -slopes[h] * (d_cur * dil).astype(np.float32), NEG_INF))
        prev.append(np.where(d_prev <= QBLK, -slopes[h] * (d_prev * dil).astype(np.float32), NEG_INF))
    return np.stack(cur).astype(np.float32), np.stack(prev).astype(np.float32)


def _blocks_per_residue(head, n_heads_per_pattern, n_blocks):
    pattern = head // n_heads_per_pattern
    out = n_blocks // DSWA_PATTERNS[-1][1]
    for p in range(len(DSWA_PATTERNS) - 2, -1, -1):
        out = jnp.where(pattern == p, n_blocks // DSWA_PATTERNS[p][1], out)
    return out


def _qk(a, b):
    return jnp.einsum('bqe,bke->bqk', a, b, preferred_element_type=f32)


def _pv(p, v):
    return jnp.einsum('bqk,bke->bqe', p.astype(bf16), v, preferred_element_type=f32)


def _ptq(p, a):
    return jnp.einsum('bqk,bqe->bke', p.astype(bf16), a, preferred_element_type=f32)


def _prev_block(t):
    return jnp.concatenate([jnp.zeros((1,) + t.shape[1:], t.dtype), t[:-1]], axis=0)


def _next_block(t):
    return jnp.concatenate([t[1:], jnp.zeros((1,) + t.shape[1:], t.dtype)], axis=0)


def _has_prev(head, hpp, n_blocks):
    blk = lax.broadcasted_iota(jnp.int32, (n_blocks, 1, 1), 0)
    return (blk & (_blocks_per_residue(head, hpp, n_blocks) - 1)) > 0


def _attn_fwd(q, k, v, hpp, name):
    n_heads, n_blocks = q.shape[0], q.shape[1]
    bias_cur, bias_prev = _alibi_bias(hpp)
    scale = HEAD_DIM ** -0.5

    def body(q_ref, k_ref, v_ref, bc_ref, bp_ref, o_ref, lse_ref, lse_x_ref):
        has_prev = _has_prev(pl.program_id(0), hpp, n_blocks)
        qv, kv, vv = q_ref[...], k_ref[...], v_ref[...]
        s_c = _qk(qv, kv) * scale + bc_ref[...][None]
        s_p = jnp.where(has_prev, _qk(qv, _prev_block(kv)) * scale + bp_ref[...][None], NEG_INF)
        m = jnp.maximum(jnp.max(s_c, axis=-1, keepdims=True), jnp.max(s_p, axis=-1, keepdims=True))
        p_c = jnp.exp(s_c - m)
        p_p = jnp.exp(s_p - m)
        l = jnp.sum(p_c, axis=-1, keepdims=True) + jnp.sum(p_p, axis=-1, keepdims=True)
        o_ref[...] = (_pv(p_c, vv) + _pv(p_p, _prev_block(vv))) / l
        lse = m + jnp.log(l)
        lse_ref[...] = lse
        lse_x_ref[...] = jnp.broadcast_to(lse, lse_x_ref.shape)

    head = pl.BlockSpec((None, n_blocks, QBLK, HEAD_DIM), lambda a: (a, 0, 0, 0))
    bias = pl.BlockSpec((None, QBLK, QBLK), lambda a: (a, 0, 0))
    return pl.pallas_call(
        body, grid=(n_heads,), in_specs=[head, head, head, bias, bias],
        out_specs=[head, pl.BlockSpec((None, n_blocks, QBLK, 1), lambda a: (a, 0, 0, 0)), head],
        out_shape=[jax.ShapeDtypeStruct(q.shape, f32), jax.ShapeDtypeStruct(q.shape[:3] + (1,), f32),
                   jax.ShapeDtypeStruct(q.shape, f32)],
        compiler_params=_params("parallel"), name=name)(q, k, v, jnp.asarray(bias_cur), jnp.asarray(bias_prev))


def _attn_bwd(q, k, v, o, lse, do, dlse_x, hpp, name):
    n_heads, n_blocks = q.shape[0], q.shape[1]
    bias_cur, bias_prev = _alibi_bias(hpp)
    scale = HEAD_DIM ** -0.5

    def body(q_ref, k_ref, v_ref, o_ref, l_ref, do_ref, dl_ref, bc_ref, bp_ref, dq_ref, dk_ref, dv_ref):
        has_prev = _has_prev(pl.program_id(0), hpp, n_blocks)
        qv, kv, vv = q_ref[...], k_ref[...], v_ref[...]
        kp, vp = _prev_block(kv), _prev_block(vv)
        dov = do_ref[...]
        do_b = dov.astype(bf16)
        lse = l_ref[...]
        corr = jnp.sum(dl_ref[...], axis=-1, keepdims=True) - jnp.sum(dov * o_ref[...], axis=-1, keepdims=True)
        p_c = jnp.exp(_qk(qv, kv) * scale + bc_ref[...][None] - lse)
        ds_c = p_c * (_qk(do_b, vv) + corr)
        p_p = jnp.where(has_prev, jnp.exp(_qk(qv, kp) * scale + bp_ref[...][None] - lse), 0.0)
        ds_p = p_p * (_qk(do_b, vp) + corr)
        dq_ref[...] = ((_pv(ds_c, kv) + _pv(ds_p, kp)) * scale).astype(dq_ref.dtype)
        dk_ref[...] = ((_ptq(ds_c, qv) + _next_block(_ptq(ds_p, qv))) * scale).astype(dk_ref.dtype)
        dv_ref[...] = (_ptq(p_c, do_b) + _next_block(_ptq(p_p, do_b))).astype(dv_ref.dtype)

    head = pl.BlockSpec((None, n_blocks, QBLK, HEAD_DIM), lambda a: (a, 0, 0, 0))
    head1 = pl.BlockSpec((None, n_blocks, QBLK, 1), lambda a: (a, 0, 0, 0))
    bias = pl.BlockSpec((None, QBLK, QBLK), lambda a: (a, 0, 0))
    out = jax.ShapeDtypeStruct(q.shape, bf16)
    return pl.pallas_call(
        body, grid=(n_heads,), in_specs=[head, head, head, head, head1, head, head, bias, bias],
        out_specs=[head, head, head], out_shape=[out, out, out],
        compiler_params=_params("parallel"), name=name)(q, k, v, o, lse, do, dlse_x, jnp.asarray(bias_cur), jnp.asarray(bias_prev))


def _to_blocks(t, hpp):
    n_rows = t.shape[0]
    t = t.reshape(n_rows, len(DSWA_PATTERNS), hpp, HEAD_DIM)
    out = []
    for p, (_, dil) in enumerate(DSWA_PATTERNS):
        tp = t[:, p].reshape(n_rows // dil, dil, hpp, HEAD_DIM).transpose(2, 1, 0, 3)
        out.append(tp.reshape(hpp, n_rows // QBLK, QBLK, HEAD_DIM))
    return jnp.concatenate(out, axis=0)


def _from_blocks(t, hpp):
    n_blocks, last = t.shape[1], t.shape[3]
    n_rows = n_blocks * QBLK
    out = []
    for p, (_, dil) in enumerate(DSWA_PATTERNS):
        tp = t[p * hpp:(p + 1) * hpp].reshape(hpp, dil, n_rows // dil, last).transpose(2, 1, 0, 3)
        out.append(tp.reshape(n_rows, hpp * last))
    return out


def _loss_head(y, target, name):
    n_rows, d = y.shape
    tm = _tile(n_rows, 256, SUBLANES)

    def body(y_ref, t_ref, dy_ref, loss_ref):
        diff = y_ref[...] - t_ref[...]
        dy_ref[...] = diff * (1.0 / d)

        @pl.when(pl.program_id(0) == 0)
        def _():
            loss_ref[...] = jnp.zeros_like(loss_ref)

        loss_ref[...] += jnp.sum(jnp.sum(diff * diff, axis=1, keepdims=True), axis=0, keepdims=True) * (0.5 / d)

    row = pl.BlockSpec((tm, d), lambda i: (i, 0))
    return pl.pallas_call(
        body, grid=(n_rows // tm,), in_specs=[row, row], out_specs=[row, pl.BlockSpec((1, 1), lambda i: (0, 0))],
        out_shape=[jax.ShapeDtypeStruct(y.shape, f32), jax.ShapeDtypeStruct((1, 1), f32)],
        compiler_params=_params("arbitrary"), name=name)(y, target)


def _as2d(a):
    if a.ndim >= 2 and a.shape[-1] >= LANES:
        return a.reshape(-1, a.shape[-1])
    return a.reshape(-1, LANES) if a.size % LANES == 0 else a.reshape(1, -1)


ELEMENTWISE_BLOCK_BYTES = 2 << 20


def _row_tile(n_rows, n_cols):
    return _tile(n_rows, max(SUBLANES, ELEMENTWISE_BLOCK_BYTES // (4 * n_cols)), SUBLANES)


def _adamw_math(w, g, m, v):
    mn = ADAM_B1 * m + (1.0 - ADAM_B1) * g
    vn = ADAM_B2 * v + (1.0 - ADAM_B2) * jnp.square(g)
    m_hat = mn / (1.0 - ADAM_B1 ** ADAM_STEP)
    v_hat = vn / (1.0 - ADAM_B2 ** ADAM_STEP)
    return -ADAM_LR * (m_hat / (jnp.sqrt(v_hat) + ADAM_EPS) + ADAM_WD * w), mn, vn


def _adamw_layers(w, g, m, v, name):
    depth, n_rows, n_cols = w.shape
    tm = _row_tile(n_rows, n_cols)

    def body(w_ref, g_ref, m_ref, v_ref, d_ref, mo_ref, vo_ref):
        d_ref[...], mo_ref[...], vo_ref[...] = _adamw_math(w_ref[...], g_ref[...], m_ref[...], v_ref[...])

    blk = pl.BlockSpec((None, tm, n_cols), lambda l, i: (l, i, 0))
    out = jax.ShapeDtypeStruct(w.shape, f32)
    return pl.pallas_call(body, grid=(depth, n_rows // tm), in_specs=[blk] * 4, out_specs=[blk] * 3, out_shape=[out] * 3,
                          compiler_params=_params("parallel", "parallel"), name=name)(w, g, m, v)


def _adamw(w, g, m, v, name):
    shape = w.shape
    if w.ndim == 3 and shape[-1] >= LANES and shape[1] % SUBLANES == 0:
        return _adamw_layers(w, g, m, v, name)
    w2, g2, m2, v2 = _as2d(w), _as2d(g), _as2d(m), _as2d(v)
    n_rows, n_cols = w2.shape
    tm = _row_tile(n_rows, n_cols)

    def body(w_ref, g_ref, m_ref, v_ref, d_ref, mo_ref, vo_ref):
        d_ref[...], mo_ref[...], vo_ref[...] = _adamw_math(w_ref[...], g_ref[...], m_ref[...], v_ref[...])

    row = pl.BlockSpec((tm, n_cols), lambda i: (i, 0))
    out = jax.ShapeDtypeStruct(w2.shape, f32)
    d, mn, vn = pl.pallas_call(body, grid=(n_rows // tm,), in_specs=[row] * 4, out_specs=[row] * 3, out_shape=[out] * 3,
                               compiler_params=_params("parallel"), name=name)(w2, g2, m2, v2)
    return d.reshape(shape), mn.reshape(shape), vn.reshape(shape)


def _sum_slabs(x, name):
    n = x.shape[0]
    shape = x.shape[1:]
    x3 = x.reshape(n, -1, shape[-1])
    n_rows, n_cols = x3.shape[1:]
    tm = _row_tile(n_rows, n_cols)

    def body(x_ref, o_ref):
        acc = x_ref[0].astype(f32)
        for s in range(1, n):
            acc = acc + x_ref[s].astype(f32)
        o_ref[...] = acc

    out = pl.pallas_call(
        body, grid=(n_rows // tm,), in_specs=[pl.BlockSpec((n, tm, n_cols), lambda i: (0, i, 0))],
        out_specs=pl.BlockSpec((tm, n_cols), lambda i: (i, 0)), out_shape=jax.ShapeDtypeStruct((n_rows, n_cols), f32),
        compiler_params=_params("parallel"), name=name)(x3)
    return out.reshape(shape)


def _mod_fwd(c_all, w_mod, b_cols, name):
    depth, d, cols = w_mod.shape
    tn = _tile(cols, 512)

    def body(c_ref, w_ref, b_ref, o_ref):
        cond = jax.nn.silu(c_ref[...])
        o_ref[...] = jnp.dot(cond, w_ref[...], preferred_element_type=f32, precision=lax.Precision.HIGHEST) + b_ref[...]

    return pl.pallas_call(
        body, grid=(depth, cols // tn),
        in_specs=[pl.BlockSpec((N_DEV, d), lambda l, j: (0, 0)), pl.BlockSpec((None, d, tn), lambda l, j: (l, 0, j)),
                  pl.BlockSpec((None, 1, tn), lambda l, j: (l, 0, j))],
        out_specs=pl.BlockSpec((None, N_DEV, tn), lambda l, j: (l, 0, j)),
        out_shape=jax.ShapeDtypeStruct((depth, N_DEV, cols), f32),
        compiler_params=_params("parallel", "parallel"), name=name)(c_all, w_mod, b_cols)


def _mod_wgrad(c_all_t, dmod_cols, name):
    d = c_all_t.shape[0]
    depth, _, cols = dmod_cols.shape
    tm = _tile(d, 256, SUBLANES)

    def body(c_ref, g_ref, o_ref):
        cond = jax.nn.silu(c_ref[...])
        o_ref[...] = jnp.dot(cond, g_ref[...], preferred_element_type=f32, precision=lax.Precision.HIGHEST)

    return pl.pallas_call(
        body, grid=(depth, d // tm),
        in_specs=[pl.BlockSpec((tm, N_DEV), lambda l, i: (i, 0)), pl.BlockSpec((None, N_DEV, cols), lambda l, i: (l, 0, 0))],
        out_specs=pl.BlockSpec((None, tm, cols), lambda l, i: (l, i, 0)),
        out_shape=jax.ShapeDtypeStruct((depth, d, cols), f32),
        compiler_params=_params("parallel", "parallel"), name=name)(c_all_t, dmod_cols)


def _position():
    return lax.axis_index("x"), lax.axis_index("y"), lax.axis_index("c")


def _linear(p):
    return 4 * p[0] + 2 * p[1] + p[2]


def _all_gather(xs, name):
    n = len(xs)

    def body(*refs):
        x_refs, o_refs = refs[:n], refs[n:2 * n]
        send_sems, recv_sems, local_sems = refs[2 * n:]
        x, y, c = _position()
        me, sibling = (x, y, c), (x, y, 1 - c)
        chips = [(1 - x, y), (x, 1 - y), (1 - x, 1 - y)]

        def copy(a, k, block, to, src=None):
            slab = o_refs[a].at[_linear(block)]
            return pltpu.make_async_remote_copy(
                src_ref=slab if src is None else src, dst_ref=slab, send_sem=send_sems.at[a, k], recv_sem=recv_sems.at[a, k],
                device_id=to, device_id_type=MESH)

        mine = [pltpu.make_async_copy(x_refs[a], o_refs[a].at[_linear(me)], local_sems.at[a]) for a in range(n)]
        first = []
        for a in range(n):
            mine[a].start()
            first.append(copy(a, 0, me, sibling, src=x_refs[a]))
            first += [copy(a, 1 + j, me, (*chip, c), src=x_refs[a]) for j, chip in enumerate(chips)]
        for cp in first:
            cp.start()
        passed = []
        for j, chip in enumerate(chips):
            for a in range(n):
                copy(a, 1 + j, (*chip, c), me).wait_recv()
                cp = copy(a, 4 + j, (*chip, c), sibling)
                cp.start()
                passed.append(cp)
        for a in range(n):
            copy(a, 0, sibling, me).wait_recv()
            for j, chip in enumerate(chips):
                copy(a, 4 + j, (*chip, 1 - c), me).wait_recv()
        for cp in first + passed:
            cp.wait_send()
        for cp in mine:
            cp.wait()

    return pl.pallas_call(
        body, in_specs=[ANY] * n, out_specs=[ANY] * n,
        out_shape=[jax.ShapeDtypeStruct((N_DEV,) + x.shape, x.dtype) for x in xs],
        scratch_shapes=[pltpu.SemaphoreType.DMA((n, 7)), pltpu.SemaphoreType.DMA((n, 7)), pltpu.SemaphoreType.DMA((n,))],
        name=name)(*xs)


def _all_gather_relayed(xs, name):
    n = len(xs)
    halves = [x.shape[0] // 2 for x in xs]

    def body(*refs):
        x_refs, o_refs = refs[:n], refs[n:2 * n]
        send_sems, recv_sems = refs[2 * n:]
        x, y, c = _position()
        across_x, across_y, across_xy = (1 - x, y, c), (x, 1 - y, c), (1 - x, 1 - y, c)
        sibling = (x, y, 1 - c)

        def copy(a, k, slab, to, rows=None, src=None):
            dst = o_refs[a].at[_linear(slab)] if rows is None else o_refs[a].at[_linear(slab), rows]
            return pltpu.make_async_remote_copy(
                src_ref=dst if src is None else src, dst_ref=dst, send_sem=send_sems.at[a, k], recv_sem=recv_sems.at[a, k],
                device_id=to, device_id_type=MESH)

        def other_core(p):
            return (p[0], p[1], 1 - c)

        me = (x, y, c)
        started = []
        for a in range(n):
            started += [copy(a, 0, me, sibling, src=x_refs[a]), copy(a, 1, me, across_x, src=x_refs[a]),
                        copy(a, 2, me, across_y, src=x_refs[a])]
        for cp in started:
            cp.start()
        later = []
        for a in range(n):
            first, rest = pl.ds(0, halves[a]), pl.ds(halves[a], xs[a].shape[0] - halves[a])
            copy(a, 1, across_x, me).wait_recv()
            later += [copy(a, 3, across_x, across_y, rows=first), copy(a, 5, across_x, sibling)]
            later[-2].start()
            later[-1].start()
            copy(a, 2, across_y, me).wait_recv()
            later += [copy(a, 4, across_y, across_x, rows=rest), copy(a, 6, across_y, sibling)]
            later[-2].start()
            later[-1].start()
        for a in range(n):
            first, rest = pl.ds(0, halves[a]), pl.ds(halves[a], xs[a].shape[0] - halves[a])
            copy(a, 3, across_xy, me, rows=first).wait_recv()
            copy(a, 4, across_xy, me, rows=rest).wait_recv()
            later.append(copy(a, 7, across_xy, sibling))
            later[-1].start()
        for a in range(n):
            copy(a, 0, sibling, me).wait_recv()
            for k, slab in ((5, across_x), (6, across_y), (7, across_xy)):
                copy(a, k, other_core(slab), me).wait_recv()
        for cp in started + later:
            cp.wait_send()

    sems = pltpu.SemaphoreType.DMA((n, 8))
    return pl.pallas_call(
        body, in_specs=[ANY] * n, out_specs=[ANY] * n,
        out_shape=[jax.ShapeDtypeStruct((N_DEV,) + x.shape, x.dtype) for x in xs],
        scratch_shapes=[sems, sems], name=name)(*xs)


def _pair_exchange(xs, name):
    n = len(xs)

    def body(*refs):
        x_refs, got_refs = refs[:n], refs[n:2 * n]
        send_sems, recv_sems = refs[2 * n:]
        x, y, c = _position()
        sends = []
        for a in range(n):
            for p in range(N_CHIPS):
                sends.append(pltpu.make_async_remote_copy(
                    src_ref=x_refs[a].at[p, 1 - c], dst_ref=got_refs[a].at[p], send_sem=send_sems.at[a, p],
                    recv_sem=recv_sems.at[a, p], device_id=(x, y, 1 - c), device_id_type=MESH))
        for cp in sends:
            cp.start()
        for cp in sends:
            cp.wait_recv()
        for cp in sends:
            cp.wait_send()

    sems = pltpu.SemaphoreType.DMA((n, N_CHIPS))
    return pl.pallas_call(body, in_specs=[ANY] * n, out_specs=[ANY] * n,
                          out_shape=[jax.ShapeDtypeStruct((N_CHIPS,) + x.shape[2:], x.dtype) for x in xs],
                          scratch_shapes=[sems, sems], name=name)(*xs)


def _chip_copies(x_refs, o_refs, send_sems, recv_sems, ranges=None):
    x, y, c = _position()
    copies = []
    for a in range(len(x_refs)):
        rows = pl.ds(0, x_refs[a].shape[1]) if ranges is None else pl.ds(ranges[a][0], ranges[a][1] - ranges[a][0])
        for k, (px, py) in enumerate([(1 - x, y), (x, 1 - y), (1 - x, 1 - y)]):
            copies.append(pltpu.make_async_remote_copy(
                src_ref=x_refs[a].at[2 * px + py, rows], dst_ref=o_refs[a].at[k, rows], send_sem=send_sems.at[a, k],
                recv_sem=recv_sems.at[a, k], device_id=(px, py, c), device_id_type=MESH))
    return copies


def _chip_exchange(xs, name):
    n = len(xs)

    def body(*refs):
        sends = _chip_copies(refs[:n], refs[n:2 * n], refs[2 * n], refs[2 * n + 1])
        for cp in sends:
            cp.start()
        for cp in sends:
            cp.wait_recv()
        for cp in sends:
            cp.wait_send()

    sems = pltpu.SemaphoreType.DMA((n, N_CHIPS - 1))
    return pl.pallas_call(body, in_specs=[ANY] * n, out_specs=[ANY] * n,
                          out_shape=[jax.ShapeDtypeStruct((N_CHIPS - 1,) + x.shape[1:], x.dtype) for x in xs],
                          scratch_shapes=[sems, sems], name=name)(*xs)


def _pick_add(picked, index, others, out_dtype, name):
    per_slab = picked.ndim == 4
    n_rows, n_cols = picked.shape[-2:]
    tm = _row_tile(n_rows, n_cols)
    n_other = 1 if per_slab else others.shape[0]

    def body(idx_ref, p_ref, o_ref, out_ref):
        acc = p_ref[...].astype(f32)
        if per_slab:
            acc = acc + o_ref[...].astype(f32)
        else:
            for k in range(n_other):
                acc = acc + o_ref[k].astype(f32)
        out_ref[...] = acc.astype(out_ref.dtype)

    if per_slab:
        grid = (picked.shape[0], n_rows // tm)
        in_specs = [pl.BlockSpec((None, None, tm, n_cols), lambda p, i, idx: (p, idx[0], i, 0)),
                    pl.BlockSpec((None, tm, n_cols), lambda p, i, idx: (p, i, 0))]
        out_specs = pl.BlockSpec((None, tm, n_cols), lambda p, i, idx: (p, i, 0))
        out_shape = jax.ShapeDtypeStruct((picked.shape[0], n_rows, n_cols), out_dtype)
    else:
        grid = (n_rows // tm,)
        in_specs = [pl.BlockSpec((None, tm, n_cols), lambda i, idx: (idx[0], i, 0)),
                    pl.BlockSpec((n_other, tm, n_cols), lambda i, idx: (0, i, 0))]
        out_specs = pl.BlockSpec((tm, n_cols), lambda i, idx: (i, 0))
        out_shape = jax.ShapeDtypeStruct((n_rows, n_cols), out_dtype)
    return pl.pallas_call(
        body, grid_spec=pltpu.PrefetchScalarGridSpec(num_scalar_prefetch=1, grid=grid, in_specs=in_specs, out_specs=out_specs),
        out_shape=out_shape, compiler_params=_params(*(["parallel"] * len(grid))), name=name)(
            jnp.reshape(index, (1,)).astype(jnp.int32), picked, others)


def _chip_sums(xs, name):
    c = lax.axis_index("c")
    pairs = [v.reshape((N_CHIPS, 2) + v.shape[1:]) for v in xs]
    got = _pair_exchange(pairs, f"{name}_pair")
    return [_pick_add(a, c, b, a.dtype, f"{name}_pair_sum{i}") for i, (a, b) in enumerate(zip(pairs, got))]


def _total_sums(chip_sums, arrived, name):
    my_chip = 2 * lax.axis_index("x") + lax.axis_index("y")
    return [_pick_add(a, my_chip, b, f32, f"{name}_sum{i}") for i, (a, b) in enumerate(zip(chip_sums, arrived))]


def _column_group(j, tc, shape):
    col = lax.broadcasted_iota(jnp.int32, shape, 1)
    return (j * tc + col % tc) // SSM_STATE


def _bd_build(bt, n_groups, tc, name):
    n_rows = n_groups * SSM_GROUP
    two_s = bt.shape[1]

    def body(b_ref, o_ref):
        grp = _column_group(pl.program_id(0), tc, (n_rows, 2 * tc))
        row = lax.broadcasted_iota(jnp.int32, (n_rows, 2 * tc), 0)
        tiled = jnp.concatenate([b_ref[...]] * n_groups, axis=0)
        o_ref[...] = jnp.where(grp == row // SSM_GROUP, tiled, 0.0).astype(o_ref.dtype)

    return pl.pallas_call(
        body, grid=(two_s // (2 * tc),), in_specs=[pl.BlockSpec((SSM_GROUP, 2 * tc), lambda j: (0, j))],
        out_specs=pl.BlockSpec((n_rows, 2 * tc), lambda j: (0, j)), out_shape=jax.ShapeDtypeStruct((n_rows, two_s), bf16),
        compiler_params=_params("parallel"), name=name)(bt)


def _bd_fold(m, n_groups, tc, name):
    n_rows, two_s = m.shape

    def body(m_ref, o_ref):
        grp = _column_group(pl.program_id(0), tc, (SSM_GROUP, 2 * tc))
        acc = jnp.zeros((SSM_GROUP, 2 * tc), f32)
        for g in range(n_groups):
            acc = acc + jnp.where(grp == g, m_ref[g * SSM_GROUP:(g + 1) * SSM_GROUP, :], 0.0)
        o_ref[...] = acc

    return pl.pallas_call(
        body, grid=(two_s // (2 * tc),), in_specs=[pl.BlockSpec((n_rows, 2 * tc), lambda j: (0, j))],
        out_specs=pl.BlockSpec((SSM_GROUP, 2 * tc), lambda j: (0, j)), out_shape=jax.ShapeDtypeStruct((SSM_GROUP, two_s), f32),
        compiler_params=_params("parallel"), name=name)(m)


def _ssm_setup(p, tag):
    n_groups = p["ssm_a_re"].shape[0]
    s = n_groups * SSM_STATE
    tc = min(SCAN_TC, s)
    log_dt = p["ssm_log_dt"].reshape(n_groups, 1)
    br_t = p["ssm_b_re"].transpose(2, 0, 1)
    bi_t = p["ssm_b_im"].transpose(2, 0, 1)
    disc_in = (log_dt, p["ssm_a_re"], p["ssm_a_im"], br_t, bi_t)
    lr, li, bbr, bbi = _disc_fwd(*disc_in, name=f"ssm_disc{tag}")
    lam = _interleave(lr.reshape(1, s), li.reshape(1, s), tc)
    bb = _bd_build(_interleave(bbr.reshape(SSM_GROUP, s), bbi.reshape(SSM_GROUP, s), tc), n_groups, tc, f"ssm_in_map{tag}")
    cr_t = p["ssm_c_re"].transpose(1, 0, 2).reshape(SSM_GROUP, s)
    ci_t = p["ssm_c_im"].transpose(1, 0, 2).reshape(SSM_GROUP, s)
    cc = _bd_build(_interleave(cr_t, -ci_t, tc), n_groups, tc, f"ssm_out_map{tag}")
    return dict(disc_in=disc_in, lam=lam, bb=bb, cc=cc, tc=tc, n_groups=n_groups)


def _apply_w(a, w, key, out_dtype, name):
    return _mm(a, w[key], out_dtype, name, rhs_t=key in COL_SHARDED)


def _apply_wt(dy, w, key, out_dtype, name, carry=()):
    return _mm(dy, w[key], out_dtype, name, rhs_t=key not in COL_SHARDED, carry=carry)


def _weight_grad(a, dy, key, name, carry=()):
    lhs, rhs = (dy, a) if key in COL_SHARDED else (a, dy)
    return _mm(lhs, rhs, bf16, name, lhs_t=True, carry=carry)


class _Cargo:
    def __init__(self, chip_sums):
        self.chip_sums = chip_sums
        self.arrived = {}

    def hosted(self, call, jobs):
        if self.chip_sums is None:
            return call(())
        carry = []
        for key, part, parts in jobs:
            n_rows = self.chip_sums[key].shape[1]
            step = n_rows if parts == 1 else -(-n_rows // parts // BF16_ROWS) * BF16_ROWS
            carry.append((self.chip_sums[key], part * step, min(n_rows, (part + 1) * step), self.arrived.get(key)))
        out, got = call(carry)
        for (key, _, _), buf in zip(jobs, got):
            self.arrived[key] = buf
        return out


def _layer_fwd(x, mod, w, p, tag):
    n_rows, d = x.shape
    w4 = d // 4
    hpp = w4 // HEAD_DIM
    sh1, sc1, gt1, sh2, sc2, gt2 = mod
    vec = lambda a: a.reshape(1, -1)
    s = dict(x=x)
    (s["h"],) = _rowwise(_f_prenorm, [x], [vec(p["g_pre_mix"]), sc1, sh1], [bf16], 256, f"prenorm_mix{tag}")
    sections = [(w4, f32), (9 * w4, bf16), (3 * w4, f32), (d, f32), (d, f32), (d, f32)]
    s["u32"], qkv, s["conv"], *s["pg"] = _mm_sections(s["h"], w["w_in"], sections, f"in_proj{tag}")
    ssm = s["ssm"] = _ssm_setup(p, tag)
    s["u"] = s["u32"].astype(bf16)
    xin = _mm(s["u"], ssm["bb"], f32, f"ssm_in{tag}")
    s["hf"], s["hb"] = _scan(xin, ssm["lam"], False, f"ssm_scan{tag}")
    s["yc"] = _mm(s["hb"], ssm["cc"], f32, f"ssm_out{tag}", rhs_t=True)
    (s["g"],) = _rowwise(_f_gelu, [s["yc"], s["u32"]], [vec(p["ssm_d"])], [bf16], 256, f"ssm_gelu{tag}")
    s["z"] = _apply_w(s["g"], w, "w_glu", f32, f"glu_proj{tag}")
    (s["s_ssm"],) = _rowwise(_f_glu, [s["g"], s["z"]], [vec(p["b_glu"])], [bf16], 256, f"glu{tag}")
    s["y_ssm"] = _apply_w(s["s_ssm"], w, "w_ssm_out", f32, f"ssm_proj{tag}")
    s["qb"], s["kb"], s["vb"] = (_to_blocks(qkv[:, i * 3 * w4:(i + 1) * 3 * w4], hpp) for i in range(3))
    s["ob"], s["lseb"], lse_xb = _attn_fwd(s["qb"], s["kb"], s["vb"], hpp, f"attn{tag}")
    s["o_tok"] = _from_blocks(s["ob"], hpp)
    s["lse_x"] = _from_blocks(lse_xb, hpp)
    (s["attn_o"],) = _rowwise(_f_amerge, s["o_tok"] + s["lse_x"], [], [bf16], 256, f"attn_merge{tag}")
    s["y_attn"] = _apply_w(s["attn_o"], w, "w_attn_out", f32, f"attn_proj{tag}")
    s["cm"] = _convgate_fwd(s["conv"], p["conv_mix_w"], 0, w4, f"convgate{tag}")
    s["y_conv"] = _apply_w(s["cm"], w, "w_conv_out", f32, f"conv_proj{tag}")
    s["bg"] = [p["b_gate"][i * d:(i + 1) * d].reshape(1, d) for i in range(3)]
    (s["merged"],) = _rowwise(_f_merge, s["pg"] + [s["y_ssm"], s["y_attn"], s["y_conv"]], s["bg"], [bf16], 128, f"merge{tag}")
    s["y"] = _apply_w(s["merged"], w, "w_o", f32, f"o_proj{tag}")
    (s["x1"],) = _rowwise(_f_postres, [x, s["y"]], [vec(p["g_post_mix"]), gt1], [f32], 256, f"postres_mix{tag}")
    (s["h2"],) = _rowwise(_f_prenorm, [s["x1"]], [vec(p["g_pre_ffn"]), sc2, sh2], [bf16], 256, f"prenorm_ffn{tag}")
    s["up"] = _apply_w(s["h2"], w, "w_up", f32, f"up_proj{tag}")
    s["act"] = _ffnconv_fwd(s["up"], p["ffn_conv_w"], f"ffnconv{tag}")
    s["y2"] = _apply_w(s["act"], w, "w_down", f32, f"down_proj{tag}")
    (x2,) = _rowwise(_f_postres, [s["x1"], s["y2"]], [vec(p["g_post_ffn"]), gt2], [f32], 256, f"postres_ffn{tag}")
    return x2, s


def _layer_bwd(dx2, s, mod, w, p, tag, cargo):
    x = s["x"]
    n_rows, d = x.shape
    w4 = d // 4
    hpp = w4 // HEAD_DIM
    sh1, sc1, gt1, sh2, sc2, gt2 = mod
    vec = lambda a: a.reshape(1, -1)
    gw, gp = {}, {}
    dy2, d_gpf, d_gt2 = _rowwise_bwd(_f_postres, [s["x1"], s["y2"]], [vec(p["g_post_ffn"]), gt2], [[dx2]], [None, bf16], True,
                                     256, f"postres_ffn_bwd{tag}")
    gp["g_post_ffn"] = d_gpf
    dact = cargo.hosted(lambda carry: _apply_wt(dy2, w, "w_down", bf16, f"down_dx{tag}", carry), [("w_down", 0, 2)])
    gw["w_down"] = cargo.hosted(lambda carry: _weight_grad(s["act"], dy2, "w_down", f"down_dw{tag}", carry), [("w_down", 1, 2)])
    da, db, gp["ffn_conv_w"] = _ffnconv_bwd(s["up"], p["ffn_conv_w"], dact, f"ffnconv_bwd{tag}")
    dup = jnp.concatenate([da, db], axis=1)
    dh2 = cargo.hosted(lambda carry: _apply_wt(dup, w, "w_up", f32, f"up_dx{tag}", carry), [("w_in", 0, 2)])
    gw["w_up"] = cargo.hosted(lambda carry: _weight_grad(s["h2"], dup, "w_up", f"up_dw{tag}", carry), [("w_up", 0, 2)])
    dx1, d_g, d_sc2, d_sh2 = _rowwise_bwd(_f_prenorm, [s["x1"]], [vec(p["g_pre_ffn"]), sc2, sh2], [[dh2]], [f32], True,
                                          256, f"prenorm_ffn_bwd{tag}", add_to={0: dx2})
    gp["g_pre_ffn"] = d_g
    dy, d_gpm, d_gt1 = _rowwise_bwd(_f_postres, [x, s["y"]], [vec(p["g_post_mix"]), gt1], [[dx1]], [None, bf16], True,
                                    256, f"postres_mix_bwd{tag}")
    gp["g_post_mix"] = d_gpm
    dmerged = cargo.hosted(lambda carry: _apply_wt(dy, w, "w_o", f32, f"o_dx{tag}", carry), [("w_o", 0, 1)])
    gw["w_o"] = cargo.hosted(lambda carry: _weight_grad(s["merged"], dy, "w_o", f"o_dw{tag}", carry),
                             [(k, 0, 1) for k in LIGHT_SHARDED])
    res = _rowwise_bwd(_f_merge, s["pg"] + [s["y_ssm"], s["y_attn"], s["y_conv"]], s["bg"], [[dmerged]], [bf16] * 6, True,
                       128, f"merge_bwd{tag}")
    dpg, (dys, dya, dyv), dbg = list(res[0:3]), res[3:6], res[6:9]
    gp["b_gate"] = jnp.concatenate(dbg, axis=1)
    ds_ssm = _apply_wt(dys, w, "w_ssm_out", f32, f"ssm_proj_dx{tag}")
    gw["w_ssm_out"] = _weight_grad(s["s_ssm"], dys, "w_ssm_out", f"ssm_proj_dw{tag}")
    dattn_o = _apply_wt(dya, w, "w_attn_out", f32, f"attn_proj_dx{tag}")
    gw["w_attn_out"] = _weight_grad(s["attn_o"], dya, "w_attn_out", f"attn_proj_dw{tag}")
    dcm = _apply_wt(dyv, w, "w_conv_out", f32, f"conv_proj_dx{tag}")
    gw["w_conv_out"] = _weight_grad(s["cm"], dyv, "w_conv_out", f"conv_proj_dw{tag}")
    ssm = s["ssm"]
    tc, n_groups = ssm["tc"], ssm["n_groups"]
    dg_a, dz, gp["b_glu"] = _rowwise_bwd(_f_glu, [s["g"], s["z"]], [vec(p["b_glu"])], [[ds_ssm]], [f32, bf16], True,
                                         256, f"glu_bwd{tag}")
    dg_b = _apply_wt(dz, w, "w_glu", f32, f"glu_proj_dx{tag}")
    gw["w_glu"] = _weight_grad(s["g"], dz, "w_glu", f"glu_proj_dw{tag}")
    dyc, du_skip, gp["ssm_d"] = _rowwise_bwd(_f_gelu, [s["yc"], s["u32"]], [vec(p["ssm_d"])], [[dg_a, dg_b]], [bf16, f32], True,
                                             256, f"ssm_gelu_bwd{tag}")
    dh_state = _mm(dyc, ssm["cc"], f32, f"ssm_out_dx{tag}")
    g_cc = _bd_fold(_mm(dyc, s["hb"], f32, f"ssm_out_dw{tag}", lhs_t=True), n_groups, tc, f"ssm_out_dw_fold{tag}")
    gf, gb = _scan(dh_state, ssm["lam"], True, f"ssm_scan_bwd{tag}")
    d_lam = _lam_grad(gf, s["hf"], f"ssm_lam_grad{tag}")
    du_x = _mm(gb, ssm["bb"], f32, f"ssm_in_dx{tag}", rhs_t=True)
    g_bb = _bd_fold(_mm(s["u"], gb, f32, f"ssm_in_dw{tag}", lhs_t=True), n_groups, tc, f"ssm_in_dw_fold{tag}")
    (du,) = _rowwise(_f_add, [du_skip, du_x], [], [bf16], 256, f"ssm_du{tag}")
    d_lr, d_li = _deinterleave(d_lam, tc)
    g_bbr, g_bbi = _deinterleave(g_bb, tc)
    group_shape = (SSM_GROUP, n_groups, SSM_STATE)
    cots = (d_lr.reshape(n_groups, SSM_STATE), d_li.reshape(n_groups, SSM_STATE), g_bbr.reshape(group_shape), g_bbi.reshape(group_shape))
    d_ldt, d_ar, d_ai, d_br, d_bi = _disc_bwd(*ssm["disc_in"], cots, f"ssm_disc_bwd{tag}")
    gp["ssm_log_dt"], gp["ssm_a_re"], gp["ssm_a_im"] = d_ldt.reshape(-1), d_ar, d_ai
    gp["ssm_b_re"], gp["ssm_b_im"] = d_br.transpose(1, 2, 0), d_bi.transpose(1, 2, 0)
    g_ccr, g_cci = _deinterleave(g_cc, tc)
    gp["ssm_c_re"] = g_ccr.reshape(group_shape).transpose(1, 0, 2)
    gp["ssm_c_im"] = -g_cci.reshape(group_shape).transpose(1, 0, 2)
    res = _rowwise_bwd(_f_amerge, s["o_tok"] + s["lse_x"], [], [[dattn_o]], [f32] * 6, False, 256, f"attn_merge_bwd{tag}")
    do_b = _to_blocks(jnp.concatenate(res[0:3], axis=1), hpp)
    dl_b = _to_blocks(jnp.concatenate(res[3:6], axis=1), hpp)
    dqb, dkb, dvb = _attn_bwd(s["qb"], s["kb"], s["vb"], s["ob"], s["lseb"], do_b, dl_b, hpp, f"attn_bwd{tag}")
    dq, dk, dv = (jnp.concatenate(_from_blocks(t, hpp), axis=1) for t in (dqb, dkb, dvb))
    dcb, dcc, dch, gp["conv_mix_w"] = _convgate_bwd(s["conv"], p["conv_mix_w"], dcm, 0, w4, f"convgate_bwd{tag}")
    dproj = jnp.concatenate([du, dq, dk, dv, dcb, dcc, dch] + dpg, axis=1)
    dh = cargo.hosted(lambda carry: _apply_wt(dproj, w, "w_in", f32, f"in_dx{tag}", carry), [("w_in", 1, 2)])
    gw["w_in"] = cargo.hosted(lambda carry: _weight_grad(s["h"], dproj, "w_in", f"in_dw{tag}", carry), [("w_up", 1, 2)])
    dx0, d_g, d_sc1, d_sh1 = _rowwise_bwd(_f_prenorm, [x], [vec(p["g_pre_mix"]), sc1, sh1], [[dh]], [f32], True,
                                          256, f"prenorm_mix_bwd{tag}", add_to={0: dx1})
    gp["g_pre_mix"] = d_g
    return dx0, [d_sh1, d_sc1, d_gt1, d_sh2, d_sc2, d_gt2], gw, gp


COL_SHARDED = ("w_in", "w_up", "w_ssm_out", "w_attn_out", "w_conv_out")
ROW_SHARDED = ("w_down", "w_o", "w_glu")
SMALL_SHARDED = ("conv_mix_w", "ffn_conv_w")
LIGHT_SHARDED = ("w_ssm_out", "w_attn_out", "w_conv_out", "w_glu") + SMALL_SHARDED
REPLICATED = ("b_mod", "g_pre_mix", "g_post_mix", "g_pre_ffn", "g_post_ffn", "ssm_log_dt", "ssm_a_re", "ssm_a_im", "ssm_b_re",
              "ssm_b_im", "ssm_c_re", "ssm_c_im", "ssm_d", "b_glu", "b_gate")
WEIGHTS = ('w_mod', 'b_mod', 'g_pre_mix', 'g_post_mix', 'g_pre_ffn', 'g_post_ffn', 'w_in', 'ssm_log_dt', 'ssm_a_re', 'ssm_a_im',
           'ssm_b_re', 'ssm_b_im', 'ssm_c_re', 'ssm_c_im', 'ssm_d', 'w_glu', 'b_glu', 'conv_mix_w', 'w_ssm_out', 'w_attn_out',
           'w_conv_out', 'b_gate', 'w_o', 'w_up', 'ffn_conv_w', 'w_down')


def _full_cols(g):
    return g.transpose(1, 0, 2).reshape(g.shape[1], -1)


def _col_slabs(m):
    return m.reshape(m.shape[0], N_DEV, -1).transpose(1, 0, 2)


def _step(x, c, loss_target, wts, m_in, v_in):
    depth = wts["w_in"].shape[0]
    n_rows, d = x.shape[1], x.shape[2]
    me = _linear(_position())
    x2d, target2d = x[0], loss_target[0]

    (c_all,) = _all_gather([c], "gather_c")
    c_all = c_all.reshape(N_DEV, d)
    mod_cols = wts["w_mod"].shape[2]
    b_cols = lax.dynamic_slice_in_dim(wts["b_mod"], me * mod_cols, mod_cols, axis=1).reshape(depth, 1, mod_cols)
    mod_mine = _mod_fwd(c_all, wts["w_mod"], b_cols, "mod_fwd")
    (mod_all,) = _all_gather([mod_mine], "gather_mod")
    mod_me = lax.dynamic_index_in_dim(mod_all, me, axis=2, keepdims=False).transpose(1, 0, 2).reshape(depth, 6, 1, d)

    mats = []
    for l in range(depth):
        names = COL_SHARDED + ROW_SHARDED
        shards = [(wts[k][l].T if k in COL_SHARDED else wts[k][l]).astype(bf16) for k in names]
        got = _all_gather_relayed(shards, f"gather_w{l}")
        got = [lax.dynamic_update_index_in_dim(g, shard, me, axis=0) for g, shard in zip(got, shards)]
        mats.append({k: g.reshape(-1, g.shape[2]) for k, g in zip(names, got)})
    (conv_mix_all, ffn_conv_all) = _all_gather([wts["conv_mix_w"], wts["ffn_conv_w"]], "gather_conv_w")
    small = []
    for l in range(depth):
        p = {k: wts[k][l] for k in REPLICATED if k != "b_mod"}
        p["conv_mix_w"] = _full_cols(conv_mix_all[:, l])
        p["ffn_conv_w"] = _full_cols(ffn_conv_all[:, l])
        small.append(p)

    saved = []
    h = x2d
    for l in range(depth):
        mod = [mod_me[l, i] for i in range(6)]
        h, s = _layer_fwd(h, mod, mats[l], small[l], f"_l{l}")
        saved.append(s)
    dy, loss_part = _loss_head(h, target2d, "loss_head")
    loss = lax.psum(loss_part[0, 0], ("x", "y", "c"))

    grads = {k: [None] * depth for k in WEIGHTS}
    dmod_rows = [None] * depth
    small_parts = [None] * depth
    sharded = COL_SHARDED + ROW_SHARDED + SMALL_SHARDED
    waiting = None
    for l in reversed(range(depth)):
        mod = [mod_me[l, i] for i in range(6)]
        cargo = _Cargo(waiting)
        dy, dmod, gw, gp = _layer_bwd(dy, saved[l], mod, mats[l], small[l], f"_l{l}", cargo)
        if waiting is not None:
            summed = _total_sums([waiting[k] for k in sharded], [cargo.arrived[k] for k in sharded], f"scatter_g{l + 1}")
            for k, g in zip(sharded, summed):
                grads[k][l + 1] = g
        dmod_rows[l] = jnp.concatenate(dmod, axis=1)
        slabs = [gw[k].reshape(N_DEV, -1, gw[k].shape[1]) for k in COL_SHARDED + ROW_SHARDED]
        slabs += [_col_slabs(gp[k]) for k in SMALL_SHARDED]
        waiting = dict(zip(sharded, _chip_sums(slabs, f"scatter_g{l}")))
        small_parts[l] = gp
    last = [waiting[k] for k in sharded]
    for k, g in zip(sharded, _total_sums(last, _chip_exchange(last, "scatter_g0_chips"), "scatter_g0")):
        grads[k][0] = g
    grad_x = dy.reshape(x.shape)

    dmod_mine = jnp.concatenate(dmod_rows, axis=0)
    rep_names = [k for k in REPLICATED if k != "b_mod"]
    pieces = [dmod_mine.reshape(-1)] + [jnp.stack([small_parts[l][k].reshape(-1) for l in range(depth)]).reshape(-1) for k in rep_names]
    sizes = [int(pc.shape[0]) for pc in pieces]
    total = sum(sizes)
    padded = -(-total // (SUBLANES * LANES)) * (SUBLANES * LANES)
    pack = jnp.concatenate(pieces + [jnp.zeros((padded - total,), f32)]).reshape(-1, LANES)
    (pack_all,) = _all_gather([pack], "gather_small_grads")
    pack_sum = _sum_slabs(pack_all, "sum_small_grads").reshape(-1)
    offs = np.cumsum([0] + sizes)
    grads["b_mod"] = pack_sum[offs[0]:offs[1]].reshape(wts["b_mod"].shape)
    for i, k in enumerate(rep_names):
        grads[k] = pack_sum[offs[i + 1]:offs[i + 2]].reshape(wts[k].shape)
    dmod_all = pack_all.reshape(N_DEV, -1)[:, :sizes[0]].reshape(N_DEV, depth, 6 * d)
    dmod_cols = lax.dynamic_slice_in_dim(dmod_all, me * mod_cols, mod_cols, axis=2).transpose(1, 0, 2)
    grads["w_mod"] = _mod_wgrad(c_all.T, dmod_cols, "mod_wgrad")
    for k in COL_SHARDED + ROW_SHARDED + SMALL_SHARDED:
        grads[k] = jnp.stack([g.T for g in grads[k]] if k in COL_SHARDED else grads[k])

    delta, new_m, new_v = {}, {}, {}
    for k in WEIGHTS:
        delta[k], new_m[k], new_v[k] = _adamw(wts[k], grads[k], m_in[k], v_in[k], f"adamw_{k}")
    return (loss, grad_x, *[grads[k] for k in WEIGHTS], *[delta[k] for k in WEIGHTS], *[new_m[k] for k in WEIGHTS],
            *[new_v[k] for k in WEIGHTS])


def kernel(x, c, w_mod, b_mod, g_pre_mix, g_post_mix, g_pre_ffn, g_post_ffn, w_in, ssm_log_dt, ssm_a_re, ssm_a_im, ssm_b_re, ssm_b_im, ssm_c_re, ssm_c_im, ssm_d, w_glu, b_glu, conv_mix_w, w_ssm_out, w_attn_out, w_conv_out, b_gate, w_o, w_up, ffn_conv_w, w_down, loss_target, m_w_mod, m_b_mod, m_g_pre_mix, m_g_post_mix, m_g_pre_ffn, m_g_post_ffn, m_w_in, m_ssm_log_dt, m_ssm_a_re, m_ssm_a_im, m_ssm_b_re, m_ssm_b_im, m_ssm_c_re, m_ssm_c_im, m_ssm_d, m_w_glu, m_b_glu, m_conv_mix_w, m_w_ssm_out, m_w_attn_out, m_w_conv_out, m_b_gate, m_w_o, m_w_up, m_ffn_conv_w, m_w_down, v_w_mod, v_b_mod, v_g_pre_mix, v_g_post_mix, v_g_pre_ffn, v_g_post_ffn, v_w_in, v_ssm_log_dt, v_ssm_a_re, v_ssm_a_im, v_ssm_b_re, v_ssm_b_im, v_ssm_c_re, v_ssm_c_im, v_ssm_d, v_w_glu, v_b_glu, v_conv_mix_w, v_w_ssm_out, v_w_attn_out, v_w_conv_out, v_b_gate, v_w_o, v_w_up, v_ffn_conv_w, v_w_down):
    wts = dict(w_mod=w_mod, b_mod=b_mod, g_pre_mix=g_pre_mix, g_post_mix=g_post_mix, g_pre_ffn=g_pre_ffn, g_post_ffn=g_post_ffn, w_in=w_in, ssm_log_dt=ssm_log_dt, ssm_a_re=ssm_a_re, ssm_a_im=ssm_a_im, ssm_b_re=ssm_b_re, ssm_b_im=ssm_b_im, ssm_c_re=ssm_c_re, ssm_c_im=ssm_c_im, ssm_d=ssm_d, w_glu=w_glu, b_glu=b_glu, conv_mix_w=conv_mix_w, w_ssm_out=w_ssm_out, w_attn_out=w_attn_out, w_conv_out=w_conv_out, b_gate=b_gate, w_o=w_o, w_up=w_up, ffn_conv_w=ffn_conv_w, w_down=w_down)
    m_in = dict(w_mod=m_w_mod, b_mod=m_b_mod, g_pre_mix=m_g_pre_mix, g_post_mix=m_g_post_mix, g_pre_ffn=m_g_pre_ffn, g_post_ffn=m_g_post_ffn, w_in=m_w_in, ssm_log_dt=m_ssm_log_dt, ssm_a_re=m_ssm_a_re, ssm_a_im=m_ssm_a_im, ssm_b_re=m_ssm_b_re, ssm_b_im=m_ssm_b_im, ssm_c_re=m_ssm_c_re, ssm_c_im=m_ssm_c_im, ssm_d=m_ssm_d, w_glu=m_w_glu, b_glu=m_b_glu, conv_mix_w=m_conv_mix_w, w_ssm_out=m_w_ssm_out, w_attn_out=m_w_attn_out, w_conv_out=m_w_conv_out, b_gate=m_b_gate, w_o=m_w_o, w_up=m_w_up, ffn_conv_w=m_ffn_conv_w, w_down=m_w_down)
    v_in = dict(w_mod=v_w_mod, b_mod=v_b_mod, g_pre_mix=v_g_pre_mix, g_post_mix=v_g_post_mix, g_pre_ffn=v_g_pre_ffn, g_post_ffn=v_g_post_ffn, w_in=v_w_in, ssm_log_dt=v_ssm_log_dt, ssm_a_re=v_ssm_a_re, ssm_a_im=v_ssm_a_im, ssm_b_re=v_ssm_b_re, ssm_b_im=v_ssm_b_im, ssm_c_re=v_ssm_c_re, ssm_c_im=v_ssm_c_im, ssm_d=v_ssm_d, w_glu=v_w_glu, b_glu=v_b_glu, conv_mix_w=v_conv_mix_w, w_ssm_out=v_w_ssm_out, w_attn_out=v_w_attn_out, w_conv_out=v_w_conv_out, b_gate=v_b_gate, w_o=v_w_o, w_up=v_w_up, ffn_conv_w=v_ffn_conv_w, w_down=v_w_down)
    return _step(x, c, loss_target, wts, m_in, v_in)
```

```python
import functools
import math

import numpy as np
import jax
import jax.numpy as jnp
from jax import lax
from jax.experimental import pallas as pl
from jax.experimental.pallas import tpu as pltpu

f32 = jnp.float32
bf16 = jnp.bfloat16

N_DEV = 8
N_CHIPS = 4
V7X_VMEM_LIMIT_BYTES = 56 * 1024 * 1024
LANES = 128
SUBLANES = 8
BF16_ROWS = 16

RMS_EPS = 1e-6
NEG_INF = -1e30
HEAD_DIM = 64
SSM_GROUP = 16
SSM_STATE = 64
DSWA_PATTERNS = ((128, 1), (512, 4), (2048, 16))
QBLK = 128

ADAM_LR = 0.001
ADAM_B1 = 0.9
ADAM_B2 = 0.999
ADAM_EPS = 1e-08
ADAM_WD = 0.01
ADAM_STEP = 10

MESH = pl.DeviceIdType.MESH
ANY = pl.BlockSpec(memory_space=pl.ANY)


def _tile(n, pref, mult=LANES):
    t = min(pref, n) // mult * mult
    while t >= mult:
        if n % t == 0:
            return t
        t -= mult
    return n


def _params(*sem):
    return pltpu.CompilerParams(dimension_semantics=sem, vmem_limit_bytes=V7X_VMEM_LIMIT_BYTES)


def _mm(a, b, out_dtype, name, rhs_t=False, lhs_t=False, carry=(), gather=False):
    k, m = a.shape if lhs_t else a.shape[::-1]
    n, k2 = b.shape if rhs_t else b.shape[::-1]
    assert k == k2 and a.dtype == bf16 and b.dtype == bf16, (a.shape, b.shape, a.dtype, b.dtype)
    tm, tn, tk = _tile(m, 1024, LANES if lhs_t else SUBLANES), _tile(n, 512), _tile(k, 2048)
    grid = (m // tm, n // tn, k // tk)
    contract = (((0 if lhs_t else 1,), (1 if rhs_t else 0,)), ((), ()))
    nc = len(carry)
    srcs = [job[0] for job in carry]
    ranges = [(job[1], job[2]) for job in carry]
    intos = [(i, job[3]) for i, job in enumerate(carry) if job[3] is not None]
    n_in = 2 + nc + len(intos)

    def body(*refs):
        a_ref, b_ref = refs[:2]
        o_ref = refs[n_in]
        acc_ref = refs[n_in + 1 + nc]
        steps = [pl.program_id(axis) for axis in range(3)]
        if nc:
            rides = (refs[2:2 + nc], refs[n_in + 1:n_in + 1 + nc], refs[n_in + 2 + nc], refs[n_in + 3 + nc])
            copies = _gather_first_copies(*rides) if gather else _chip_copies(*rides, ranges)

            @pl.when((steps[0] == 0) & (steps[1] == 0) & (steps[2] == 0))
            def _():
                for cp in copies:
                    cp.start()

        @pl.when(steps[2] == 0)
        def _():
            acc_ref[...] = jnp.zeros_like(acc_ref)

        acc_ref[...] += lax.dot_general(a_ref[...], b_ref[...], contract, preferred_element_type=f32)

        @pl.when(steps[2] == grid[2] - 1)
        def _():
            o_ref[...] = acc_ref[...].astype(o_ref.dtype)

        if nc:
            @pl.when((steps[0] == grid[0] - 1) & (steps[1] == grid[1] - 1) & (steps[2] == grid[2] - 1))
            def _():
                for cp in copies:
                    cp.wait_recv()
                for cp in copies:
                    cp.wait_send()

    a_spec = pl.BlockSpec((tk, tm), lambda i, j, kk: (kk, i)) if lhs_t else pl.BlockSpec((tm, tk), lambda i, j, kk: (i, kk))
    b_spec = pl.BlockSpec((tn, tk), lambda i, j, kk: (j, kk)) if rhs_t else pl.BlockSpec((tk, tn), lambda i, j, kk: (kk, j))
    sems = pltpu.SemaphoreType.DMA((max(nc, 1), N_CHIPS - 1))
    res = pl.pallas_call(
        body, grid=grid,
        in_specs=[a_spec, b_spec] + [ANY] * (nc + len(intos)),
        out_specs=[pl.BlockSpec((tm, tn), lambda i, j, kk: (i, j))] + [ANY] * nc,
        out_shape=[jax.ShapeDtypeStruct((m, n), out_dtype)]
        + [jax.ShapeDtypeStruct(((N_DEV,) + x.shape) if gather else ((N_CHIPS - 1,) + x.shape[1:]), x.dtype) for x in srcs],
        scratch_shapes=[pltpu.VMEM((tm, tn), f32)] + ([sems, sems] if nc else []),
        input_output_aliases={2 + nc + j: 1 + i for j, (i, _) in enumerate(intos)},
        compiler_params=_params(*(("arbitrary",) * 3 if nc else ("parallel", "parallel", "arbitrary"))), name=name)(
            a, b, *srcs, *[buf for _, buf in intos])
    return (res[0], list(res[1:])) if nc else res[0]


def _mm_sections(a, b_t, sections, name):
    m, k = a.shape
    n = b_t.shape[0]
    assert b_t.shape[1] == k and sum(w for w, _ in sections) == n
    tn = _tile(math.gcd(*[w for w, _ in sections]), 512)
    tm, tk = _tile(m, 1024, SUBLANES), _tile(k, 2048)
    grid = (m // tm, n // tn, k // tk)
    bounds, lo = [], 0
    for w, _ in sections:
        bounds.append((lo, lo + w // tn))
        lo += w // tn

    def body(a_ref, b_ref, *rest):
        o_refs, acc_ref = rest[:-1], rest[-1]
        j, kk = pl.program_id(1), pl.program_id(2)

        @pl.when(kk == 0)
        def _():
            acc_ref[...] = jnp.zeros_like(acc_ref)

        acc_ref[...] += lax.dot_general(a_ref[...], b_ref[...], (((1,), (1,)), ((), ())), preferred_element_type=f32)
        for o_ref, (first, end) in zip(o_refs, bounds):
            @pl.when((kk == grid[2] - 1) & (j >= first) & (j < end))
            def _(o_ref=o_ref):
                o_ref[...] = acc_ref[...].astype(o_ref.dtype)

    def out_spec(first, end):
        return pl.BlockSpec((tm, tn), lambda i, j, kk: (i, jnp.clip(j - first, 0, end - first - 1)))

    return pl.pallas_call(
        body, grid=grid,
        in_specs=[pl.BlockSpec((tm, tk), lambda i, j, kk: (i, kk)), pl.BlockSpec((tn, tk), lambda i, j, kk: (j, kk))],
        out_specs=[out_spec(first, end) for first, end in bounds],
        out_shape=[jax.ShapeDtypeStruct((m, w), dt) for w, dt in sections],
        scratch_shapes=[pltpu.VMEM((tm, tn), f32)],
        compiler_params=_params("arbitrary", "arbitrary", "arbitrary"), name=name)(a, b_t)


def _rowwise(fn, rows, vecs, out_dtypes, tm, name):
    n_rows = rows[0].shape[0]
    nr, nv = len(rows), len(vecs)
    outs = jax.eval_shape(fn, *[jax.ShapeDtypeStruct((tm, r.shape[1]), f32) for r in rows],
                          *[jax.ShapeDtypeStruct(v.shape, f32) for v in vecs])

    def body(*refs):
        vals = [r[...].astype(f32) for r in refs[:nr]] + [v[...] for v in refs[nr:nr + nv]]
        for o_ref, val in zip(refs[nr + nv:], fn(*vals)):
            o_ref[...] = val.astype(o_ref.dtype)

    res = pl.pallas_call(
        body, grid=(n_rows // tm,),
        in_specs=[pl.BlockSpec((tm, r.shape[1]), lambda i: (i, 0)) for r in rows]
        + [pl.BlockSpec(v.shape, lambda i: (0, 0)) for v in vecs],
        out_specs=[pl.BlockSpec((tm, o.shape[1]), lambda i: (i, 0)) for o in outs],
        out_shape=[jax.ShapeDtypeStruct((n_rows, o.shape[1]), dt) for o, dt in zip(outs, out_dtypes)],
        compiler_params=_params("parallel"), name=name)(*rows, *vecs)
    return res


def _rowwise_bwd(fn, rows, vecs, cots, row_dtypes, need_vecs, tm, name, add_to=None):
    n_rows = rows[0].shape[0]
    nr, nv = len(rows), len(vecs)
    add_to = add_to or {}
    flat_cots = [c for group in cots for c in group]
    add_keys = sorted(add_to)
    row_out = [i for i, dt in enumerate(row_dtypes) if dt is not None]
    n_in = nr + nv + len(flat_cots) + len(add_keys)

    def body(*refs):
        vals = [r[...].astype(f32) for r in refs[:nr]] + [v[...] for v in refs[nr:nr + nv]]
        pos = nr + nv
        cvals = []
        for group in cots:
            acc = refs[pos][...].astype(f32)
            for extra in range(1, len(group)):
                acc = acc + refs[pos + extra][...].astype(f32)
            pos += len(group)
            cvals.append(acc)
        adds = {key: refs[pos + j] for j, key in enumerate(add_keys)}
        _, vjp = jax.vjp(fn, *vals)
        grads = vjp(tuple(cvals))
        out_refs = refs[n_in:]
        for j, i in enumerate(row_out):
            g = grads[i]
            if i in adds:
                g = g + adds[i][...].astype(f32)
            out_refs[j][...] = g.astype(out_refs[j].dtype)
        if need_vecs:
            step = pl.program_id(0)
            for j in range(nv):
                acc_ref = out_refs[len(row_out) + j]

                @pl.when(step == 0)
                def _(acc_ref=acc_ref):
                    acc_ref[...] = jnp.zeros_like(acc_ref)

                acc_ref[...] += grads[nr + j]

    row_spec = lambda a: pl.BlockSpec((tm, a.shape[1]), lambda i: (i, 0))
    in_specs = ([row_spec(r) for r in rows] + [pl.BlockSpec(v.shape, lambda i: (0, 0)) for v in vecs]
                + [row_spec(c) for c in flat_cots] + [row_spec(add_to[key]) for key in add_keys])
    out_specs = [row_spec(rows[i]) for i in row_out]
    out_shape = [jax.ShapeDtypeStruct(rows[i].shape, row_dtypes[i]) for i in row_out]
    if need_vecs:
        out_specs += [pl.BlockSpec(v.shape, lambda i: (0, 0)) for v in vecs]
        out_shape += [jax.ShapeDtypeStruct(v.shape, f32) for v in vecs]
    return pl.pallas_call(
        body, grid=(n_rows // tm,), in_specs=in_specs, out_specs=out_specs, out_shape=out_shape,
        compiler_params=_params("arbitrary"), name=name)(*rows, *vecs, *flat_cots, *[add_to[key] for key in add_keys])


def _f_prenorm(x, g, sc, sh):
    y = x * lax.rsqrt(jnp.mean(x * x, axis=-1, keepdims=True) + RMS_EPS)
    return (y * g * (1.0 + sc) + sh,)


def _f_postres(x, y, g, gt):
    n = y * lax.rsqrt(jnp.mean(y * y, axis=-1, keepdims=True) + RMS_EPS)
    return (x + gt * (n * g),)


def _f_gelu(yc, u, d):
    return (jax.nn.gelu(yc + d * u),)


def _f_glu(g, z, b):
    return (g * jax.nn.sigmoid(z + b),)


def _f_merge(p0, p1, p2, ys, ya, yc, b0, b1, b2):
    return (jax.nn.sigmoid(p0 + b0) * ys + jax.nn.sigmoid(p1 + b1) * ya + jax.nn.sigmoid(p2 + b2) * yc,)


def _f_amerge(o0, o1, o2, l0, l1, l2):
    m = jnp.maximum(jnp.maximum(l0, l1), l2)
    e0, e1, e2 = jnp.exp(l0 - m), jnp.exp(l1 - m), jnp.exp(l2 - m)
    return ((e0 * o0 + e1 * o1 + e2 * o2) / (e0 + e1 + e2),)


def _f_add(a, b):
    return (a + b,)


def _shift_down(z, k, row):
    return jnp.where(row >= k, pltpu.roll(z, k, 0), 0.0)


def _shift_up(z, k, row):
    n = z.shape[0]
    return jnp.where(row < n - k, pltpu.roll(z, n - k, 0), 0.0)


def _conv3(z, w, row):
    return w[0:1, :] * z + w[1:2, :] * _shift_down(z, 1, row) + w[2:3, :] * _shift_down(z, 2, row)


def _conv3_bwd(z, w, dy, row):
    dz = w[0:1, :] * dy + w[1:2, :] * _shift_up(dy, 1, row) + w[2:3, :] * _shift_up(dy, 2, row)
    dw = [jnp.sum(dy * z, axis=0, keepdims=True),
          jnp.sum(dy * _shift_down(z, 1, row), axis=0, keepdims=True),
          jnp.sum(dy * _shift_down(z, 2, row), axis=0, keepdims=True)]
    return dz, dw


def _convgate_fwd(proj, w, off, width, name):
    n_rows = proj.shape[0]
    tc = _tile(width, 256)
    nb = width // tc
    base = off // tc

    def body(b_ref, c_ref, h_ref, w_ref, o_ref):
        row = lax.broadcasted_iota(jnp.int32, (n_rows, tc), 0)
        o_ref[...] = (b_ref[...] * _conv3(c_ref[...] * h_ref[...], w_ref[...], row)).astype(o_ref.dtype)

    col = lambda s: pl.BlockSpec((n_rows, tc), lambda j: (0, base + s * nb + j))
    return pl.pallas_call(
        body, grid=(nb,), in_specs=[col(0), col(1), col(2), pl.BlockSpec((3, tc), lambda j: (0, j))],
        out_specs=pl.BlockSpec((n_rows, tc), lambda j: (0, j)),
        out_shape=jax.ShapeDtypeStruct((n_rows, width), bf16),
        compiler_params=_params("parallel"), name=name)(proj, proj, proj, w)


def _convgate_bwd(proj, w, dcm, off, width, name):
    n_rows = proj.shape[0]
    tc = _tile(width, 128)
    nb = width // tc
    base = off // tc

    def body(b_ref, c_ref, h_ref, w_ref, d_ref, db_ref, dc_ref, dh_ref, dw_ref):
        row = lax.broadcasted_iota(jnp.int32, (n_rows, tc), 0)
        cb, cc, ch, wv, d = b_ref[...], c_ref[...], h_ref[...], w_ref[...], d_ref[...].astype(f32)
        z = cc * ch
        db_ref[...] = (d * _conv3(z, wv, row)).astype(db_ref.dtype)
        dz, dw = _conv3_bwd(z, wv, d * cb, row)
        dc_ref[...] = (dz * ch).astype(dc_ref.dtype)
        dh_ref[...] = (dz * cc).astype(dh_ref.dtype)
        for k in range(3):
            dw_ref[k:k + 1, :] = dw[k]

    col = lambda s: pl.BlockSpec((n_rows, tc), lambda j: (0, base + s * nb + j))
    own = pl.BlockSpec((n_rows, tc), lambda j: (0, j))
    wspec = pl.BlockSpec((3, tc), lambda j: (0, j))
    sec = jax.ShapeDtypeStruct((n_rows, width), bf16)
    return pl.pallas_call(
        body, grid=(nb,), in_specs=[col(0), col(1), col(2), wspec, own], out_specs=[own, own, own, wspec],
        out_shape=[sec, sec, sec, jax.ShapeDtypeStruct((3, width), f32)],
        compiler_params=_params("parallel"), name=name)(proj, proj, proj, w, dcm)


def _ffnconv_fwd(up, w, name):
    n_rows, two_f = up.shape
    half = two_f // 2
    tc = _tile(half, 256)
    nb = half // tc

    def body(a_ref, b_ref, wa_ref, wb_ref, o_ref):
        row = lax.broadcasted_iota(jnp.int32, (n_rows, tc), 0)
        ua = _conv3(a_ref[...], wa_ref[...], row)
        ub = _conv3(b_ref[...], wb_ref[...], row)
        o_ref[...] = (jax.nn.silu(ua) * ub).astype(o_ref.dtype)

    return pl.pallas_call(
        body, grid=(nb,),
        in_specs=[pl.BlockSpec((n_rows, tc), lambda j: (0, j)), pl.BlockSpec((n_rows, tc), lambda j: (0, nb + j)),
                  pl.BlockSpec((3, tc), lambda j: (0, j)), pl.BlockSpec((3, tc), lambda j: (0, nb + j))],
        out_specs=pl.BlockSpec((n_rows, tc), lambda j: (0, j)),
        out_shape=jax.ShapeDtypeStruct((n_rows, half), bf16),
        compiler_params=_params("parallel"), name=name)(up, up, w, w)


def _ffnconv_bwd(up, w, dact, name):
    n_rows, two_f = up.shape
    half = two_f // 2
    tc = _tile(half, 128)
    nb = half // tc

    def body(a_ref, b_ref, wa_ref, wb_ref, d_ref, da_ref, db_ref, dwa_ref, dwb_ref):
        row = lax.broadcasted_iota(jnp.int32, (n_rows, tc), 0)
        a, b, wa, wb, d = a_ref[...], b_ref[...], wa_ref[...], wb_ref[...], d_ref[...].astype(f32)
        ua = _conv3(a, wa, row)
        ub = _conv3(b, wb, row)
        sg = jax.nn.sigmoid(ua)
        d_ua = d * ub * (sg * (1.0 + ua * (1.0 - sg)))
        d_ub = d * (ua * sg)
        da, dwa = _conv3_bwd(a, wa, d_ua, row)
        db, dwb = _conv3_bwd(b, wb, d_ub, row)
        da_ref[...] = da.astype(da_ref.dtype)
        db_ref[...] = db.astype(db_ref.dtype)
        for k in range(3):
            dwa_ref[k:k + 1, :] = dwa[k]
            dwb_ref[k:k + 1, :] = dwb[k]

    lo = pl.BlockSpec((n_rows, tc), lambda j: (0, j))
    hi = pl.BlockSpec((n_rows, tc), lambda j: (0, nb + j))
    wlo = pl.BlockSpec((3, tc), lambda j: (0, j))
    whi = pl.BlockSpec((3, tc), lambda j: (0, nb + j))
    sec = jax.ShapeDtypeStruct((n_rows, half), bf16)
    wsec = jax.ShapeDtypeStruct((3, half), f32)
    da, db, dwa, dwb = pl.pallas_call(
        body, grid=(nb,), in_specs=[lo, hi, wlo, whi, lo], out_specs=[lo, lo, wlo, wlo],
        out_shape=[sec, sec, wsec, wsec], compiler_params=_params("parallel"), name=name)(up, up, w, w, dact)
    return da, db, jnp.concatenate([dwa, dwb], axis=1)


SCAN_TC = 256


def _cmul(ar, ai, br, bi):
    return ar * br - ai * bi, ar * bi + ai * br


def _scan(x, lam, adjoint, name):
    n_rows, two_s = x.shape
    tc = min(SCAN_TC, two_s // 2)
    n_tiles = two_s // (2 * tc)
    n_chunks = n_rows // SUBLANES

    def body(x_ref, lam_ref, hf_ref, hb_ref):
        lr = lam_ref[:, 0:tc]
        li = lam_ref[:, tc:2 * tc]
        if adjoint:
            li = -li
        row = lax.broadcasted_iota(jnp.int32, (SUBLANES, tc), 0)
        powers = [(lr, li)]
        for _ in range(SUBLANES - 1):
            powers.append(_cmul(powers[-1][0], powers[-1][1], lr, li))
        pr = jnp.zeros((SUBLANES, tc), f32)
        pi = jnp.zeros((SUBLANES, tc), f32)
        for t in range(SUBLANES):
            e = (SUBLANES - 1 - t) if adjoint else t
            pr = jnp.where(row == t, powers[e][0], pr)
            pi = jnp.where(row == t, powers[e][1], pi)
        last = 0 if adjoint else SUBLANES - 1

        def chunk(i, carry):
            cr, ci = carry
            c = (n_chunks - 1 - i) if adjoint else i
            rows = pl.ds(pl.multiple_of(c * SUBLANES, SUBLANES), SUBLANES)
            xr = x_ref[rows, 0:tc]
            xi = x_ref[rows, tc:2 * tc]
            for k in (1, 2, 4):
                ar, ai = powers[k - 1]
                if adjoint:
                    sr = jnp.where(row < SUBLANES - k, pltpu.roll(xr, SUBLANES - k, 0), 0.0)
                    si = jnp.where(row < SUBLANES - k, pltpu.roll(xi, SUBLANES - k, 0), 0.0)
                else:
                    sr = jnp.where(row >= k, pltpu.roll(xr, k, 0), 0.0)
                    si = jnp.where(row >= k, pltpu.roll(xi, k, 0), 0.0)
                xr, xi = xr + ar * sr - ai * si, xi + ar * si + ai * sr
            hr = xr + pr * cr - pi * ci
            hi = xi + pr * ci + pi * cr
            hf_ref[rows, 0:tc] = hr
            hf_ref[rows, tc:2 * tc] = hi
            hb_ref[rows, 0:tc] = hr.astype(bf16)
            hb_ref[rows, tc:2 * tc] = hi.astype(bf16)
            return (jnp.sum(jnp.where(row == last, hr, 0.0), axis=0, keepdims=True),
                    jnp.sum(jnp.where(row == last, hi, 0.0), axis=0, keepdims=True))

        zero = jnp.zeros((1, tc), f32)
        lax.fori_loop(0, n_chunks, chunk, (zero, zero))

    blk = pl.BlockSpec((n_rows, 2 * tc), lambda j: (0, j))
    return pl.pallas_call(
        body, grid=(n_tiles,), in_specs=[blk, pl.BlockSpec((1, 2 * tc), lambda j: (0, j))], out_specs=[blk, blk],
        out_shape=[jax.ShapeDtypeStruct(x.shape, f32), jax.ShapeDtypeStruct(x.shape, bf16)],
        compiler_params=_params("parallel"), name=name)(x, lam)


def _lam_grad(g, h, name):
    n_rows, two_s = g.shape
    tc = min(SCAN_TC, two_s // 2)

    def body(g_ref, h_ref, o_ref):
        row = lax.broadcasted_iota(jnp.int32, (n_rows, tc), 0)
        gr, gi = g_ref[:, 0:tc], g_ref[:, tc:2 * tc]
        hr, hi = _shift_down(h_ref[:, 0:tc], 1, row), _shift_down(h_ref[:, tc:2 * tc], 1, row)
        o_ref[:, 0:tc] = jnp.sum(gr * hr + gi * hi, axis=0, keepdims=True)
        o_ref[:, tc:2 * tc] = jnp.sum(gi * hr - gr * hi, axis=0, keepdims=True)

    blk = pl.BlockSpec((n_rows, 2 * tc), lambda j: (0, j))
    return pl.pallas_call(
        body, grid=(two_s // (2 * tc),), in_specs=[blk, blk], out_specs=pl.BlockSpec((1, 2 * tc), lambda j: (0, j)),
        out_shape=jax.ShapeDtypeStruct((1, two_s), f32), compiler_params=_params("parallel"), name=name)(g, h)


def _interleave(re, im, tc):
    lead = re.shape[:-1]
    s = re.shape[-1]
    return jnp.concatenate([re.reshape(*lead, s // tc, 1, tc), im.reshape(*lead, s // tc, 1, tc)], axis=-2).reshape(*lead, 2 * s)


def _deinterleave(z, tc):
    lead = z.shape[:-1]
    s = z.shape[-1] // 2
    z = z.reshape(*lead, s // tc, 2, tc)
    return z[..., 0, :].reshape(*lead, s), z[..., 1, :].reshape(*lead, s)


def _f_disc(log_dt, ar, ai, br, bi):
    dt = jnp.exp(log_dt)
    mag = jnp.exp(ar * dt)
    lr, li = mag * jnp.cos(ai * dt), mag * jnp.sin(ai * dt)
    den = ar * ar + ai * ai
    fr = ((lr - 1.0) * ar + li * ai) / den
    fi = (li * ar - (lr - 1.0) * ai) / den
    return lr, li, fr[None] * br - fi[None] * bi, fr[None] * bi + fi[None] * br


def _disc_fwd(log_dt, ar, ai, br, bi, name):
    def body(*refs):
        for o_ref, val in zip(refs[5:], _f_disc(*[r[...] for r in refs[:5]])):
            o_ref[...] = val

    sd = jax.ShapeDtypeStruct
    return pl.pallas_call(body, out_shape=[sd(ar.shape, f32), sd(ar.shape, f32), sd(br.shape, f32), sd(br.shape, f32)],
                          name=name)(log_dt, ar, ai, br, bi)


def _disc_bwd(log_dt, ar, ai, br, bi, cots, name):
    def body(*refs):
        _, vjp = jax.vjp(_f_disc, *[r[...] for r in refs[:5]])
        for o_ref, val in zip(refs[9:], vjp(tuple(r[...] for r in refs[5:9]))):
            o_ref[...] = val

    sd = jax.ShapeDtypeStruct
    return pl.pallas_call(
        body, out_shape=[sd(log_dt.shape, f32), sd(ar.shape, f32), sd(ar.shape, f32), sd(br.shape, f32), sd(br.shape, f32)],
        name=name)(log_dt, ar, ai, br, bi, *cots)


def _alibi_bias(n_heads_per_pattern):
    n_heads = len(DSWA_PATTERNS) * n_heads_per_pattern
    slopes = np.array([2.0 ** (-8.0 * (h + 1) / n_heads) for h in range(n_heads)], dtype=np.float32)
    qi = np.arange(QBLK)[:, None]
    kj = np.arange(QBLK)[None, :]
    cur, prev = [], []
    for h in range(n_heads):
        dil = DSWA_PATTERNS[h // n_heads_per_pattern][1]
        d_cur = qi - kj
        d_prev = QBLK + qi - kj
        cur.append(np.where(d_cur >= 0, -slopes[h] * (d_cur * dil).astype(np.float32), NEG_INF))
        prev.append(np.where(d_prev <= QBLK, -slopes[h] * (d_prev * dil).astype(np.float32), NEG_INF))
    return np.stack(cur).astype(np.float32), np.stack(prev).astype(np.float32)


def _blocks_per_residue(head, n_heads_per_pattern, n_blocks):
    pattern = head // n_heads_per_pattern
    out = n_blocks // DSWA_PATTERNS[-1][1]
    for p in range(len(DSWA_PATTERNS) - 2, -1, -1):
        out = jnp.where(pattern == p, n_blocks // DSWA_PATTERNS[p][1], out)
    return out


def _qk(a, b):
    return jnp.einsum('bqe,bke->bqk', a, b, preferred_element_type=f32)


def _pv(p, v):
    return jnp.einsum('bqk,bke->bqe', p.astype(bf16), v, preferred_element_type=f32)


def _ptq(p, a):
    return jnp.einsum('bqk,bqe->bke', p.astype(bf16), a, preferred_element_type=f32)


def _prev_block(t):
    return jnp.concatenate([jnp.zeros((1,) + t.shape[1:], t.dtype), t[:-1]], axis=0)


def _next_block(t):
    return jnp.concatenate([t[1:], jnp.zeros((1,) + t.shape[1:], t.dtype)], axis=0)


def _has_prev(head, hpp, n_blocks):
    blk = lax.broadcasted_iota(jnp.int32, (n_blocks, 1, 1), 0)
    return (blk & (_blocks_per_residue(head, hpp, n_blocks) - 1)) > 0


def _attn_fwd(q, k, v, hpp, name):
    n_heads, n_blocks = q.shape[0], q.shape[1]
    bias_cur, bias_prev = _alibi_bias(hpp)
    scale = HEAD_DIM ** -0.5

    def body(q_ref, k_ref, v_ref, bc_ref, bp_ref, o_ref, lse_ref):
        has_prev = _has_prev(pl.program_id(0), hpp, n_blocks)
        qv, kv, vv = q_ref[...], k_ref[...], v_ref[...]
        s_c = _qk(qv, kv) * scale + bc_ref[...][None]
        s_p = jnp.where(has_prev, _qk(qv, _prev_block(kv)) * scale + bp_ref[...][None], NEG_INF)
        m = jnp.maximum(jnp.max(s_c, axis=-1, keepdims=True), jnp.max(s_p, axis=-1, keepdims=True))
        p_c = jnp.exp(s_c - m)
        p_p = jnp.exp(s_p - m)
        l = jnp.sum(p_c, axis=-1, keepdims=True) + jnp.sum(p_p, axis=-1, keepdims=True)
        o_ref[...] = (_pv(p_c, vv) + _pv(p_p, _prev_block(vv))) / l
        lse_ref[...] = m + jnp.log(l)

    head = pl.BlockSpec((None, n_blocks, QBLK, HEAD_DIM), lambda a: (a, 0, 0, 0))
    bias = pl.BlockSpec((None, QBLK, QBLK), lambda a: (a, 0, 0))
    return pl.pallas_call(
        body, grid=(n_heads,), in_specs=[head, head, head, bias, bias],
        out_specs=[head, pl.BlockSpec((None, n_blocks, QBLK, 1), lambda a: (a, 0, 0, 0))],
        out_shape=[jax.ShapeDtypeStruct(q.shape, f32), jax.ShapeDtypeStruct(q.shape[:3] + (1,), f32)],
        compiler_params=_params("parallel"), name=name)(q, k, v, jnp.asarray(bias_cur), jnp.asarray(bias_prev))


def _attn_bwd(q, k, v, o, lse, do, dlse_x, hpp, name):
    n_heads, n_blocks = q.shape[0], q.shape[1]
    bias_cur, bias_prev = _alibi_bias(hpp)
    scale = HEAD_DIM ** -0.5

    def body(q_ref, k_ref, v_ref, o_ref, l_ref, do_ref, dl_ref, bc_ref, bp_ref, dq_ref, dk_ref, dv_ref):
        has_prev = _has_prev(pl.program_id(0), hpp, n_blocks)
        qv, kv, vv = q_ref[...], k_ref[...], v_ref[...]
        kp, vp = _prev_block(kv), _prev_block(vv)
        dov = do_ref[...]
        do_b = dov.astype(bf16)
        lse = l_ref[...]
        corr = jnp.sum(dl_ref[...], axis=-1, keepdims=True) - jnp.sum(dov * o_ref[...], axis=-1, keepdims=True)
        p_c = jnp.exp(_qk(qv, kv) * scale + bc_ref[...][None] - lse)
        ds_c = p_c * (_qk(do_b, vv) + corr)
        p_p = jnp.where(has_prev, jnp.exp(_qk(qv, kp) * scale + bp_ref[...][None] - lse), 0.0)
        ds_p = p_p * (_qk(do_b, vp) + corr)
        dq_ref[...] = ((_pv(ds_c, kv) + _pv(ds_p, kp)) * scale).astype(dq_ref.dtype)
        dk_ref[...] = ((_ptq(ds_c, qv) + _next_block(_ptq(ds_p, qv))) * scale).astype(dk_ref.dtype)
        dv_ref[...] = (_ptq(p_c, do_b) + _next_block(_ptq(p_p, do_b))).astype(dv_ref.dtype)

    head = pl.BlockSpec((None, n_blocks, QBLK, HEAD_DIM), lambda a: (a, 0, 0, 0))
    head1 = pl.BlockSpec((None, n_blocks, QBLK, 1), lambda a: (a, 0, 0, 0))
    bias = pl.BlockSpec((None, QBLK, QBLK), lambda a: (a, 0, 0))
    out = jax.ShapeDtypeStruct(q.shape, bf16)
    return pl.pallas_call(
        body, grid=(n_heads,), in_specs=[head, head, head, head, head1, head, head, bias, bias],
        out_specs=[head, head, head], out_shape=[out, out, out],
        compiler_params=_params("parallel"), name=name)(q, k, v, o, lse, do, dlse_x, jnp.asarray(bias_cur), jnp.asarray(bias_prev))


def _to_blocks(t, hpp):
    n_rows = t.shape[0]
    t = t.reshape(n_rows, len(DSWA_PATTERNS), hpp, HEAD_DIM)
    out = []
    for p, (_, dil) in enumerate(DSWA_PATTERNS):
        tp = t[:, p].reshape(n_rows // dil, dil, hpp, HEAD_DIM).transpose(2, 1, 0, 3)
        out.append(tp.reshape(hpp, n_rows // QBLK, QBLK, HEAD_DIM))
    return jnp.concatenate(out, axis=0)


def _from_blocks(t, hpp):
    n_blocks, last = t.shape[1], t.shape[3]
    n_rows = n_blocks * QBLK
    out = []
    for p, (_, dil) in enumerate(DSWA_PATTERNS):
        tp = t[p * hpp:(p + 1) * hpp].reshape(hpp, dil, n_rows // dil, last).transpose(2, 1, 0, 3)
        out.append(tp.reshape(n_rows, hpp * last))
    return out


def _loss_head(y, target, name):
    n_rows, d = y.shape
    tm = _tile(n_rows, 256, SUBLANES)

    def body(y_ref, t_ref, dy_ref, loss_ref):
        diff = y_ref[...] - t_ref[...]
        dy_ref[...] = diff * (1.0 / d)

        @pl.when(pl.program_id(0) == 0)
        def _():
            loss_ref[...] = jnp.zeros_like(loss_ref)

        loss_ref[...] += jnp.sum(jnp.sum(diff * diff, axis=1, keepdims=True), axis=0, keepdims=True) * (0.5 / d)

    row = pl.BlockSpec((tm, d), lambda i: (i, 0))
    return pl.pallas_call(
        body, grid=(n_rows // tm,), in_specs=[row, row], out_specs=[row, pl.BlockSpec((1, 1), lambda i: (0, 0))],
        out_shape=[jax.ShapeDtypeStruct(y.shape, f32), jax.ShapeDtypeStruct((1, 1), f32)],
        compiler_params=_params("arbitrary"), name=name)(y, target)


def _as2d(a):
    if a.ndim >= 2 and a.shape[-1] >= LANES:
        return a.reshape(-1, a.shape[-1])
    return a.reshape(-1, LANES) if a.size % LANES == 0 else a.reshape(1, -1)


ELEMENTWISE_BLOCK_BYTES = 2 << 20


def _row_tile(n_rows, n_cols):
    return _tile(n_rows, max(SUBLANES, ELEMENTWISE_BLOCK_BYTES // (4 * n_cols)), SUBLANES)


def _adamw(w, g, m, v, name):
    shape = w.shape
    w2, g2, m2, v2 = _as2d(w), _as2d(g), _as2d(m), _as2d(v)
    n_rows, n_cols = w2.shape
    tm = _row_tile(n_rows, n_cols)

    def body(w_ref, g_ref, m_ref, v_ref, d_ref, mo_ref, vo_ref):
        gv = g_ref[...]
        mn = ADAM_B1 * m_ref[...] + (1.0 - ADAM_B1) * gv
        vn = ADAM_B2 * v_ref[...] + (1.0 - ADAM_B2) * jnp.square(gv)
        m_hat = mn / (1.0 - ADAM_B1 ** ADAM_STEP)
        v_hat = vn / (1.0 - ADAM_B2 ** ADAM_STEP)
        d_ref[...] = -ADAM_LR * (m_hat / (jnp.sqrt(v_hat) + ADAM_EPS) + ADAM_WD * w_ref[...])
        mo_ref[...] = mn
        vo_ref[...] = vn

    row = pl.BlockSpec((tm, n_cols), lambda i: (i, 0))
    out = jax.ShapeDtypeStruct(w2.shape, f32)
    d, mn, vn = pl.pallas_call(body, grid=(n_rows // tm,), in_specs=[row] * 4, out_specs=[row] * 3, out_shape=[out] * 3,
                               compiler_params=_params("parallel"), name=name)(w2, g2, m2, v2)
    return d.reshape(shape), mn.reshape(shape), vn.reshape(shape)


def _sum_slabs(x, name):
    n = x.shape[0]
    shape = x.shape[1:]
    x3 = x.reshape(n, -1, shape[-1])
    n_rows, n_cols = x3.shape[1:]
    tm = _row_tile(n_rows, n_cols)

    def body(x_ref, o_ref):
        acc = x_ref[0].astype(f32)
        for s in range(1, n):
            acc = acc + x_ref[s].astype(f32)
        o_ref[...] = acc

    out = pl.pallas_call(
        body, grid=(n_rows // tm,), in_specs=[pl.BlockSpec((n, tm, n_cols), lambda i: (0, i, 0))],
        out_specs=pl.BlockSpec((tm, n_cols), lambda i: (i, 0)), out_shape=jax.ShapeDtypeStruct((n_rows, n_cols), f32),
        compiler_params=_params("parallel"), name=name)(x3)
    return out.reshape(shape)


def _mod_fwd(c_all, w_mod, b_cols, name):
    depth, d, cols = w_mod.shape
    tn = _tile(cols, 512)

    def body(c_ref, w_ref, b_ref, o_ref):
        cond = jax.nn.silu(c_ref[...])
        o_ref[...] = jnp.dot(cond, w_ref[...], preferred_element_type=f32, precision=lax.Precision.HIGHEST) + b_ref[...]

    return pl.pallas_call(
        body, grid=(depth, cols // tn),
        in_specs=[pl.BlockSpec((N_DEV, d), lambda l, j: (0, 0)), pl.BlockSpec((None, d, tn), lambda l, j: (l, 0, j)),
                  pl.BlockSpec((None, 1, tn), lambda l, j: (l, 0, j))],
        out_specs=pl.BlockSpec((None, N_DEV, tn), lambda l, j: (l, 0, j)),
        out_shape=jax.ShapeDtypeStruct((depth, N_DEV, cols), f32),
        compiler_params=_params("parallel", "parallel"), name=name)(c_all, w_mod, b_cols)


def _mod_wgrad(c_all_t, dmod_cols, name):
    d = c_all_t.shape[0]
    depth, _, cols = dmod_cols.shape
    tm = _tile(d, 256, SUBLANES)

    def body(c_ref, g_ref, o_ref):
        cond = jax.nn.silu(c_ref[...])
        o_ref[...] = jnp.dot(cond, g_ref[...], preferred_element_type=f32, precision=lax.Precision.HIGHEST)

    return pl.pallas_call(
        body, grid=(depth, d // tm),
        in_specs=[pl.BlockSpec((tm, N_DEV), lambda l, i: (i, 0)), pl.BlockSpec((None, N_DEV, cols), lambda l, i: (l, 0, 0))],
        out_specs=pl.BlockSpec((None, tm, cols), lambda l, i: (l, i, 0)),
        out_shape=jax.ShapeDtypeStruct((depth, d, cols), f32),
        compiler_params=_params("parallel", "parallel"), name=name)(c_all_t, dmod_cols)


def _position():
    return lax.axis_index("x"), lax.axis_index("y"), lax.axis_index("c")


def _linear(p):
    return 4 * p[0] + 2 * p[1] + p[2]


def _all_gather(xs, name):
    n = len(xs)

    def body(*refs):
        x_refs, o_refs = refs[:n], refs[n:2 * n]
        send_sems, recv_sems, local_sems = refs[2 * n:]
        x, y, c = _position()
        me, sibling = (x, y, c), (x, y, 1 - c)
        chips = [(1 - x, y), (x, 1 - y), (1 - x, 1 - y)]

        def copy(a, k, block, to, src=None):
            slab = o_refs[a].at[_linear(block)]
            return pltpu.make_async_remote_copy(
                src_ref=slab if src is None else src, dst_ref=slab, send_sem=send_sems.at[a, k], recv_sem=recv_sems.at[a, k],
                device_id=to, device_id_type=MESH)

        mine = [pltpu.make_async_copy(x_refs[a], o_refs[a].at[_linear(me)], local_sems.at[a]) for a in range(n)]
        first = []
        for a in range(n):
            mine[a].start()
            first.append(copy(a, 0, me, sibling, src=x_refs[a]))
            first += [copy(a, 1 + j, me, (*chip, c), src=x_refs[a]) for j, chip in enumerate(chips)]
        for cp in first:
            cp.start()
        passed = []
        for j, chip in enumerate(chips):
            for a in range(n):
                copy(a, 1 + j, (*chip, c), me).wait_recv()
                cp = copy(a, 4 + j, (*chip, c), sibling)
                cp.start()
                passed.append(cp)
        for a in range(n):
            copy(a, 0, sibling, me).wait_recv()
            for j, chip in enumerate(chips):
                copy(a, 4 + j, (*chip, 1 - c), me).wait_recv()
        for cp in first + passed:
            cp.wait_send()
        for cp in mine:
            cp.wait()

    return pl.pallas_call(
        body, in_specs=[ANY] * n, out_specs=[ANY] * n,
        out_shape=[jax.ShapeDtypeStruct((N_DEV,) + x.shape, x.dtype) for x in xs],
        scratch_shapes=[pltpu.SemaphoreType.DMA((n, 7)), pltpu.SemaphoreType.DMA((n, 7)), pltpu.SemaphoreType.DMA((n,))],
        name=name)(*xs)


def _gather_first_copies(x_refs, o_refs, send_sems, recv_sems):
    x, y, c = _position()
    copies = []
    for a in range(len(x_refs)):
        for k, to in enumerate([(x, y, 1 - c), (1 - x, y, c), (x, 1 - y, c)]):
            copies.append(pltpu.make_async_remote_copy(
                src_ref=x_refs[a], dst_ref=o_refs[a].at[_linear((x, y, c))], send_sem=send_sems.at[a, k],
                recv_sem=recv_sems.at[a, k], device_id=to, device_id_type=MESH))
    return copies


def _all_gather_relayed(xs, name, begun=None):
    n = len(xs)
    halves = [x.shape[0] // 2 for x in xs]
    begun = begun or [None] * n
    resumed = [(i, buf) for i, buf in enumerate(begun) if buf is not None]

    def body(*refs):
        x_refs, o_refs = refs[:n], refs[n + len(resumed):2 * n + len(resumed)]
        send_sems, recv_sems = refs[2 * n + len(resumed):]
        x, y, c = _position()
        across_x, across_y, across_xy = (1 - x, y, c), (x, 1 - y, c), (1 - x, 1 - y, c)
        sibling = (x, y, 1 - c)

        def copy(a, k, slab, to, rows=None, src=None):
            dst = o_refs[a].at[_linear(slab)] if rows is None else o_refs[a].at[_linear(slab), rows]
            return pltpu.make_async_remote_copy(
                src_ref=dst if src is None else src, dst_ref=dst, send_sem=send_sems.at[a, k], recv_sem=recv_sems.at[a, k],
                device_id=to, device_id_type=MESH)

        def other_core(p):
            return (p[0], p[1], 1 - c)

        me = (x, y, c)
        fresh = [a for a in range(n) if begun[a] is None]
        started = []
        for a in fresh:
            started += [copy(a, 0, me, sibling, src=x_refs[a]), copy(a, 1, me, across_x, src=x_refs[a]),
                        copy(a, 2, me, across_y, src=x_refs[a])]
        for cp in started:
            cp.start()
        later = []
        for a in range(n):
            first, rest = pl.ds(0, halves[a]), pl.ds(halves[a], xs[a].shape[0] - halves[a])
            if a in fresh:
                copy(a, 1, across_x, me).wait_recv()
            later += [copy(a, 3, across_x, across_y, rows=first), copy(a, 5, across_x, sibling)]
            later[-2].start()
            later[-1].start()
            if a in fresh:
                copy(a, 2, across_y, me).wait_recv()
            later += [copy(a, 4, across_y, across_x, rows=rest), copy(a, 6, across_y, sibling)]
            later[-2].start()
            later[-1].start()
        for a in range(n):
            first, rest = pl.ds(0, halves[a]), pl.ds(halves[a], xs[a].shape[0] - halves[a])
            copy(a, 3, across_xy, me, rows=first).wait_recv()
            copy(a, 4, across_xy, me, rows=rest).wait_recv()
            later.append(copy(a, 7, across_xy, sibling))
            later[-1].start()
        for a in range(n):
            if a in fresh:
                copy(a, 0, sibling, me).wait_recv()
            for k, slab in ((5, across_x), (6, across_y), (7, across_xy)):
                copy(a, k, other_core(slab), me).wait_recv()
        for cp in started + later:
            cp.wait_send()

    sems = pltpu.SemaphoreType.DMA((n, 8))
    return pl.pallas_call(
        body, in_specs=[ANY] * (n + len(resumed)), out_specs=[ANY] * n,
        out_shape=[jax.ShapeDtypeStruct((N_DEV,) + x.shape, x.dtype) for x in xs],
        input_output_aliases={n + j: i for j, (i, _) in enumerate(resumed)},
        scratch_shapes=[sems, sems], name=name)(*xs, *[buf for _, buf in resumed])


def _pair_exchange(xs, name):
    n = len(xs)

    def body(*refs):
        x_refs, got_refs = refs[:n], refs[n:2 * n]
        send_sems, recv_sems = refs[2 * n:]
        x, y, c = _position()
        sends = []
        for a in range(n):
            for p in range(N_CHIPS):
                sends.append(pltpu.make_async_remote_copy(
                    src_ref=x_refs[a].at[p, 1 - c], dst_ref=got_refs[a].at[p], send_sem=send_sems.at[a, p],
                    recv_sem=recv_sems.at[a, p], device_id=(x, y, 1 - c), device_id_type=MESH))
        for cp in sends:
            cp.start()
        for cp in sends:
            cp.wait_recv()
        for cp in sends:
            cp.wait_send()

    sems = pltpu.SemaphoreType.DMA((n, N_CHIPS))
    return pl.pallas_call(body, in_specs=[ANY] * n, out_specs=[ANY] * n,
                          out_shape=[jax.ShapeDtypeStruct((N_CHIPS,) + x.shape[2:], x.dtype) for x in xs],
                          scratch_shapes=[sems, sems], name=name)(*xs)


def _chip_copies(x_refs, o_refs, send_sems, recv_sems, ranges=None):
    x, y, c = _position()
    copies = []
    for a in range(len(x_refs)):
        rows = pl.ds(0, x_refs[a].shape[1]) if ranges is None else pl.ds(ranges[a][0], ranges[a][1] - ranges[a][0])
        for k, (px, py) in enumerate([(1 - x, y), (x, 1 - y), (1 - x, 1 - y)]):
            copies.append(pltpu.make_async_remote_copy(
                src_ref=x_refs[a].at[2 * px + py, rows], dst_ref=o_refs[a].at[k, rows], send_sem=send_sems.at[a, k],
                recv_sem=recv_sems.at[a, k], device_id=(px, py, c), device_id_type=MESH))
    return copies


def _chip_exchange(xs, name):
    n = len(xs)

    def body(*refs):
        sends = _chip_copies(refs[:n], refs[n:2 * n], refs[2 * n], refs[2 * n + 1])
        for cp in sends:
            cp.start()
        for cp in sends:
            cp.wait_recv()
        for cp in sends:
            cp.wait_send()

    sems = pltpu.SemaphoreType.DMA((n, N_CHIPS - 1))
    return pl.pallas_call(body, in_specs=[ANY] * n, out_specs=[ANY] * n,
                          out_shape=[jax.ShapeDtypeStruct((N_CHIPS - 1,) + x.shape[1:], x.dtype) for x in xs],
                          scratch_shapes=[sems, sems], name=name)(*xs)


def _pick_add(picked, index, others, out_dtype, name):
    per_slab = picked.ndim == 4
    n_rows, n_cols = picked.shape[-2:]
    tm = _row_tile(n_rows, n_cols)
    n_other = 1 if per_slab else others.shape[0]

    def body(idx_ref, p_ref, o_ref, out_ref):
        acc = p_ref[...].astype(f32)
        if per_slab:
            acc = acc + o_ref[...].astype(f32)
        else:
            for k in range(n_other):
                acc = acc + o_ref[k].astype(f32)
        out_ref[...] = acc.astype(out_ref.dtype)

    if per_slab:
        grid = (picked.shape[0], n_rows // tm)
        in_specs = [pl.BlockSpec((None, None, tm, n_cols), lambda p, i, idx: (p, idx[0], i, 0)),
                    pl.BlockSpec((None, tm, n_cols), lambda p, i, idx: (p, i, 0))]
        out_specs = pl.BlockSpec((None, tm, n_cols), lambda p, i, idx: (p, i, 0))
        out_shape = jax.ShapeDtypeStruct((picked.shape[0], n_rows, n_cols), out_dtype)
    else:
        grid = (n_rows // tm,)
        in_specs = [pl.BlockSpec((None, tm, n_cols), lambda i, idx: (idx[0], i, 0)),
                    pl.BlockSpec((n_other, tm, n_cols), lambda i, idx: (0, i, 0))]
        out_specs = pl.BlockSpec((tm, n_cols), lambda i, idx: (i, 0))
        out_shape = jax.ShapeDtypeStruct((n_rows, n_cols), out_dtype)
    return pl.pallas_call(
        body, grid_spec=pltpu.PrefetchScalarGridSpec(num_scalar_prefetch=1, grid=grid, in_specs=in_specs, out_specs=out_specs),
        out_shape=out_shape, compiler_params=_params(*(["parallel"] * len(grid))), name=name)(
            jnp.reshape(index, (1,)).astype(jnp.int32), picked, others)


def _chip_sums(xs, name):
    c = lax.axis_index("c")
    pairs = [v.reshape((N_CHIPS, 2) + v.shape[1:]) for v in xs]
    got = _pair_exchange(pairs, f"{name}_pair")
    return [_pick_add(a, c, b, a.dtype, f"{name}_pair_sum{i}") for i, (a, b) in enumerate(zip(pairs, got))]


def _total_sums(chip_sums, arrived, name):
    my_chip = 2 * lax.axis_index("x") + lax.axis_index("y")
    return [_pick_add(a, my_chip, b, f32, f"{name}_sum{i}") for i, (a, b) in enumerate(zip(chip_sums, arrived))]


def _column_group(j, tc, shape):
    col = lax.broadcasted_iota(jnp.int32, shape, 1)
    return (j * tc + col % tc) // SSM_STATE


def _bd_build(bt, n_groups, tc, name):
    n_rows = n_groups * SSM_GROUP
    two_s = bt.shape[1]

    def body(b_ref, o_ref):
        grp = _column_group(pl.program_id(0), tc, (n_rows, 2 * tc))
        row = lax.broadcasted_iota(jnp.int32, (n_rows, 2 * tc), 0)
        tiled = jnp.concatenate([b_ref[...]] * n_groups, axis=0)
        o_ref[...] = jnp.where(grp == row // SSM_GROUP, tiled, 0.0).astype(o_ref.dtype)

    return pl.pallas_call(
        body, grid=(two_s // (2 * tc),), in_specs=[pl.BlockSpec((SSM_GROUP, 2 * tc), lambda j: (0, j))],
        out_specs=pl.BlockSpec((n_rows, 2 * tc), lambda j: (0, j)), out_shape=jax.ShapeDtypeStruct((n_rows, two_s), bf16),
        compiler_params=_params("parallel"), name=name)(bt)


def _bd_fold(m, n_groups, tc, name):
    n_rows, two_s = m.shape

    def body(m_ref, o_ref):
        grp = _column_group(pl.program_id(0), tc, (SSM_GROUP, 2 * tc))
        acc = jnp.zeros((SSM_GROUP, 2 * tc), f32)
        for g in range(n_groups):
            acc = acc + jnp.where(grp == g, m_ref[g * SSM_GROUP:(g + 1) * SSM_GROUP, :], 0.0)
        o_ref[...] = acc

    return pl.pallas_call(
        body, grid=(two_s // (2 * tc),), in_specs=[pl.BlockSpec((n_rows, 2 * tc), lambda j: (0, j))],
        out_specs=pl.BlockSpec((SSM_GROUP, 2 * tc), lambda j: (0, j)), out_shape=jax.ShapeDtypeStruct((SSM_GROUP, two_s), f32),
        compiler_params=_params("parallel"), name=name)(m)


def _ssm_setup(p, tag):
    n_groups = p["ssm_a_re"].shape[0]
    s = n_groups * SSM_STATE
    tc = min(SCAN_TC, s)
    log_dt = p["ssm_log_dt"].reshape(n_groups, 1)
    br_t = p["ssm_b_re"].transpose(2, 0, 1)
    bi_t = p["ssm_b_im"].transpose(2, 0, 1)
    disc_in = (log_dt, p["ssm_a_re"], p["ssm_a_im"], br_t, bi_t)
    lr, li, bbr, bbi = _disc_fwd(*disc_in, name=f"ssm_disc{tag}")
    lam = _interleave(lr.reshape(1, s), li.reshape(1, s), tc)
    bb = _bd_build(_interleave(bbr.reshape(SSM_GROUP, s), bbi.reshape(SSM_GROUP, s), tc), n_groups, tc, f"ssm_in_map{tag}")
    cr_t = p["ssm_c_re"].transpose(1, 0, 2).reshape(SSM_GROUP, s)
    ci_t = p["ssm_c_im"].transpose(1, 0, 2).reshape(SSM_GROUP, s)
    cc = _bd_build(_interleave(cr_t, -ci_t, tc), n_groups, tc, f"ssm_out_map{tag}")
    return dict(disc_in=disc_in, lam=lam, bb=bb, cc=cc, tc=tc, n_groups=n_groups)


def _apply_w(a, w, key, out_dtype, name, shards=()):
    return _mm(a, w[key], out_dtype, name, rhs_t=key in COL_SHARDED, gather=True,
               carry=[(shard, 0, shard.shape[0], None) for shard in shards])


def _apply_wt(dy, w, key, out_dtype, name, carry=()):
    return _mm(dy, w[key], out_dtype, name, rhs_t=key not in COL_SHARDED, carry=carry)


def _weight_grad(a, dy, key, name, carry=()):
    lhs, rhs = (dy, a) if key in COL_SHARDED else (a, dy)
    return _mm(lhs, rhs, bf16, name, lhs_t=True, carry=carry)


class _Cargo:
    def __init__(self, chip_sums):
        self.chip_sums = chip_sums
        self.arrived = {}

    def hosted(self, call, jobs):
        if self.chip_sums is None:
            return call(())
        carry = []
        for key, part, parts in jobs:
            n_rows = self.chip_sums[key].shape[1]
            step = n_rows if parts == 1 else -(-n_rows // parts // BF16_ROWS) * BF16_ROWS
            carry.append((self.chip_sums[key], part * step, min(n_rows, (part + 1) * step), self.arrived.get(key)))
        out, got = call(carry)
        for (key, _, _), buf in zip(jobs, got):
            self.arrived[key] = buf
        return out


def _layer_fwd(x, mod, w, p, tag, ride):
    n_rows, d = x.shape
    w4 = d // 4
    hpp = w4 // HEAD_DIM
    sh1, sc1, gt1, sh2, sc2, gt2 = mod
    vec = lambda a: a.reshape(1, -1)
    s = dict(x=x)
    (s["h"],) = _rowwise(_f_prenorm, [x], [vec(p["g_pre_mix"]), sc1, sh1], [bf16], 256, f"prenorm_mix{tag}")
    sections = [(w4, f32), (9 * w4, bf16), (3 * w4, f32), (d, f32), (d, f32), (d, f32)]
    s["u32"], qkv, s["conv"], *s["pg"] = _mm_sections(s["h"], w["w_in"], sections, f"in_proj{tag}")
    ssm = s["ssm"] = _ssm_setup(p, tag)
    s["u"] = s["u32"].astype(bf16)
    xin = _mm(s["u"], ssm["bb"], f32, f"ssm_in{tag}")
    s["hf"], s["hb"] = _scan(xin, ssm["lam"], False, f"ssm_scan{tag}")
    s["yc"] = _mm(s["hb"], ssm["cc"], f32, f"ssm_out{tag}", rhs_t=True)
    (s["g"],) = _rowwise(_f_gelu, [s["yc"], s["u32"]], [vec(p["ssm_d"])], [bf16], 256, f"ssm_gelu{tag}")
    s["z"] = _apply_w(s["g"], w, "w_glu", f32, f"glu_proj{tag}")
    (s["s_ssm"],) = _rowwise(_f_glu, [s["g"], s["z"]], [vec(p["b_glu"])], [bf16], 256, f"glu{tag}")
    s["y_ssm"] = _apply_w(s["s_ssm"], w, "w_ssm_out", f32, f"ssm_proj{tag}")
    s["qb"], s["kb"], s["vb"] = (_to_blocks(qkv[:, i * 3 * w4:(i + 1) * 3 * w4], hpp) for i in range(3))
    s["ob"], s["lseb"] = _attn_fwd(s["qb"], s["kb"], s["vb"], hpp, f"attn{tag}")
    s["o_tok"] = _from_blocks(s["ob"], hpp)
    s["lse_x"] = [jnp.repeat(l, HEAD_DIM, axis=1) for l in _from_blocks(s["lseb"], hpp)]
    (s["attn_o"],) = _rowwise(_f_amerge, s["o_tok"] + s["lse_x"], [], [bf16], 256, f"attn_merge{tag}")
    s["y_attn"] = _apply_w(s["attn_o"], w, "w_attn_out", f32, f"attn_proj{tag}")
    s["cm"] = _convgate_fwd(s["conv"], p["conv_mix_w"], 0, w4, f"convgate{tag}")
    s["y_conv"] = _apply_w(s["cm"], w, "w_conv_out", f32, f"conv_proj{tag}")
    s["bg"] = [p["b_gate"][i * d:(i + 1) * d].reshape(1, d) for i in range(3)]
    (s["merged"],) = _rowwise(_f_merge, s["pg"] + [s["y_ssm"], s["y_attn"], s["y_conv"]], s["bg"], [bf16], 128, f"merge{tag}")
    s["y"] = _apply_w(s["merged"], w, "w_o", f32, f"o_proj{tag}")
    (s["x1"],) = _rowwise(_f_postres, [x, s["y"]], [vec(p["g_post_mix"]), gt1], [f32], 256, f"postres_mix{tag}")
    (s["h2"],) = _rowwise(_f_prenorm, [s["x1"]], [vec(p["g_pre_ffn"]), sc2, sh2], [bf16], 256, f"prenorm_ffn{tag}")
    begun = {}
    if ride is None:
        s["up"] = _apply_w(s["h2"], w, "w_up", f32, f"up_proj{tag}")
    else:
        s["up"], begun["up_proj"] = _apply_w(s["h2"], w, "w_up", f32, f"up_proj{tag}", ride["up_proj"])
    s["act"] = _ffnconv_fwd(s["up"], p["ffn_conv_w"], f"ffnconv{tag}")
    if ride is None:
        s["y2"] = _apply_w(s["act"], w, "w_down", f32, f"down_proj{tag}")
    else:
        s["y2"], begun["down_proj"] = _apply_w(s["act"], w, "w_down", f32, f"down_proj{tag}", ride["down_proj"])
    (x2,) = _rowwise(_f_postres, [s["x1"], s["y2"]], [vec(p["g_post_ffn"]), gt2], [f32], 256, f"postres_ffn{tag}")
    return x2, s, begun


def _layer_bwd(dx2, s, mod, w, p, tag, cargo):
    x = s["x"]
    n_rows, d = x.shape
    w4 = d // 4
    hpp = w4 // HEAD_DIM
    sh1, sc1, gt1, sh2, sc2, gt2 = mod
    vec = lambda a: a.reshape(1, -1)
    gw, gp = {}, {}
    dy2, d_gpf, d_gt2 = _rowwise_bwd(_f_postres, [s["x1"], s["y2"]], [vec(p["g_post_ffn"]), gt2], [[dx2]], [None, bf16], True,
                                     256, f"postres_ffn_bwd{tag}")
    gp["g_post_ffn"] = d_gpf
    dact = cargo.hosted(lambda carry: _apply_wt(dy2, w, "w_down", bf16, f"down_dx{tag}", carry), [("w_down", 0, 2)])
    gw["w_down"] = cargo.hosted(lambda carry: _weight_grad(s["act"], dy2, "w_down", f"down_dw{tag}", carry), [("w_down", 1, 2)])
    da, db, gp["ffn_conv_w"] = _ffnconv_bwd(s["up"], p["ffn_conv_w"], dact, f"ffnconv_bwd{tag}")
    dup = jnp.concatenate([da, db], axis=1)
    dh2 = cargo.hosted(lambda carry: _apply_wt(dup, w, "w_up", f32, f"up_dx{tag}", carry), [("w_in", 0, 2)])
    gw["w_up"] = cargo.hosted(lambda carry: _weight_grad(s["h2"], dup, "w_up", f"up_dw{tag}", carry), [("w_up", 0, 2)])
    dx1, d_g, d_sc2, d_sh2 = _rowwise_bwd(_f_prenorm, [s["x1"]], [vec(p["g_pre_ffn"]), sc2, sh2], [[dh2]], [f32], True,
                                          256, f"prenorm_ffn_bwd{tag}", add_to={0: dx2})
    gp["g_pre_ffn"] = d_g
    dy, d_gpm, d_gt1 = _rowwise_bwd(_f_postres, [x, s["y"]], [vec(p["g_post_mix"]), gt1], [[dx1]], [None, bf16], True,
                                    256, f"postres_mix_bwd{tag}")
    gp["g_post_mix"] = d_gpm
    dmerged = cargo.hosted(lambda carry: _apply_wt(dy, w, "w_o", f32, f"o_dx{tag}", carry), [("w_o", 0, 1)])
    gw["w_o"] = cargo.hosted(lambda carry: _weight_grad(s["merged"], dy, "w_o", f"o_dw{tag}", carry),
                             [(k, 0, 1) for k in LIGHT_SHARDED])
    res = _rowwise_bwd(_f_merge, s["pg"] + [s["y_ssm"], s["y_attn"], s["y_conv"]], s["bg"], [[dmerged]], [bf16] * 6, True,
                       128, f"merge_bwd{tag}")
    dpg, (dys, dya, dyv), dbg = list(res[0:3]), res[3:6], res[6:9]
    gp["b_gate"] = jnp.concatenate(dbg, axis=1)
    ds_ssm = _apply_wt(dys, w, "w_ssm_out", f32, f"ssm_proj_dx{tag}")
    gw["w_ssm_out"] = _weight_grad(s["s_ssm"], dys, "w_ssm_out", f"ssm_proj_dw{tag}")
    dattn_o = _apply_wt(dya, w, "w_attn_out", f32, f"attn_proj_dx{tag}")
    gw["w_attn_out"] = _weight_grad(s["attn_o"], dya, "w_attn_out", f"attn_proj_dw{tag}")
    dcm = _apply_wt(dyv, w, "w_conv_out", f32, f"conv_proj_dx{tag}")
    gw["w_conv_out"] = _weight_grad(s["cm"], dyv, "w_conv_out", f"conv_proj_dw{tag}")
    ssm = s["ssm"]
    tc, n_groups = ssm["tc"], ssm["n_groups"]
    dg_a, dz, gp["b_glu"] = _rowwise_bwd(_f_glu, [s["g"], s["z"]], [vec(p["b_glu"])], [[ds_ssm]], [f32, bf16], True,
                                         256, f"glu_bwd{tag}")
    dg_b = _apply_wt(dz, w, "w_glu", f32, f"glu_proj_dx{tag}")
    gw["w_glu"] = _weight_grad(s["g"], dz, "w_glu", f"glu_proj_dw{tag}")
    dyc, du_skip, gp["ssm_d"] = _rowwise_bwd(_f_gelu, [s["yc"], s["u32"]], [vec(p["ssm_d"])], [[dg_a, dg_b]], [bf16, f32], True,
                                             256, f"ssm_gelu_bwd{tag}")
    dh_state = _mm(dyc, ssm["cc"], f32, f"ssm_out_dx{tag}")
    g_cc = _bd_fold(_mm(dyc, s["hb"], f32, f"ssm_out_dw{tag}", lhs_t=True), n_groups, tc, f"ssm_out_dw_fold{tag}")
    gf, gb = _scan(dh_state, ssm["lam"], True, f"ssm_scan_bwd{tag}")
    d_lam = _lam_grad(gf, s["hf"], f"ssm_lam_grad{tag}")
    du_x = _mm(gb, ssm["bb"], f32, f"ssm_in_dx{tag}", rhs_t=True)
    g_bb = _bd_fold(_mm(s["u"], gb, f32, f"ssm_in_dw{tag}", lhs_t=True), n_groups, tc, f"ssm_in_dw_fold{tag}")
    (du,) = _rowwise(_f_add, [du_skip, du_x], [], [bf16], 256, f"ssm_du{tag}")
    d_lr, d_li = _deinterleave(d_lam, tc)
    g_bbr, g_bbi = _deinterleave(g_bb, tc)
    group_shape = (SSM_GROUP, n_groups, SSM_STATE)
    cots = (d_lr.reshape(n_groups, SSM_STATE), d_li.reshape(n_groups, SSM_STATE), g_bbr.reshape(group_shape), g_bbi.reshape(group_shape))
    d_ldt, d_ar, d_ai, d_br, d_bi = _disc_bwd(*ssm["disc_in"], cots, f"ssm_disc_bwd{tag}")
    gp["ssm_log_dt"], gp["ssm_a_re"], gp["ssm_a_im"] = d_ldt.reshape(-1), d_ar, d_ai
    gp["ssm_b_re"], gp["ssm_b_im"] = d_br.transpose(1, 2, 0), d_bi.transpose(1, 2, 0)
    g_ccr, g_cci = _deinterleave(g_cc, tc)
    gp["ssm_c_re"] = g_ccr.reshape(group_shape).transpose(1, 0, 2)
    gp["ssm_c_im"] = -g_cci.reshape(group_shape).transpose(1, 0, 2)
    res = _rowwise_bwd(_f_amerge, s["o_tok"] + s["lse_x"], [], [[dattn_o]], [f32] * 6, False, 256, f"attn_merge_bwd{tag}")
    do_b = _to_blocks(jnp.concatenate(res[0:3], axis=1), hpp)
    dl_b = _to_blocks(jnp.concatenate(res[3:6], axis=1), hpp)
    dqb, dkb, dvb = _attn_bwd(s["qb"], s["kb"], s["vb"], s["ob"], s["lseb"], do_b, dl_b, hpp, f"attn_bwd{tag}")
    dq, dk, dv = (jnp.concatenate(_from_blocks(t, hpp), axis=1) for t in (dqb, dkb, dvb))
    dcb, dcc, dch, gp["conv_mix_w"] = _convgate_bwd(s["conv"], p["conv_mix_w"], dcm, 0, w4, f"convgate_bwd{tag}")
    dproj = jnp.concatenate([du, dq, dk, dv, dcb, dcc, dch] + dpg, axis=1)
    dh = cargo.hosted(lambda carry: _apply_wt(dproj, w, "w_in", f32, f"in_dx{tag}", carry), [("w_in", 1, 2)])
    gw["w_in"] = cargo.hosted(lambda carry: _weight_grad(s["h"], dproj, "w_in", f"in_dw{tag}", carry), [("w_up", 1, 2)])
    dx0, d_g, d_sc1, d_sh1 = _rowwise_bwd(_f_prenorm, [x], [vec(p["g_pre_mix"]), sc1, sh1], [[dh]], [f32], True,
                                          256, f"prenorm_mix_bwd{tag}", add_to={0: dx1})
    gp["g_pre_mix"] = d_g
    return dx0, [d_sh1, d_sc1, d_gt1, d_sh2, d_sc2, d_gt2], gw, gp


COL_SHARDED = ("w_in", "w_up", "w_ssm_out", "w_attn_out", "w_conv_out")
ROW_SHARDED = ("w_down", "w_o", "w_glu")
SMALL_SHARDED = ("conv_mix_w", "ffn_conv_w")
UP_RIDERS = ("w_in",)
DOWN_RIDERS = ("w_ssm_out", "w_attn_out", "w_conv_out", "w_down", "w_o", "w_glu")
LIGHT_SHARDED = ("w_ssm_out", "w_attn_out", "w_conv_out", "w_glu") + SMALL_SHARDED
REPLICATED = ("b_mod", "g_pre_mix", "g_post_mix", "g_pre_ffn", "g_post_ffn", "ssm_log_dt", "ssm_a_re", "ssm_a_im", "ssm_b_re",
              "ssm_b_im", "ssm_c_re", "ssm_c_im", "ssm_d", "b_glu", "b_gate")
WEIGHTS = ('w_mod', 'b_mod', 'g_pre_mix', 'g_post_mix', 'g_pre_ffn', 'g_post_ffn', 'w_in', 'ssm_log_dt', 'ssm_a_re', 'ssm_a_im',
           'ssm_b_re', 'ssm_b_im', 'ssm_c_re', 'ssm_c_im', 'ssm_d', 'w_glu', 'b_glu', 'conv_mix_w', 'w_ssm_out', 'w_attn_out',
           'w_conv_out', 'b_gate', 'w_o', 'w_up', 'ffn_conv_w', 'w_down')


def _full_cols(g):
    return g.transpose(1, 0, 2).reshape(g.shape[1], -1)


def _col_slabs(m):
    return m.reshape(m.shape[0], N_DEV, -1).transpose(1, 0, 2)


def _step(x, c, loss_target, wts, m_in, v_in):
    depth = wts["w_in"].shape[0]
    n_rows, d = x.shape[1], x.shape[2]
    me = _linear(_position())
    x2d, target2d = x[0], loss_target[0]

    (c_all,) = _all_gather([c], "gather_c")
    c_all = c_all.reshape(N_DEV, d)
    mod_cols = wts["w_mod"].shape[2]
    b_cols = lax.dynamic_slice_in_dim(wts["b_mod"], me * mod_cols, mod_cols, axis=1).reshape(depth, 1, mod_cols)
    mod_mine = _mod_fwd(c_all, wts["w_mod"], b_cols, "mod_fwd")
    (mod_all,) = _all_gather([mod_mine], "gather_mod")
    mod_me = lax.dynamic_index_in_dim(mod_all, me, axis=2, keepdims=False).transpose(1, 0, 2).reshape(depth, 6, 1, d)

    names = COL_SHARDED + ROW_SHARDED

    def layer_shards(l):
        return [(wts[k][l].T if k in COL_SHARDED else wts[k][l]).astype(bf16) for k in names]

    def full_matrices(shards, l, begun=None):
        got = _all_gather_relayed(shards, f"gather_w{l}", begun)
        got = [lax.dynamic_update_index_in_dim(g, shard, me, axis=0) for g, shard in zip(got, shards)]
        return {k: g.reshape(-1, g.shape[2]) for k, g in zip(names, got)}

    mats = [full_matrices(layer_shards(0), 0)]
    (conv_mix_all, ffn_conv_all) = _all_gather([wts["conv_mix_w"], wts["ffn_conv_w"]], "gather_conv_w")
    small = []
    for l in range(depth):
        p = {k: wts[k][l] for k in REPLICATED if k != "b_mod"}
        p["conv_mix_w"] = _full_cols(conv_mix_all[:, l])
        p["ffn_conv_w"] = _full_cols(ffn_conv_all[:, l])
        small.append(p)

    saved = []
    h = x2d
    for l in range(depth):
        mod = [mod_me[l, i] for i in range(6)]
        ride = None
        if l + 1 < depth:
            shards = layer_shards(l + 1)
            by_name = dict(zip(names, shards))
            ride = {"up_proj": [by_name[k] for k in UP_RIDERS], "down_proj": [by_name[k] for k in DOWN_RIDERS]}
        h, s, begun = _layer_fwd(h, mod, mats[l], small[l], f"_l{l}", ride)
        saved.append(s)
        if ride is not None:
            bufs = dict(zip(UP_RIDERS + DOWN_RIDERS, begun["up_proj"] + begun["down_proj"]))
            mats.append(full_matrices(shards, l + 1, [bufs.get(k) for k in names]))
    dy, loss_part = _loss_head(h, target2d, "loss_head")
    loss = lax.psum(loss_part[0, 0], ("x", "y", "c"))

    grads = {k: [None] * depth for k in WEIGHTS}
    dmod_rows = [None] * depth
    small_parts = [None] * depth
    sharded = COL_SHARDED + ROW_SHARDED + SMALL_SHARDED
    waiting = None
    for l in reversed(range(depth)):
        mod = [mod_me[l, i] for i in range(6)]
        cargo = _Cargo(waiting)
        dy, dmod, gw, gp = _layer_bwd(dy, saved[l], mod, mats[l], small[l], f"_l{l}", cargo)
        if waiting is not None:
            summed = _total_sums([waiting[k] for k in sharded], [cargo.arrived[k] for k in sharded], f"scatter_g{l + 1}")
            for k, g in zip(sharded, summed):
                grads[k][l + 1] = g
        dmod_rows[l] = jnp.concatenate(dmod, axis=1)
        slabs = [gw[k].reshape(N_DEV, -1, gw[k].shape[1]) for k in COL_SHARDED + ROW_SHARDED]
        slabs += [_col_slabs(gp[k]) for k in SMALL_SHARDED]
        waiting = dict(zip(sharded, _chip_sums(slabs, f"scatter_g{l}")))
        small_parts[l] = gp
    last = [waiting[k] for k in sharded]
    for k, g in zip(sharded, _total_sums(last, _chip_exchange(last, "scatter_g0_chips"), "scatter_g0")):
        grads[k][0] = g
    grad_x = dy.reshape(x.shape)

    dmod_mine = jnp.concatenate(dmod_rows, axis=0)
    rep_names = [k for k in REPLICATED if k != "b_mod"]
    pieces = [dmod_mine.reshape(-1)] + [jnp.stack([small_parts[l][k].reshape(-1) for l in range(depth)]).reshape(-1) for k in rep_names]
    sizes = [int(pc.shape[0]) for pc in pieces]
    total = sum(sizes)
    padded = -(-total // (SUBLANES * LANES)) * (SUBLANES * LANES)
    pack = jnp.concatenate(pieces + [jnp.zeros((padded - total,), f32)]).reshape(-1, LANES)
    (pack_all,) = _all_gather([pack], "gather_small_grads")
    pack_sum = _sum_slabs(pack_all, "sum_small_grads").reshape(-1)
    offs = np.cumsum([0] + sizes)
    grads["b_mod"] = pack_sum[offs[0]:offs[1]].reshape(wts["b_mod"].shape)
    for i, k in enumerate(rep_names):
        grads[k] = pack_sum[offs[i + 1]:offs[i + 2]].reshape(wts[k].shape)
    dmod_all = pack_all.reshape(N_DEV, -1)[:, :sizes[0]].reshape(N_DEV, depth, 6 * d)
    dmod_cols = lax.dynamic_slice_in_dim(dmod_all, me * mod_cols, mod_cols, axis=2).transpose(1, 0, 2)
    grads["w_mod"] = _mod_wgrad(c_all.T, dmod_cols, "mod_wgrad")
    for k in COL_SHARDED + ROW_SHARDED + SMALL_SHARDED:
        grads[k] = jnp.stack([g.T for g in grads[k]] if k in COL_SHARDED else grads[k])

    delta, new_m, new_v = {}, {}, {}
    for k in WEIGHTS:
        delta[k], new_m[k], new_v[k] = _adamw(wts[k], grads[k], m_in[k], v_in[k], f"adamw_{k}")
    return (loss, grad_x, *[grads[k] for k in WEIGHTS], *[delta[k] for k in WEIGHTS], *[new_m[k] for k in WEIGHTS],
            *[new_v[k] for k in WEIGHTS])


def kernel(x, c, w_mod, b_mod, g_pre_mix, g_post_mix, g_pre_ffn, g_post_ffn, w_in, ssm_log_dt, ssm_a_re, ssm_a_im, ssm_b_re, ssm_b_im, ssm_c_re, ssm_c_im, ssm_d, w_glu, b_glu, conv_mix_w, w_ssm_out, w_attn_out, w_conv_out, b_gate, w_o, w_up, ffn_conv_w, w_down, loss_target, m_w_mod, m_b_mod, m_g_pre_mix, m_g_post_mix, m_g_pre_ffn, m_g_post_ffn, m_w_in, m_ssm_log_dt, m_ssm_a_re, m_ssm_a_im, m_ssm_b_re, m_ssm_b_im, m_ssm_c_re, m_ssm_c_im, m_ssm_d, m_w_glu, m_b_glu, m_conv_mix_w, m_w_ssm_out, m_w_attn_out, m_w_conv_out, m_b_gate, m_w_o, m_w_up, m_ffn_conv_w, m_w_down, v_w_mod, v_b_mod, v_g_pre_mix, v_g_post_mix, v_g_pre_ffn, v_g_post_ffn, v_w_in, v_ssm_log_dt, v_ssm_a_re, v_ssm_a_im, v_ssm_b_re, v_ssm_b_im, v_ssm_c_re, v_ssm_c_im, v_ssm_d, v_w_glu, v_b_glu, v_conv_mix_w, v_w_ssm_out, v_w_attn_out, v_w_conv_out, v_b_gate, v_w_o, v_w_up, v_ffn_conv_w, v_w_down):
    wts = dict(w_mod=w_mod, b_mod=b_mod, g_pre_mix=g_pre_mix, g_post_mix=g_post_mix, g_pre_ffn=g_pre_ffn, g_post_ffn=g_post_ffn, w_in=w_in, ssm_log_dt=ssm_log_dt, ssm_a_re=ssm_a_re, ssm_a_im=ssm_a_im, ssm_b_re=ssm_b_re, ssm_b_im=ssm_b_im, ssm_c_re=ssm_c_re, ssm_c_im=ssm_c_im, ssm_d=ssm_d, w_glu=w_glu, b_glu=b_glu, conv_mix_w=conv_mix_w, w_ssm_out=w_ssm_out, w_attn_out=w_attn_out, w_conv_out=w_conv_out, b_gate=b_gate, w_o=w_o, w_up=w_up, ffn_conv_w=ffn_conv_w, w_down=w_down)
    m_in = dict(w_mod=m_w_mod, b_mod=m_b_mod, g_pre_mix=m_g_pre_mix, g_post_mix=m_g_post_mix, g_pre_ffn=m_g_pre_ffn, g_post_ffn=m_g_post_ffn, w_in=m_w_in, ssm_log_dt=m_ssm_log_dt, ssm_a_re=m_ssm_a_re, ssm_a_im=m_ssm_a_im, ssm_b_re=m_ssm_b_re, ssm_b_im=m_ssm_b_im, ssm_c_re=m_ssm_c_re, ssm_c_im=m_ssm_c_im, ssm_d=m_ssm_d, w_glu=m_w_glu, b_glu=m_b_glu, conv_mix_w=m_conv_mix_w, w_ssm_out=m_w_ssm_out, w_attn_out=m_w_attn_out, w_conv_out=m_w_conv_out, b_gate=m_b_gate, w_o=m_w_o, w_up=m_w_up, ffn_conv_w=m_ffn_conv_w, w_down=m_w_down)
    v_in = dict(w_mod=v_w_mod, b_mod=v_b_mod, g_pre_mix=v_g_pre_mix, g_post_mix=v_g_post_mix, g_pre_ffn=v_g_pre_ffn, g_post_ffn=v_g_post_ffn, w_in=v_w_in, ssm_log_dt=v_ssm_log_dt, ssm_a_re=v_ssm_a_re, ssm_a_im=v_ssm_a_im, ssm_b_re=v_ssm_b_re, ssm_b_im=v_ssm_b_im, ssm_c_re=v_ssm_c_re, ssm_c_im=v_ssm_c_im, ssm_d=v_ssm_d, w_glu=v_w_glu, b_glu=v_b_glu, conv_mix_w=v_conv_mix_w, w_ssm_out=v_w_ssm_out, w_attn_out=v_w_attn_out, w_conv_out=v_w_conv_out, b_gate=v_b_gate, w_o=v_w_o, w_up=v_w_up, ffn_conv_w=v_ffn_conv_w, w_down=v_w_down)
    return _step(x, c, loss_target, wts, m_in, v_in)
```

```python
import functools
import math

import numpy as np
import jax
import jax.numpy as jnp
from jax import lax
from jax.experimental import pallas as pl
from jax.experimental.pallas import tpu as pltpu

f32 = jnp.float32
bf16 = jnp.bfloat16

N_DEV = 8
N_CHIPS = 4
V7X_VMEM_LIMIT_BYTES = 56 * 1024 * 1024
LANES = 128
SUBLANES = 8
BF16_ROWS = 16

RMS_EPS = 1e-6
NEG_INF = -1e30
HEAD_DIM = 64
SSM_GROUP = 16
SSM_STATE = 64
DSWA_PATTERNS = ((128, 1), (512, 4), (2048, 16))
QBLK = 128

ADAM_LR = 0.001
ADAM_B1 = 0.9
ADAM_B2 = 0.999
ADAM_EPS = 1e-08
ADAM_WD = 0.01
ADAM_STEP = 10

MESH = pl.DeviceIdType.MESH
ANY = pl.BlockSpec(memory_space=pl.ANY)


def _tile(n, pref, mult=LANES):
    t = min(pref, n) // mult * mult
    while t >= mult:
        if n % t == 0:
            return t
        t -= mult
    return n


def _params(*sem):
    return pltpu.CompilerParams(dimension_semantics=sem, vmem_limit_bytes=V7X_VMEM_LIMIT_BYTES)


def _mm(a, b, out_dtype, name, rhs_t=False, lhs_t=False, carry=(), gather=False):
    k, m = a.shape if lhs_t else a.shape[::-1]
    n, k2 = b.shape if rhs_t else b.shape[::-1]
    assert k == k2 and a.dtype == bf16 and b.dtype == bf16, (a.shape, b.shape, a.dtype, b.dtype)
    tm, tn, tk = _tile(m, 1024, LANES if lhs_t else SUBLANES), _tile(n, 512), _tile(k, 2048)
    grid = (m // tm, n // tn, k // tk)
    contract = (((0 if lhs_t else 1,), (1 if rhs_t else 0,)), ((), ()))
    nc = len(carry)
    srcs = [job[0] for job in carry]
    ranges = [(job[1], job[2]) for job in carry]
    intos = [(i, job[3]) for i, job in enumerate(carry) if job[3] is not None]
    n_in = 2 + nc + len(intos)

    def body(*refs):
        a_ref, b_ref = refs[:2]
        o_ref = refs[n_in]
        acc_ref = refs[n_in + 1 + nc]
        steps = [pl.program_id(axis) for axis in range(3)]
        if nc:
            rides = (refs[2:2 + nc], refs[n_in + 1:n_in + 1 + nc], refs[n_in + 2 + nc], refs[n_in + 3 + nc])
            copies = _gather_first_copies(*rides) if gather else _chip_copies(*rides, ranges)

            @pl.when((steps[0] == 0) & (steps[1] == 0) & (steps[2] == 0))
            def _():
                for cp in copies:
                    cp.start()

        @pl.when(steps[2] == 0)
        def _():
            acc_ref[...] = jnp.zeros_like(acc_ref)

        acc_ref[...] += lax.dot_general(a_ref[...], b_ref[...], contract, preferred_element_type=f32)

        @pl.when(steps[2] == grid[2] - 1)
        def _():
            o_ref[...] = acc_ref[...].astype(o_ref.dtype)

        if nc:
            @pl.when((steps[0] == grid[0] - 1) & (steps[1] == grid[1] - 1) & (steps[2] == grid[2] - 1))
            def _():
                for cp in copies:
                    cp.wait_recv()
                for cp in copies:
                    cp.wait_send()

    a_spec = pl.BlockSpec((tk, tm), lambda i, j, kk: (kk, i)) if lhs_t else pl.BlockSpec((tm, tk), lambda i, j, kk: (i, kk))
    b_spec = pl.BlockSpec((tn, tk), lambda i, j, kk: (j, kk)) if rhs_t else pl.BlockSpec((tk, tn), lambda i, j, kk: (kk, j))
    sems = pltpu.SemaphoreType.DMA((max(nc, 1), N_CHIPS - 1))
    res = pl.pallas_call(
        body, grid=grid,
        in_specs=[a_spec, b_spec] + [ANY] * (nc + len(intos)),
        out_specs=[pl.BlockSpec((tm, tn), lambda i, j, kk: (i, j))] + [ANY] * nc,
        out_shape=[jax.ShapeDtypeStruct((m, n), out_dtype)]
        + [jax.ShapeDtypeStruct(((N_DEV,) + x.shape) if gather else ((N_CHIPS - 1,) + x.shape[1:]), x.dtype) for x in srcs],
        scratch_shapes=[pltpu.VMEM((tm, tn), f32)] + ([sems, sems] if nc else []),
        input_output_aliases={2 + nc + j: 1 + i for j, (i, _) in enumerate(intos)},
        compiler_params=_params(*(("arbitrary",) * 3 if nc else ("parallel", "parallel", "arbitrary"))), name=name)(
            a, b, *srcs, *[buf for _, buf in intos])
    return (res[0], list(res[1:])) if nc else res[0]


def _mm_sections(a, b_t, sections, name, shards=()):
    m, k = a.shape
    n = b_t.shape[0]
    assert b_t.shape[1] == k and sum(w for w, _ in sections) == n
    tn = _tile(math.gcd(*[w for w, _ in sections]), 512)
    tm, tk = _tile(m, 1024, SUBLANES), _tile(k, 2048)
    grid = (m // tm, n // tn, k // tk)
    bounds, lo = [], 0
    for w, _ in sections:
        bounds.append((lo, lo + w // tn))
        lo += w // tn
    ns, nr = len(sections), len(shards)

    def body(a_ref, b_ref, *rest):
        o_refs, acc_ref = rest[nr:nr + ns], rest[2 * nr + ns]
        i, j, kk = pl.program_id(0), pl.program_id(1), pl.program_id(2)
        if nr:
            copies = _gather_first_copies(rest[:nr], rest[nr + ns:2 * nr + ns], rest[2 * nr + ns + 1], rest[2 * nr + ns + 2])

            @pl.when((i == 0) & (j == 0) & (kk == 0))
            def _():
                for cp in copies:
                    cp.start()

            @pl.when((i == grid[0] - 1) & (j == grid[1] - 1) & (kk == grid[2] - 1))
            def _():
                for cp in copies:
                    cp.wait_recv()
                for cp in copies:
                    cp.wait_send()

        @pl.when(kk == 0)
        def _():
            acc_ref[...] = jnp.zeros_like(acc_ref)

        acc_ref[...] += lax.dot_general(a_ref[...], b_ref[...], (((1,), (1,)), ((), ())), preferred_element_type=f32)
        for o_ref, (first, end) in zip(o_refs, bounds):
            @pl.when((kk == grid[2] - 1) & (j >= first) & (j < end))
            def _(o_ref=o_ref):
                o_ref[...] = acc_ref[...].astype(o_ref.dtype)

    def out_spec(first, end):
        return pl.BlockSpec((tm, tn), lambda i, j, kk: (i, jnp.clip(j - first, 0, end - first - 1)))

    sems = pltpu.SemaphoreType.DMA((max(nr, 1), 3))
    return pl.pallas_call(
        body, grid=grid,
        in_specs=[pl.BlockSpec((tm, tk), lambda i, j, kk: (i, kk)), pl.BlockSpec((tn, tk), lambda i, j, kk: (j, kk))] + [ANY] * nr,
        out_specs=[out_spec(first, end) for first, end in bounds] + [ANY] * nr,
        out_shape=[jax.ShapeDtypeStruct((m, w), dt) for w, dt in sections]
        + [jax.ShapeDtypeStruct((N_DEV,) + x.shape, x.dtype) for x in shards],
        scratch_shapes=[pltpu.VMEM((tm, tn), f32)] + ([sems, sems] if nr else []),
        compiler_params=_params("arbitrary", "arbitrary", "arbitrary"), name=name)(a, b_t, *shards)


def _rowwise(fn, rows, vecs, out_dtypes, tm, name):
    n_rows = rows[0].shape[0]
    nr, nv = len(rows), len(vecs)
    outs = jax.eval_shape(fn, *[jax.ShapeDtypeStruct((tm, r.shape[1]), f32) for r in rows],
                          *[jax.ShapeDtypeStruct(v.shape, f32) for v in vecs])

    def body(*refs):
        vals = [r[...].astype(f32) for r in refs[:nr]] + [v[...] for v in refs[nr:nr + nv]]
        for o_ref, val in zip(refs[nr + nv:], fn(*vals)):
            o_ref[...] = val.astype(o_ref.dtype)

    res = pl.pallas_call(
        body, grid=(n_rows // tm,),
        in_specs=[pl.BlockSpec((tm, r.shape[1]), lambda i: (i, 0)) for r in rows]
        + [pl.BlockSpec(v.shape, lambda i: (0, 0)) for v in vecs],
        out_specs=[pl.BlockSpec((tm, o.shape[1]), lambda i: (i, 0)) for o in outs],
        out_shape=[jax.ShapeDtypeStruct((n_rows, o.shape[1]), dt) for o, dt in zip(outs, out_dtypes)],
        compiler_params=_params("parallel"), name=name)(*rows, *vecs)
    return res


def _rowwise_bwd(fn, rows, vecs, cots, row_dtypes, need_vecs, tm, name, add_to=None):
    n_rows = rows[0].shape[0]
    nr, nv = len(rows), len(vecs)
    add_to = add_to or {}
    flat_cots = [c for group in cots for c in group]
    add_keys = sorted(add_to)
    row_out = [i for i, dt in enumerate(row_dtypes) if dt is not None]
    n_in = nr + nv + len(flat_cots) + len(add_keys)

    def body(*refs):
        vals = [r[...].astype(f32) for r in refs[:nr]] + [v[...] for v in refs[nr:nr + nv]]
        pos = nr + nv
        cvals = []
        for group in cots:
            acc = refs[pos][...].astype(f32)
            for extra in range(1, len(group)):
                acc = acc + refs[pos + extra][...].astype(f32)
            pos += len(group)
            cvals.append(acc)
        adds = {key: refs[pos + j] for j, key in enumerate(add_keys)}
        _, vjp = jax.vjp(fn, *vals)
        grads = vjp(tuple(cvals))
        out_refs = refs[n_in:]
        for j, i in enumerate(row_out):
            g = grads[i]
            if i in adds:
                g = g + adds[i][...].astype(f32)
            out_refs[j][...] = g.astype(out_refs[j].dtype)
        if need_vecs:
            step = pl.program_id(0)
            for j in range(nv):
                acc_ref = out_refs[len(row_out) + j]

                @pl.when(step == 0)
                def _(acc_ref=acc_ref):
                    acc_ref[...] = jnp.zeros_like(acc_ref)

                acc_ref[...] += grads[nr + j]

    row_spec = lambda a: pl.BlockSpec((tm, a.shape[1]), lambda i: (i, 0))
    in_specs = ([row_spec(r) for r in rows] + [pl.BlockSpec(v.shape, lambda i: (0, 0)) for v in vecs]
                + [row_spec(c) for c in flat_cots] + [row_spec(add_to[key]) for key in add_keys])
    out_specs = [row_spec(rows[i]) for i in row_out]
    out_shape = [jax.ShapeDtypeStruct(rows[i].shape, row_dtypes[i]) for i in row_out]
    if need_vecs:
        out_specs += [pl.BlockSpec(v.shape, lambda i: (0, 0)) for v in vecs]
        out_shape += [jax.ShapeDtypeStruct(v.shape, f32) for v in vecs]
    return pl.pallas_call(
        body, grid=(n_rows // tm,), in_specs=in_specs, out_specs=out_specs, out_shape=out_shape,
        compiler_params=_params("arbitrary"), name=name)(*rows, *vecs, *flat_cots, *[add_to[key] for key in add_keys])


def _f_prenorm(x, g, sc, sh):
    y = x * lax.rsqrt(jnp.mean(x * x, axis=-1, keepdims=True) + RMS_EPS)
    return (y * g * (1.0 + sc) + sh,)


def _f_postres(x, y, g, gt):
    n = y * lax.rsqrt(jnp.mean(y * y, axis=-1, keepdims=True) + RMS_EPS)
    return (x + gt * (n * g),)


def _f_gelu(yc, u, d):
    return (jax.nn.gelu(yc + d * u),)


def _f_glu(g, z, b):
    return (g * jax.nn.sigmoid(z + b),)


def _f_merge(p0, p1, p2, ys, ya, yc, b0, b1, b2):
    return (jax.nn.sigmoid(p0 + b0) * ys + jax.nn.sigmoid(p1 + b1) * ya + jax.nn.sigmoid(p2 + b2) * yc,)


def _f_amerge(o0, o1, o2, l0, l1, l2):
    m = jnp.maximum(jnp.maximum(l0, l1), l2)
    e0, e1, e2 = jnp.exp(l0 - m), jnp.exp(l1 - m), jnp.exp(l2 - m)
    return ((e0 * o0 + e1 * o1 + e2 * o2) / (e0 + e1 + e2),)


def _f_add(a, b):
    return (a + b,)


def _shift_down(z, k, row):
    return jnp.where(row >= k, pltpu.roll(z, k, 0), 0.0)


def _shift_up(z, k, row):
    n = z.shape[0]
    return jnp.where(row < n - k, pltpu.roll(z, n - k, 0), 0.0)


def _conv3(z, w, row):
    return w[0:1, :] * z + w[1:2, :] * _shift_down(z, 1, row) + w[2:3, :] * _shift_down(z, 2, row)


def _conv3_bwd(z, w, dy, row):
    dz = w[0:1, :] * dy + w[1:2, :] * _shift_up(dy, 1, row) + w[2:3, :] * _shift_up(dy, 2, row)
    dw = [jnp.sum(dy * z, axis=0, keepdims=True),
          jnp.sum(dy * _shift_down(z, 1, row), axis=0, keepdims=True),
          jnp.sum(dy * _shift_down(z, 2, row), axis=0, keepdims=True)]
    return dz, dw


def _convgate_fwd(proj, w, off, width, name):
    n_rows = proj.shape[0]
    tc = _tile(width, 256)
    nb = width // tc
    base = off // tc

    def body(b_ref, c_ref, h_ref, w_ref, o_ref):
        row = lax.broadcasted_iota(jnp.int32, (n_rows, tc), 0)
        o_ref[...] = (b_ref[...] * _conv3(c_ref[...] * h_ref[...], w_ref[...], row)).astype(o_ref.dtype)

    col = lambda s: pl.BlockSpec((n_rows, tc), lambda j: (0, base + s * nb + j))
    return pl.pallas_call(
        body, grid=(nb,), in_specs=[col(0), col(1), col(2), pl.BlockSpec((3, tc), lambda j: (0, j))],
        out_specs=pl.BlockSpec((n_rows, tc), lambda j: (0, j)),
        out_shape=jax.ShapeDtypeStruct((n_rows, width), bf16),
        compiler_params=_params("parallel"), name=name)(proj, proj, proj, w)


def _convgate_bwd(proj, w, dcm, off, width, name):
    n_rows = proj.shape[0]
    tc = _tile(width, 128)
    nb = width // tc
    base = off // tc

    def body(b_ref, c_ref, h_ref, w_ref, d_ref, db_ref, dc_ref, dh_ref, dw_ref):
        row = lax.broadcasted_iota(jnp.int32, (n_rows, tc), 0)
        cb, cc, ch, wv, d = b_ref[...], c_ref[...], h_ref[...], w_ref[...], d_ref[...].astype(f32)
        z = cc * ch
        db_ref[...] = (d * _conv3(z, wv, row)).astype(db_ref.dtype)
        dz, dw = _conv3_bwd(z, wv, d * cb, row)
        dc_ref[...] = (dz * ch).astype(dc_ref.dtype)
        dh_ref[...] = (dz * cc).astype(dh_ref.dtype)
        for k in range(3):
            dw_ref[k:k + 1, :] = dw[k]

    col = lambda s: pl.BlockSpec((n_rows, tc), lambda j: (0, base + s * nb + j))
    own = pl.BlockSpec((n_rows, tc), lambda j: (0, j))
    wspec = pl.BlockSpec((3, tc), lambda j: (0, j))
    sec = jax.ShapeDtypeStruct((n_rows, width), bf16)
    return pl.pallas_call(
        body, grid=(nb,), in_specs=[col(0), col(1), col(2), wspec, own], out_specs=[own, own, own, wspec],
        out_shape=[sec, sec, sec, jax.ShapeDtypeStruct((3, width), f32)],
        compiler_params=_params("parallel"), name=name)(proj, proj, proj, w, dcm)


def _ffnconv_fwd(up, w, name):
    n_rows, two_f = up.shape
    half = two_f // 2
    tc = _tile(half, 256)
    nb = half // tc

    def body(a_ref, b_ref, wa_ref, wb_ref, o_ref):
        row = lax.broadcasted_iota(jnp.int32, (n_rows, tc), 0)
        ua = _conv3(a_ref[...], wa_ref[...], row)
        ub = _conv3(b_ref[...], wb_ref[...], row)
        o_ref[...] = (jax.nn.silu(ua) * ub).astype(o_ref.dtype)

    return pl.pallas_call(
        body, grid=(nb,),
        in_specs=[pl.BlockSpec((n_rows, tc), lambda j: (0, j)), pl.BlockSpec((n_rows, tc), lambda j: (0, nb + j)),
                  pl.BlockSpec((3, tc), lambda j: (0, j)), pl.BlockSpec((3, tc), lambda j: (0, nb + j))],
        out_specs=pl.BlockSpec((n_rows, tc), lambda j: (0, j)),
        out_shape=jax.ShapeDtypeStruct((n_rows, half), bf16),
        compiler_params=_params("parallel"), name=name)(up, up, w, w)


def _ffnconv_bwd(up, w, dact, name):
    n_rows, two_f = up.shape
    half = two_f // 2
    tc = _tile(half, 128)
    nb = half // tc

    def body(a_ref, b_ref, wa_ref, wb_ref, d_ref, da_ref, db_ref, dwa_ref, dwb_ref):
        row = lax.broadcasted_iota(jnp.int32, (n_rows, tc), 0)
        a, b, wa, wb, d = a_ref[...], b_ref[...], wa_ref[...], wb_ref[...], d_ref[...].astype(f32)
        ua = _conv3(a, wa, row)
        ub = _conv3(b, wb, row)
        sg = jax.nn.sigmoid(ua)
        d_ua = d * ub * (sg * (1.0 + ua * (1.0 - sg)))
        d_ub = d * (ua * sg)
        da, dwa = _conv3_bwd(a, wa, d_ua, row)
        db, dwb = _conv3_bwd(b, wb, d_ub, row)
        da_ref[...] = da.astype(da_ref.dtype)
        db_ref[...] = db.astype(db_ref.dtype)
        for k in range(3):
            dwa_ref[k:k + 1, :] = dwa[k]
            dwb_ref[k:k + 1, :] = dwb[k]

    lo = pl.BlockSpec((n_rows, tc), lambda j: (0, j))
    hi = pl.BlockSpec((n_rows, tc), lambda j: (0, nb + j))
    wlo = pl.BlockSpec((3, tc), lambda j: (0, j))
    whi = pl.BlockSpec((3, tc), lambda j: (0, nb + j))
    sec = jax.ShapeDtypeStruct((n_rows, half), bf16)
    wsec = jax.ShapeDtypeStruct((3, half), f32)
    da, db, dwa, dwb = pl.pallas_call(
        body, grid=(nb,), in_specs=[lo, hi, wlo, whi, lo], out_specs=[lo, lo, wlo, wlo],
        out_shape=[sec, sec, wsec, wsec], compiler_params=_params("parallel"), name=name)(up, up, w, w, dact)
    return da, db, jnp.concatenate([dwa, dwb], axis=1)


SCAN_TC = 256


def _cmul(ar, ai, br, bi):
    return ar * br - ai * bi, ar * bi + ai * br


def _scan(x, lam, adjoint, name):
    n_rows, two_s = x.shape
    tc = min(SCAN_TC, two_s // 2)
    n_tiles = two_s // (2 * tc)
    n_chunks = n_rows // SUBLANES

    def body(x_ref, lam_ref, hf_ref, hb_ref):
        lr = lam_ref[:, 0:tc]
        li = lam_ref[:, tc:2 * tc]
        if adjoint:
            li = -li
        row = lax.broadcasted_iota(jnp.int32, (SUBLANES, tc), 0)
        powers = [(lr, li)]
        for _ in range(SUBLANES - 1):
            powers.append(_cmul(powers[-1][0], powers[-1][1], lr, li))
        pr = jnp.zeros((SUBLANES, tc), f32)
        pi = jnp.zeros((SUBLANES, tc), f32)
        for t in range(SUBLANES):
            e = (SUBLANES - 1 - t) if adjoint else t
            pr = jnp.where(row == t, powers[e][0], pr)
            pi = jnp.where(row == t, powers[e][1], pi)
        last = 0 if adjoint else SUBLANES - 1

        def chunk(i, carry):
            cr, ci = carry
            c = (n_chunks - 1 - i) if adjoint else i
            rows = pl.ds(pl.multiple_of(c * SUBLANES, SUBLANES), SUBLANES)
            xr = x_ref[rows, 0:tc]
            xi = x_ref[rows, tc:2 * tc]
            for k in (1, 2, 4):
                ar, ai = powers[k - 1]
                if adjoint:
                    sr = jnp.where(row < SUBLANES - k, pltpu.roll(xr, SUBLANES - k, 0), 0.0)
                    si = jnp.where(row < SUBLANES - k, pltpu.roll(xi, SUBLANES - k, 0), 0.0)
                else:
                    sr = jnp.where(row >= k, pltpu.roll(xr, k, 0), 0.0)
                    si = jnp.where(row >= k, pltpu.roll(xi, k, 0), 0.0)
                xr, xi = xr + ar * sr - ai * si, xi + ar * si + ai * sr
            hr = xr + pr * cr - pi * ci
            hi = xi + pr * ci + pi * cr
            hf_ref[rows, 0:tc] = hr
            hf_ref[rows, tc:2 * tc] = hi
            hb_ref[rows, 0:tc] = hr.astype(bf16)
            hb_ref[rows, tc:2 * tc] = hi.astype(bf16)
            return (jnp.sum(jnp.where(row == last, hr, 0.0), axis=0, keepdims=True),
                    jnp.sum(jnp.where(row == last, hi, 0.0), axis=0, keepdims=True))

        zero = jnp.zeros((1, tc), f32)
        lax.fori_loop(0, n_chunks, chunk, (zero, zero))

    blk = pl.BlockSpec((n_rows, 2 * tc), lambda j: (0, j))
    return pl.pallas_call(
        body, grid=(n_tiles,), in_specs=[blk, pl.BlockSpec((1, 2 * tc), lambda j: (0, j))], out_specs=[blk, blk],
        out_shape=[jax.ShapeDtypeStruct(x.shape, f32), jax.ShapeDtypeStruct(x.shape, bf16)],
        compiler_params=_params("parallel"), name=name)(x, lam)


def _lam_grad(g, h, name):
    n_rows, two_s = g.shape
    tc = min(SCAN_TC, two_s // 2)

    def body(g_ref, h_ref, o_ref):
        row = lax.broadcasted_iota(jnp.int32, (n_rows, tc), 0)
        gr, gi = g_ref[:, 0:tc], g_ref[:, tc:2 * tc]
        hr, hi = _shift_down(h_ref[:, 0:tc], 1, row), _shift_down(h_ref[:, tc:2 * tc], 1, row)
        o_ref[:, 0:tc] = jnp.sum(gr * hr + gi * hi, axis=0, keepdims=True)
        o_ref[:, tc:2 * tc] = jnp.sum(gi * hr - gr * hi, axis=0, keepdims=True)

    blk = pl.BlockSpec((n_rows, 2 * tc), lambda j: (0, j))
    return pl.pallas_call(
        body, grid=(two_s // (2 * tc),), in_specs=[blk, blk], out_specs=pl.BlockSpec((1, 2 * tc), lambda j: (0, j)),
        out_shape=jax.ShapeDtypeStruct((1, two_s), f32), compiler_params=_params("parallel"), name=name)(g, h)


def _interleave(re, im, tc):
    lead = re.shape[:-1]
    s = re.shape[-1]
    return jnp.concatenate([re.reshape(*lead, s // tc, 1, tc), im.reshape(*lead, s // tc, 1, tc)], axis=-2).reshape(*lead, 2 * s)


def _deinterleave(z, tc):
    lead = z.shape[:-1]
    s = z.shape[-1] // 2
    z = z.reshape(*lead, s // tc, 2, tc)
    return z[..., 0, :].reshape(*lead, s), z[..., 1, :].reshape(*lead, s)


def _f_disc(log_dt, ar, ai, br, bi):
    dt = jnp.exp(log_dt)
    mag = jnp.exp(ar * dt)
    lr, li = mag * jnp.cos(ai * dt), mag * jnp.sin(ai * dt)
    den = ar * ar + ai * ai
    fr = ((lr - 1.0) * ar + li * ai) / den
    fi = (li * ar - (lr - 1.0) * ai) / den
    return lr, li, fr[None] * br - fi[None] * bi, fr[None] * bi + fi[None] * br


def _disc_fwd(log_dt, ar, ai, br, bi, name):
    def body(*refs):
        for o_ref, val in zip(refs[5:], _f_disc(*[r[...] for r in refs[:5]])):
            o_ref[...] = val

    sd = jax.ShapeDtypeStruct
    return pl.pallas_call(body, out_shape=[sd(ar.shape, f32), sd(ar.shape, f32), sd(br.shape, f32), sd(br.shape, f32)],
                          name=name)(log_dt, ar, ai, br, bi)


def _disc_bwd(log_dt, ar, ai, br, bi, cots, name):
    def body(*refs):
        _, vjp = jax.vjp(_f_disc, *[r[...] for r in refs[:5]])
        for o_ref, val in zip(refs[9:], vjp(tuple(r[...] for r in refs[5:9]))):
            o_ref[...] = val

    sd = jax.ShapeDtypeStruct
    return pl.pallas_call(
        body, out_shape=[sd(log_dt.shape, f32), sd(ar.shape, f32), sd(ar.shape, f32), sd(br.shape, f32), sd(br.shape, f32)],
        name=name)(log_dt, ar, ai, br, bi, *cots)


def _alibi_bias(n_heads_per_pattern):
    n_heads = len(DSWA_PATTERNS) * n_heads_per_pattern
    slopes = np.array([2.0 ** (-8.0 * (h + 1) / n_heads) for h in range(n_heads)], dtype=np.float32)
    qi = np.arange(QBLK)[:, None]
    kj = np.arange(QBLK)[None, :]
    cur, prev = [], []
    for h in range(n_heads):
        dil = DSWA_PATTERNS[h // n_heads_per_pattern][1]
        d_cur = qi - kj
        d_prev = QBLK + qi - kj
        cur.append(np.where(d_cur >= 0, -slopes[h] * (d_cur * dil).astype(np.float32), NEG_INF))
        prev.append(np.where(d_prev <= QBLK, -slopes[h] * (d_prev * dil).astype(np.float32), NEG_INF))
    return np.stack(cur).astype(np.float32), np.stack(prev).astype(np.float32)


def _blocks_per_residue(head, n_heads_per_pattern, n_blocks):
    pattern = head // n_heads_per_pattern
    out = n_blocks // DSWA_PATTERNS[-1][1]
    for p in range(len(DSWA_PATTERNS) - 2, -1, -1):
        out = jnp.where(pattern == p, n_blocks // DSWA_PATTERNS[p][1], out)
    return out


def _qk(a, b):
    return jnp.einsum('bqe,bke->bqk', a, b, preferred_element_type=f32)


def _pv(p, v):
    return jnp.einsum('bqk,bke->bqe', p.astype(bf16), v, preferred_element_type=f32)


def _ptq(p, a):
    return jnp.einsum('bqk,bqe->bke', p.astype(bf16), a, preferred_element_type=f32)


def _prev_block(t):
    return jnp.concatenate([jnp.zeros((1,) + t.shape[1:], t.dtype), t[:-1]], axis=0)


def _next_block(t):
    return jnp.concatenate([t[1:], jnp.zeros((1,) + t.shape[1:], t.dtype)], axis=0)


def _has_prev(head, hpp, n_blocks):
    blk = lax.broadcasted_iota(jnp.int32, (n_blocks, 1, 1), 0)
    return (blk & (_blocks_per_residue(head, hpp, n_blocks) - 1)) > 0


def _attn_fwd(q, k, v, hpp, name):
    n_heads, n_blocks = q.shape[0], q.shape[1]
    bias_cur, bias_prev = _alibi_bias(hpp)
    scale = HEAD_DIM ** -0.5

    def body(q_ref, k_ref, v_ref, bc_ref, bp_ref, o_ref, lse_ref):
        has_prev = _has_prev(pl.program_id(0), hpp, n_blocks)
        qv, kv, vv = q_ref[...], k_ref[...], v_ref[...]
        s_c = _qk(qv, kv) * scale + bc_ref[...][None]
        s_p = jnp.where(has_prev, _qk(qv, _prev_block(kv)) * scale + bp_ref[...][None], NEG_INF)
        m = jnp.maximum(jnp.max(s_c, axis=-1, keepdims=True), jnp.max(s_p, axis=-1, keepdims=True))
        p_c = jnp.exp(s_c - m)
        p_p = jnp.exp(s_p - m)
        l = jnp.sum(p_c, axis=-1, keepdims=True) + jnp.sum(p_p, axis=-1, keepdims=True)
        o_ref[...] = (_pv(p_c, vv) + _pv(p_p, _prev_block(vv))) / l
        lse_ref[...] = m + jnp.log(l)

    head = pl.BlockSpec((None, n_blocks, QBLK, HEAD_DIM), lambda a: (a, 0, 0, 0))
    bias = pl.BlockSpec((None, QBLK, QBLK), lambda a: (a, 0, 0))
    return pl.pallas_call(
        body, grid=(n_heads,), in_specs=[head, head, head, bias, bias],
        out_specs=[head, pl.BlockSpec((None, n_blocks, QBLK, 1), lambda a: (a, 0, 0, 0))],
        out_shape=[jax.ShapeDtypeStruct(q.shape, f32), jax.ShapeDtypeStruct(q.shape[:3] + (1,), f32)],
        compiler_params=_params("parallel"), name=name)(q, k, v, jnp.asarray(bias_cur), jnp.asarray(bias_prev))


def _attn_bwd(q, k, v, o, lse, do, dlse_x, hpp, name):
    n_heads, n_blocks = q.shape[0], q.shape[1]
    bias_cur, bias_prev = _alibi_bias(hpp)
    scale = HEAD_DIM ** -0.5

    def body(q_ref, k_ref, v_ref, o_ref, l_ref, do_ref, dl_ref, bc_ref, bp_ref, dq_ref, dk_ref, dv_ref):
        has_prev = _has_prev(pl.program_id(0), hpp, n_blocks)
        qv, kv, vv = q_ref[...], k_ref[...], v_ref[...]
        kp, vp = _prev_block(kv), _prev_block(vv)
        dov = do_ref[...]
        do_b = dov.astype(bf16)
        lse = l_ref[...]
        corr = jnp.sum(dl_ref[...], axis=-1, keepdims=True) - jnp.sum(dov * o_ref[...], axis=-1, keepdims=True)
        p_c = jnp.exp(_qk(qv, kv) * scale + bc_ref[...][None] - lse)
        ds_c = p_c * (_qk(do_b, vv) + corr)
        p_p = jnp.where(has_prev, jnp.exp(_qk(qv, kp) * scale + bp_ref[...][None] - lse), 0.0)
        ds_p = p_p * (_qk(do_b, vp) + corr)
        dq_ref[...] = ((_pv(ds_c, kv) + _pv(ds_p, kp)) * scale).astype(dq_ref.dtype)
        dk_ref[...] = ((_ptq(ds_c, qv) + _next_block(_ptq(ds_p, qv))) * scale).astype(dk_ref.dtype)
        dv_ref[...] = (_ptq(p_c, do_b) + _next_block(_ptq(p_p, do_b))).astype(dv_ref.dtype)

    head = pl.BlockSpec((None, n_blocks, QBLK, HEAD_DIM), lambda a: (a, 0, 0, 0))
    head1 = pl.BlockSpec((None, n_blocks, QBLK, 1), lambda a: (a, 0, 0, 0))
    bias = pl.BlockSpec((None, QBLK, QBLK), lambda a: (a, 0, 0))
    out = jax.ShapeDtypeStruct(q.shape, bf16)
    return pl.pallas_call(
        body, grid=(n_heads,), in_specs=[head, head, head, head, head1, head, head, bias, bias],
        out_specs=[head, head, head], out_shape=[out, out, out],
        compiler_params=_params("parallel"), name=name)(q, k, v, o, lse, do, dlse_x, jnp.asarray(bias_cur), jnp.asarray(bias_prev))


def _to_blocks(t, hpp):
    n_rows = t.shape[0]
    t = t.reshape(n_rows, len(DSWA_PATTERNS), hpp, HEAD_DIM)
    out = []
    for p, (_, dil) in enumerate(DSWA_PATTERNS):
        tp = t[:, p].reshape(n_rows // dil, dil, hpp, HEAD_DIM).transpose(2, 1, 0, 3)
        out.append(tp.reshape(hpp, n_rows // QBLK, QBLK, HEAD_DIM))
    return jnp.concatenate(out, axis=0)


def _from_blocks(t, hpp):
    n_blocks, last = t.shape[1], t.shape[3]
    n_rows = n_blocks * QBLK
    out = []
    for p, (_, dil) in enumerate(DSWA_PATTERNS):
        tp = t[p * hpp:(p + 1) * hpp].reshape(hpp, dil, n_rows // dil, last).transpose(2, 1, 0, 3)
        out.append(tp.reshape(n_rows, hpp * last))
    return out


def _loss_head(y, target, name):
    n_rows, d = y.shape
    tm = _tile(n_rows, 256, SUBLANES)

    def body(y_ref, t_ref, dy_ref, loss_ref):
        diff = y_ref[...] - t_ref[...]
        dy_ref[...] = diff * (1.0 / d)

        @pl.when(pl.program_id(0) == 0)
        def _():
            loss_ref[...] = jnp.zeros_like(loss_ref)

        loss_ref[...] += jnp.sum(jnp.sum(diff * diff, axis=1, keepdims=True), axis=0, keepdims=True) * (0.5 / d)

    row = pl.BlockSpec((tm, d), lambda i: (i, 0))
    return pl.pallas_call(
        body, grid=(n_rows // tm,), in_specs=[row, row], out_specs=[row, pl.BlockSpec((1, 1), lambda i: (0, 0))],
        out_shape=[jax.ShapeDtypeStruct(y.shape, f32), jax.ShapeDtypeStruct((1, 1), f32)],
        compiler_params=_params("arbitrary"), name=name)(y, target)


def _as2d(a):
    if a.ndim >= 2 and a.shape[-1] >= LANES:
        return a.reshape(-1, a.shape[-1])
    return a.reshape(-1, LANES) if a.size % LANES == 0 else a.reshape(1, -1)


ELEMENTWISE_BLOCK_BYTES = 2 << 20


def _row_tile(n_rows, n_cols):
    return _tile(n_rows, max(SUBLANES, ELEMENTWISE_BLOCK_BYTES // (4 * n_cols)), SUBLANES)


def _adamw(w, g, m, v, name):
    shape = w.shape
    w2, g2, m2, v2 = _as2d(w), _as2d(g), _as2d(m), _as2d(v)
    n_rows, n_cols = w2.shape
    tm = _row_tile(n_rows, n_cols)

    def body(w_ref, g_ref, m_ref, v_ref, d_ref, mo_ref, vo_ref):
        gv = g_ref[...]
        mn = ADAM_B1 * m_ref[...] + (1.0 - ADAM_B1) * gv
        vn = ADAM_B2 * v_ref[...] + (1.0 - ADAM_B2) * jnp.square(gv)
        m_hat = mn / (1.0 - ADAM_B1 ** ADAM_STEP)
        v_hat = vn / (1.0 - ADAM_B2 ** ADAM_STEP)
        d_ref[...] = -ADAM_LR * (m_hat / (jnp.sqrt(v_hat) + ADAM_EPS) + ADAM_WD * w_ref[...])
        mo_ref[...] = mn
        vo_ref[...] = vn

    row = pl.BlockSpec((tm, n_cols), lambda i: (i, 0))
    out = jax.ShapeDtypeStruct(w2.shape, f32)
    d, mn, vn = pl.pallas_call(body, grid=(n_rows // tm,), in_specs=[row] * 4, out_specs=[row] * 3, out_shape=[out] * 3,
                               compiler_params=_params("parallel"), name=name)(w2, g2, m2, v2)
    return d.reshape(shape), mn.reshape(shape), vn.reshape(shape)


def _sum_slabs(x, name):
    n = x.shape[0]
    shape = x.shape[1:]
    x3 = x.reshape(n, -1, shape[-1])
    n_rows, n_cols = x3.shape[1:]
    tm = _row_tile(n_rows, n_cols)

    def body(x_ref, o_ref):
        acc = x_ref[0].astype(f32)
        for s in range(1, n):
            acc = acc + x_ref[s].astype(f32)
        o_ref[...] = acc

    out = pl.pallas_call(
        body, grid=(n_rows // tm,), in_specs=[pl.BlockSpec((n, tm, n_cols), lambda i: (0, i, 0))],
        out_specs=pl.BlockSpec((tm, n_cols), lambda i: (i, 0)), out_shape=jax.ShapeDtypeStruct((n_rows, n_cols), f32),
        compiler_params=_params("parallel"), name=name)(x3)
    return out.reshape(shape)


def _mod_fwd(c_all, w_mod, b_cols, name):
    depth, d, cols = w_mod.shape
    tn = _tile(cols, 512)

    def body(c_ref, w_ref, b_ref, o_ref):
        cond = jax.nn.silu(c_ref[...])
        o_ref[...] = jnp.dot(cond, w_ref[...], preferred_element_type=f32, precision=lax.Precision.HIGHEST) + b_ref[...]

    return pl.pallas_call(
        body, grid=(depth, cols // tn),
        in_specs=[pl.BlockSpec((N_DEV, d), lambda l, j: (0, 0)), pl.BlockSpec((None, d, tn), lambda l, j: (l, 0, j)),
                  pl.BlockSpec((None, 1, tn), lambda l, j: (l, 0, j))],
        out_specs=pl.BlockSpec((None, N_DEV, tn), lambda l, j: (l, 0, j)),
        out_shape=jax.ShapeDtypeStruct((depth, N_DEV, cols), f32),
        compiler_params=_params("parallel", "parallel"), name=name)(c_all, w_mod, b_cols)


def _mod_wgrad(c_all_t, dmod_cols, name):
    d = c_all_t.shape[0]
    depth, _, cols = dmod_cols.shape
    tm = _tile(d, 256, SUBLANES)

    def body(c_ref, g_ref, o_ref):
        cond = jax.nn.silu(c_ref[...])
        o_ref[...] = jnp.dot(cond, g_ref[...], preferred_element_type=f32, precision=lax.Precision.HIGHEST)

    return pl.pallas_call(
        body, grid=(depth, d // tm),
        in_specs=[pl.BlockSpec((tm, N_DEV), lambda l, i: (i, 0)), pl.BlockSpec((None, N_DEV, cols), lambda l, i: (l, 0, 0))],
        out_specs=pl.BlockSpec((None, tm, cols), lambda l, i: (l, i, 0)),
        out_shape=jax.ShapeDtypeStruct((depth, d, cols), f32),
        compiler_params=_params("parallel", "parallel"), name=name)(c_all_t, dmod_cols)


def _position():
    return lax.axis_index("x"), lax.axis_index("y"), lax.axis_index("c")


def _linear(p):
    return 4 * p[0] + 2 * p[1] + p[2]


def _all_gather(xs, name):
    n = len(xs)

    def body(*refs):
        x_refs, o_refs = refs[:n], refs[n:2 * n]
        send_sems, recv_sems, local_sems = refs[2 * n:]
        x, y, c = _position()
        me, sibling = (x, y, c), (x, y, 1 - c)
        chips = [(1 - x, y), (x, 1 - y), (1 - x, 1 - y)]

        def copy(a, k, block, to, src=None):
            slab = o_refs[a].at[_linear(block)]
            return pltpu.make_async_remote_copy(
                src_ref=slab if src is None else src, dst_ref=slab, send_sem=send_sems.at[a, k], recv_sem=recv_sems.at[a, k],
                device_id=to, device_id_type=MESH)

        mine = [pltpu.make_async_copy(x_refs[a], o_refs[a].at[_linear(me)], local_sems.at[a]) for a in range(n)]
        first = []
        for a in range(n):
            mine[a].start()
            first.append(copy(a, 0, me, sibling, src=x_refs[a]))
            first += [copy(a, 1 + j, me, (*chip, c), src=x_refs[a]) for j, chip in enumerate(chips)]
        for cp in first:
            cp.start()
        passed = []
        for j, chip in enumerate(chips):
            for a in range(n):
                copy(a, 1 + j, (*chip, c), me).wait_recv()
                cp = copy(a, 4 + j, (*chip, c), sibling)
                cp.start()
                passed.append(cp)
        for a in range(n):
            copy(a, 0, sibling, me).wait_recv()
            for j, chip in enumerate(chips):
                copy(a, 4 + j, (*chip, 1 - c), me).wait_recv()
        for cp in first + passed:
            cp.wait_send()
        for cp in mine:
            cp.wait()

    return pl.pallas_call(
        body, in_specs=[ANY] * n, out_specs=[ANY] * n,
        out_shape=[jax.ShapeDtypeStruct((N_DEV,) + x.shape, x.dtype) for x in xs],
        scratch_shapes=[pltpu.SemaphoreType.DMA((n, 7)), pltpu.SemaphoreType.DMA((n, 7)), pltpu.SemaphoreType.DMA((n,))],
        name=name)(*xs)


def _gather_first_copies(x_refs, o_refs, send_sems, recv_sems):
    x, y, c = _position()
    copies = []
    for a in range(len(x_refs)):
        for k, to in enumerate([(x, y, 1 - c), (1 - x, y, c), (x, 1 - y, c)]):
            copies.append(pltpu.make_async_remote_copy(
                src_ref=x_refs[a], dst_ref=o_refs[a].at[_linear((x, y, c))], send_sem=send_sems.at[a, k],
                recv_sem=recv_sems.at[a, k], device_id=to, device_id_type=MESH))
    return copies


def _all_gather_relayed(xs, name, begun=None):
    n = len(xs)
    halves = [x.shape[0] // 2 for x in xs]
    begun = begun or [None] * n
    resumed = [(i, buf) for i, buf in enumerate(begun) if buf is not None]

    def body(*refs):
        x_refs, o_refs = refs[:n], refs[n + len(resumed):2 * n + len(resumed)]
        send_sems, recv_sems = refs[2 * n + len(resumed):]
        x, y, c = _position()
        across_x, across_y, across_xy = (1 - x, y, c), (x, 1 - y, c), (1 - x, 1 - y, c)
        sibling = (x, y, 1 - c)

        def copy(a, k, slab, to, rows=None, src=None):
            dst = o_refs[a].at[_linear(slab)] if rows is None else o_refs[a].at[_linear(slab), rows]
            return pltpu.make_async_remote_copy(
                src_ref=dst if src is None else src, dst_ref=dst, send_sem=send_sems.at[a, k], recv_sem=recv_sems.at[a, k],
                device_id=to, device_id_type=MESH)

        def other_core(p):
            return (p[0], p[1], 1 - c)

        me = (x, y, c)
        fresh = [a for a in range(n) if begun[a] is None]
        started = []
        for a in fresh:
            started += [copy(a, 0, me, sibling, src=x_refs[a]), copy(a, 1, me, across_x, src=x_refs[a]),
                        copy(a, 2, me, across_y, src=x_refs[a])]
        for cp in started:
            cp.start()
        later = []
        for a in range(n):
            first, rest = pl.ds(0, halves[a]), pl.ds(halves[a], xs[a].shape[0] - halves[a])
            if a in fresh:
                copy(a, 1, across_x, me).wait_recv()
            later += [copy(a, 3, across_x, across_y, rows=first), copy(a, 5, across_x, sibling)]
            later[-2].start()
            later[-1].start()
            if a in fresh:
                copy(a, 2, across_y, me).wait_recv()
            later += [copy(a, 4, across_y, across_x, rows=rest), copy(a, 6, across_y, sibling)]
            later[-2].start()
            later[-1].start()
        for a in range(n):
            first, rest = pl.ds(0, halves[a]), pl.ds(halves[a], xs[a].shape[0] - halves[a])
            copy(a, 3, across_xy, me, rows=first).wait_recv()
            copy(a, 4, across_xy, me, rows=rest).wait_recv()
            later.append(copy(a, 7, across_xy, sibling))
            later[-1].start()
        for a in range(n):
            if a in fresh:
                copy(a, 0, sibling, me).wait_recv()
            for k, slab in ((5, across_x), (6, across_y), (7, across_xy)):
                copy(a, k, other_core(slab), me).wait_recv()
        for cp in started + later:
            cp.wait_send()

    sems = pltpu.SemaphoreType.DMA((n, 8))
    return pl.pallas_call(
        body, in_specs=[ANY] * (n + len(resumed)), out_specs=[ANY] * n,
        out_shape=[jax.ShapeDtypeStruct((N_DEV,) + x.shape, x.dtype) for x in xs],
        input_output_aliases={n + j: i for j, (i, _) in enumerate(resumed)},
        scratch_shapes=[sems, sems], name=name)(*xs, *[buf for _, buf in resumed])


def _pair_exchange(xs, name):
    n = len(xs)

    def body(*refs):
        x_refs, got_refs = refs[:n], refs[n:2 * n]
        send_sems, recv_sems = refs[2 * n:]
        x, y, c = _position()
        sends = []
        for a in range(n):
            for p in range(N_CHIPS):
                sends.append(pltpu.make_async_remote_copy(
                    src_ref=x_refs[a].at[p, 1 - c], dst_ref=got_refs[a].at[p], send_sem=send_sems.at[a, p],
                    recv_sem=recv_sems.at[a, p], device_id=(x, y, 1 - c), device_id_type=MESH))
        for cp in sends:
            cp.start()
        for cp in sends:
            cp.wait_recv()
        for cp in sends:
            cp.wait_send()

    sems = pltpu.SemaphoreType.DMA((n, N_CHIPS))
    return pl.pallas_call(body, in_specs=[ANY] * n, out_specs=[ANY] * n,
                          out_shape=[jax.ShapeDtypeStruct((N_CHIPS,) + x.shape[2:], x.dtype) for x in xs],
                          scratch_shapes=[sems, sems], name=name)(*xs)


def _chip_copies(x_refs, o_refs, send_sems, recv_sems, ranges=None):
    x, y, c = _position()
    copies = []
    for a in range(len(x_refs)):
        rows = pl.ds(0, x_refs[a].shape[1]) if ranges is None else pl.ds(ranges[a][0], ranges[a][1] - ranges[a][0])
        for k, (px, py) in enumerate([(1 - x, y), (x, 1 - y), (1 - x, 1 - y)]):
            copies.append(pltpu.make_async_remote_copy(
                src_ref=x_refs[a].at[2 * px + py, rows], dst_ref=o_refs[a].at[k, rows], send_sem=send_sems.at[a, k],
                recv_sem=recv_sems.at[a, k], device_id=(px, py, c), device_id_type=MESH))
    return copies


def _chip_exchange(xs, name):
    n = len(xs)

    def body(*refs):
        sends = _chip_copies(refs[:n], refs[n:2 * n], refs[2 * n], refs[2 * n + 1])
        for cp in sends:
            cp.start()
        for cp in sends:
            cp.wait_recv()
        for cp in sends:
            cp.wait_send()

    sems = pltpu.SemaphoreType.DMA((n, N_CHIPS - 1))
    return pl.pallas_call(body, in_specs=[ANY] * n, out_specs=[ANY] * n,
                          out_shape=[jax.ShapeDtypeStruct((N_CHIPS - 1,) + x.shape[1:], x.dtype) for x in xs],
                          scratch_shapes=[sems, sems], name=name)(*xs)


def _pick_add(picked, index, others, out_dtype, name):
    per_slab = picked.ndim == 4
    n_rows, n_cols = picked.shape[-2:]
    tm = _row_tile(n_rows, n_cols)
    n_other = 1 if per_slab else others.shape[0]

    def body(idx_ref, p_ref, o_ref, out_ref):
        acc = p_ref[...].astype(f32)
        if per_slab:
            acc = acc + o_ref[...].astype(f32)
        else:
            for k in range(n_other):
                acc = acc + o_ref[k].astype(f32)
        out_ref[...] = acc.astype(out_ref.dtype)

    if per_slab:
        grid = (picked.shape[0], n_rows // tm)
        in_specs = [pl.BlockSpec((None, None, tm, n_cols), lambda p, i, idx: (p, idx[0], i, 0)),
                    pl.BlockSpec((None, tm, n_cols), lambda p, i, idx: (p, i, 0))]
        out_specs = pl.BlockSpec((None, tm, n_cols), lambda p, i, idx: (p, i, 0))
        out_shape = jax.ShapeDtypeStruct((picked.shape[0], n_rows, n_cols), out_dtype)
    else:
        grid = (n_rows // tm,)
        in_specs = [pl.BlockSpec((None, tm, n_cols), lambda i, idx: (idx[0], i, 0)),
                    pl.BlockSpec((n_other, tm, n_cols), lambda i, idx: (0, i, 0))]
        out_specs = pl.BlockSpec((tm, n_cols), lambda i, idx: (i, 0))
        out_shape = jax.ShapeDtypeStruct((n_rows, n_cols), out_dtype)
    return pl.pallas_call(
        body, grid_spec=pltpu.PrefetchScalarGridSpec(num_scalar_prefetch=1, grid=grid, in_specs=in_specs, out_specs=out_specs),
        out_shape=out_shape, compiler_params=_params(*(["parallel"] * len(grid))), name=name)(
            jnp.reshape(index, (1,)).astype(jnp.int32), picked, others)


def _chip_sums(xs, name):
    c = lax.axis_index("c")
    pairs = [v.reshape((N_CHIPS, 2) + v.shape[1:]) for v in xs]
    got = _pair_exchange(pairs, f"{name}_pair")
    return [_pick_add(a, c, b, a.dtype, f"{name}_pair_sum{i}") for i, (a, b) in enumerate(zip(pairs, got))]


def _total_sums(chip_sums, arrived, name):
    my_chip = 2 * lax.axis_index("x") + lax.axis_index("y")
    return [_pick_add(a, my_chip, b, f32, f"{name}_sum{i}") for i, (a, b) in enumerate(zip(chip_sums, arrived))]


def _column_group(j, tc, shape):
    col = lax.broadcasted_iota(jnp.int32, shape, 1)
    return (j * tc + col % tc) // SSM_STATE


def _bd_build(bt, n_groups, tc, name):
    n_rows = n_groups * SSM_GROUP
    two_s = bt.shape[1]

    def body(b_ref, o_ref):
        grp = _column_group(pl.program_id(0), tc, (n_rows, 2 * tc))
        row = lax.broadcasted_iota(jnp.int32, (n_rows, 2 * tc), 0)
        tiled = jnp.concatenate([b_ref[...]] * n_groups, axis=0)
        o_ref[...] = jnp.where(grp == row // SSM_GROUP, tiled, 0.0).astype(o_ref.dtype)

    return pl.pallas_call(
        body, grid=(two_s // (2 * tc),), in_specs=[pl.BlockSpec((SSM_GROUP, 2 * tc), lambda j: (0, j))],
        out_specs=pl.BlockSpec((n_rows, 2 * tc), lambda j: (0, j)), out_shape=jax.ShapeDtypeStruct((n_rows, two_s), bf16),
        compiler_params=_params("parallel"), name=name)(bt)


def _bd_fold(m, n_groups, tc, name):
    n_rows, two_s = m.shape

    def body(m_ref, o_ref):
        grp = _column_group(pl.program_id(0), tc, (SSM_GROUP, 2 * tc))
        acc = jnp.zeros((SSM_GROUP, 2 * tc), f32)
        for g in range(n_groups):
            acc = acc + jnp.where(grp == g, m_ref[g * SSM_GROUP:(g + 1) * SSM_GROUP, :], 0.0)
        o_ref[...] = acc

    return pl.pallas_call(
        body, grid=(two_s // (2 * tc),), in_specs=[pl.BlockSpec((n_rows, 2 * tc), lambda j: (0, j))],
        out_specs=pl.BlockSpec((SSM_GROUP, 2 * tc), lambda j: (0, j)), out_shape=jax.ShapeDtypeStruct((SSM_GROUP, two_s), f32),
        compiler_params=_params("parallel"), name=name)(m)


def _ssm_setup(p, tag):
    n_groups = p["ssm_a_re"].shape[0]
    s = n_groups * SSM_STATE
    tc = min(SCAN_TC, s)
    log_dt = p["ssm_log_dt"].reshape(n_groups, 1)
    br_t = p["ssm_b_re"].transpose(2, 0, 1)
    bi_t = p["ssm_b_im"].transpose(2, 0, 1)
    disc_in = (log_dt, p["ssm_a_re"], p["ssm_a_im"], br_t, bi_t)
    lr, li, bbr, bbi = _disc_fwd(*disc_in, name=f"ssm_disc{tag}")
    lam = _interleave(lr.reshape(1, s), li.reshape(1, s), tc)
    bb = _bd_build(_interleave(bbr.reshape(SSM_GROUP, s), bbi.reshape(SSM_GROUP, s), tc), n_groups, tc, f"ssm_in_map{tag}")
    cr_t = p["ssm_c_re"].transpose(1, 0, 2).reshape(SSM_GROUP, s)
    ci_t = p["ssm_c_im"].transpose(1, 0, 2).reshape(SSM_GROUP, s)
    cc = _bd_build(_interleave(cr_t, -ci_t, tc), n_groups, tc, f"ssm_out_map{tag}")
    return dict(disc_in=disc_in, lam=lam, bb=bb, cc=cc, tc=tc, n_groups=n_groups)


def _apply_w(a, w, key, out_dtype, name, shards=()):
    return _mm(a, w[key], out_dtype, name, rhs_t=key in COL_SHARDED, gather=True,
               carry=[(shard, 0, shard.shape[0], None) for shard in shards])


def _apply_wt(dy, w, key, out_dtype, name, carry=()):
    return _mm(dy, w[key], out_dtype, name, rhs_t=key not in COL_SHARDED, carry=carry)


def _weight_grad(a, dy, key, name, carry=()):
    lhs, rhs = (dy, a) if key in COL_SHARDED else (a, dy)
    return _mm(lhs, rhs, bf16, name, lhs_t=True, carry=carry)


class _Cargo:
    def __init__(self, chip_sums):
        self.chip_sums = chip_sums
        self.arrived = {}

    def hosted(self, call, jobs):
        if self.chip_sums is None:
            return call(())
        carry = []
        for key, part, parts in jobs:
            n_rows = self.chip_sums[key].shape[1]
            step = n_rows if parts == 1 else -(-n_rows // parts // BF16_ROWS) * BF16_ROWS
            carry.append((self.chip_sums[key], part * step, min(n_rows, (part + 1) * step), self.arrived.get(key)))
        out, got = call(carry)
        for (key, _, _), buf in zip(jobs, got):
            self.arrived[key] = buf
        return out


def _layer_fwd(x, mod, w, p, tag, ride):
    n_rows, d = x.shape
    w4 = d // 4
    hpp = w4 // HEAD_DIM
    sh1, sc1, gt1, sh2, sc2, gt2 = mod
    vec = lambda a: a.reshape(1, -1)
    s = dict(x=x)
    (s["h"],) = _rowwise(_f_prenorm, [x], [vec(p["g_pre_mix"]), sc1, sh1], [bf16], 256, f"prenorm_mix{tag}")
    sections = [(w4, f32), (9 * w4, bf16), (3 * w4, f32), (d, f32), (d, f32), (d, f32)]
    res = _mm_sections(s["h"], w["w_in"], sections, f"in_proj{tag}", () if ride is None else ride["in_proj"])
    s["u32"], qkv, s["conv"], *s["pg"] = res[:len(sections)]
    begun = {"in_proj": list(res[len(sections):])}
    ssm = s["ssm"] = _ssm_setup(p, tag)
    s["u"] = s["u32"].astype(bf16)
    xin = _mm(s["u"], ssm["bb"], f32, f"ssm_in{tag}")
    s["hf"], s["hb"] = _scan(xin, ssm["lam"], False, f"ssm_scan{tag}")
    s["yc"] = _mm(s["hb"], ssm["cc"], f32, f"ssm_out{tag}", rhs_t=True)
    (s["g"],) = _rowwise(_f_gelu, [s["yc"], s["u32"]], [vec(p["ssm_d"])], [bf16], 256, f"ssm_gelu{tag}")
    s["z"] = _apply_w(s["g"], w, "w_glu", f32, f"glu_proj{tag}")
    (s["s_ssm"],) = _rowwise(_f_glu, [s["g"], s["z"]], [vec(p["b_glu"])], [bf16], 256, f"glu{tag}")
    s["y_ssm"] = _apply_w(s["s_ssm"], w, "w_ssm_out", f32, f"ssm_proj{tag}")
    s["qb"], s["kb"], s["vb"] = (_to_blocks(qkv[:, i * 3 * w4:(i + 1) * 3 * w4], hpp) for i in range(3))
    s["ob"], s["lseb"] = _attn_fwd(s["qb"], s["kb"], s["vb"], hpp, f"attn{tag}")
    s["o_tok"] = _from_blocks(s["ob"], hpp)
    s["lse_x"] = [jnp.repeat(l, HEAD_DIM, axis=1) for l in _from_blocks(s["lseb"], hpp)]
    (s["attn_o"],) = _rowwise(_f_amerge, s["o_tok"] + s["lse_x"], [], [bf16], 256, f"attn_merge{tag}")
    s["y_attn"] = _apply_w(s["attn_o"], w, "w_attn_out", f32, f"attn_proj{tag}")
    s["cm"] = _convgate_fwd(s["conv"], p["conv_mix_w"], 0, w4, f"convgate{tag}")
    s["y_conv"] = _apply_w(s["cm"], w, "w_conv_out", f32, f"conv_proj{tag}")
    s["bg"] = [p["b_gate"][i * d:(i + 1) * d].reshape(1, d) for i in range(3)]
    (s["merged"],) = _rowwise(_f_merge, s["pg"] + [s["y_ssm"], s["y_attn"], s["y_conv"]], s["bg"], [bf16], 128, f"merge{tag}")
    s["y"] = _apply_w(s["merged"], w, "w_o", f32, f"o_proj{tag}")
    (s["x1"],) = _rowwise(_f_postres, [x, s["y"]], [vec(p["g_post_mix"]), gt1], [f32], 256, f"postres_mix{tag}")
    (s["h2"],) = _rowwise(_f_prenorm, [s["x1"]], [vec(p["g_pre_ffn"]), sc2, sh2], [bf16], 256, f"prenorm_ffn{tag}")
    if ride is None:
        s["up"] = _apply_w(s["h2"], w, "w_up", f32, f"up_proj{tag}")
    else:
        s["up"], begun["up_proj"] = _apply_w(s["h2"], w, "w_up", f32, f"up_proj{tag}", ride["up_proj"])
    s["act"] = _ffnconv_fwd(s["up"], p["ffn_conv_w"], f"ffnconv{tag}")
    if ride is None:
        s["y2"] = _apply_w(s["act"], w, "w_down", f32, f"down_proj{tag}")
    else:
        s["y2"], begun["down_proj"] = _apply_w(s["act"], w, "w_down", f32, f"down_proj{tag}", ride["down_proj"])
    (x2,) = _rowwise(_f_postres, [s["x1"], s["y2"]], [vec(p["g_post_ffn"]), gt2], [f32], 256, f"postres_ffn{tag}")
    return x2, s, begun


def _layer_bwd(dx2, s, mod, w, p, tag, cargo):
    x = s["x"]
    n_rows, d = x.shape
    w4 = d // 4
    hpp = w4 // HEAD_DIM
    sh1, sc1, gt1, sh2, sc2, gt2 = mod
    vec = lambda a: a.reshape(1, -1)
    gw, gp = {}, {}
    dy2, d_gpf, d_gt2 = _rowwise_bwd(_f_postres, [s["x1"], s["y2"]], [vec(p["g_post_ffn"]), gt2], [[dx2]], [None, bf16], True,
                                     256, f"postres_ffn_bwd{tag}")
    gp["g_post_ffn"] = d_gpf
    dact = cargo.hosted(lambda carry: _apply_wt(dy2, w, "w_down", bf16, f"down_dx{tag}", carry), [("w_down", 0, 2)])
    gw["w_down"] = cargo.hosted(lambda carry: _weight_grad(s["act"], dy2, "w_down", f"down_dw{tag}", carry), [("w_down", 1, 2)])
    da, db, gp["ffn_conv_w"] = _ffnconv_bwd(s["up"], p["ffn_conv_w"], dact, f"ffnconv_bwd{tag}")
    dup = jnp.concatenate([da, db], axis=1)
    dh2 = cargo.hosted(lambda carry: _apply_wt(dup, w, "w_up", f32, f"up_dx{tag}", carry), [("w_in", 0, 2)])
    gw["w_up"] = cargo.hosted(lambda carry: _weight_grad(s["h2"], dup, "w_up", f"up_dw{tag}", carry), [("w_up", 0, 2)])
    dx1, d_g, d_sc2, d_sh2 = _rowwise_bwd(_f_prenorm, [s["x1"]], [vec(p["g_pre_ffn"]), sc2, sh2], [[dh2]], [f32], True,
                                          256, f"prenorm_ffn_bwd{tag}", add_to={0: dx2})
    gp["g_pre_ffn"] = d_g
    dy, d_gpm, d_gt1 = _rowwise_bwd(_f_postres, [x, s["y"]], [vec(p["g_post_mix"]), gt1], [[dx1]], [None, bf16], True,
                                    256, f"postres_mix_bwd{tag}")
    gp["g_post_mix"] = d_gpm
    dmerged = cargo.hosted(lambda carry: _apply_wt(dy, w, "w_o", f32, f"o_dx{tag}", carry), [("w_o", 0, 1)])
    gw["w_o"] = cargo.hosted(lambda carry: _weight_grad(s["merged"], dy, "w_o", f"o_dw{tag}", carry),
                             [(k, 0, 1) for k in LIGHT_SHARDED])
    res = _rowwise_bwd(_f_merge, s["pg"] + [s["y_ssm"], s["y_attn"], s["y_conv"]], s["bg"], [[dmerged]], [bf16] * 6, True,
                       128, f"merge_bwd{tag}")
    dpg, (dys, dya, dyv), dbg = list(res[0:3]), res[3:6], res[6:9]
    gp["b_gate"] = jnp.concatenate(dbg, axis=1)
    ds_ssm = _apply_wt(dys, w, "w_ssm_out", f32, f"ssm_proj_dx{tag}")
    gw["w_ssm_out"] = _weight_grad(s["s_ssm"], dys, "w_ssm_out", f"ssm_proj_dw{tag}")
    dattn_o = _apply_wt(dya, w, "w_attn_out", f32, f"attn_proj_dx{tag}")
    gw["w_attn_out"] = _weight_grad(s["attn_o"], dya, "w_attn_out", f"attn_proj_dw{tag}")
    dcm = _apply_wt(dyv, w, "w_conv_out", f32, f"conv_proj_dx{tag}")
    gw["w_conv_out"] = _weight_grad(s["cm"], dyv, "w_conv_out", f"conv_proj_dw{tag}")
    ssm = s["ssm"]
    tc, n_groups = ssm["tc"], ssm["n_groups"]
    dg_a, dz, gp["b_glu"] = _rowwise_bwd(_f_glu, [s["g"], s["z"]], [vec(p["b_glu"])], [[ds_ssm]], [f32, bf16], True,
                                         256, f"glu_bwd{tag}")
    dg_b = _apply_wt(dz, w, "w_glu", f32, f"glu_proj_dx{tag}")
    gw["w_glu"] = _weight_grad(s["g"], dz, "w_glu", f"glu_proj_dw{tag}")
    dyc, du_skip, gp["ssm_d"] = _rowwise_bwd(_f_gelu, [s["yc"], s["u32"]], [vec(p["ssm_d"])], [[dg_a, dg_b]], [bf16, f32], True,
                                             256, f"ssm_gelu_bwd{tag}")
    dh_state = _mm(dyc, ssm["cc"], f32, f"ssm_out_dx{tag}")
    g_cc = _bd_fold(_mm(dyc, s["hb"], f32, f"ssm_out_dw{tag}", lhs_t=True), n_groups, tc, f"ssm_out_dw_fold{tag}")
    gf, gb = _scan(dh_state, ssm["lam"], True, f"ssm_scan_bwd{tag}")
    d_lam = _lam_grad(gf, s["hf"], f"ssm_lam_grad{tag}")
    du_x = _mm(gb, ssm["bb"], f32, f"ssm_in_dx{tag}", rhs_t=True)
    g_bb = _bd_fold(_mm(s["u"], gb, f32, f"ssm_in_dw{tag}", lhs_t=True), n_groups, tc, f"ssm_in_dw_fold{tag}")
    (du,) = _rowwise(_f_add, [du_skip, du_x], [], [bf16], 256, f"ssm_du{tag}")
    d_lr, d_li = _deinterleave(d_lam, tc)
    g_bbr, g_bbi = _deinterleave(g_bb, tc)
    group_shape = (SSM_GROUP, n_groups, SSM_STATE)
    cots = (d_lr.reshape(n_groups, SSM_STATE), d_li.reshape(n_groups, SSM_STATE), g_bbr.reshape(group_shape), g_bbi.reshape(group_shape))
    d_ldt, d_ar, d_ai, d_br, d_bi = _disc_bwd(*ssm["disc_in"], cots, f"ssm_disc_bwd{tag}")
    gp["ssm_log_dt"], gp["ssm_a_re"], gp["ssm_a_im"] = d_ldt.reshape(-1), d_ar, d_ai
    gp["ssm_b_re"], gp["ssm_b_im"] = d_br.transpose(1, 2, 0), d_bi.transpose(1, 2, 0)
    g_ccr, g_cci = _deinterleave(g_cc, tc)
    gp["ssm_c_re"] = g_ccr.reshape(group_shape).transpose(1, 0, 2)
    gp["ssm_c_im"] = -g_cci.reshape(group_shape).transpose(1, 0, 2)
    res = _rowwise_bwd(_f_amerge, s["o_tok"] + s["lse_x"], [], [[dattn_o]], [f32] * 6, False, 256, f"attn_merge_bwd{tag}")
    do_b = _to_blocks(jnp.concatenate(res[0:3], axis=1), hpp)
    dl_b = _to_blocks(jnp.concatenate(res[3:6], axis=1), hpp)
    dqb, dkb, dvb = _attn_bwd(s["qb"], s["kb"], s["vb"], s["ob"], s["lseb"], do_b, dl_b, hpp, f"attn_bwd{tag}")
    dq, dk, dv = (jnp.concatenate(_from_blocks(t, hpp), axis=1) for t in (dqb, dkb, dvb))
    dcb, dcc, dch, gp["conv_mix_w"] = _convgate_bwd(s["conv"], p["conv_mix_w"], dcm, 0, w4, f"convgate_bwd{tag}")
    dproj = jnp.concatenate([du, dq, dk, dv, dcb, dcc, dch] + dpg, axis=1)
    dh = cargo.hosted(lambda carry: _apply_wt(dproj, w, "w_in", f32, f"in_dx{tag}", carry), [("w_in", 1, 2)])
    gw["w_in"] = cargo.hosted(lambda carry: _weight_grad(s["h"], dproj, "w_in", f"in_dw{tag}", carry), [("w_up", 1, 2)])
    dx0, d_g, d_sc1, d_sh1 = _rowwise_bwd(_f_prenorm, [x], [vec(p["g_pre_mix"]), sc1, sh1], [[dh]], [f32], True,
                                          256, f"prenorm_mix_bwd{tag}", add_to={0: dx1})
    gp["g_pre_mix"] = d_g
    return dx0, [d_sh1, d_sc1, d_gt1, d_sh2, d_sc2, d_gt2], gw, gp


COL_SHARDED = ("w_in", "w_up", "w_ssm_out", "w_attn_out", "w_conv_out")
ROW_SHARDED = ("w_down", "w_o", "w_glu")
SMALL_SHARDED = ("conv_mix_w", "ffn_conv_w")
IN_RIDERS = ("w_up",)
UP_RIDERS = ("w_in",)
DOWN_RIDERS = ("w_ssm_out", "w_attn_out", "w_conv_out", "w_down", "w_o", "w_glu")
LIGHT_SHARDED = ("w_ssm_out", "w_attn_out", "w_conv_out", "w_glu") + SMALL_SHARDED
REPLICATED = ("b_mod", "g_pre_mix", "g_post_mix", "g_pre_ffn", "g_post_ffn", "ssm_log_dt", "ssm_a_re", "ssm_a_im", "ssm_b_re",
              "ssm_b_im", "ssm_c_re", "ssm_c_im", "ssm_d", "b_glu", "b_gate")
WEIGHTS = ('w_mod', 'b_mod', 'g_pre_mix', 'g_post_mix', 'g_pre_ffn', 'g_post_ffn', 'w_in', 'ssm_log_dt', 'ssm_a_re', 'ssm_a_im',
           'ssm_b_re', 'ssm_b_im', 'ssm_c_re', 'ssm_c_im', 'ssm_d', 'w_glu', 'b_glu', 'conv_mix_w', 'w_ssm_out', 'w_attn_out',
           'w_conv_out', 'b_gate', 'w_o', 'w_up', 'ffn_conv_w', 'w_down')


def _full_cols(g):
    return g.transpose(1, 0, 2).reshape(g.shape[1], -1)


def _col_slabs(m):
    return m.reshape(m.shape[0], N_DEV, -1).transpose(1, 0, 2)


def _step(x, c, loss_target, wts, m_in, v_in):
    depth = wts["w_in"].shape[0]
    n_rows, d = x.shape[1], x.shape[2]
    me = _linear(_position())
    x2d, target2d = x[0], loss_target[0]

    (c_all,) = _all_gather([c], "gather_c")
    c_all = c_all.reshape(N_DEV, d)
    mod_cols = wts["w_mod"].shape[2]
    b_cols = lax.dynamic_slice_in_dim(wts["b_mod"], me * mod_cols, mod_cols, axis=1).reshape(depth, 1, mod_cols)
    mod_mine = _mod_fwd(c_all, wts["w_mod"], b_cols, "mod_fwd")
    (mod_all,) = _all_gather([mod_mine], "gather_mod")
    mod_me = lax.dynamic_index_in_dim(mod_all, me, axis=2, keepdims=False).transpose(1, 0, 2).reshape(depth, 6, 1, d)

    names = COL_SHARDED + ROW_SHARDED

    def layer_shards(l):
        return [(wts[k][l].T if k in COL_SHARDED else wts[k][l]).astype(bf16) for k in names]

    def full_matrices(shards, l, begun=None):
        got = _all_gather_relayed(shards, f"gather_w{l}", begun)
        got = [lax.dynamic_update_index_in_dim(g, shard, me, axis=0) for g, shard in zip(got, shards)]
        return {k: g.reshape(-1, g.shape[2]) for k, g in zip(names, got)}

    mats = [full_matrices(layer_shards(0), 0)]
    (conv_mix_all, ffn_conv_all) = _all_gather([wts["conv_mix_w"], wts["ffn_conv_w"]], "gather_conv_w")
    small = []
    for l in range(depth):
        p = {k: wts[k][l] for k in REPLICATED if k != "b_mod"}
        p["conv_mix_w"] = _full_cols(conv_mix_all[:, l])
        p["ffn_conv_w"] = _full_cols(ffn_conv_all[:, l])
        small.append(p)

    saved = []
    h = x2d
    for l in range(depth):
        mod = [mod_me[l, i] for i in range(6)]
        ride = None
        if l + 1 < depth:
            shards = layer_shards(l + 1)
            by_name = dict(zip(names, shards))
            ride = {"in_proj": [by_name[k] for k in IN_RIDERS], "up_proj": [by_name[k] for k in UP_RIDERS],
                    "down_proj": [by_name[k] for k in DOWN_RIDERS]}
        h, s, begun = _layer_fwd(h, mod, mats[l], small[l], f"_l{l}", ride)
        saved.append(s)
        if ride is not None:
            bufs = dict(zip(IN_RIDERS + UP_RIDERS + DOWN_RIDERS, begun["in_proj"] + begun["up_proj"] + begun["down_proj"]))
            mats.append(full_matrices(shards, l + 1, [bufs.get(k) for k in names]))
    dy, loss_part = _loss_head(h, target2d, "loss_head")
    loss = lax.psum(loss_part[0, 0], ("x", "y", "c"))

    grads = {k: [None] * depth for k in WEIGHTS}
    dmod_rows = [None] * depth
    small_parts = [None] * depth
    sharded = COL_SHARDED + ROW_SHARDED + SMALL_SHARDED
    waiting = None
    for l in reversed(range(depth)):
        mod = [mod_me[l, i] for i in range(6)]
        cargo = _Cargo(waiting)
        dy, dmod, gw, gp = _layer_bwd(dy, saved[l], mod, mats[l], small[l], f"_l{l}", cargo)
        if waiting is not None:
            summed = _total_sums([waiting[k] for k in sharded], [cargo.arrived[k] for k in sharded], f"scatter_g{l + 1}")
            for k, g in zip(sharded, summed):
                grads[k][l + 1] = g
        dmod_rows[l] = jnp.concatenate(dmod, axis=1)
        slabs = [gw[k].reshape(N_DEV, -1, gw[k].shape[1]) for k in COL_SHARDED + ROW_SHARDED]
        slabs += [_col_slabs(gp[k]) for k in SMALL_SHARDED]
        waiting = dict(zip(sharded, _chip_sums(slabs, f"scatter_g{l}")))
        small_parts[l] = gp
    last = [waiting[k] for k in sharded]
    for k, g in zip(sharded, _total_sums(last, _chip_exchange(last, "scatter_g0_chips"), "scatter_g0")):
        grads[k][0] = g
    grad_x = dy.reshape(x.shape)

    dmod_mine = jnp.concatenate(dmod_rows, axis=0)
    rep_names = [k for k in REPLICATED if k != "b_mod"]
    pieces = [dmod_mine.reshape(-1)] + [jnp.stack([small_parts[l][k].reshape(-1) for l in range(depth)]).reshape(-1) for k in rep_names]
    sizes = [int(pc.shape[0]) for pc in pieces]
    total = sum(sizes)
    padded = -(-total // (SUBLANES * LANES)) * (SUBLANES * LANES)
    pack = jnp.concatenate(pieces + [jnp.zeros((padded - total,), f32)]).reshape(-1, LANES)
    (pack_all,) = _all_gather([pack], "gather_small_grads")
    pack_sum = _sum_slabs(pack_all, "sum_small_grads").reshape(-1)
    offs = np.cumsum([0] + sizes)
    grads["b_mod"] = pack_sum[offs[0]:offs[1]].reshape(wts["b_mod"].shape)
    for i, k in enumerate(rep_names):
        grads[k] = pack_sum[offs[i + 1]:offs[i + 2]].reshape(wts[k].shape)
    dmod_all = pack_all.reshape(N_DEV, -1)[:, :sizes[0]].reshape(N_DEV, depth, 6 * d)
    dmod_cols = lax.dynamic_slice_in_dim(dmod_all, me * mod_cols, mod_cols, axis=2).transpose(1, 0, 2)
    grads["w_mod"] = _mod_wgrad(c_all.T, dmod_cols, "mod_wgrad")
    for k in COL_SHARDED + ROW_SHARDED + SMALL_SHARDED:
        grads[k] = jnp.stack([g.T for g in grads[k]] if k in COL_SHARDED else grads[k])

    delta, new_m, new_v = {}, {}, {}
    for k in WEIGHTS:
        delta[k], new_m[k], new_v[k] = _adamw(wts[k], grads[k], m_in[k], v_in[k], f"adamw_{k}")
    return (loss, grad_x, *[grads[k] for k in WEIGHTS], *[delta[k] for k in WEIGHTS], *[new_m[k] for k in WEIGHTS],
            *[new_v[k] for k in WEIGHTS])


def kernel(x, c, w_mod, b_mod, g_pre_mix, g_post_mix, g_pre_ffn, g_post_ffn, w_in, ssm_log_dt, ssm_a_re, ssm_a_im, ssm_b_re, ssm_b_im, ssm_c_re, ssm_c_im, ssm_d, w_glu, b_glu, conv_mix_w, w_ssm_out, w_attn_out, w_conv_out, b_gate, w_o, w_up, ffn_conv_w, w_down, loss_target, m_w_mod, m_b_mod, m_g_pre_mix, m_g_post_mix, m_g_pre_ffn, m_g_post_ffn, m_w_in, m_ssm_log_dt, m_ssm_a_re, m_ssm_a_im, m_ssm_b_re, m_ssm_b_im, m_ssm_c_re, m_ssm_c_im, m_ssm_d, m_w_glu, m_b_glu, m_conv_mix_w, m_w_ssm_out, m_w_attn_out, m_w_conv_out, m_b_gate, m_w_o, m_w_up, m_ffn_conv_w, m_w_down, v_w_mod, v_b_mod, v_g_pre_mix, v_g_post_mix, v_g_pre_ffn, v_g_post_ffn, v_w_in, v_ssm_log_dt, v_ssm_a_re, v_ssm_a_im, v_ssm_b_re, v_ssm_b_im, v_ssm_c_re, v_ssm_c_im, v_ssm_d, v_w_glu, v_b_glu, v_conv_mix_w, v_w_ssm_out, v_w_attn_out, v_w_conv_out, v_b_gate, v_w_o, v_w_up, v_ffn_conv_w, v_w_down):
    wts = dict(w_mod=w_mod, b_mod=b_mod, g_pre_mix=g_pre_mix, g_post_mix=g_post_mix, g_pre_ffn=g_pre_ffn, g_post_ffn=g_post_ffn, w_in=w_in, ssm_log_dt=ssm_log_dt, ssm_a_re=ssm_a_re, ssm_a_im=ssm_a_im, ssm_b_re=ssm_b_re, ssm_b_im=ssm_b_im, ssm_c_re=ssm_c_re, ssm_c_im=ssm_c_im, ssm_d=ssm_d, w_glu=w_glu, b_glu=b_glu, conv_mix_w=conv_mix_w, w_ssm_out=w_ssm_out, w_attn_out=w_attn_out, w_conv_out=w_conv_out, b_gate=b_gate, w_o=w_o, w_up=w_up, ffn_conv_w=ffn_conv_w, w_down=w_down)
    m_in = dict(w_mod=m_w_mod, b_mod=m_b_mod, g_pre_mix=m_g_pre_mix, g_post_mix=m_g_post_mix, g_pre_ffn=m_g_pre_ffn, g_post_ffn=m_g_post_ffn, w_in=m_w_in, ssm_log_dt=m_ssm_log_dt, ssm_a_re=m_ssm_a_re, ssm_a_im=m_ssm_a_im, ssm_b_re=m_ssm_b_re, ssm_b_im=m_ssm_b_im, ssm_c_re=m_ssm_c_re, ssm_c_im=m_ssm_c_im, ssm_d=m_ssm_d, w_glu=m_w_glu, b_glu=m_b_glu, conv_mix_w=m_conv_mix_w, w_ssm_out=m_w_ssm_out, w_attn_out=m_w_attn_out, w_conv_out=m_w_conv_out, b_gate=m_b_gate, w_o=m_w_o, w_up=m_w_up, ffn_conv_w=m_ffn_conv_w, w_down=m_w_down)
    v_in = dict(w_mod=v_w_mod, b_mod=v_b_mod, g_pre_mix=v_g_pre_mix, g_post_mix=v_g_post_mix, g_pre_ffn=v_g_pre_ffn, g_post_ffn=v_g_post_ffn, w_in=v_w_in, ssm_log_dt=v_ssm_log_dt, ssm_a_re=v_ssm_a_re, ssm_a_im=v_ssm_a_im, ssm_b_re=v_ssm_b_re, ssm_b_im=v_ssm_b_im, ssm_c_re=v_ssm_c_re, ssm_c_im=v_ssm_c_im, ssm_d=v_ssm_d, w_glu=v_w_glu, b_glu=v_b_glu, conv_mix_w=v_conv_mix_w, w_ssm_out=v_w_ssm_out, w_attn_out=v_w_attn_out, w_conv_out=v_w_conv_out, b_gate=v_b_gate, w_o=v_w_o, w_up=v_w_up, ffn_conv_w=v_ffn_conv_w, w_down=v_w_down)
    return _step(x, c, loss_target, wts, m_in, v_in)
```
